```python
import jax, jax.numpy as jnp
from jax import lax
import numpy as np

D_MODEL = 2048
BATCH = 4
SEQ = 8192
DEPTH = 1

D_CONV = 1024
CONV_K = 3
N_HEADS = 16
N_KV_HEADS = 4
HEAD_DIM = 64
D_ATTN = N_HEADS * HEAD_DIM
D_KV = N_KV_HEADS * HEAD_DIM
D_MIX = D_CONV + D_ATTN
D_IN = 3 * D_CONV + D_ATTN + 2 * D_KV
WINDOW = 128
ATTN_BLOCK = 128
N_GROUPS = 4
EXPERTS_PER_GROUP = 8
N_EXPERTS = N_GROUPS * EXPERTS_PER_GROUP
TOP_K = 2
D_EXPERT = 512
MOE_BLOCK = 256
PLE_DIM = 256
EPS = 1e-6
NEG_INF = -1e30

kernel_name = "hymba_conv_swa_hiermoe_ple"


def rms_norm(x, g):
    xf = x.astype(jnp.float32)
    y = xf * lax.rsqrt(jnp.mean(xf * xf, axis=-1, keepdims=True) + EPS)
    return (y * g.astype(jnp.float32)).astype(x.dtype)


def short_conv_mixer(b_gate, c_gate, hc, conv_w):
    u = c_gate * hc
    z = lax.conv_general_dilated(
        u, conv_w[:, None, :].astype(u.dtype), window_strides=(1,),
        padding=[(CONV_K - 1, 0)], dimension_numbers=('NWC', 'WIO', 'NWC'),
        feature_group_count=D_CONV)
    return b_gate * z


def sliding_window_attention(q, k, v, sinks, slopes):
    bsz, seq = q.shape[0], q.shape[1]
    nb = seq // ATTN_BLOCK
    grp = N_HEADS // N_KV_HEADS
    qb = q.reshape(bsz, nb, ATTN_BLOCK, N_KV_HEADS, grp, HEAD_DIM)

    def band(t):
        tb = t.reshape(bsz, nb, ATTN_BLOCK, N_KV_HEADS, HEAD_DIM)
        prev = jnp.pad(tb[:, :-1], ((0, 0), (1, 0), (0, 0), (0, 0), (0, 0)))
        return jnp.concatenate([prev, tb], axis=2)

    kb, vb = band(k), band(v)
    s = jnp.einsum('bnqkgd,bnskd->bnkgqs', qb, kb,
                   preferred_element_type=jnp.float32) * (HEAD_DIM ** -0.5)
    qi = jnp.arange(ATTN_BLOCK)[:, None]
    sj = jnp.arange(2 * ATTN_BLOCK)[None, :]
    dist = ATTN_BLOCK + qi - sj
    in_window = (dist >= 0) & (dist < WINDOW)
    key_exists = (jnp.arange(nb)[:, None] > 0) | (sj >= ATTN_BLOCK)
    mask = in_window[None, :, :] & key_exists[:, None, :]
    m_h = slopes.astype(jnp.float32).reshape(N_KV_HEADS, grp, 1, 1)
    alibi = -m_h * dist.astype(jnp.float32)
    s = jnp.where(mask[None, :, None, None], s + alibi, NEG_INF)
    sink = sinks.astype(jnp.float32).reshape(N_KV_HEADS, grp, 1, 1)
    m = jnp.maximum(jnp.max(s, axis=-1, keepdims=True), sink)
    pr = jnp.exp(s - m)
    denom = jnp.sum(pr, axis=-1, keepdims=True) + jnp.exp(sink - m)
    pr = pr / denom
    o = jnp.einsum('bnkgqs,bnskd->bnqkgd', pr, vb.astype(jnp.float32))
    return o.reshape(bsz, seq, D_ATTN).astype(q.dtype)


def hierarchical_moe(x, w_group, b_group, w_router, b_router, w_gate, w_up, w_down):
    bsz, seq, d = x.shape
    n = bsz * seq
    xt = x.reshape(n, d)
    group_probs = jax.nn.softmax((xt @ w_group).astype(jnp.float32) + b_group.astype(jnp.float32), axis=-1)
    g_w, g_idx = lax.top_k(group_probs, 1)
    g_w, g_idx = g_w[:, 0], g_idx[:, 0]
    expert_logits = ((xt @ w_router).astype(jnp.float32) + b_router.astype(jnp.float32))
    expert_logits = expert_logits.reshape(n, N_GROUPS, EXPERTS_PER_GROUP)
    within = expert_logits[jnp.arange(n), g_idx]
    within_probs = jax.nn.softmax(within, axis=-1)
    top_w, top_local = lax.top_k(within_probs, TOP_K)
    top_w = top_w / jnp.sum(top_w, axis=-1, keepdims=True)
    weights = g_w[:, None] * top_w
    expert_id = g_idx[:, None] * EXPERTS_PER_GROUP + top_local

    a = n * TOP_K
    e_flat = expert_id.reshape(a).astype(jnp.int32)
    tok_flat = jnp.repeat(jnp.arange(n, dtype=jnp.int32), TOP_K)
    w_flat = weights.reshape(a)
    order = jnp.argsort(e_flat)
    e_sorted, tok_sorted, w_sorted = e_flat[order], tok_flat[order], w_flat[order]
    counts = jax.ops.segment_sum(jnp.ones((a,), jnp.int32), e_flat, num_segments=N_EXPERTS)
    starts = jnp.cumsum(counts) - counts
    padded = (counts + MOE_BLOCK - 1) // MOE_BLOCK * MOE_BLOCK
    pad_ends = jnp.cumsum(padded)
    pad_starts = pad_ends - padded
    dest = pad_starts[e_sorted] + (jnp.arange(a, dtype=jnp.int32) - starts[e_sorted])
    n_rows = a + N_EXPERTS * MOE_BLOCK
    n_blocks = n_rows // MOE_BLOCK
    row_tok = jnp.zeros((n_rows,), jnp.int32).at[dest].set(tok_sorted)
    row_w = jnp.zeros((n_rows,), jnp.float32).at[dest].set(w_sorted)
    block_e = jnp.searchsorted(pad_ends, jnp.arange(n_blocks, dtype=jnp.int32) * MOE_BLOCK, side='right')
    block_e = jnp.minimum(block_e, N_EXPERTS - 1)

    def expert_block(args):
        e, toks, ws = args
        xb = xt[toks]
        hidden = jax.nn.silu(xb @ w_gate[e]) * (xb @ w_up[e])
        y = hidden @ w_down[e]
        return y * ws[:, None].astype(y.dtype)

    y = lax.map(expert_block, (block_e, row_tok.reshape(n_blocks, MOE_BLOCK),
                               row_w.reshape(n_blocks, MOE_BLOCK)))
    out = jax.ops.segment_sum(y.reshape(n_rows, d), row_tok, num_segments=n)
    return out.reshape(bsz, seq, d)


def setup_inputs(seed: int = 0) -> dict:
    key = jax.random.key(seed)
    ks = jax.random.split(key, 24)
    f32 = jnp.float32

    def nrm(k, shape, scale):
        return jax.random.normal(k, shape, f32) * scale

    def gain(k, shape):
        return 1.0 + 0.02 * jax.random.normal(k, shape, f32)

    L = DEPTH
    return {
        "x": nrm(ks[0], (BATCH, SEQ, D_MODEL), 1.0),
        "p": nrm(ks[1], (DEPTH, BATCH, SEQ, PLE_DIM), 1.0),
        "norm_mix": gain(ks[2], (L, D_MODEL)),
        "w_in": nrm(ks[3], (L, D_MODEL, D_IN), D_MODEL ** -0.5),
        "conv_w": nrm(ks[4], (L, CONV_K, D_CONV), CONV_K ** -0.5),
        "q_norm": gain(ks[5], (L, HEAD_DIM)),
        "k_norm": gain(ks[6], (L, HEAD_DIM)),
        "sinks": nrm(ks[7], (L, N_HEADS), 0.5),
        "out_norm_conv": gain(ks[8], (L, D_CONV)),
        "out_norm_attn": gain(ks[9], (L, D_ATTN)),
        "w_out": nrm(ks[10], (L, D_MIX, D_MODEL), D_MIX ** -0.5),
        "norm_ffn": gain(ks[11], (L, D_MODEL)),
        "w_group": nrm(ks[12], (L, D_MODEL, N_GROUPS), D_MODEL ** -0.5),
        "b_group": nrm(ks[13], (L, N_GROUPS), 0.01),
        "w_router": nrm(ks[14], (L, D_MODEL, N_EXPERTS), D_MODEL ** -0.5),
        "b_router": nrm(ks[15], (L, N_EXPERTS), 0.01),
        "w_gate": nrm(ks[16], (L, N_EXPERTS, D_MODEL, D_EXPERT), D_MODEL ** -0.5),
        "w_up": nrm(ks[17], (L, N_EXPERTS, D_MODEL, D_EXPERT), D_MODEL ** -0.5),
        "w_down": nrm(ks[18], (L, N_EXPERTS, D_EXPERT, D_MODEL), D_EXPERT ** -0.5),
        "norm_ple": gain(ks[19], (L, D_MODEL)),
        "w_ple_gate": nrm(ks[20], (L, D_MODEL, D_MODEL), D_MODEL ** -0.5),
        "w_ple": nrm(ks[21], (L, PLE_DIM, D_MODEL), PLE_DIM ** -0.5),
    }


def reference(x, p, norm_mix, w_in, conv_w, q_norm, k_norm, sinks, out_norm_conv,
              out_norm_attn, w_out, norm_ffn, w_group, b_group, w_router, b_router,
              w_gate, w_up, w_down, norm_ple, w_ple_gate, w_ple):
    bsz, seq, _ = x.shape
    slopes = 2.0 ** (-8.0 * jnp.arange(1, N_HEADS + 1, dtype=jnp.float32) / N_HEADS)
    split_at = [D_CONV, 2 * D_CONV, 3 * D_CONV, 3 * D_CONV + D_ATTN, 3 * D_CONV + D_ATTN + D_KV]
    h = x
    for i in range(DEPTH):
        xn = rms_norm(h, norm_mix[i])
        proj = xn @ w_in[i]
        b_g, c_g, hc, q, k, v = jnp.split(proj, split_at, axis=-1)
        y_conv = short_conv_mixer(b_g, c_g, hc, conv_w[i])
        q = rms_norm(q.reshape(bsz, seq, N_HEADS, HEAD_DIM), q_norm[i])
        k = rms_norm(k.reshape(bsz, seq, N_KV_HEADS, HEAD_DIM), k_norm[i])
        v = v.reshape(bsz, seq, N_KV_HEADS, HEAD_DIM)
        y_attn = sliding_window_attention(q, k, v, sinks[i], slopes)
        mixed = jnp.concatenate([rms_norm(y_conv, out_norm_conv[i]),
                                 rms_norm(y_attn, out_norm_attn[i])], axis=-1)
        h = h + mixed @ w_out[i]
        h = h + hierarchical_moe(rms_norm(h, norm_ffn[i]), w_group[i], b_group[i],
                                 w_router[i], b_router[i], w_gate[i], w_up[i], w_down[i])
        gate = jax.nn.sigmoid(rms_norm(h, norm_ple[i]) @ w_ple_gate[i])
        h = h + gate * (p[i] @ w_ple[i])
    return h
```

```python
import functools

import jax
import jax.numpy as jnp
from jax import lax
from jax.experimental import pallas as pl
from jax.experimental.pallas import tpu as pltpu

D_MODEL = 2048
D_CONV = 1024
N_HEADS = 16
N_KV_HEADS = 4
HEAD_DIM = 64
D_ATTN = N_HEADS * HEAD_DIM
D_KV = N_KV_HEADS * HEAD_DIM
D_IN = 3 * D_CONV + D_ATTN + 2 * D_KV
ATTN_BLOCK = 128
N_GROUPS = 4
EXPERTS_PER_GROUP = 8
N_EXPERTS = N_GROUPS * EXPERTS_PER_GROUP
TOP_K = 2
D_EXPERT = 512
PLE_DIM = 256
EPS = 1e-6
NEG_INF = -1e30

LANES = 128
HALF = D_MODEL // 2
KV_DUP = 2 * D_KV

TM_PROJ = 512
TM_MIX = 256
TM_ROUTE = 1024
TM_DISP = 256
TM_COMB = 256
MOE_BLK = 256
VMEM_LIMIT = 56 * 1024 * 1024

_F32 = jnp.float32
_BF16 = jnp.bfloat16
_U32 = jnp.uint32
_HI_MASK = 0xFFFF0000


def _rms(x, gain):
    return x * lax.rsqrt(jnp.mean(x * x, axis=-1, keepdims=True) + EPS) * gain


def _dot(a, b):
    return jnp.dot(a, b, preferred_element_type=_F32)


def _pack_rows(x):
    h = x.shape[-1] // 2
    bits = lax.bitcast_convert_type(x.astype(_BF16).astype(_F32), _U32)
    return (bits[:, h:] & _U32(_HI_MASK)) | (bits[:, :h] >> 16)


def _unpack_rows(w):
    lo = lax.bitcast_convert_type(w << 16, _F32)
    hi = lax.bitcast_convert_type(w & _U32(_HI_MASK), _F32)
    return lo, hi


def _inproj_kernel(x_ref, g_ref, w_ref, qg_ref, kg_ref, bu_ref, q_ref, kd_ref, vd_ref):
    tm = x_ref.shape[0]
    xn = _rms(x_ref[...], g_ref[...]).astype(_BF16)
    ch = 512
    for j in range(D_CONV // ch):
        b = _dot(xn, w_ref[:, j * ch:(j + 1) * ch])
        c = _dot(xn, w_ref[:, D_CONV + j * ch:D_CONV + (j + 1) * ch])
        hc = _dot(xn, w_ref[:, 2 * D_CONV + j * ch:2 * D_CONV + (j + 1) * ch])
        bu_ref[:, j * ch:(j + 1) * ch] = b.astype(_BF16)
        bu_ref[:, D_CONV + j * ch:D_CONV + (j + 1) * ch] = (c * hc).astype(_BF16)

    lo = lax.broadcasted_iota(jnp.int32, (tm, LANES), 1) < HEAD_DIM
    q = _dot(xn, w_ref[:, 3 * D_CONV:3 * D_CONV + D_ATTN])
    scale = HEAD_DIM ** -0.5
    for pr in range(D_ATTN // LANES):
        qp = q[:, pr * LANES:(pr + 1) * LANES]
        sq = qp * qp
        s_lo = jnp.sum(jnp.where(lo, sq, 0.0), axis=-1, keepdims=True)
        s_hi = jnp.sum(jnp.where(lo, 0.0, sq), axis=-1, keepdims=True)
        ms = jnp.where(lo, s_lo, s_hi) * (1.0 / HEAD_DIM)
        qn = qp * lax.rsqrt(ms + EPS) * qg_ref[...] * scale
        q_ref[:, pr * LANES:(pr + 1) * LANES] = qn.astype(_BF16)

    kv = _dot(xn, w_ref[:, 3 * D_CONV + D_ATTN:D_IN])
    for pr in range(2 * D_KV // LANES):
        pair = kv[:, pr * LANES:(pr + 1) * LANES]
        swapped = pltpu.roll(pair, HEAD_DIM, 1)
        for half in range(2):
            dup = jnp.where(lo, pair, swapped) if half == 0 else jnp.where(lo, swapped, pair)
            head = 2 * pr + half
            if head < N_KV_HEADS:
                ms = jnp.sum(dup * dup, axis=-1, keepdims=True) * (1.0 / LANES)
                dup = dup * lax.rsqrt(ms + EPS) * kg_ref[...]
                kd_ref[:, head * LANES:(head + 1) * LANES] = dup.astype(_BF16)
            else:
                head -= N_KV_HEADS
                vd_ref[:, head * LANES:(head + 1) * LANES] = dup.astype(_BF16)


def _inproj(x2, g, w_in, qg2, kg2):
    n = x2.shape[0]
    tm = TM_PROJ
    const = lambda i: (0, 0)
    return pl.pallas_call(
        _inproj_kernel,
        grid=(n // tm,),
        in_specs=[
            pl.BlockSpec((tm, D_MODEL), lambda i: (i, 0)),
            pl.BlockSpec((1, D_MODEL), const),
            pl.BlockSpec((D_MODEL, D_IN), const, pipeline_mode=pl.Buffered(1)),
            pl.BlockSpec((1, LANES), const),
            pl.BlockSpec((1, LANES), const),
        ],
        out_specs=[
            pl.BlockSpec((tm, 2 * D_CONV), lambda i: (i, 0)),
            pl.BlockSpec((tm, D_ATTN), lambda i: (i, 0)),
            pl.BlockSpec((tm, KV_DUP), lambda i: (i, 0)),
            pl.BlockSpec((tm, KV_DUP), lambda i: (i, 0)),
        ],
        out_shape=[
            jax.ShapeDtypeStruct((n, 2 * D_CONV), _BF16),
            jax.ShapeDtypeStruct((n, D_ATTN), _BF16),
            jax.ShapeDtypeStruct((n, KV_DUP), _BF16),
            jax.ShapeDtypeStruct((n, KV_DUP), _BF16),
        ],
        compiler_params=pltpu.CompilerParams(
            dimension_semantics=("parallel",), vmem_limit_bytes=VMEM_LIMIT),
        name="inproj",
    )(x2, g, w_in, qg2, kg2)


def _mixer_kernel(tiles_per_seq,
                  bu_ref, uh_ref, q_ref, kd_ref, vd_ref, kdh_ref, vdh_ref, x_ref,
                  cw_ref, tab_ref, sink_ref, onc_ref, ona_ref, wout_ref, nffn_ref,
                  wrh_ref, wrl_ref, rb_ref,
                  h1_ref, xp_ref, lg_ref, ya_ref):
    tm = x_ref.shape[0]
    blk = ATTN_BLOCK
    is_first = (pl.program_id(0) % tiles_per_seq) == 0

    lo = lax.broadcasted_iota(jnp.int32, (blk, LANES), 1) < HEAD_DIM
    no_prev = jnp.logical_and(
        is_first, lax.broadcasted_iota(jnp.int32, (4 * blk, 2 * blk), 1) < blk)
    zero = jnp.zeros((blk, LANES), _BF16)
    for j in range(tm // blk):
        rows = slice(j * blk, (j + 1) * blk)
        for g in range(N_KV_HEADS):
            cols = slice(g * LANES, (g + 1) * LANES)
            qa = q_ref[rows, 2 * g * LANES:(2 * g + 1) * LANES]
            qb = q_ref[rows, (2 * g + 1) * LANES:(2 * g + 2) * LANES]
            lhs = jnp.concatenate(
                [jnp.where(lo, qa, zero), jnp.where(lo, zero, qa),
                 jnp.where(lo, qb, zero), jnp.where(lo, zero, qb)], axis=0)
            if j == 0:
                keys = jnp.concatenate([kdh_ref[:, cols], kd_ref[0:blk, cols]], axis=0)
                vals = jnp.concatenate([vdh_ref[:, cols], vd_ref[0:blk, cols]], axis=0)
            else:
                keys = kd_ref[(j - 1) * blk:(j + 1) * blk, cols]
                vals = vd_ref[(j - 1) * blk:(j + 1) * blk, cols]
            s = lax.dot_general(lhs, keys, (((1,), (1,)), ((), ())),
                                preferred_element_type=_F32)
            s = s + tab_ref[g]
            if j == 0:
                s = jnp.where(no_prev, NEG_INF, s)
            sink = sink_ref[g]
            m = jnp.maximum(jnp.max(s, axis=-1, keepdims=True), sink)
            p = jnp.exp(s - m)
            denom = jnp.sum(p, axis=-1, keepdims=True) + jnp.exp(sink - m)
            o = _dot(p.astype(_BF16), vals) / denom
            ya_ref[rows, 2 * g * LANES:(2 * g + 1) * LANES] = jnp.where(
                lo, o[0:blk], o[blk:2 * blk])
            ya_ref[rows, (2 * g + 1) * LANES:(2 * g + 2) * LANES] = jnp.where(
                lo, o[2 * blk:3 * blk], o[3 * blk:4 * blk])

    b = bu_ref[:, 0:D_CONV].astype(_F32)
    u = bu_ref[:, D_CONV:2 * D_CONV].astype(_F32)
    uh = jnp.where(is_first, 0.0, uh_ref[...].astype(_F32))
    row = lax.broadcasted_iota(jnp.int32, (tm, D_CONV), 0)
    u1 = jnp.where(row == 0, uh[15:16], pltpu.roll(u, 1, 0))
    u2 = jnp.where(row == 0, uh[14:15],
                   jnp.where(row == 1, uh[15:16], pltpu.roll(u, 2, 0)))
    z = cw_ref[2:3, :] * u + cw_ref[1:2, :] * u1 + cw_ref[0:1, :] * u2
    yc = _rms(b * z, onc_ref[...])
    ya = _rms(ya_ref[...], ona_ref[...])
    mixed = jnp.concatenate([yc, ya], axis=-1).astype(_BF16)
    h1 = x_ref[...] + _dot(mixed, wout_ref[...])
    h1_ref[...] = h1

    xn = _rms(h1, nffn_ref[...])
    xp_ref[...] = _pack_rows(xn)
    hi = xn.astype(_BF16)
    rem = (xn - hi.astype(_F32)).astype(_BF16)
    lg_ref[...] = (_dot(hi, wrh_ref[...]) + _dot(rem, wrh_ref[...])
                   + _dot(hi, wrl_ref[...]) + rb_ref[...])


def _mixer(bu, q, kd, vd, x2, cw, tab, sink, onc, ona, w_out, nffn, wrh, wrl, rb, seq):
    n = x2.shape[0]
    tm = TM_MIX
    const2 = lambda i: (0, 0)
    const3 = lambda i: (0, 0, 0)
    prev_blk = lambda i: (jnp.maximum(i * (tm // ATTN_BLOCK) - 1, 0), 0)
    return pl.pallas_call(
        functools.partial(_mixer_kernel, seq // tm),
        grid=(n // tm,),
        in_specs=[
            pl.BlockSpec((tm, 2 * D_CONV), lambda i: (i, 0)),
            pl.BlockSpec((16, D_CONV), lambda i: (jnp.maximum(i * (tm // 16) - 1, 0), 1)),
            pl.BlockSpec((tm, D_ATTN), lambda i: (i, 0)),
            pl.BlockSpec((tm, KV_DUP), lambda i: (i, 0)),
            pl.BlockSpec((tm, KV_DUP), lambda i: (i, 0)),
            pl.BlockSpec((ATTN_BLOCK, KV_DUP), prev_blk),
            pl.BlockSpec((ATTN_BLOCK, KV_DUP), prev_blk),
            pl.BlockSpec((tm, D_MODEL), lambda i: (i, 0)),
            pl.BlockSpec((3, D_CONV), const2),
            pl.BlockSpec((N_KV_HEADS, 4 * ATTN_BLOCK, 2 * ATTN_BLOCK), const3,
                         pipeline_mode=pl.Buffered(1)),
            pl.BlockSpec((N_KV_HEADS, 4 * ATTN_BLOCK, 1), const3, pipeline_mode=pl.Buffered(1)),
            pl.BlockSpec((1, D_CONV), const2),
            pl.BlockSpec((1, D_ATTN), const2),
            pl.BlockSpec((D_MODEL, D_MODEL), const2, pipeline_mode=pl.Buffered(1)),
            pl.BlockSpec((1, D_MODEL), const2),
            pl.BlockSpec((D_MODEL, LANES), const2),
            pl.BlockSpec((D_MODEL, LANES), const2),
            pl.BlockSpec((1, LANES), const2),
        ],
        out_specs=[
            pl.BlockSpec((tm, D_MODEL), lambda i: (i, 0)),
            pl.BlockSpec((tm, HALF), lambda i: (i, 0)),
            pl.BlockSpec((tm, LANES), lambda i: (i, 0)),
        ],
        out_shape=[
            jax.ShapeDtypeStruct((n, D_MODEL), _F32),
            jax.ShapeDtypeStruct((n, HALF), _U32),
            jax.ShapeDtypeStruct((n, LANES), _F32),
        ],
        scratch_shapes=[pltpu.VMEM((tm, D_ATTN), _F32)],
        compiler_params=pltpu.CompilerParams(
            dimension_semantics=("parallel",), vmem_limit_bytes=VMEM_LIMIT),
        name="mixer",
    )(bu, bu, q, kd, vd, kd, vd, x2, cw, tab, sink, onc, ona, w_out, nffn, wrh, wrl, rb)


def _route_kernel(lg_ref, route_ref, cnt_ref, run_ref):
    t = lg_ref.shape[0]

    @pl.when(pl.program_id(0) == 0)
    def _():
        run_ref[...] = jnp.zeros_like(run_ref)

    lg = lg_ref[...]
    lane = lax.broadcasted_iota(jnp.int32, (t, LANES), 1)
    ninf = -jnp.inf

    lane_f = lane.astype(_F32)

    def first_max(v):
        m = jnp.max(v, axis=-1, keepdims=True)
        idx = jnp.min(jnp.where(v == m, lane_f, float(LANES)), axis=-1, keepdims=True)
        return m, idx.astype(jnp.int32)

    is_g = lane < N_GROUPS
    gmax, gidx = first_max(jnp.where(is_g, lg, ninf))
    gsum = jnp.sum(jnp.where(is_g, jnp.exp(jnp.where(is_g, lg, gmax) - gmax), 0.0),
                   axis=-1, keepdims=True)
    g_w = 1.0 / gsum
    base = N_GROUPS + EXPERTS_PER_GROUP * gidx
    in_grp = jnp.logical_and(lane >= base, lane < base + EXPERTS_PER_GROUP)
    el = jnp.where(in_grp, lg, ninf)
    m1, i1 = first_max(el)
    m2, i2 = first_max(jnp.where(lane == i1, ninf, el))
    e21 = jnp.exp(m2 - m1)
    w1 = g_w / (1.0 + e21)
    w2 = g_w * e21 / (1.0 + e21)
    e1 = i1 - N_GROUPS
    e2 = i2 - N_GROUPS

    oh1 = lane == e1
    oh2 = lane == e2
    earlier = (lax.broadcasted_iota(jnp.int32, (t, t), 0)
               > lax.broadcasted_iota(jnp.int32, (t, t), 1)).astype(_BF16)
    c1 = _dot(earlier, oh1.astype(_BF16))
    c2 = _dot(earlier, oh2.astype(_BF16))
    tot1 = jnp.sum(oh1.astype(_F32), axis=0, keepdims=True)
    tot2 = jnp.sum(oh2.astype(_F32), axis=0, keepdims=True)
    run = run_ref[0:1, :]
    r1 = jnp.sum(jnp.where(oh1, run + c1, 0.0), axis=-1, keepdims=True)
    r2 = jnp.sum(jnp.where(oh2, run + tot1 + c2, 0.0), axis=-1, keepdims=True)
    run_ref[...] = jnp.broadcast_to(run + tot1 + tot2, run_ref.shape)
    cnt_ref[...] = run_ref[...]

    out = jnp.where(lane == 0, e1.astype(_F32), 0.0)
    out = jnp.where(lane == 1, e2.astype(_F32), out)
    out = jnp.where(lane == 2, w1, out)
    out = jnp.where(lane == 3, w2, out)
    out = jnp.where(lane == 4, r1, out)
    out = jnp.where(lane == 5, r2, out)
    route_ref[...] = out


def _route(logits):
    n = logits.shape[0]
    t = TM_ROUTE
    return pl.pallas_call(
        _route_kernel,
        grid=(n // t,),
        in_specs=[pl.BlockSpec((t, LANES), lambda i: (i, 0))],
        out_specs=[pl.BlockSpec((t, LANES), lambda i: (i, 0)),
                   pl.BlockSpec((8, LANES), lambda i: (0, 0))],
        out_shape=[jax.ShapeDtypeStruct((n, LANES), _F32),
                   jax.ShapeDtypeStruct((8, LANES), _F32)],
        scratch_shapes=[pltpu.VMEM((8, LANES), _F32)],
        compiler_params=pltpu.CompilerParams(dimension_semantics=("arbitrary",)),
        name="route",
    )(logits)


def _dispatch_kernel(pad_end_ref, padded_ref, xp_ref, dest_ref, xs_ref, zero_ref, sem, zsem):
    td = xp_ref.shape[0]
    nb = xs_ref.shape[0] // MOE_BLK
    nb_used = pad_end_ref[N_EXPERTS - 1] // MOE_BLK

    def zero_copy(start):
        start = pl.multiple_of(start, MOE_BLK)
        return pltpu.make_async_copy(zero_ref, xs_ref.at[pl.ds(start, MOE_BLK)], zsem)

    def zero_blocks(fn):
        for e in range(N_EXPERTS):
            @pl.when(padded_ref[e] > 0)
            def _():
                fn(zero_copy(pad_end_ref[e] - MOE_BLK))
        for b in range(nb - N_EXPERTS, nb):
            @pl.when(b >= nb_used)
            def _():
                fn(zero_copy(b * MOE_BLK))

    @pl.when(pl.program_id(0) == 0)
    def _():
        zero_ref[...] = jnp.zeros_like(zero_ref)
        zero_blocks(lambda c: c.start())
        zero_blocks(lambda c: c.wait())

    def row_copy(r, k):
        d = dest_ref[0, 0, TOP_K * r + k]
        return pltpu.make_async_copy(xp_ref.at[pl.ds(r, 1)], xs_ref.at[pl.ds(d, 1)], sem)

    def issue(r, c):
        for k in range(TOP_K):
            row_copy(r, k).start()
        return c

    lax.fori_loop(0, td, issue, 0, unroll=8)
    for k in range(TOP_K):
        pltpu.make_async_copy(xp_ref, xs_ref.at[pl.ds(0, td)], sem).wait()


def _dispatch(xp, dest3, pad_end, padded, n_rows):
    n = xp.shape[0]
    td = TM_DISP
    return pl.pallas_call(
        _dispatch_kernel,
        grid_spec=pltpu.PrefetchScalarGridSpec(
            num_scalar_prefetch=2,
            grid=(n // td,),
            in_specs=[
                pl.BlockSpec((td, HALF), lambda i, pe, pd: (i, 0)),
                pl.BlockSpec((1, 1, TOP_K * td), lambda i, pe, pd: (i, 0, 0),
                             memory_space=pltpu.SMEM),
            ],
            out_specs=pl.BlockSpec(memory_space=pl.ANY),
            scratch_shapes=[pltpu.VMEM((MOE_BLK, HALF), _U32),
                            pltpu.SemaphoreType.DMA, pltpu.SemaphoreType.DMA],
        ),
        out_shape=jax.ShapeDtypeStruct((n_rows, HALF), _U32),
        compiler_params=pltpu.CompilerParams(
            dimension_semantics=("arbitrary",)),
        name="dispatch",
    )(pad_end, padded, xp, dest3)


def _experts_kernel(be_ref, nb_ref, xs_ref, wg_ref, wu_ref, wd_ref, ys_ref,
                    wgb_ref, wub_ref, wdb_ref):
    i = pl.program_id(0)
    valid = i < nb_ref[0]
    changed = jnp.logical_or(i == 0, be_ref[i] != be_ref[jnp.maximum(i - 1, 0)])

    @pl.when(jnp.logical_and(valid, changed))
    def _():
        wgb_ref[...] = wg_ref[0].astype(_BF16)
        wub_ref[...] = wu_ref[0].astype(_BF16)
        wdb_ref[...] = wd_ref[0].astype(_BF16)

    @pl.when(valid)
    def _():
        lo, hi = _unpack_rows(xs_ref[...])
        lo = lo.astype(_BF16)
        hi = hi.astype(_BF16)
        g = _dot(lo, wgb_ref[0:HALF, :]) + _dot(hi, wgb_ref[HALF:D_MODEL, :])
        u = _dot(lo, wub_ref[0:HALF, :]) + _dot(hi, wub_ref[HALF:D_MODEL, :])
        hidden = (g * jax.nn.sigmoid(g) * u).astype(_BF16)
        ys_ref[...] = _pack_rows(_dot(hidden, wdb_ref[...]))

    @pl.when(jnp.logical_not(valid))
    def _():
        ys_ref[...] = jnp.zeros_like(ys_ref)


def _experts(xs, w_gate, w_up, w_down, block_e, nb_used):
    n_rows = xs.shape[0]
    nb = n_rows // MOE_BLK
    row_blk = lambda i, be, nbu: (jnp.minimum(i, nbu[0] - 1), 0)
    w_blk = lambda i, be, nbu: (be[i], 0, 0)
    return pl.pallas_call(
        _experts_kernel,
        grid_spec=pltpu.PrefetchScalarGridSpec(
            num_scalar_prefetch=2,
            grid=(nb,),
            in_specs=[
                pl.BlockSpec((MOE_BLK, HALF), row_blk),
                pl.BlockSpec((1, D_MODEL, D_EXPERT), w_blk),
                pl.BlockSpec((1, D_MODEL, D_EXPERT), w_blk),
                pl.BlockSpec((1, D_EXPERT, D_MODEL), w_blk),
            ],
            out_specs=pl.BlockSpec((MOE_BLK, HALF), lambda i, be, nbu: (i, 0)),
            scratch_shapes=[pltpu.VMEM((D_MODEL, D_EXPERT), _BF16),
                            pltpu.VMEM((D_MODEL, D_EXPERT), _BF16),
                            pltpu.VMEM((D_EXPERT, D_MODEL), _BF16)],
        ),
        out_shape=jax.ShapeDtypeStruct((n_rows, HALF), _U32),
        compiler_params=pltpu.CompilerParams(
            dimension_semantics=("arbitrary",), vmem_limit_bytes=VMEM_LIMIT),
        name="experts",
    )(block_e, nb_used, xs, w_gate, w_up, w_down)


def _combine_kernel(h1_ref, route_ref, dcur_ref, dnext_ref, ys_ref, p_ref, nple_ref,
                    wgate_ref, wple_ref, out_ref, ybuf_ref, sem):
    tk = h1_ref.shape[0]
    i = pl.program_id(0)
    n_steps = pl.num_programs(0)
    slot = i % 2

    def issue(d_ref, s):
        def body(r, c):
            for k in range(TOP_K):
                d = d_ref[0, 0, TOP_K * r + k]
                pltpu.make_async_copy(ys_ref.at[pl.ds(d, 1)],
                                      ybuf_ref.at[s, k, pl.ds(r, 1)], sem.at[s]).start()
            return c
        lax.fori_loop(0, tk, body, 0, unroll=8)

    @pl.when(i == 0)
    def _():
        issue(dcur_ref, 0)

    @pl.when(i + 1 < n_steps)
    def _():
        issue(dnext_ref, 1 - slot)

    for k in range(TOP_K):
        pltpu.make_async_copy(ys_ref.at[pl.ds(0, tk)], ybuf_ref.at[slot, k],
                              sem.at[slot]).wait()

    y1 = jnp.concatenate(_unpack_rows(ybuf_ref[slot, 0]), axis=-1)
    y2 = jnp.concatenate(_unpack_rows(ybuf_ref[slot, 1]), axis=-1)
    route = route_ref[...]
    h2 = h1_ref[...] + route[:, 2:3] * y1 + route[:, 3:4] * y2
    xn = _rms(h2, nple_ref[...]).astype(_BF16)
    gate = jax.nn.sigmoid(_dot(xn, wgate_ref[...]))
    emb = _dot(p_ref[...].astype(_BF16), wple_ref[...])
    out_ref[...] = h2 + gate * emb


def _combine(h1, route, dest3, ys, p2, nple, w_gate, w_ple):
    n = h1.shape[0]
    tk = TM_COMB
    steps = n // tk
    const2 = lambda i: (0, 0)
    return pl.pallas_call(
        _combine_kernel,
        grid=(steps,),
        in_specs=[
            pl.BlockSpec((tk, D_MODEL), lambda i: (i, 0)),
            pl.BlockSpec((tk, LANES), lambda i: (i, 0)),
            pl.BlockSpec((1, 1, TOP_K * tk), lambda i: (i, 0, 0), memory_space=pltpu.SMEM),
            pl.BlockSpec((1, 1, TOP_K * tk), lambda i: (jnp.minimum(i + 1, steps - 1), 0, 0),
                         memory_space=pltpu.SMEM),
            pl.BlockSpec(memory_space=pl.ANY),
            pl.BlockSpec((tk, PLE_DIM), lambda i: (i, 0)),
            pl.BlockSpec((1, D_MODEL), const2),
            pl.BlockSpec((D_MODEL, D_MODEL), const2, pipeline_mode=pl.Buffered(1)),
            pl.BlockSpec((PLE_DIM, D_MODEL), const2, pipeline_mode=pl.Buffered(1)),
        ],
        out_specs=pl.BlockSpec((tk, D_MODEL), lambda i: (i, 0)),
        out_shape=jax.ShapeDtypeStruct((n, D_MODEL), _F32),
        scratch_shapes=[pltpu.VMEM((2, TOP_K, tk, HALF), _U32),
                        pltpu.SemaphoreType.DMA((2,))],
        compiler_params=pltpu.CompilerParams(
            dimension_semantics=("arbitrary",), vmem_limit_bytes=VMEM_LIMIT),
        name="combine",
    )(h1, route, dest3, dest3, ys, p2, nple, w_gate, w_ple)


def _attention_tables(sinks):
    slopes = 2.0 ** (-8.0 * jnp.arange(1, N_HEADS + 1, dtype=_F32) / N_HEADS)
    qi = jnp.arange(ATTN_BLOCK)[:, None]
    sj = jnp.arange(2 * ATTN_BLOCK)[None, :]
    dist = ATTN_BLOCK + qi - sj
    in_window = (dist >= 0) & (dist < ATTN_BLOCK)
    alibi = -slopes[:, None, None] * dist.astype(_F32)[None]
    tab = jnp.where(in_window[None], alibi, NEG_INF)
    tab = tab.reshape(N_KV_HEADS, 4 * ATTN_BLOCK, 2 * ATTN_BLOCK)
    sink = jnp.repeat(sinks.astype(_F32), ATTN_BLOCK).reshape(N_KV_HEADS, 4 * ATTN_BLOCK, 1)
    return tab, sink


def _layer(h, p_i, norm_mix, w_in, conv_w, q_norm, k_norm, sinks, out_norm_conv,
           out_norm_attn, w_out, norm_ffn, w_group, b_group, w_router, b_router,
           w_gate, w_up, w_down, norm_ple, w_ple_gate, w_ple):
    bsz, seq, _ = h.shape
    n = bsz * seq
    x2 = h.reshape(n, D_MODEL)
    row = lambda v: v.reshape(1, -1).astype(_F32)

    bu, q, kd, vd = _inproj(
        x2, row(norm_mix), w_in.astype(_BF16),
        row(jnp.tile(q_norm, 2)), row(jnp.tile(k_norm, 2)))

    tab, sink = _attention_tables(sinks)
    w_r = jnp.zeros((D_MODEL, LANES), _F32)
    w_r = w_r.at[:, :N_GROUPS].set(w_group).at[:, N_GROUPS:N_GROUPS + N_EXPERTS].set(w_router)
    w_r_hi = w_r.astype(_BF16)
    w_r_lo = (w_r - w_r_hi.astype(_F32)).astype(_BF16)
    r_bias = jnp.zeros((1, LANES), _F32)
    r_bias = r_bias.at[0, :N_GROUPS].set(b_group).at[0, N_GROUPS:N_GROUPS + N_EXPERTS].set(b_router)
    h1, xp, logits = _mixer(
        bu, q, kd, vd, x2, conv_w.astype(_F32), tab, sink, row(out_norm_conv),
        row(out_norm_attn), w_out.astype(_BF16), row(norm_ffn), w_r_hi, w_r_lo, r_bias, seq)

    route, counts = _route(logits)

    cnt = counts[0, :N_EXPERTS].astype(jnp.int32)
    padded = (cnt + MOE_BLK - 1) // MOE_BLK * MOE_BLK
    pad_end = jnp.cumsum(padded)
    pad_start = pad_end - padded
    eid = route[:, 0:TOP_K].astype(jnp.int32)
    dest = pad_start[eid] + route[:, 4:4 + TOP_K].astype(jnp.int32)
    n_rows = n * TOP_K + N_EXPERTS * MOE_BLK
    nb = n_rows // MOE_BLK
    nb_used = (pad_end[-1] // MOE_BLK).astype(jnp.int32)
    block_e = jnp.searchsorted(pad_end, jnp.arange(nb, dtype=jnp.int32) * MOE_BLK, side="right")
    block_e = jnp.minimum(block_e, N_EXPERTS - 1).astype(jnp.int32)
    block_e = jnp.where(jnp.arange(nb) < nb_used, block_e, block_e[nb_used - 1])

    xs = _dispatch(xp, dest.reshape(n // TM_DISP, 1, TOP_K * TM_DISP),
                   pad_end.astype(jnp.int32), padded.astype(jnp.int32), n_rows)
    ys = _experts(xs, w_gate, w_up, w_down, block_e, nb_used.reshape(1))
    out = _combine(h1, route, dest.reshape(n // TM_COMB, 1, TOP_K * TM_COMB), ys,
                   p_i.reshape(n, PLE_DIM), row(norm_ple), w_ple_gate.astype(_BF16),
                   w_ple.astype(_BF16))
    return out.reshape(bsz, seq, D_MODEL)


def kernel(x, p, norm_mix, w_in, conv_w, q_norm, k_norm, sinks, out_norm_conv, out_norm_attn,
           w_out, norm_ffn, w_group, b_group, w_router, b_router, w_gate, w_up, w_down,
           norm_ple, w_ple_gate, w_ple):
    h = x
    for i in range(p.shape[0]):
        h = _layer(h, p[i], norm_mix[i], w_in[i], conv_w[i], q_norm[i], k_norm[i], sinks[i],
                   out_norm_conv[i], out_norm_attn[i], w_out[i], norm_ffn[i], w_group[i],
                   b_group[i], w_router[i], b_router[i], w_gate[i], w_up[i], w_down[i],
                   norm_ple[i], w_ple_gate[i], w_ple[i])
    return h
```

```python
import functools

import jax
import jax.numpy as jnp
from jax import lax
from jax.experimental import pallas as pl
from jax.experimental.pallas import tpu as pltpu

D_MODEL = 2048
D_CONV = 1024
N_HEADS = 16
N_KV_HEADS = 4
HEAD_DIM = 64
D_ATTN = N_HEADS * HEAD_DIM
D_KV = N_KV_HEADS * HEAD_DIM
D_IN = 3 * D_CONV + D_ATTN + 2 * D_KV
ATTN_BLOCK = 128
N_GROUPS = 4
EXPERTS_PER_GROUP = 8
N_EXPERTS = N_GROUPS * EXPERTS_PER_GROUP
TOP_K = 2
D_EXPERT = 512
PLE_DIM = 256
EPS = 1e-6
NEG_INF = -1e30

LANES = 128
HALF = D_MODEL // 2
KV_DUP = 2 * D_KV

TM_PROJ = 512
TM_MIX = 256
TM_ROUTE = 1024
TM_DISP = 256
TM_COMB = 512
MOE_BLK = 512
N_CHUNK = 8
VMEM_LIMIT = 56 * 1024 * 1024

_F32 = jnp.float32
_BF16 = jnp.bfloat16
_U32 = jnp.uint32
_HI_MASK = 0xFFFF0000


def _rms(x, gain):
    return x * lax.rsqrt(jnp.mean(x * x, axis=-1, keepdims=True) + EPS) * gain


def _dot(a, b):
    return jnp.dot(a, b, preferred_element_type=_F32)


def _pack_rows(x):
    h = x.shape[-1] // 2
    bits = lax.bitcast_convert_type(x.astype(_BF16).astype(_F32), _U32)
    return (bits[:, h:] & _U32(_HI_MASK)) | (bits[:, :h] >> 16)


def _unpack_rows(w):
    lo = lax.bitcast_convert_type(w << 16, _F32)
    hi = lax.bitcast_convert_type(w & _U32(_HI_MASK), _F32)
    return lo, hi


def _inproj_kernel(x_ref, g_ref, w_ref, qg_ref, kg_ref, bu_ref, q_ref, kd_ref, vd_ref):
    tm = x_ref.shape[0]
    xn = _rms(x_ref[...], g_ref[...]).astype(_BF16)
    ch = 512
    for j in range(D_CONV // ch):
        b = _dot(xn, w_ref[:, j * ch:(j + 1) * ch])
        c = _dot(xn, w_ref[:, D_CONV + j * ch:D_CONV + (j + 1) * ch])
        hc = _dot(xn, w_ref[:, 2 * D_CONV + j * ch:2 * D_CONV + (j + 1) * ch])
        bu_ref[:, j * ch:(j + 1) * ch] = b.astype(_BF16)
        bu_ref[:, D_CONV + j * ch:D_CONV + (j + 1) * ch] = (c * hc).astype(_BF16)

    lo = lax.broadcasted_iota(jnp.int32, (tm, LANES), 1) < HEAD_DIM
    q = _dot(xn, w_ref[:, 3 * D_CONV:3 * D_CONV + D_ATTN])
    scale = HEAD_DIM ** -0.5
    for pr in range(D_ATTN // LANES):
        qp = q[:, pr * LANES:(pr + 1) * LANES]
        sq = qp * qp
        s_lo = jnp.sum(jnp.where(lo, sq, 0.0), axis=-1, keepdims=True)
        s_hi = jnp.sum(jnp.where(lo, 0.0, sq), axis=-1, keepdims=True)
        ms = jnp.where(lo, s_lo, s_hi) * (1.0 / HEAD_DIM)
        qn = qp * lax.rsqrt(ms + EPS) * qg_ref[...] * scale
        q_ref[:, pr * LANES:(pr + 1) * LANES] = qn.astype(_BF16)

    kv = _dot(xn, w_ref[:, 3 * D_CONV + D_ATTN:D_IN])
    for pr in range(2 * D_KV // LANES):
        pair = kv[:, pr * LANES:(pr + 1) * LANES]
        swapped = pltpu.roll(pair, HEAD_DIM, 1)
        for half in range(2):
            dup = jnp.where(lo, pair, swapped) if half == 0 else jnp.where(lo, swapped, pair)
            head = 2 * pr + half
            if head < N_KV_HEADS:
                ms = jnp.sum(dup * dup, axis=-1, keepdims=True) * (1.0 / LANES)
                dup = dup * lax.rsqrt(ms + EPS) * kg_ref[...]
                kd_ref[:, head * LANES:(head + 1) * LANES] = dup.astype(_BF16)
            else:
                head -= N_KV_HEADS
                vd_ref[:, head * LANES:(head + 1) * LANES] = dup.astype(_BF16)


def _inproj(x2, g, w_in, qg2, kg2):
    n = x2.shape[0]
    tm = TM_PROJ
    const = lambda i: (0, 0)
    return pl.pallas_call(
        _inproj_kernel,
        grid=(n // tm,),
        in_specs=[
            pl.BlockSpec((tm, D_MODEL), lambda i: (i, 0)),
            pl.BlockSpec((1, D_MODEL), const),
            pl.BlockSpec((D_MODEL, D_IN), const, pipeline_mode=pl.Buffered(1)),
            pl.BlockSpec((1, LANES), const),
            pl.BlockSpec((1, LANES), const),
        ],
        out_specs=[
            pl.BlockSpec((tm, 2 * D_CONV), lambda i: (i, 0)),
            pl.BlockSpec((tm, D_ATTN), lambda i: (i, 0)),
            pl.BlockSpec((tm, KV_DUP), lambda i: (i, 0)),
            pl.BlockSpec((tm, KV_DUP), lambda i: (i, 0)),
        ],
        out_shape=[
            jax.ShapeDtypeStruct((n, 2 * D_CONV), _BF16),
            jax.ShapeDtypeStruct((n, D_ATTN), _BF16),
            jax.ShapeDtypeStruct((n, KV_DUP), _BF16),
            jax.ShapeDtypeStruct((n, KV_DUP), _BF16),
        ],
        compiler_params=pltpu.CompilerParams(
            dimension_semantics=("parallel",), vmem_limit_bytes=VMEM_LIMIT),
        name="inproj",
    )(x2, g, w_in, qg2, kg2)


def _mixer_kernel(tiles_per_seq,
                  bu_ref, uh_ref, q_ref, kd_ref, vd_ref, kdh_ref, vdh_ref, x_ref,
                  cw_ref, tab_ref, sink_ref, onc_ref, ona_ref, wout_ref, nffn_ref,
                  wr_ref, rb_ref,
                  h1_ref, xp_ref, lg_ref, ya_ref):
    tm = x_ref.shape[0]
    blk = ATTN_BLOCK
    is_first = (pl.program_id(0) % tiles_per_seq) == 0

    lo = lax.broadcasted_iota(jnp.int32, (blk, LANES), 1) < HEAD_DIM
    no_prev = jnp.logical_and(
        is_first, lax.broadcasted_iota(jnp.int32, (4 * blk, 2 * blk), 1) < blk)
    zero = jnp.zeros((blk, LANES), _BF16)
    for j in range(tm // blk):
        rows = slice(j * blk, (j + 1) * blk)
        for g in range(N_KV_HEADS):
            cols = slice(g * LANES, (g + 1) * LANES)
            qa = q_ref[rows, 2 * g * LANES:(2 * g + 1) * LANES]
            qb = q_ref[rows, (2 * g + 1) * LANES:(2 * g + 2) * LANES]
            lhs = jnp.concatenate(
                [jnp.where(lo, qa, zero), jnp.where(lo, zero, qa),
                 jnp.where(lo, qb, zero), jnp.where(lo, zero, qb)], axis=0)
            if j == 0:
                keys = jnp.concatenate([kdh_ref[:, cols], kd_ref[0:blk, cols]], axis=0)
                vals = jnp.concatenate([vdh_ref[:, cols], vd_ref[0:blk, cols]], axis=0)
            else:
                keys = kd_ref[(j - 1) * blk:(j + 1) * blk, cols]
                vals = vd_ref[(j - 1) * blk:(j + 1) * blk, cols]
            s = lax.dot_general(lhs, keys, (((1,), (1,)), ((), ())),
                                preferred_element_type=_F32)
            s = s + tab_ref[g]
            if j == 0:
                s = jnp.where(no_prev, NEG_INF, s)
            sink = sink_ref[g]
            m = jnp.maximum(jnp.max(s, axis=-1, keepdims=True), sink)
            p = jnp.exp(s - m)
            denom = jnp.sum(p, axis=-1, keepdims=True) + jnp.exp(sink - m)
            o = _dot(p.astype(_BF16), vals) / denom
            ya_ref[rows, 2 * g * LANES:(2 * g + 1) * LANES] = jnp.where(
                lo, o[0:blk], o[blk:2 * blk])
            ya_ref[rows, (2 * g + 1) * LANES:(2 * g + 2) * LANES] = jnp.where(
                lo, o[2 * blk:3 * blk], o[3 * blk:4 * blk])

    b = bu_ref[:, 0:D_CONV].astype(_F32)
    u = bu_ref[:, D_CONV:2 * D_CONV].astype(_F32)
    uh = jnp.where(is_first, 0.0, uh_ref[...].astype(_F32))
    row = lax.broadcasted_iota(jnp.int32, (tm, D_CONV), 0)
    u1 = jnp.where(row == 0, uh[15:16], pltpu.roll(u, 1, 0))
    u2 = jnp.where(row == 0, uh[14:15],
                   jnp.where(row == 1, uh[15:16], pltpu.roll(u, 2, 0)))
    z = cw_ref[2:3, :] * u + cw_ref[1:2, :] * u1 + cw_ref[0:1, :] * u2
    yc = _rms(b * z, onc_ref[...])
    ya = _rms(ya_ref[...], ona_ref[...])
    mixed = jnp.concatenate([yc, ya], axis=-1).astype(_BF16)
    h1 = x_ref[...] + _dot(mixed, wout_ref[...])
    h1_ref[...] = h1

    xn = _rms(h1, nffn_ref[...])
    xp_ref[...] = _pack_rows(xn)
    hi = xn.astype(_BF16)
    rem = (xn - hi.astype(_F32)).astype(_BF16)
    r = _dot(jnp.concatenate([hi, rem], axis=0), wr_ref[...])
    lg_ref[...] = (r[0:tm, 0:LANES] + r[0:tm, LANES:2 * LANES] + r[tm:2 * tm, 0:LANES]
                   + rb_ref[...])


def _mixer(bu, q, kd, vd, x2, cw, tab, sink, onc, ona, w_out, nffn, wr, rb, seq):
    n = x2.shape[0]
    tm = TM_MIX
    const2 = lambda i: (0, 0)
    const3 = lambda i: (0, 0, 0)
    prev_blk = lambda i: (jnp.maximum(i * (tm // ATTN_BLOCK) - 1, 0), 0)
    return pl.pallas_call(
        functools.partial(_mixer_kernel, seq // tm),
        grid=(n // tm,),
        in_specs=[
            pl.BlockSpec((tm, 2 * D_CONV), lambda i: (i, 0)),
            pl.BlockSpec((16, D_CONV), lambda i: (jnp.maximum(i * (tm // 16) - 1, 0), 1)),
            pl.BlockSpec((tm, D_ATTN), lambda i: (i, 0)),
            pl.BlockSpec((tm, KV_DUP), lambda i: (i, 0)),
            pl.BlockSpec((tm, KV_DUP), lambda i: (i, 0)),
            pl.BlockSpec((ATTN_BLOCK, KV_DUP), prev_blk),
            pl.BlockSpec((ATTN_BLOCK, KV_DUP), prev_blk),
            pl.BlockSpec((tm, D_MODEL), lambda i: (i, 0)),
            pl.BlockSpec((3, D_CONV), const2),
            pl.BlockSpec((N_KV_HEADS, 4 * ATTN_BLOCK, 2 * ATTN_BLOCK), const3,
                         pipeline_mode=pl.Buffered(1)),
            pl.BlockSpec((N_KV_HEADS, 4 * ATTN_BLOCK, 1), const3, pipeline_mode=pl.Buffered(1)),
            pl.BlockSpec((1, D_CONV), const2),
            pl.BlockSpec((1, D_ATTN), const2),
            pl.BlockSpec((D_MODEL, D_MODEL), const2, pipeline_mode=pl.Buffered(1)),
            pl.BlockSpec((1, D_MODEL), const2),
            pl.BlockSpec((D_MODEL, 2 * LANES), const2),
            pl.BlockSpec((1, LANES), const2),
        ],
        out_specs=[
            pl.BlockSpec((tm, D_MODEL), lambda i: (i, 0)),
            pl.BlockSpec((tm, HALF), lambda i: (i, 0)),
            pl.BlockSpec((tm, LANES), lambda i: (i, 0)),
        ],
        out_shape=[
            jax.ShapeDtypeStruct((n, D_MODEL), _F32),
            jax.ShapeDtypeStruct((n, HALF), _U32),
            jax.ShapeDtypeStruct((n, LANES), _F32),
        ],
        scratch_shapes=[pltpu.VMEM((tm, D_ATTN), _F32)],
        compiler_params=pltpu.CompilerParams(
            dimension_semantics=("parallel",), vmem_limit_bytes=VMEM_LIMIT),
        name="mixer",
    )(bu, bu, q, kd, vd, kd, vd, x2, cw, tab, sink, onc, ona, w_out, nffn, wr, rb)


def _route_kernel(lg_ref, route_ref, cnt_ref, run_ref):
    t = lg_ref.shape[0]

    @pl.when(pl.program_id(0) == 0)
    def _():
        run_ref[...] = jnp.zeros_like(run_ref)

    lg = lg_ref[...]
    lane = lax.broadcasted_iota(jnp.int32, (t, LANES), 1)
    ninf = -jnp.inf

    lane_f = lane.astype(_F32)

    def first_max(v):
        m = jnp.max(v, axis=-1, keepdims=True)
        idx = jnp.min(jnp.where(v == m, lane_f, float(LANES)), axis=-1, keepdims=True)
        return m, idx.astype(jnp.int32)

    is_g = lane < N_GROUPS
    gmax, gidx = first_max(jnp.where(is_g, lg, ninf))
    gsum = jnp.sum(jnp.where(is_g, jnp.exp(jnp.where(is_g, lg, gmax) - gmax), 0.0),
                   axis=-1, keepdims=True)
    g_w = 1.0 / gsum
    base = N_GROUPS + EXPERTS_PER_GROUP * gidx
    in_grp = jnp.logical_and(lane >= base, lane < base + EXPERTS_PER_GROUP)
    el = jnp.where(in_grp, lg, ninf)
    m1, i1 = first_max(el)
    m2, i2 = first_max(jnp.where(lane == i1, ninf, el))
    e21 = jnp.exp(m2 - m1)
    w1 = g_w / (1.0 + e21)
    w2 = g_w * e21 / (1.0 + e21)
    e1 = i1 - N_GROUPS
    e2 = i2 - N_GROUPS

    oh1 = lane == e1
    oh2 = lane == e2
    earlier = (lax.broadcasted_iota(jnp.int32, (t, t), 0)
               > lax.broadcasted_iota(jnp.int32, (t, t), 1)).astype(_BF16)
    c1 = _dot(earlier, oh1.astype(_BF16))
    c2 = _dot(earlier, oh2.astype(_BF16))
    tot1 = jnp.sum(oh1.astype(_F32), axis=0, keepdims=True)
    tot2 = jnp.sum(oh2.astype(_F32), axis=0, keepdims=True)
    run = run_ref[0:1, :]
    r1 = jnp.sum(jnp.where(oh1, run + c1, 0.0), axis=-1, keepdims=True)
    r2 = jnp.sum(jnp.where(oh2, run + tot1 + c2, 0.0), axis=-1, keepdims=True)
    run_ref[...] = jnp.broadcast_to(run + tot1 + tot2, run_ref.shape)
    cnt_ref[...] = run_ref[...]

    out = jnp.where(lane == 0, e1.astype(_F32), 0.0)
    out = jnp.where(lane == 1, e2.astype(_F32), out)
    out = jnp.where(lane == 2, w1, out)
    out = jnp.where(lane == 3, w2, out)
    out = jnp.where(lane == 4, r1, out)
    out = jnp.where(lane == 5, r2, out)
    route_ref[...] = out


def _route(logits):
    n = logits.shape[0]
    t = TM_ROUTE
    return pl.pallas_call(
        _route_kernel,
        grid=(n // t,),
        in_specs=[pl.BlockSpec((t, LANES), lambda i: (i, 0))],
        out_specs=[pl.BlockSpec((t, LANES), lambda i: (i, 0)),
                   pl.BlockSpec((8, LANES), lambda i: (0, 0))],
        out_shape=[jax.ShapeDtypeStruct((n, LANES), _F32),
                   jax.ShapeDtypeStruct((8, LANES), _F32)],
        scratch_shapes=[pltpu.VMEM((8, LANES), _F32)],
        compiler_params=pltpu.CompilerParams(dimension_semantics=("arbitrary",)),
        name="route",
    )(logits)


def _dispatch_kernel(pad_end_ref, padded_ref, xp_ref, dest_ref, xs_ref, zero_ref, sem, zsem):
    td = xp_ref.shape[0]
    nb = xs_ref.shape[0] // MOE_BLK
    nb_used = pad_end_ref[N_EXPERTS - 1] // MOE_BLK

    def zero_copy(start):
        start = pl.multiple_of(start, MOE_BLK)
        return pltpu.make_async_copy(zero_ref, xs_ref.at[pl.ds(start, MOE_BLK)], zsem)

    def zero_blocks(fn):
        for e in range(N_EXPERTS):
            @pl.when(padded_ref[e] > 0)
            def _():
                fn(zero_copy(pad_end_ref[e] - MOE_BLK))
        for b in range(nb - N_EXPERTS, nb):
            @pl.when(b >= nb_used)
            def _():
                fn(zero_copy(b * MOE_BLK))

    @pl.when(pl.program_id(0) == 0)
    def _():
        zero_ref[...] = jnp.zeros_like(zero_ref)
        zero_blocks(lambda c: c.start())
        zero_blocks(lambda c: c.wait())

    def row_copy(r, k):
        d = dest_ref[0, 0, TOP_K * r + k]
        return pltpu.make_async_copy(xp_ref.at[pl.ds(r, 1)], xs_ref.at[pl.ds(d, 1)], sem)

    def issue(r, c):
        for k in range(TOP_K):
            row_copy(r, k).start()
        return c

    lax.fori_loop(0, td, issue, 0, unroll=8)
    for k in range(TOP_K):
        pltpu.make_async_copy(xp_ref, xs_ref.at[pl.ds(0, td)], sem).wait()


def _dispatch(xp, dest3, pad_end, padded, n_rows):
    n = xp.shape[0]
    td = TM_DISP
    return pl.pallas_call(
        _dispatch_kernel,
        grid_spec=pltpu.PrefetchScalarGridSpec(
            num_scalar_prefetch=2,
            grid=(n // td,),
            in_specs=[
                pl.BlockSpec((td, HALF), lambda i, pe, pd: (i, 0)),
                pl.BlockSpec((1, 1, TOP_K * td), lambda i, pe, pd: (i, 0, 0),
                             memory_space=pltpu.SMEM),
            ],
            out_specs=pl.BlockSpec(memory_space=pl.ANY),
            scratch_shapes=[pltpu.VMEM((MOE_BLK, HALF), _U32),
                            pltpu.SemaphoreType.DMA, pltpu.SemaphoreType.DMA],
        ),
        out_shape=jax.ShapeDtypeStruct((n_rows, HALF), _U32),
        compiler_params=pltpu.CompilerParams(
            dimension_semantics=("arbitrary",)),
        name="dispatch",
    )(pad_end, padded, xp, dest3)


def _experts_kernel(be_ref, nb_ref, xs_ref, wg_ref, wu_ref, wd_ref, ys_ref,
                    wgb_ref, wub_ref, wdb_ref):
    i = pl.program_id(0)
    valid = i < nb_ref[0]
    changed = jnp.logical_or(i == 0, be_ref[i] != be_ref[jnp.maximum(i - 1, 0)])

    @pl.when(jnp.logical_and(valid, changed))
    def _():
        wgb_ref[...] = wg_ref[0].astype(_BF16)
        wub_ref[...] = wu_ref[0].astype(_BF16)
        wdb_ref[...] = wd_ref[0].astype(_BF16)

    @pl.when(valid)
    def _():
        lo, hi = _unpack_rows(xs_ref[...])
        lo = lo.astype(_BF16)
        hi = hi.astype(_BF16)
        g = _dot(lo, wgb_ref[0:HALF, :]) + _dot(hi, wgb_ref[HALF:D_MODEL, :])
        u = _dot(lo, wub_ref[0:HALF, :]) + _dot(hi, wub_ref[HALF:D_MODEL, :])
        hidden = (g * jax.nn.sigmoid(g) * u).astype(_BF16)
        ys_ref[...] = _pack_rows(_dot(hidden, wdb_ref[...]))

    @pl.when(jnp.logical_not(valid))
    def _():
        ys_ref[...] = jnp.zeros_like(ys_ref)


def _experts(xs, w_gate, w_up, w_down, block_e, nb_used):
    n_rows = xs.shape[0]
    nb = n_rows // MOE_BLK
    row_blk = lambda i, be, nbu: (jnp.minimum(i, nbu[0] - 1), 0)
    w_blk = lambda i, be, nbu: (be[i], 0, 0)
    return pl.pallas_call(
        _experts_kernel,
        grid_spec=pltpu.PrefetchScalarGridSpec(
            num_scalar_prefetch=2,
            grid=(nb,),
            in_specs=[
                pl.BlockSpec((MOE_BLK, HALF), row_blk),
                pl.BlockSpec((1, D_MODEL, D_EXPERT), w_blk),
                pl.BlockSpec((1, D_MODEL, D_EXPERT), w_blk),
                pl.BlockSpec((1, D_EXPERT, D_MODEL), w_blk),
            ],
            out_specs=pl.BlockSpec((MOE_BLK, HALF), lambda i, be, nbu: (i, 0)),
            scratch_shapes=[pltpu.VMEM((D_MODEL, D_EXPERT), _BF16),
                            pltpu.VMEM((D_MODEL, D_EXPERT), _BF16),
                            pltpu.VMEM((D_EXPERT, D_MODEL), _BF16)],
        ),
        out_shape=jax.ShapeDtypeStruct((n_rows, HALF), _U32),
        compiler_params=pltpu.CompilerParams(
            dimension_semantics=("arbitrary",), vmem_limit_bytes=VMEM_LIMIT),
        name="experts",
    )(block_e, nb_used, xs, w_gate, w_up, w_down)


def _combine_kernel(h1_ref, route_ref, dcur_ref, dnext_ref, ys_ref, p_ref, nple_ref,
                    wgate_ref, wple_ref, out_ref, ybuf_ref, h2_ref, xn_ref, pb_ref, sem):
    tk = h1_ref.shape[0]
    cw = D_MODEL // N_CHUNK
    per = TOP_K * tk // N_CHUNK
    i = pl.program_id(0)
    n_steps = pl.num_programs(0)
    slot = i % 2

    def row_copy(d, s, k, r):
        return pltpu.make_async_copy(ys_ref.at[pl.ds(d, 1)],
                                     ybuf_ref.at[s, k, pl.ds(r, 1)], sem.at[s])

    def wait_slot(s):
        for k in range(TOP_K):
            pltpu.make_async_copy(ys_ref.at[pl.ds(0, tk)], ybuf_ref.at[s, k], sem.at[s]).wait()

    @pl.when(i == 0)
    def _():
        def body(r, c):
            for k in range(TOP_K):
                row_copy(dcur_ref[0, 0, TOP_K * r + k], 0, k, r).start()
            return c
        lax.fori_loop(0, tk, body, 0, unroll=8)

    wait_slot(slot)
    y1 = jnp.concatenate(_unpack_rows(ybuf_ref[slot, 0]), axis=-1)
    y2 = jnp.concatenate(_unpack_rows(ybuf_ref[slot, 1]), axis=-1)
    route = route_ref[...]
    h2 = h1_ref[...] + route[:, 2:3] * y1 + route[:, 3:4] * y2
    xn = _rms(h2, nple_ref[...]).astype(_BF16)
    pb = p_ref[...].astype(_BF16)

    def issue_batch(c):
        for j in range(per):
            d = dnext_ref[0, 0, c * per + j]
            row_copy(d, 1 - slot, j % TOP_K, c * (per // TOP_K) + j // TOP_K).start()

    issue_batch(0)
    h2_ref[...] = h2
    xn_ref[...] = xn
    pb_ref[...] = pb
    for c in range(N_CHUNK):
        if c > 0:
            issue_batch(c)
        cols = slice(c * cw, (c + 1) * cw)
        gate = jax.nn.sigmoid(_dot(xn_ref[...], wgate_ref[:, cols]))
        emb = _dot(pb_ref[...], wple_ref[:, cols])
        out_ref[:, cols] = h2_ref[:, cols] + gate * emb

    @pl.when(i == n_steps - 1)
    def _():
        wait_slot(1 - slot)


def _combine(h1, route, dest3, ys, p2, nple, w_gate, w_ple):
    n = h1.shape[0]
    tk = TM_COMB
    steps = n // tk
    const2 = lambda i: (0, 0)
    return pl.pallas_call(
        _combine_kernel,
        grid=(steps,),
        in_specs=[
            pl.BlockSpec((tk, D_MODEL), lambda i: (i, 0)),
            pl.BlockSpec((tk, LANES), lambda i: (i, 0)),
            pl.BlockSpec((1, 1, TOP_K * tk), lambda i: (i, 0, 0), memory_space=pltpu.SMEM),
            pl.BlockSpec((1, 1, TOP_K * tk), lambda i: (jnp.minimum(i + 1, steps - 1), 0, 0),
                         memory_space=pltpu.SMEM),
            pl.BlockSpec(memory_space=pl.ANY),
            pl.BlockSpec((tk, PLE_DIM), lambda i: (i, 0)),
            pl.BlockSpec((1, D_MODEL), const2),
            pl.BlockSpec((D_MODEL, D_MODEL), const2, pipeline_mode=pl.Buffered(1)),
            pl.BlockSpec((PLE_DIM, D_MODEL), const2, pipeline_mode=pl.Buffered(1)),
        ],
        out_specs=pl.BlockSpec((tk, D_MODEL), lambda i: (i, 0)),
        out_shape=jax.ShapeDtypeStruct((n, D_MODEL), _F32),
        scratch_shapes=[pltpu.VMEM((2, TOP_K, tk, HALF), _U32),
                        pltpu.VMEM((tk, D_MODEL), _F32),
                        pltpu.VMEM((tk, D_MODEL), _BF16),
                        pltpu.VMEM((tk, PLE_DIM), _BF16),
                        pltpu.SemaphoreType.DMA((2,))],
        compiler_params=pltpu.CompilerParams(
            dimension_semantics=("arbitrary",), vmem_limit_bytes=VMEM_LIMIT),
        name="combine",
    )(h1, route, dest3, dest3, ys, p2, nple, w_gate, w_ple)


def _attention_tables(sinks):
    slopes = 2.0 ** (-8.0 * jnp.arange(1, N_HEADS + 1, dtype=_F32) / N_HEADS)
    qi = jnp.arange(ATTN_BLOCK)[:, None]
    sj = jnp.arange(2 * ATTN_BLOCK)[None, :]
    dist = ATTN_BLOCK + qi - sj
    in_window = (dist >= 0) & (dist < ATTN_BLOCK)
    alibi = -slopes[:, None, None] * dist.astype(_F32)[None]
    tab = jnp.where(in_window[None], alibi, NEG_INF)
    tab = tab.reshape(N_KV_HEADS, 4 * ATTN_BLOCK, 2 * ATTN_BLOCK)
    sink = jnp.repeat(sinks.astype(_F32), ATTN_BLOCK).reshape(N_KV_HEADS, 4 * ATTN_BLOCK, 1)
    return tab, sink


def _layer(h, p_i, norm_mix, w_in, conv_w, q_norm, k_norm, sinks, out_norm_conv,
           out_norm_attn, w_out, norm_ffn, w_group, b_group, w_router, b_router,
           w_gate, w_up, w_down, norm_ple, w_ple_gate, w_ple):
    bsz, seq, _ = h.shape
    n = bsz * seq
    x2 = h.reshape(n, D_MODEL)
    row = lambda v: v.reshape(1, -1).astype(_F32)

    bu, q, kd, vd = _inproj(
        x2, row(norm_mix), w_in.astype(_BF16),
        row(jnp.tile(q_norm, 2)), row(jnp.tile(k_norm, 2)))

    tab, sink = _attention_tables(sinks)
    w_r = jnp.zeros((D_MODEL, LANES), _F32)
    w_r = w_r.at[:, :N_GROUPS].set(w_group).at[:, N_GROUPS:N_GROUPS + N_EXPERTS].set(w_router)
    w_r_hi = w_r.astype(_BF16)
    w_r_lo = (w_r - w_r_hi.astype(_F32)).astype(_BF16)
    w_r_cat = jnp.concatenate([w_r_hi, w_r_lo], axis=1)
    r_bias = jnp.zeros((1, LANES), _F32)
    r_bias = r_bias.at[0, :N_GROUPS].set(b_group).at[0, N_GROUPS:N_GROUPS + N_EXPERTS].set(b_router)
    h1, xp, logits = _mixer(
        bu, q, kd, vd, x2, conv_w.astype(_F32), tab, sink, row(out_norm_conv),
        row(out_norm_attn), w_out.astype(_BF16), row(norm_ffn), w_r_cat, r_bias, seq)

    route, counts = _route(logits)

    cnt = counts[0, :N_EXPERTS].astype(jnp.int32)
    padded = (cnt + MOE_BLK - 1) // MOE_BLK * MOE_BLK
    pad_end = jnp.cumsum(padded)
    pad_start = pad_end - padded
    eid = route[:, 0:TOP_K].astype(jnp.int32)
    onehot = eid[..., None] == jnp.arange(N_EXPERTS, dtype=jnp.int32)
    dest = (jnp.sum(jnp.where(onehot, pad_start, 0), axis=-1)
            + route[:, 4:4 + TOP_K].astype(jnp.int32))
    n_rows = n * TOP_K + N_EXPERTS * MOE_BLK
    nb = n_rows // MOE_BLK
    nb_used = (pad_end[-1] // MOE_BLK).astype(jnp.int32)
    blk_start = jnp.arange(nb, dtype=jnp.int32) * MOE_BLK
    block_e = jnp.sum((blk_start[:, None] >= pad_end[None, :]).astype(jnp.int32), axis=1)
    block_e = jnp.minimum(block_e, N_EXPERTS - 1)
    block_e = jnp.where(jnp.arange(nb) < nb_used, block_e, block_e[nb_used - 1])

    xs = _dispatch(xp, dest.reshape(n // TM_DISP, 1, TOP_K * TM_DISP),
                   pad_end.astype(jnp.int32), padded.astype(jnp.int32), n_rows)
    ys = _experts(xs, w_gate, w_up, w_down, block_e, nb_used.reshape(1))
    out = _combine(h1, route, dest.reshape(n // TM_COMB, 1, TOP_K * TM_COMB), ys,
                   p_i.reshape(n, PLE_DIM), row(norm_ple), w_ple_gate.astype(_BF16),
                   w_ple.astype(_BF16))
    return out.reshape(bsz, seq, D_MODEL)


def kernel(x, p, norm_mix, w_in, conv_w, q_norm, k_norm, sinks, out_norm_conv, out_norm_attn,
           w_out, norm_ffn, w_group, b_group, w_router, b_router, w_gate, w_up, w_down,
           norm_ple, w_ple_gate, w_ple):
    h = x
    for i in range(p.shape[0]):
        h = _layer(h, p[i], norm_mix[i], w_in[i], conv_w[i], q_norm[i], k_norm[i], sinks[i],
                   out_norm_conv[i], out_norm_attn[i], w_out[i], norm_ffn[i], w_group[i],
                   b_group[i], w_router[i], b_router[i], w_gate[i], w_up[i], w_down[i],
                   norm_ple[i], w_ple_gate[i], w_ple[i])
    return h
```

```python
import functools

import jax
import jax.numpy as jnp
from jax import lax
from jax.experimental import pallas as pl
from jax.experimental.pallas import tpu as pltpu

D_MODEL = 2048
D_CONV = 1024
N_HEADS = 16
N_KV_HEADS = 4
HEAD_DIM = 64
D_ATTN = N_HEADS * HEAD_DIM
D_KV = N_KV_HEADS * HEAD_DIM
D_IN = 3 * D_CONV + D_ATTN + 2 * D_KV
ATTN_BLOCK = 128
N_GROUPS = 4
EXPERTS_PER_GROUP = 8
N_EXPERTS = N_GROUPS * EXPERTS_PER_GROUP
TOP_K = 2
D_EXPERT = 512
PLE_DIM = 256
EPS = 1e-6
NEG_INF = -1e30

LANES = 128
HALF = D_MODEL // 2
KV_DUP = 2 * D_KV

TM_PROJ = 512
TM_MIX = 256
TM_ROUTE = 1024
TM_DISP = 256
TM_COMB = 512
MOE_BLK = 512
N_CHUNK = 8
VMEM_LIMIT = 56 * 1024 * 1024

_F32 = jnp.float32
_BF16 = jnp.bfloat16
_U32 = jnp.uint32
_HI_MASK = 0xFFFF0000


def _rms(x, gain):
    return x * lax.rsqrt(jnp.mean(x * x, axis=-1, keepdims=True) + EPS) * gain


def _dot(a, b):
    return jnp.dot(a, b, preferred_element_type=_F32)


def _pack_rows(x):
    h = x.shape[-1] // 2
    bits = lax.bitcast_convert_type(x.astype(_BF16).astype(_F32), _U32)
    return (bits[:, h:] & _U32(_HI_MASK)) | (bits[:, :h] >> 16)


def _unpack_rows(w):
    lo = lax.bitcast_convert_type(w << 16, _F32)
    hi = lax.bitcast_convert_type(w & _U32(_HI_MASK), _F32)
    return lo, hi


def _inproj_kernel(x_ref, g_ref, w_ref, qg_ref, kg_ref, bu_ref, q_ref, kd_ref, vd_ref):
    tm = x_ref.shape[0]
    xn = _rms(x_ref[...], g_ref[...]).astype(_BF16)
    ch = 512
    for j in range(D_CONV // ch):
        b = _dot(xn, w_ref[:, j * ch:(j + 1) * ch])
        c = _dot(xn, w_ref[:, D_CONV + j * ch:D_CONV + (j + 1) * ch])
        hc = _dot(xn, w_ref[:, 2 * D_CONV + j * ch:2 * D_CONV + (j + 1) * ch])
        bu_ref[:, j * ch:(j + 1) * ch] = b.astype(_BF16)
        bu_ref[:, D_CONV + j * ch:D_CONV + (j + 1) * ch] = (c * hc).astype(_BF16)

    lo = lax.broadcasted_iota(jnp.int32, (tm, LANES), 1) < HEAD_DIM
    q = _dot(xn, w_ref[:, 3 * D_CONV:3 * D_CONV + D_ATTN])
    scale = HEAD_DIM ** -0.5
    for pr in range(D_ATTN // LANES):
        qp = q[:, pr * LANES:(pr + 1) * LANES]
        sq = qp * qp
        s_lo = jnp.sum(jnp.where(lo, sq, 0.0), axis=-1, keepdims=True)
        s_hi = jnp.sum(jnp.where(lo, 0.0, sq), axis=-1, keepdims=True)
        ms = jnp.where(lo, s_lo, s_hi) * (1.0 / HEAD_DIM)
        qn = qp * lax.rsqrt(ms + EPS) * qg_ref[...] * scale
        q_ref[:, pr * LANES:(pr + 1) * LANES] = qn.astype(_BF16)

    kv = _dot(xn, w_ref[:, 3 * D_CONV + D_ATTN:D_IN])
    for pr in range(2 * D_KV // LANES):
        pair = kv[:, pr * LANES:(pr + 1) * LANES]
        swapped = pltpu.roll(pair, HEAD_DIM, 1)
        for half in range(2):
            dup = jnp.where(lo, pair, swapped) if half == 0 else jnp.where(lo, swapped, pair)
            head = 2 * pr + half
            if head < N_KV_HEADS:
                ms = jnp.sum(dup * dup, axis=-1, keepdims=True) * (1.0 / LANES)
                dup = dup * lax.rsqrt(ms + EPS) * kg_ref[...]
                kd_ref[:, head * LANES:(head + 1) * LANES] = dup.astype(_BF16)
            else:
                head -= N_KV_HEADS
                vd_ref[:, head * LANES:(head + 1) * LANES] = dup.astype(_BF16)


def _inproj(x2, g, w_in, qg2, kg2):
    n = x2.shape[0]
    tm = TM_PROJ
    const = lambda i: (0, 0)
    return pl.pallas_call(
        _inproj_kernel,
        grid=(n // tm,),
        in_specs=[
            pl.BlockSpec((tm, D_MODEL), lambda i: (i, 0)),
            pl.BlockSpec((1, D_MODEL), const),
            pl.BlockSpec((D_MODEL, D_IN), const, pipeline_mode=pl.Buffered(1)),
            pl.BlockSpec((1, LANES), const),
            pl.BlockSpec((1, LANES), const),
        ],
        out_specs=[
            pl.BlockSpec((tm, 2 * D_CONV), lambda i: (i, 0)),
            pl.BlockSpec((tm, D_ATTN), lambda i: (i, 0)),
            pl.BlockSpec((tm, KV_DUP), lambda i: (i, 0)),
            pl.BlockSpec((tm, KV_DUP), lambda i: (i, 0)),
        ],
        out_shape=[
            jax.ShapeDtypeStruct((n, 2 * D_CONV), _BF16),
            jax.ShapeDtypeStruct((n, D_ATTN), _BF16),
            jax.ShapeDtypeStruct((n, KV_DUP), _BF16),
            jax.ShapeDtypeStruct((n, KV_DUP), _BF16),
        ],
        compiler_params=pltpu.CompilerParams(
            dimension_semantics=("parallel",), vmem_limit_bytes=VMEM_LIMIT),
        name="inproj",
    )(x2, g, w_in, qg2, kg2)


def _mixer_kernel(tiles_per_seq,
                  bu_ref, uh_ref, q_ref, kd_ref, vd_ref, kdh_ref, vdh_ref, x_ref,
                  cw_ref, tab_ref, sink_ref, onc_ref, ona_ref, wout_ref, nffn_ref,
                  wr_ref, rb_ref,
                  h1_ref, xp_ref, lg_ref, ya_ref):
    tm = x_ref.shape[0]
    blk = ATTN_BLOCK
    is_first = (pl.program_id(0) % tiles_per_seq) == 0

    lo = lax.broadcasted_iota(jnp.int32, (blk, LANES), 1) < HEAD_DIM
    qi = lax.broadcasted_iota(jnp.int32, (4 * blk, blk), 0) % blk
    from_prev = lax.broadcasted_iota(jnp.int32, (4 * blk, blk), 1) > qi
    no_prev = jnp.logical_and(is_first, from_prev)
    zero = jnp.zeros((blk, LANES), _BF16)
    zero_p = jnp.zeros((4 * blk, blk), _BF16)
    for j in range(tm // blk):
        rows = slice(j * blk, (j + 1) * blk)
        for g in range(N_KV_HEADS):
            cols = slice(g * LANES, (g + 1) * LANES)
            qa = q_ref[rows, 2 * g * LANES:(2 * g + 1) * LANES]
            qb = q_ref[rows, (2 * g + 1) * LANES:(2 * g + 2) * LANES]
            lhs = jnp.concatenate(
                [jnp.where(lo, qa, zero), jnp.where(lo, zero, qa),
                 jnp.where(lo, qb, zero), jnp.where(lo, zero, qb)], axis=0)
            if j == 0:
                keys = jnp.concatenate([kdh_ref[:, cols], kd_ref[0:blk, cols]], axis=0)
                vals = jnp.concatenate([vdh_ref[:, cols], vd_ref[0:blk, cols]], axis=0)
            else:
                keys = kd_ref[(j - 1) * blk:(j + 1) * blk, cols]
                vals = vd_ref[(j - 1) * blk:(j + 1) * blk, cols]
            s = lax.dot_general(lhs, keys, (((1,), (1,)), ((), ())),
                                preferred_element_type=_F32)
            s = jnp.where(from_prev, s[:, 0:blk], s[:, blk:2 * blk]) + tab_ref[g]
            if j == 0:
                s = jnp.where(no_prev, NEG_INF, s)
            sink = sink_ref[g]
            m = jnp.maximum(jnp.max(s, axis=-1, keepdims=True), sink)
            p = jnp.exp(s - m)
            denom = jnp.sum(p, axis=-1, keepdims=True) + jnp.exp(sink - m)
            p = p.astype(_BF16)
            p = jnp.concatenate([jnp.where(from_prev, p, zero_p),
                                 jnp.where(from_prev, zero_p, p)], axis=1)
            o = _dot(p, vals) / denom
            ya_ref[rows, 2 * g * LANES:(2 * g + 1) * LANES] = jnp.where(
                lo, o[0:blk], o[blk:2 * blk])
            ya_ref[rows, (2 * g + 1) * LANES:(2 * g + 2) * LANES] = jnp.where(
                lo, o[2 * blk:3 * blk], o[3 * blk:4 * blk])

    b = bu_ref[:, 0:D_CONV].astype(_F32)
    u = bu_ref[:, D_CONV:2 * D_CONV].astype(_F32)
    uh = jnp.where(is_first, 0.0, uh_ref[...].astype(_F32))
    row = lax.broadcasted_iota(jnp.int32, (tm, D_CONV), 0)
    u1 = jnp.where(row == 0, uh[15:16], pltpu.roll(u, 1, 0))
    u2 = jnp.where(row == 0, uh[14:15],
                   jnp.where(row == 1, uh[15:16], pltpu.roll(u, 2, 0)))
    z = cw_ref[2:3, :] * u + cw_ref[1:2, :] * u1 + cw_ref[0:1, :] * u2
    yc = _rms(b * z, onc_ref[...])
    ya = _rms(ya_ref[...], ona_ref[...])
    mixed = jnp.concatenate([yc, ya], axis=-1).astype(_BF16)
    h1 = x_ref[...] + _dot(mixed, wout_ref[...])
    h1_ref[...] = h1

    xn = _rms(h1, nffn_ref[...])
    xp_ref[...] = _pack_rows(xn)
    lg_ref[...] = _dot(xn.astype(_BF16), wr_ref[...]) + rb_ref[...]


def _mixer(bu, q, kd, vd, x2, cw, tab, sink, onc, ona, w_out, nffn, wr, rb, seq):
    n = x2.shape[0]
    tm = TM_MIX
    const2 = lambda i: (0, 0)
    const3 = lambda i: (0, 0, 0)
    prev_blk = lambda i: (jnp.maximum(i * (tm // ATTN_BLOCK) - 1, 0), 0)
    return pl.pallas_call(
        functools.partial(_mixer_kernel, seq // tm),
        grid=(n // tm,),
        in_specs=[
            pl.BlockSpec((tm, 2 * D_CONV), lambda i: (i, 0)),
            pl.BlockSpec((16, D_CONV), lambda i: (jnp.maximum(i * (tm // 16) - 1, 0), 1)),
            pl.BlockSpec((tm, D_ATTN), lambda i: (i, 0)),
            pl.BlockSpec((tm, KV_DUP), lambda i: (i, 0)),
            pl.BlockSpec((tm, KV_DUP), lambda i: (i, 0)),
            pl.BlockSpec((ATTN_BLOCK, KV_DUP), prev_blk),
            pl.BlockSpec((ATTN_BLOCK, KV_DUP), prev_blk),
            pl.BlockSpec((tm, D_MODEL), lambda i: (i, 0)),
            pl.BlockSpec((3, D_CONV), const2),
            pl.BlockSpec((N_KV_HEADS, 4 * ATTN_BLOCK, ATTN_BLOCK), const3,
                         pipeline_mode=pl.Buffered(1)),
            pl.BlockSpec((N_KV_HEADS, 4 * ATTN_BLOCK, 1), const3, pipeline_mode=pl.Buffered(1)),
            pl.BlockSpec((1, D_CONV), const2),
            pl.BlockSpec((1, D_ATTN), const2),
            pl.BlockSpec((D_MODEL, D_MODEL), const2, pipeline_mode=pl.Buffered(1)),
            pl.BlockSpec((1, D_MODEL), const2),
            pl.BlockSpec((D_MODEL, LANES), const2),
            pl.BlockSpec((1, LANES), const2),
        ],
        out_specs=[
            pl.BlockSpec((tm, D_MODEL), lambda i: (i, 0)),
            pl.BlockSpec((tm, HALF), lambda i: (i, 0)),
            pl.BlockSpec((tm, LANES), lambda i: (i, 0)),
        ],
        out_shape=[
            jax.ShapeDtypeStruct((n, D_MODEL), _F32),
            jax.ShapeDtypeStruct((n, HALF), _U32),
            jax.ShapeDtypeStruct((n, LANES), _F32),
        ],
        scratch_shapes=[pltpu.VMEM((tm, D_ATTN), _F32)],
        compiler_params=pltpu.CompilerParams(
            dimension_semantics=("parallel",), vmem_limit_bytes=VMEM_LIMIT),
        name="mixer",
    )(bu, bu, q, kd, vd, kd, vd, x2, cw, tab, sink, onc, ona, w_out, nffn, wr, rb)


def _route_kernel(lg_ref, route_ref, cnt_ref, run_ref):
    t = lg_ref.shape[0]

    @pl.when(pl.program_id(0) == 0)
    def _():
        run_ref[...] = jnp.zeros_like(run_ref)

    lg = lg_ref[...]
    lane = lax.broadcasted_iota(jnp.int32, (t, LANES), 1)
    ninf = -jnp.inf

    lane_f = lane.astype(_F32)

    def first_max(v):
        m = jnp.max(v, axis=-1, keepdims=True)
        idx = jnp.min(jnp.where(v == m, lane_f, float(LANES)), axis=-1, keepdims=True)
        return m, idx.astype(jnp.int32)

    is_g = lane < N_GROUPS
    gmax, gidx = first_max(jnp.where(is_g, lg, ninf))
    gsum = jnp.sum(jnp.where(is_g, jnp.exp(jnp.where(is_g, lg, gmax) - gmax), 0.0),
                   axis=-1, keepdims=True)
    g_w = 1.0 / gsum
    base = N_GROUPS + EXPERTS_PER_GROUP * gidx
    in_grp = jnp.logical_and(lane >= base, lane < base + EXPERTS_PER_GROUP)
    el = jnp.where(in_grp, lg, ninf)
    m1, i1 = first_max(el)
    m2, i2 = first_max(jnp.where(lane == i1, ninf, el))
    e21 = jnp.exp(m2 - m1)
    w1 = g_w / (1.0 + e21)
    w2 = g_w * e21 / (1.0 + e21)
    e1 = i1 - N_GROUPS
    e2 = i2 - N_GROUPS

    oh1 = lane == e1
    oh2 = lane == e2
    earlier = (lax.broadcasted_iota(jnp.int32, (t, t), 0)
               > lax.broadcasted_iota(jnp.int32, (t, t), 1)).astype(_BF16)
    c1 = _dot(earlier, oh1.astype(_BF16))
    c2 = _dot(earlier, oh2.astype(_BF16))
    tot1 = jnp.sum(oh1.astype(_F32), axis=0, keepdims=True)
    tot2 = jnp.sum(oh2.astype(_F32), axis=0, keepdims=True)
    run = run_ref[0:1, :]
    r1 = jnp.sum(jnp.where(oh1, run + c1, 0.0), axis=-1, keepdims=True)
    r2 = jnp.sum(jnp.where(oh2, run + tot1 + c2, 0.0), axis=-1, keepdims=True)
    run_ref[...] = jnp.broadcast_to(run + tot1 + tot2, run_ref.shape)
    cnt_ref[...] = run_ref[...]

    out = jnp.where(lane == 0, e1.astype(_F32), 0.0)
    out = jnp.where(lane == 1, e2.astype(_F32), out)
    out = jnp.where(lane == 2, w1, out)
    out = jnp.where(lane == 3, w2, out)
    out = jnp.where(lane == 4, r1, out)
    out = jnp.where(lane == 5, r2, out)
    route_ref[...] = out


def _route(logits):
    n = logits.shape[0]
    t = TM_ROUTE
    return pl.pallas_call(
        _route_kernel,
        grid=(n // t,),
        in_specs=[pl.BlockSpec((t, LANES), lambda i: (i, 0))],
        out_specs=[pl.BlockSpec((t, LANES), lambda i: (i, 0)),
                   pl.BlockSpec((8, LANES), lambda i: (0, 0))],
        out_shape=[jax.ShapeDtypeStruct((n, LANES), _F32),
                   jax.ShapeDtypeStruct((8, LANES), _F32)],
        scratch_shapes=[pltpu.VMEM((8, LANES), _F32)],
        compiler_params=pltpu.CompilerParams(dimension_semantics=("arbitrary",)),
        name="route",
    )(logits)


def _dispatch_kernel(pad_end_ref, padded_ref, xp_ref, dest_ref, xs_ref, zero_ref, sem, zsem):
    td = xp_ref.shape[0]
    nb = xs_ref.shape[0] // MOE_BLK
    nb_used = pad_end_ref[N_EXPERTS - 1] // MOE_BLK

    def zero_copy(start):
        start = pl.multiple_of(start, MOE_BLK)
        return pltpu.make_async_copy(zero_ref, xs_ref.at[pl.ds(start, MOE_BLK)], zsem)

    def zero_blocks(fn):
        for e in range(N_EXPERTS):
            @pl.when(padded_ref[e] > 0)
            def _():
                fn(zero_copy(pad_end_ref[e] - MOE_BLK))
        for b in range(nb - N_EXPERTS, nb):
            @pl.when(b >= nb_used)
            def _():
                fn(zero_copy(b * MOE_BLK))

    @pl.when(pl.program_id(0) == 0)
    def _():
        zero_ref[...] = jnp.zeros_like(zero_ref)
        zero_blocks(lambda c: c.start())
        zero_blocks(lambda c: c.wait())

    def row_copy(r, k):
        d = dest_ref[0, 0, TOP_K * r + k]
        return pltpu.make_async_copy(xp_ref.at[pl.ds(r, 1)], xs_ref.at[pl.ds(d, 1)], sem)

    def issue(r, c):
        for k in range(TOP_K):
            row_copy(r, k).start()
        return c

    lax.fori_loop(0, td, issue, 0, unroll=8)
    for k in range(TOP_K):
        pltpu.make_async_copy(xp_ref, xs_ref.at[pl.ds(0, td)], sem).wait()


def _dispatch(xp, dest3, pad_end, padded, n_rows):
    n = xp.shape[0]
    td = TM_DISP
    return pl.pallas_call(
        _dispatch_kernel,
        grid_spec=pltpu.PrefetchScalarGridSpec(
            num_scalar_prefetch=2,
            grid=(n // td,),
            in_specs=[
                pl.BlockSpec((td, HALF), lambda i, pe, pd: (i, 0)),
                pl.BlockSpec((1, 1, TOP_K * td), lambda i, pe, pd: (i, 0, 0),
                             memory_space=pltpu.SMEM),
            ],
            out_specs=pl.BlockSpec(memory_space=pl.ANY),
            scratch_shapes=[pltpu.VMEM((MOE_BLK, HALF), _U32),
                            pltpu.SemaphoreType.DMA, pltpu.SemaphoreType.DMA],
        ),
        out_shape=jax.ShapeDtypeStruct((n_rows, HALF), _U32),
        compiler_params=pltpu.CompilerParams(
            dimension_semantics=("arbitrary",)),
        name="dispatch",
    )(pad_end, padded, xp, dest3)


def _experts_kernel(be_ref, nxt_ref, nb_ref, xs_ref, wg_hbm, wu_hbm, wd_hbm, ys_ref,
                    wgs_ref, wus_ref, wds_ref, wgb_ref, wub_ref, wdb_ref, sem):
    i = pl.program_id(0)
    valid = i < nb_ref[0]
    e = be_ref[i]
    changed = jnp.logical_or(i == 0, e != be_ref[jnp.maximum(i - 1, 0)])

    def fetch(ex):
        return (pltpu.make_async_copy(wg_hbm.at[ex], wgs_ref, sem.at[0]),
                pltpu.make_async_copy(wu_hbm.at[ex], wus_ref, sem.at[1]),
                pltpu.make_async_copy(wd_hbm.at[ex], wds_ref, sem.at[2]))

    @pl.when(i == 0)
    def _():
        for c in fetch(e):
            c.start()

    @pl.when(jnp.logical_and(valid, changed))
    def _():
        for c in fetch(e):
            c.wait()
        wgb_ref[...] = wgs_ref[...].astype(_BF16)
        wub_ref[...] = wus_ref[...].astype(_BF16)
        wdb_ref[...] = wds_ref[...].astype(_BF16)
        nxt = nxt_ref[i]

        @pl.when(nxt >= 0)
        def _():
            for c in fetch(nxt):
                c.start()

    @pl.when(valid)
    def _():
        lo, hi = _unpack_rows(xs_ref[...])
        lo = lo.astype(_BF16)
        hi = hi.astype(_BF16)
        g = _dot(lo, wgb_ref[0:HALF, :]) + _dot(hi, wgb_ref[HALF:D_MODEL, :])
        u = _dot(lo, wub_ref[0:HALF, :]) + _dot(hi, wub_ref[HALF:D_MODEL, :])
        hidden = (g * jax.nn.sigmoid(g) * u).astype(_BF16)
        ys_ref[...] = _pack_rows(_dot(hidden, wdb_ref[...]))

    @pl.when(jnp.logical_not(valid))
    def _():
        ys_ref[...] = jnp.zeros_like(ys_ref)


def _experts(xs, w_gate, w_up, w_down, block_e, next_e, nb_used):
    n_rows = xs.shape[0]
    nb = n_rows // MOE_BLK
    row_blk = lambda i, be, nx, nbu: (jnp.minimum(i, nbu[0] - 1), 0)
    any_spec = pl.BlockSpec(memory_space=pl.ANY)
    return pl.pallas_call(
        _experts_kernel,
        grid_spec=pltpu.PrefetchScalarGridSpec(
            num_scalar_prefetch=3,
            grid=(nb,),
            in_specs=[pl.BlockSpec((MOE_BLK, HALF), row_blk), any_spec, any_spec, any_spec],
            out_specs=pl.BlockSpec((MOE_BLK, HALF), lambda i, be, nx, nbu: (i, 0)),
            scratch_shapes=[pltpu.VMEM((D_MODEL, D_EXPERT), _F32),
                            pltpu.VMEM((D_MODEL, D_EXPERT), _F32),
                            pltpu.VMEM((D_EXPERT, D_MODEL), _F32),
                            pltpu.VMEM((D_MODEL, D_EXPERT), _BF16),
                            pltpu.VMEM((D_MODEL, D_EXPERT), _BF16),
                            pltpu.VMEM((D_EXPERT, D_MODEL), _BF16),
                            pltpu.SemaphoreType.DMA((3,))],
        ),
        out_shape=jax.ShapeDtypeStruct((n_rows, HALF), _U32),
        compiler_params=pltpu.CompilerParams(
            dimension_semantics=("arbitrary",), vmem_limit_bytes=VMEM_LIMIT),
        name="experts",
    )(block_e, next_e, nb_used, xs, w_gate, w_up, w_down)


def _combine_kernel(h1_ref, route_ref, dcur_ref, dnext_ref, ys_ref, p_ref, nple_ref,
                    wgate_ref, wple_ref, out_ref, ybuf_ref, h2_ref, xn_ref, pb_ref, sem):
    tk = h1_ref.shape[0]
    cw = D_MODEL // N_CHUNK
    per = TOP_K * tk // N_CHUNK
    i = pl.program_id(0)
    n_steps = pl.num_programs(0)
    slot = i % 2

    def row_copy(d, s, k, r):
        return pltpu.make_async_copy(ys_ref.at[pl.ds(d, 1)],
                                     ybuf_ref.at[s, k, pl.ds(r, 1)], sem.at[s])

    def wait_slot(s):
        for k in range(TOP_K):
            pltpu.make_async_copy(ys_ref.at[pl.ds(0, tk)], ybuf_ref.at[s, k], sem.at[s]).wait()

    @pl.when(i == 0)
    def _():
        def body(r, c):
            for k in range(TOP_K):
                row_copy(dcur_ref[0, 0, TOP_K * r + k], 0, k, r).start()
            return c
        lax.fori_loop(0, tk, body, 0, unroll=8)

    wait_slot(slot)
    y1 = jnp.concatenate(_unpack_rows(ybuf_ref[slot, 0]), axis=-1)
    y2 = jnp.concatenate(_unpack_rows(ybuf_ref[slot, 1]), axis=-1)
    route = route_ref[...]
    h2 = h1_ref[...] + route[:, 2:3] * y1 + route[:, 3:4] * y2
    xn = _rms(h2, nple_ref[...]).astype(_BF16)
    pb = p_ref[...].astype(_BF16)

    def issue_batch(c):
        for j in range(per):
            d = dnext_ref[0, 0, c * per + j]
            row_copy(d, 1 - slot, j % TOP_K, c * (per // TOP_K) + j // TOP_K).start()

    issue_batch(0)
    h2_ref[...] = h2
    xn_ref[...] = xn
    pb_ref[...] = pb
    for c in range(N_CHUNK):
        if c > 0:
            issue_batch(c)
        cols = slice(c * cw, (c + 1) * cw)
        gate = jax.nn.sigmoid(_dot(xn_ref[...], wgate_ref[:, cols]))
        emb = _dot(pb_ref[...], wple_ref[:, cols])
        out_ref[:, cols] = h2_ref[:, cols] + gate * emb

    @pl.when(i == n_steps - 1)
    def _():
        wait_slot(1 - slot)


def _combine(h1, route, dest3, ys, p2, nple, w_gate, w_ple):
    n = h1.shape[0]
    tk = TM_COMB
    steps = n // tk
    const2 = lambda i: (0, 0)
    return pl.pallas_call(
        _combine_kernel,
        grid=(steps,),
        in_specs=[
            pl.BlockSpec((tk, D_MODEL), lambda i: (i, 0)),
            pl.BlockSpec((tk, LANES), lambda i: (i, 0)),
            pl.BlockSpec((1, 1, TOP_K * tk), lambda i: (i, 0, 0), memory_space=pltpu.SMEM),
            pl.BlockSpec((1, 1, TOP_K * tk), lambda i: (jnp.minimum(i + 1, steps - 1), 0, 0),
                         memory_space=pltpu.SMEM),
            pl.BlockSpec(memory_space=pl.ANY),
            pl.BlockSpec((tk, PLE_DIM), lambda i: (i, 0)),
            pl.BlockSpec((1, D_MODEL), const2),
            pl.BlockSpec((D_MODEL, D_MODEL), const2, pipeline_mode=pl.Buffered(1)),
            pl.BlockSpec((PLE_DIM, D_MODEL), const2, pipeline_mode=pl.Buffered(1)),
        ],
        out_specs=pl.BlockSpec((tk, D_MODEL), lambda i: (i, 0)),
        out_shape=jax.ShapeDtypeStruct((n, D_MODEL), _F32),
        scratch_shapes=[pltpu.VMEM((2, TOP_K, tk, HALF), _U32),
                        pltpu.VMEM((tk, D_MODEL), _F32),
                        pltpu.VMEM((tk, D_MODEL), _BF16),
                        pltpu.VMEM((tk, PLE_DIM), _BF16),
                        pltpu.SemaphoreType.DMA((2,))],
        compiler_params=pltpu.CompilerParams(
            dimension_semantics=("arbitrary",), vmem_limit_bytes=VMEM_LIMIT),
        name="combine",
    )(h1, route, dest3, dest3, ys, p2, nple, w_gate, w_ple)


def _attention_tables(sinks):
    slopes = 2.0 ** (-8.0 * jnp.arange(1, N_HEADS + 1, dtype=_F32) / N_HEADS)
    qi = jnp.arange(ATTN_BLOCK)[:, None]
    kj = jnp.arange(ATTN_BLOCK)[None, :]
    dist = jnp.where(kj > qi, ATTN_BLOCK + qi - kj, qi - kj)
    tab = -slopes[:, None, None] * dist.astype(_F32)[None]
    tab = tab.reshape(N_KV_HEADS, 4 * ATTN_BLOCK, ATTN_BLOCK)
    sink = jnp.repeat(sinks.astype(_F32), ATTN_BLOCK).reshape(N_KV_HEADS, 4 * ATTN_BLOCK, 1)
    return tab, sink


def _layer(h, p_i, norm_mix, w_in, conv_w, q_norm, k_norm, sinks, out_norm_conv,
           out_norm_attn, w_out, norm_ffn, w_group, b_group, w_router, b_router,
           w_gate, w_up, w_down, norm_ple, w_ple_gate, w_ple):
    bsz, seq, _ = h.shape
    n = bsz * seq
    x2 = h.reshape(n, D_MODEL)
    row = lambda v: v.reshape(1, -1).astype(_F32)

    bu, q, kd, vd = _inproj(
        x2, row(norm_mix), w_in.astype(_BF16),
        row(jnp.tile(q_norm, 2)), row(jnp.tile(k_norm, 2)))

    tab, sink = _attention_tables(sinks)
    w_r = jnp.zeros((D_MODEL, LANES), _F32)
    w_r = w_r.at[:, :N_GROUPS].set(w_group).at[:, N_GROUPS:N_GROUPS + N_EXPERTS].set(w_router)
    r_bias = jnp.zeros((1, LANES), _F32)
    r_bias = r_bias.at[0, :N_GROUPS].set(b_group).at[0, N_GROUPS:N_GROUPS + N_EXPERTS].set(b_router)
    h1, xp, logits = _mixer(
        bu, q, kd, vd, x2, conv_w.astype(_F32), tab, sink, row(out_norm_conv),
        row(out_norm_attn), w_out.astype(_BF16), row(norm_ffn), w_r.astype(_BF16), r_bias, seq)

    route, counts = _route(logits)

    cnt = counts[0, :N_EXPERTS].astype(jnp.int32)
    padded = (cnt + MOE_BLK - 1) // MOE_BLK * MOE_BLK
    pad_end = jnp.cumsum(padded)
    pad_start = pad_end - padded
    eid = route[:, 0:TOP_K].astype(jnp.int32)
    onehot = eid[..., None] == jnp.arange(N_EXPERTS, dtype=jnp.int32)
    dest = (jnp.sum(jnp.where(onehot, pad_start, 0), axis=-1)
            + route[:, 4:4 + TOP_K].astype(jnp.int32))
    n_rows = n * TOP_K + N_EXPERTS * MOE_BLK
    nb = n_rows // MOE_BLK
    nb_used = (pad_end[-1] // MOE_BLK).astype(jnp.int32)
    blk_start = jnp.arange(nb, dtype=jnp.int32) * MOE_BLK
    block_e = jnp.sum((blk_start[:, None] >= pad_end[None, :]).astype(jnp.int32), axis=1)
    block_e = jnp.minimum(block_e, N_EXPERTS - 1)
    block_e = jnp.where(jnp.arange(nb) < nb_used, block_e, block_e[nb_used - 1])
    ids = jnp.arange(N_EXPERTS, dtype=jnp.int32)
    later = jnp.logical_and(ids[None, :] > ids[:, None], cnt[None, :] > 0)
    next_present = jnp.min(jnp.where(later, ids[None, :], N_EXPERTS), axis=1)
    next_present = jnp.where(next_present < N_EXPERTS, next_present, -1)
    onehot_b = block_e[:, None] == ids[None, :]
    next_e = jnp.sum(jnp.where(onehot_b, next_present[None, :], 0), axis=1).astype(jnp.int32)

    xs = _dispatch(xp, dest.reshape(n // TM_DISP, 1, TOP_K * TM_DISP),
                   pad_end.astype(jnp.int32), padded.astype(jnp.int32), n_rows)
    ys = _experts(xs, w_gate, w_up, w_down, block_e, next_e, nb_used.reshape(1))
    out = _combine(h1, route, dest.reshape(n // TM_COMB, 1, TOP_K * TM_COMB), ys,
                   p_i.reshape(n, PLE_DIM), row(norm_ple), w_ple_gate.astype(_BF16),
                   w_ple.astype(_BF16))
    return out.reshape(bsz, seq, D_MODEL)


def kernel(x, p, norm_mix, w_in, conv_w, q_norm, k_norm, sinks, out_norm_conv, out_norm_attn,
           w_out, norm_ffn, w_group, b_group, w_router, b_router, w_gate, w_up, w_down,
           norm_ple, w_ple_gate, w_ple):
    h = x
    for i in range(p.shape[0]):
        h = _layer(h, p[i], norm_mix[i], w_in[i], conv_w[i], q_norm[i], k_norm[i], sinks[i],
                   out_norm_conv[i], out_norm_attn[i], w_out[i], norm_ffn[i], w_group[i],
                   b_group[i], w_router[i], b_router[i], w_gate[i], w_up[i], w_down[i],
                   norm_ple[i], w_ple_gate[i], w_ple[i])
    return h
```

```python
import functools

import jax
import jax.numpy as jnp
from jax import lax
from jax.experimental import pallas as pl
from jax.experimental.pallas import tpu as pltpu

D_MODEL = 2048
D_CONV = 1024
N_HEADS = 16
N_KV_HEADS = 4
HEAD_DIM = 64
D_ATTN = N_HEADS * HEAD_DIM
D_KV = N_KV_HEADS * HEAD_DIM
D_IN = 3 * D_CONV + D_ATTN + 2 * D_KV
ATTN_BLOCK = 128
N_GROUPS = 4
EXPERTS_PER_GROUP = 8
N_EXPERTS = N_GROUPS * EXPERTS_PER_GROUP
TOP_K = 2
D_EXPERT = 512
PLE_DIM = 256
EPS = 1e-6
NEG_INF = -1e30

LANES = 128
HALF = D_MODEL // 2
KV_DUP = 2 * D_KV

TM_PROJ = 512
TM_MIX = 256
TM_ROUTE = 1024
TM_DISP = 512
TM_COMB = 512
MOE_BLK = 512
N_CHUNK = 8
VMEM_LIMIT = 56 * 1024 * 1024

_F32 = jnp.float32
_BF16 = jnp.bfloat16
_U32 = jnp.uint32
_HI_MASK = 0xFFFF0000


def _rms(x, gain):
    return x * lax.rsqrt(jnp.mean(x * x, axis=-1, keepdims=True) + EPS) * gain


def _dot(a, b):
    return jnp.dot(a, b, preferred_element_type=_F32)


def _pack_rows(x):
    h = x.shape[-1] // 2
    bits = lax.bitcast_convert_type(x.astype(_BF16).astype(_F32), _U32)
    return (bits[:, h:] & _U32(_HI_MASK)) | (bits[:, :h] >> 16)


def _unpack_rows(w):
    lo = lax.bitcast_convert_type(w << 16, _F32)
    hi = lax.bitcast_convert_type(w & _U32(_HI_MASK), _F32)
    return lo, hi


def _inproj_kernel(x_ref, g_ref, w_ref, qg_ref, kg_ref, bu_ref, q_ref, kd_ref, vd_ref):
    tm = x_ref.shape[0]
    xn = _rms(x_ref[...], g_ref[...]).astype(_BF16)
    ch = 512
    for j in range(D_CONV // ch):
        b = _dot(xn, w_ref[:, j * ch:(j + 1) * ch])
        c = _dot(xn, w_ref[:, D_CONV + j * ch:D_CONV + (j + 1) * ch])
        hc = _dot(xn, w_ref[:, 2 * D_CONV + j * ch:2 * D_CONV + (j + 1) * ch])
        bu_ref[:, j * ch:(j + 1) * ch] = b.astype(_BF16)
        bu_ref[:, D_CONV + j * ch:D_CONV + (j + 1) * ch] = (c * hc).astype(_BF16)

    lo = lax.broadcasted_iota(jnp.int32, (tm, LANES), 1) < HEAD_DIM
    q = _dot(xn, w_ref[:, 3 * D_CONV:3 * D_CONV + D_ATTN])
    scale = HEAD_DIM ** -0.5
    for pr in range(D_ATTN // LANES):
        qp = q[:, pr * LANES:(pr + 1) * LANES]
        sq = qp * qp
        s_lo = jnp.sum(jnp.where(lo, sq, 0.0), axis=-1, keepdims=True)
        s_hi = jnp.sum(jnp.where(lo, 0.0, sq), axis=-1, keepdims=True)
        ms = jnp.where(lo, s_lo, s_hi) * (1.0 / HEAD_DIM)
        qn = qp * lax.rsqrt(ms + EPS) * qg_ref[...] * scale
        q_ref[:, pr * LANES:(pr + 1) * LANES] = qn.astype(_BF16)

    kv = _dot(xn, w_ref[:, 3 * D_CONV + D_ATTN:D_IN])
    for pr in range(2 * D_KV // LANES):
        pair = kv[:, pr * LANES:(pr + 1) * LANES]
        swapped = pltpu.roll(pair, HEAD_DIM, 1)
        for half in range(2):
            dup = jnp.where(lo, pair, swapped) if half == 0 else jnp.where(lo, swapped, pair)
            head = 2 * pr + half
            if head < N_KV_HEADS:
                ms = jnp.sum(dup * dup, axis=-1, keepdims=True) * (1.0 / LANES)
                dup = dup * lax.rsqrt(ms + EPS) * kg_ref[...]
                kd_ref[:, head * LANES:(head + 1) * LANES] = dup.astype(_BF16)
            else:
                head -= N_KV_HEADS
                vd_ref[:, head * LANES:(head + 1) * LANES] = dup.astype(_BF16)


def _inproj(x2, g, w_in, qg2, kg2):
    n = x2.shape[0]
    tm = TM_PROJ
    const = lambda i: (0, 0)
    return pl.pallas_call(
        _inproj_kernel,
        grid=(n // tm,),
        in_specs=[
            pl.BlockSpec((tm, D_MODEL), lambda i: (i, 0)),
            pl.BlockSpec((1, D_MODEL), const),
            pl.BlockSpec((D_MODEL, D_IN), const, pipeline_mode=pl.Buffered(1)),
            pl.BlockSpec((1, LANES), const),
            pl.BlockSpec((1, LANES), const),
        ],
        out_specs=[
            pl.BlockSpec((tm, 2 * D_CONV), lambda i: (i, 0)),
            pl.BlockSpec((tm, D_ATTN), lambda i: (i, 0)),
            pl.BlockSpec((tm, KV_DUP), lambda i: (i, 0)),
            pl.BlockSpec((tm, KV_DUP), lambda i: (i, 0)),
        ],
        out_shape=[
            jax.ShapeDtypeStruct((n, 2 * D_CONV), _BF16),
            jax.ShapeDtypeStruct((n, D_ATTN), _BF16),
            jax.ShapeDtypeStruct((n, KV_DUP), _BF16),
            jax.ShapeDtypeStruct((n, KV_DUP), _BF16),
        ],
        compiler_params=pltpu.CompilerParams(
            dimension_semantics=("parallel",), vmem_limit_bytes=VMEM_LIMIT),
        name="inproj",
    )(x2, g, w_in, qg2, kg2)


def _mixer_kernel(tiles_per_seq,
                  bu_ref, uh_ref, q_ref, kd_ref, vd_ref, kdh_ref, vdh_ref, x_ref,
                  cw_ref, tab_ref, sink_ref, onc_ref, ona_ref, wout_ref, nffn_ref,
                  wr_ref, rb_ref,
                  h1_ref, xp_ref, lg_ref, ya_ref):
    tm = x_ref.shape[0]
    blk = ATTN_BLOCK
    is_first = (pl.program_id(0) % tiles_per_seq) == 0

    lo = lax.broadcasted_iota(jnp.int32, (blk, LANES), 1) < HEAD_DIM
    qi = lax.broadcasted_iota(jnp.int32, (4 * blk, blk), 0) % blk
    from_prev = lax.broadcasted_iota(jnp.int32, (4 * blk, blk), 1) > qi
    no_prev = jnp.logical_and(is_first, from_prev)
    zero = jnp.zeros((blk, LANES), _BF16)
    zero_p = jnp.zeros((4 * blk, blk), _BF16)
    for j in range(tm // blk):
        rows = slice(j * blk, (j + 1) * blk)
        for g in range(N_KV_HEADS):
            cols = slice(g * LANES, (g + 1) * LANES)
            qa = q_ref[rows, 2 * g * LANES:(2 * g + 1) * LANES]
            qb = q_ref[rows, (2 * g + 1) * LANES:(2 * g + 2) * LANES]
            lhs = jnp.concatenate(
                [jnp.where(lo, qa, zero), jnp.where(lo, zero, qa),
                 jnp.where(lo, qb, zero), jnp.where(lo, zero, qb)], axis=0)
            if j == 0:
                keys = jnp.concatenate([kdh_ref[:, cols], kd_ref[0:blk, cols]], axis=0)
                vals = jnp.concatenate([vdh_ref[:, cols], vd_ref[0:blk, cols]], axis=0)
            else:
                keys = kd_ref[(j - 1) * blk:(j + 1) * blk, cols]
                vals = vd_ref[(j - 1) * blk:(j + 1) * blk, cols]
            s = lax.dot_general(lhs, keys, (((1,), (1,)), ((), ())),
                                preferred_element_type=_F32)
            s = jnp.where(from_prev, s[:, 0:blk], s[:, blk:2 * blk]) + tab_ref[g]
            if j == 0:
                s = jnp.where(no_prev, NEG_INF, s)
            sink = sink_ref[g]
            m = jnp.maximum(jnp.max(s, axis=-1, keepdims=True), sink)
            p = jnp.exp(s - m)
            denom = jnp.sum(p, axis=-1, keepdims=True) + jnp.exp(sink - m)
            p = p.astype(_BF16)
            p = jnp.concatenate([jnp.where(from_prev, p, zero_p),
                                 jnp.where(from_prev, zero_p, p)], axis=1)
            o = _dot(p, vals) / denom
            ya_ref[rows, 2 * g * LANES:(2 * g + 1) * LANES] = jnp.where(
                lo, o[0:blk], o[blk:2 * blk])
            ya_ref[rows, (2 * g + 1) * LANES:(2 * g + 2) * LANES] = jnp.where(
                lo, o[2 * blk:3 * blk], o[3 * blk:4 * blk])

    b = bu_ref[:, 0:D_CONV].astype(_F32)
    u = bu_ref[:, D_CONV:2 * D_CONV].astype(_F32)
    uh = jnp.where(is_first, 0.0, uh_ref[...].astype(_F32))
    row = lax.broadcasted_iota(jnp.int32, (tm, D_CONV), 0)
    u1 = jnp.where(row == 0, uh[15:16], pltpu.roll(u, 1, 0))
    u2 = jnp.where(row == 0, uh[14:15],
                   jnp.where(row == 1, uh[15:16], pltpu.roll(u, 2, 0)))
    z = cw_ref[2:3, :] * u + cw_ref[1:2, :] * u1 + cw_ref[0:1, :] * u2
    yc = _rms(b * z, onc_ref[...])
    ya = _rms(ya_ref[...], ona_ref[...])
    mixed = jnp.concatenate([yc, ya], axis=-1).astype(_BF16)
    h1 = x_ref[...] + _dot(mixed, wout_ref[...])
    h1_ref[...] = h1

    xn = _rms(h1, nffn_ref[...])
    xp_ref[...] = _pack_rows(xn)
    lg_ref[...] = _dot(xn.astype(_BF16), wr_ref[...]) + rb_ref[...]


def _mixer(bu, q, kd, vd, x2, cw, tab, sink, onc, ona, w_out, nffn, wr, rb, seq):
    n = x2.shape[0]
    tm = TM_MIX
    const2 = lambda i: (0, 0)
    const3 = lambda i: (0, 0, 0)
    prev_blk = lambda i: (jnp.maximum(i * (tm // ATTN_BLOCK) - 1, 0), 0)
    return pl.pallas_call(
        functools.partial(_mixer_kernel, seq // tm),
        grid=(n // tm,),
        in_specs=[
            pl.BlockSpec((tm, 2 * D_CONV), lambda i: (i, 0)),
            pl.BlockSpec((16, D_CONV), lambda i: (jnp.maximum(i * (tm // 16) - 1, 0), 1)),
            pl.BlockSpec((tm, D_ATTN), lambda i: (i, 0)),
            pl.BlockSpec((tm, KV_DUP), lambda i: (i, 0)),
            pl.BlockSpec((tm, KV_DUP), lambda i: (i, 0)),
            pl.BlockSpec((ATTN_BLOCK, KV_DUP), prev_blk),
            pl.BlockSpec((ATTN_BLOCK, KV_DUP), prev_blk),
            pl.BlockSpec((tm, D_MODEL), lambda i: (i, 0)),
            pl.BlockSpec((3, D_CONV), const2),
            pl.BlockSpec((N_KV_HEADS, 4 * ATTN_BLOCK, ATTN_BLOCK), const3,
                         pipeline_mode=pl.Buffered(1)),
            pl.BlockSpec((N_KV_HEADS, 4 * ATTN_BLOCK, 1), const3, pipeline_mode=pl.Buffered(1)),
            pl.BlockSpec((1, D_CONV), const2),
            pl.BlockSpec((1, D_ATTN), const2),
            pl.BlockSpec((D_MODEL, D_MODEL), const2, pipeline_mode=pl.Buffered(1)),
            pl.BlockSpec((1, D_MODEL), const2),
            pl.BlockSpec((D_MODEL, LANES), const2),
            pl.BlockSpec((1, LANES), const2),
        ],
        out_specs=[
            pl.BlockSpec((tm, D_MODEL), lambda i: (i, 0)),
            pl.BlockSpec((tm, HALF), lambda i: (i, 0)),
            pl.BlockSpec((tm, LANES), lambda i: (i, 0)),
        ],
        out_shape=[
            jax.ShapeDtypeStruct((n, D_MODEL), _F32),
            jax.ShapeDtypeStruct((n, HALF), _U32),
            jax.ShapeDtypeStruct((n, LANES), _F32),
        ],
        scratch_shapes=[pltpu.VMEM((tm, D_ATTN), _F32)],
        compiler_params=pltpu.CompilerParams(
            dimension_semantics=("parallel",), vmem_limit_bytes=VMEM_LIMIT),
        name="mixer",
    )(bu, bu, q, kd, vd, kd, vd, x2, cw, tab, sink, onc, ona, w_out, nffn, wr, rb)


def _route_kernel(lg_ref, route_ref, cnt_ref, run_ref):
    t = lg_ref.shape[0]

    @pl.when(pl.program_id(0) == 0)
    def _():
        run_ref[...] = jnp.zeros_like(run_ref)

    lg = lg_ref[...]
    lane = lax.broadcasted_iota(jnp.int32, (t, LANES), 1)
    ninf = -jnp.inf

    lane_f = lane.astype(_F32)

    def first_max(v):
        m = jnp.max(v, axis=-1, keepdims=True)
        idx = jnp.min(jnp.where(v == m, lane_f, float(LANES)), axis=-1, keepdims=True)
        return m, idx.astype(jnp.int32)

    is_g = lane < N_GROUPS
    gmax, gidx = first_max(jnp.where(is_g, lg, ninf))
    gsum = jnp.sum(jnp.where(is_g, jnp.exp(jnp.where(is_g, lg, gmax) - gmax), 0.0),
                   axis=-1, keepdims=True)
    g_w = 1.0 / gsum
    base = N_GROUPS + EXPERTS_PER_GROUP * gidx
    in_grp = jnp.logical_and(lane >= base, lane < base + EXPERTS_PER_GROUP)
    el = jnp.where(in_grp, lg, ninf)
    m1, i1 = first_max(el)
    m2, i2 = first_max(jnp.where(lane == i1, ninf, el))
    e21 = jnp.exp(m2 - m1)
    w1 = g_w / (1.0 + e21)
    w2 = g_w * e21 / (1.0 + e21)
    e1 = i1 - N_GROUPS
    e2 = i2 - N_GROUPS

    oh1 = lane == e1
    oh2 = lane == e2
    earlier = (lax.broadcasted_iota(jnp.int32, (t, t), 0)
               > lax.broadcasted_iota(jnp.int32, (t, t), 1)).astype(_BF16)
    c1 = _dot(earlier, oh1.astype(_BF16))
    c2 = _dot(earlier, oh2.astype(_BF16))
    tot1 = jnp.sum(oh1.astype(_F32), axis=0, keepdims=True)
    tot2 = jnp.sum(oh2.astype(_F32), axis=0, keepdims=True)
    run = run_ref[0:1, :]
    r1 = jnp.sum(jnp.where(oh1, run + c1, 0.0), axis=-1, keepdims=True)
    r2 = jnp.sum(jnp.where(oh2, run + tot1 + c2, 0.0), axis=-1, keepdims=True)
    run_ref[...] = jnp.broadcast_to(run + tot1 + tot2, run_ref.shape)
    cnt_ref[...] = run_ref[...]

    out = jnp.where(lane == 0, e1.astype(_F32), 0.0)
    out = jnp.where(lane == 1, e2.astype(_F32), out)
    out = jnp.where(lane == 2, w1, out)
    out = jnp.where(lane == 3, w2, out)
    out = jnp.where(lane == 4, r1, out)
    out = jnp.where(lane == 5, r2, out)
    route_ref[...] = out


def _route(logits):
    n = logits.shape[0]
    t = TM_ROUTE
    return pl.pallas_call(
        _route_kernel,
        grid=(n // t,),
        in_specs=[pl.BlockSpec((t, LANES), lambda i: (i, 0))],
        out_specs=[pl.BlockSpec((t, LANES), lambda i: (i, 0)),
                   pl.BlockSpec((8, LANES), lambda i: (0, 0))],
        out_shape=[jax.ShapeDtypeStruct((n, LANES), _F32),
                   jax.ShapeDtypeStruct((8, LANES), _F32)],
        scratch_shapes=[pltpu.VMEM((8, LANES), _F32)],
        compiler_params=pltpu.CompilerParams(dimension_semantics=("arbitrary",)),
        name="route",
    )(logits)


def _dispatch_kernel(pad_end_ref, padded_ref, dest_ref, xp_ref, xs_ref, zero_ref, sem, zsem):
    td = dest_ref.shape[-1] // TOP_K
    nb = xs_ref.shape[0] // MOE_BLK
    nb_used = pad_end_ref[N_EXPERTS - 1] // MOE_BLK
    i = pl.program_id(0)
    slot = i % 2

    def zero_copy(start):
        start = pl.multiple_of(start, MOE_BLK)
        return pltpu.make_async_copy(zero_ref, xs_ref.at[pl.ds(start, MOE_BLK)], zsem)

    def zero_blocks(fn):
        for e in range(N_EXPERTS):
            @pl.when(padded_ref[e] > 0)
            def _():
                fn(zero_copy(pad_end_ref[e] - MOE_BLK))
        for b in range(nb - N_EXPERTS, nb):
            @pl.when(b >= nb_used)
            def _():
                fn(zero_copy(b * MOE_BLK))

    @pl.when(i == 0)
    def _():
        zero_ref[...] = jnp.zeros_like(zero_ref)
        zero_blocks(lambda c: c.start())
        zero_blocks(lambda c: c.wait())

    base = i * td

    def issue(r, c):
        for k in range(TOP_K):
            d = dest_ref[0, 0, TOP_K * r + k]
            pltpu.make_async_copy(xp_ref.at[pl.ds(base + r, 1)], xs_ref.at[pl.ds(d, 1)],
                                  sem.at[slot]).start()
        return c

    lax.fori_loop(0, td, issue, 0, unroll=8)

    def wait_step(s):
        for k in range(TOP_K):
            pltpu.make_async_copy(xp_ref.at[pl.ds(0, td)], xs_ref.at[pl.ds(0, td)],
                                  sem.at[s]).wait()

    @pl.when(i > 0)
    def _():
        wait_step(1 - slot)

    @pl.when(i == pl.num_programs(0) - 1)
    def _():
        wait_step(slot)


def _dispatch(xp, dest3, pad_end, padded, n_rows):
    steps = dest3.shape[0]
    return pl.pallas_call(
        _dispatch_kernel,
        grid_spec=pltpu.PrefetchScalarGridSpec(
            num_scalar_prefetch=2,
            grid=(steps,),
            in_specs=[
                pl.BlockSpec((1, 1, dest3.shape[-1]), lambda i, pe, pd: (i, 0, 0),
                             memory_space=pltpu.SMEM),
                pl.BlockSpec(memory_space=pl.ANY),
            ],
            out_specs=pl.BlockSpec(memory_space=pl.ANY),
            scratch_shapes=[pltpu.VMEM((MOE_BLK, HALF), _U32),
                            pltpu.SemaphoreType.DMA((2,)), pltpu.SemaphoreType.DMA],
        ),
        out_shape=jax.ShapeDtypeStruct((n_rows, HALF), _U32),
        compiler_params=pltpu.CompilerParams(
            dimension_semantics=("arbitrary",)),
        name="dispatch",
    )(pad_end, padded, dest3, xp)


def _experts_kernel(be_ref, nxt_ref, nb_ref, xs_ref, wg_hbm, wu_hbm, wd_hbm, ys_ref,
                    wgs_ref, wus_ref, wds_ref, wgb_ref, wub_ref, wdb_ref, sem):
    i = pl.program_id(0)
    valid = i < nb_ref[0]
    e = be_ref[i]
    changed = jnp.logical_or(i == 0, e != be_ref[jnp.maximum(i - 1, 0)])

    def fetch(ex):
        return (pltpu.make_async_copy(wg_hbm.at[ex], wgs_ref, sem.at[0]),
                pltpu.make_async_copy(wu_hbm.at[ex], wus_ref, sem.at[1]),
                pltpu.make_async_copy(wd_hbm.at[ex], wds_ref, sem.at[2]))

    @pl.when(i == 0)
    def _():
        for c in fetch(e):
            c.start()

    @pl.when(jnp.logical_and(valid, changed))
    def _():
        for c in fetch(e):
            c.wait()
        wgb_ref[...] = wgs_ref[...].astype(_BF16)
        wub_ref[...] = wus_ref[...].astype(_BF16)
        wdb_ref[...] = wds_ref[...].astype(_BF16)
        nxt = nxt_ref[i]

        @pl.when(nxt >= 0)
        def _():
            for c in fetch(nxt):
                c.start()

    @pl.when(valid)
    def _():
        lo, hi = _unpack_rows(xs_ref[...])
        lo = lo.astype(_BF16)
        hi = hi.astype(_BF16)
        g = _dot(lo, wgb_ref[0:HALF, :]) + _dot(hi, wgb_ref[HALF:D_MODEL, :])
        u = _dot(lo, wub_ref[0:HALF, :]) + _dot(hi, wub_ref[HALF:D_MODEL, :])
        hidden = (g * jax.nn.sigmoid(g) * u).astype(_BF16)
        ys_ref[...] = _pack_rows(_dot(hidden, wdb_ref[...]))

    @pl.when(jnp.logical_not(valid))
    def _():
        ys_ref[...] = jnp.zeros_like(ys_ref)


def _experts(xs, w_gate, w_up, w_down, block_e, next_e, nb_used):
    n_rows = xs.shape[0]
    nb = n_rows // MOE_BLK
    row_blk = lambda i, be, nx, nbu: (jnp.minimum(i, nbu[0] - 1), 0)
    any_spec = pl.BlockSpec(memory_space=pl.ANY)
    return pl.pallas_call(
        _experts_kernel,
        grid_spec=pltpu.PrefetchScalarGridSpec(
            num_scalar_prefetch=3,
            grid=(nb,),
            in_specs=[pl.BlockSpec((MOE_BLK, HALF), row_blk), any_spec, any_spec, any_spec],
            out_specs=pl.BlockSpec((MOE_BLK, HALF), lambda i, be, nx, nbu: (i, 0)),
            scratch_shapes=[pltpu.VMEM((D_MODEL, D_EXPERT), _F32),
                            pltpu.VMEM((D_MODEL, D_EXPERT), _F32),
                            pltpu.VMEM((D_EXPERT, D_MODEL), _F32),
                            pltpu.VMEM((D_MODEL, D_EXPERT), _BF16),
                            pltpu.VMEM((D_MODEL, D_EXPERT), _BF16),
                            pltpu.VMEM((D_EXPERT, D_MODEL), _BF16),
                            pltpu.SemaphoreType.DMA((3,))],
        ),
        out_shape=jax.ShapeDtypeStruct((n_rows, HALF), _U32),
        compiler_params=pltpu.CompilerParams(
            dimension_semantics=("arbitrary",), vmem_limit_bytes=VMEM_LIMIT),
        name="experts",
    )(block_e, next_e, nb_used, xs, w_gate, w_up, w_down)


def _combine_kernel(h1_ref, route_ref, dcur_ref, dnext_ref, ys_ref, p_ref, nple_ref,
                    wgate_ref, wple_ref, out_ref, ybuf_ref, h2_ref, xn_ref, pb_ref, sem):
    tk = h1_ref.shape[0]
    cw = D_MODEL // N_CHUNK
    per = TOP_K * tk // N_CHUNK
    i = pl.program_id(0)
    n_steps = pl.num_programs(0)
    slot = i % 2

    def row_copy(d, s, k, r):
        return pltpu.make_async_copy(ys_ref.at[pl.ds(d, 1)],
                                     ybuf_ref.at[s, k, pl.ds(r, 1)], sem.at[s])

    def wait_slot(s):
        for k in range(TOP_K):
            pltpu.make_async_copy(ys_ref.at[pl.ds(0, tk)], ybuf_ref.at[s, k], sem.at[s]).wait()

    @pl.when(i == 0)
    def _():
        def body(r, c):
            for k in range(TOP_K):
                row_copy(dcur_ref[0, 0, TOP_K * r + k], 0, k, r).start()
            return c
        lax.fori_loop(0, tk, body, 0, unroll=8)

    wait_slot(slot)
    y1 = jnp.concatenate(_unpack_rows(ybuf_ref[slot, 0]), axis=-1)
    y2 = jnp.concatenate(_unpack_rows(ybuf_ref[slot, 1]), axis=-1)
    route = route_ref[...]
    h2 = h1_ref[...] + route[:, 2:3] * y1 + route[:, 3:4] * y2
    xn = _rms(h2, nple_ref[...]).astype(_BF16)
    pb = p_ref[...].astype(_BF16)

    def issue_batch(c):
        for j in range(per):
            d = dnext_ref[0, 0, c * per + j]
            row_copy(d, 1 - slot, j % TOP_K, c * (per // TOP_K) + j // TOP_K).start()

    issue_batch(0)
    h2_ref[...] = h2
    xn_ref[...] = xn
    pb_ref[...] = pb
    for c in range(N_CHUNK):
        if c > 0:
            issue_batch(c)
        cols = slice(c * cw, (c + 1) * cw)
        gate = jax.nn.sigmoid(_dot(xn_ref[...], wgate_ref[:, cols]))
        emb = _dot(pb_ref[...], wple_ref[:, cols])
        out_ref[:, cols] = h2_ref[:, cols] + gate * emb

    @pl.when(i == n_steps - 1)
    def _():
        wait_slot(1 - slot)


def _combine(h1, route, dest3, ys, p2, nple, w_gate, w_ple):
    n = h1.shape[0]
    tk = TM_COMB
    steps = n // tk
    const2 = lambda i: (0, 0)
    return pl.pallas_call(
        _combine_kernel,
        grid=(steps,),
        in_specs=[
            pl.BlockSpec((tk, D_MODEL), lambda i: (i, 0)),
            pl.BlockSpec((tk, LANES), lambda i: (i, 0)),
            pl.BlockSpec((1, 1, TOP_K * tk), lambda i: (i, 0, 0), memory_space=pltpu.SMEM),
            pl.BlockSpec((1, 1, TOP_K * tk), lambda i: (jnp.minimum(i + 1, steps - 1), 0, 0),
                         memory_space=pltpu.SMEM),
            pl.BlockSpec(memory_space=pl.ANY),
            pl.BlockSpec((tk, PLE_DIM), lambda i: (i, 0)),
            pl.BlockSpec((1, D_MODEL), const2),
            pl.BlockSpec((D_MODEL, D_MODEL), const2, pipeline_mode=pl.Buffered(1)),
            pl.BlockSpec((PLE_DIM, D_MODEL), const2, pipeline_mode=pl.Buffered(1)),
        ],
        out_specs=pl.BlockSpec((tk, D_MODEL), lambda i: (i, 0)),
        out_shape=jax.ShapeDtypeStruct((n, D_MODEL), _F32),
        scratch_shapes=[pltpu.VMEM((2, TOP_K, tk, HALF), _U32),
                        pltpu.VMEM((tk, D_MODEL), _F32),
                        pltpu.VMEM((tk, D_MODEL), _BF16),
                        pltpu.VMEM((tk, PLE_DIM), _BF16),
                        pltpu.SemaphoreType.DMA((2,))],
        compiler_params=pltpu.CompilerParams(
            dimension_semantics=("arbitrary",), vmem_limit_bytes=VMEM_LIMIT),
        name="combine",
    )(h1, route, dest3, dest3, ys, p2, nple, w_gate, w_ple)


def _attention_tables(sinks):
    slopes = 2.0 ** (-8.0 * jnp.arange(1, N_HEADS + 1, dtype=_F32) / N_HEADS)
    qi = jnp.arange(ATTN_BLOCK)[:, None]
    kj = jnp.arange(ATTN_BLOCK)[None, :]
    dist = jnp.where(kj > qi, ATTN_BLOCK + qi - kj, qi - kj)
    tab = -slopes[:, None, None] * dist.astype(_F32)[None]
    tab = tab.reshape(N_KV_HEADS, 4 * ATTN_BLOCK, ATTN_BLOCK)
    sink = jnp.repeat(sinks.astype(_F32), ATTN_BLOCK).reshape(N_KV_HEADS, 4 * ATTN_BLOCK, 1)
    return tab, sink


def _layer(h, p_i, norm_mix, w_in, conv_w, q_norm, k_norm, sinks, out_norm_conv,
           out_norm_attn, w_out, norm_ffn, w_group, b_group, w_router, b_router,
           w_gate, w_up, w_down, norm_ple, w_ple_gate, w_ple):
    bsz, seq, _ = h.shape
    n = bsz * seq
    x2 = h.reshape(n, D_MODEL)
    row = lambda v: v.reshape(1, -1).astype(_F32)

    bu, q, kd, vd = _inproj(
        x2, row(norm_mix), w_in.astype(_BF16),
        row(jnp.tile(q_norm, 2)), row(jnp.tile(k_norm, 2)))

    tab, sink = _attention_tables(sinks)
    w_r = jnp.zeros((D_MODEL, LANES), _F32)
    w_r = w_r.at[:, :N_GROUPS].set(w_group).at[:, N_GROUPS:N_GROUPS + N_EXPERTS].set(w_router)
    r_bias = jnp.zeros((1, LANES), _F32)
    r_bias = r_bias.at[0, :N_GROUPS].set(b_group).at[0, N_GROUPS:N_GROUPS + N_EXPERTS].set(b_router)
    h1, xp, logits = _mixer(
        bu, q, kd, vd, x2, conv_w.astype(_F32), tab, sink, row(out_norm_conv),
        row(out_norm_attn), w_out.astype(_BF16), row(norm_ffn), w_r.astype(_BF16), r_bias, seq)

    route, counts = _route(logits)

    cnt = counts[0, :N_EXPERTS].astype(jnp.int32)
    padded = (cnt + MOE_BLK - 1) // MOE_BLK * MOE_BLK
    pad_end = jnp.cumsum(padded)
    pad_start = pad_end - padded
    eid = route[:, 0:TOP_K].astype(jnp.int32)
    onehot = eid[..., None] == jnp.arange(N_EXPERTS, dtype=jnp.int32)
    dest = (jnp.sum(jnp.where(onehot, pad_start, 0), axis=-1)
            + route[:, 4:4 + TOP_K].astype(jnp.int32))
    n_rows = n * TOP_K + N_EXPERTS * MOE_BLK
    nb = n_rows // MOE_BLK
    nb_used = (pad_end[-1] // MOE_BLK).astype(jnp.int32)
    blk_start = jnp.arange(nb, dtype=jnp.int32) * MOE_BLK
    block_e = jnp.sum((blk_start[:, None] >= pad_end[None, :]).astype(jnp.int32), axis=1)
    block_e = jnp.minimum(block_e, N_EXPERTS - 1)
    block_e = jnp.where(jnp.arange(nb) < nb_used, block_e, block_e[nb_used - 1])
    ids = jnp.arange(N_EXPERTS, dtype=jnp.int32)
    later = jnp.logical_and(ids[None, :] > ids[:, None], cnt[None, :] > 0)
    next_present = jnp.min(jnp.where(later, ids[None, :], N_EXPERTS), axis=1)
    next_present = jnp.where(next_present < N_EXPERTS, next_present, -1)
    onehot_b = block_e[:, None] == ids[None, :]
    next_e = jnp.sum(jnp.where(onehot_b, next_present[None, :], 0), axis=1).astype(jnp.int32)

    xs = _dispatch(xp, dest.reshape(n // TM_DISP, 1, TOP_K * TM_DISP),
                   pad_end.astype(jnp.int32), padded.astype(jnp.int32), n_rows)
    ys = _experts(xs, w_gate, w_up, w_down, block_e, next_e, nb_used.reshape(1))
    out = _combine(h1, route, dest.reshape(n // TM_COMB, 1, TOP_K * TM_COMB), ys,
                   p_i.reshape(n, PLE_DIM), row(norm_ple), w_ple_gate.astype(_BF16),
                   w_ple.astype(_BF16))
    return out.reshape(bsz, seq, D_MODEL)


def kernel(x, p, norm_mix, w_in, conv_w, q_norm, k_norm, sinks, out_norm_conv, out_norm_attn,
           w_out, norm_ffn, w_group, b_group, w_router, b_router, w_gate, w_up, w_down,
           norm_ple, w_ple_gate, w_ple):
    h = x
    for i in range(p.shape[0]):
        h = _layer(h, p[i], norm_mix[i], w_in[i], conv_w[i], q_norm[i], k_norm[i], sinks[i],
                   out_norm_conv[i], out_norm_attn[i], w_out[i], norm_ffn[i], w_group[i],
                   b_group[i], w_router[i], b_router[i], w_gate[i], w_up[i], w_down[i],
                   norm_ple[i], w_ple_gate[i], w_ple[i])
    return h
```

```python
import functools

import jax
import jax.numpy as jnp
from jax import lax
from jax.experimental import pallas as pl
from jax.experimental.pallas import tpu as pltpu

D_MODEL = 2048
D_CONV = 1024
N_HEADS = 16
N_KV_HEADS = 4
HEAD_DIM = 64
D_ATTN = N_HEADS * HEAD_DIM
D_KV = N_KV_HEADS * HEAD_DIM
D_IN = 3 * D_CONV + D_ATTN + 2 * D_KV
ATTN_BLOCK = 128
N_GROUPS = 4
EXPERTS_PER_GROUP = 8
N_EXPERTS = N_GROUPS * EXPERTS_PER_GROUP
TOP_K = 2
D_EXPERT = 512
PLE_DIM = 256
EPS = 1e-6
NEG_INF = -1e30

LANES = 128
HALF = D_MODEL // 2
KV_DUP = 2 * D_KV

TM_PROJ = 512
TM_MIX = 256
TM_ROUTE = 1024
TM_DISP = 512
TM_COMB = 512
MOE_BLK = 512
N_CHUNK = 8
VMEM_LIMIT = 56 * 1024 * 1024

_F32 = jnp.float32
_BF16 = jnp.bfloat16
_U32 = jnp.uint32
_HI_MASK = 0xFFFF0000


def _rms(x, gain):
    return x * lax.rsqrt(jnp.mean(x * x, axis=-1, keepdims=True) + EPS) * gain


def _dot(a, b):
    return jnp.dot(a, b, preferred_element_type=_F32)


def _pack_rows(x):
    h = x.shape[-1] // 2
    bits = lax.bitcast_convert_type(x.astype(_BF16).astype(_F32), _U32)
    return (bits[:, h:] & _U32(_HI_MASK)) | (bits[:, :h] >> 16)


def _unpack_rows(w):
    lo = lax.bitcast_convert_type(w << 16, _F32)
    hi = lax.bitcast_convert_type(w & _U32(_HI_MASK), _F32)
    return lo, hi


def _inproj_kernel(x_ref, g_ref, w_ref, qg_ref, kg_ref, bu_ref, q_ref, kd_ref, vd_ref):
    tm = x_ref.shape[0]
    xn = _rms(x_ref[...], g_ref[...]).astype(_BF16)
    ch = 512
    for j in range(D_CONV // ch):
        b = _dot(xn, w_ref[:, j * ch:(j + 1) * ch])
        c = _dot(xn, w_ref[:, D_CONV + j * ch:D_CONV + (j + 1) * ch])
        hc = _dot(xn, w_ref[:, 2 * D_CONV + j * ch:2 * D_CONV + (j + 1) * ch])
        bu_ref[:, j * ch:(j + 1) * ch] = b.astype(_BF16)
        bu_ref[:, D_CONV + j * ch:D_CONV + (j + 1) * ch] = (c * hc).astype(_BF16)

    lo = lax.broadcasted_iota(jnp.int32, (tm, LANES), 1) < HEAD_DIM
    q = _dot(xn, w_ref[:, 3 * D_CONV:3 * D_CONV + D_ATTN])
    scale = HEAD_DIM ** -0.5
    for pr in range(D_ATTN // LANES):
        qp = q[:, pr * LANES:(pr + 1) * LANES]
        sq = qp * qp
        s_lo = jnp.sum(jnp.where(lo, sq, 0.0), axis=-1, keepdims=True)
        s_hi = jnp.sum(jnp.where(lo, 0.0, sq), axis=-1, keepdims=True)
        ms = jnp.where(lo, s_lo, s_hi) * (1.0 / HEAD_DIM)
        qn = qp * lax.rsqrt(ms + EPS) * qg_ref[...] * scale
        q_ref[:, pr * LANES:(pr + 1) * LANES] = qn.astype(_BF16)

    kv = _dot(xn, w_ref[:, 3 * D_CONV + D_ATTN:D_IN])
    for pr in range(2 * D_KV // LANES):
        pair = kv[:, pr * LANES:(pr + 1) * LANES]
        swapped = pltpu.roll(pair, HEAD_DIM, 1)
        for half in range(2):
            dup = jnp.where(lo, pair, swapped) if half == 0 else jnp.where(lo, swapped, pair)
            head = 2 * pr + half
            if head < N_KV_HEADS:
                ms = jnp.sum(dup * dup, axis=-1, keepdims=True) * (1.0 / LANES)
                dup = dup * lax.rsqrt(ms + EPS) * kg_ref[...]
                kd_ref[:, head * LANES:(head + 1) * LANES] = dup.astype(_BF16)
            else:
                head -= N_KV_HEADS
                vd_ref[:, head * LANES:(head + 1) * LANES] = dup.astype(_BF16)


def _inproj(x2, g, w_in, qg2, kg2):
    n = x2.shape[0]
    tm = TM_PROJ
    const = lambda i: (0, 0)
    return pl.pallas_call(
        _inproj_kernel,
        grid=(n // tm,),
        in_specs=[
            pl.BlockSpec((tm, D_MODEL), lambda i: (i, 0)),
            pl.BlockSpec((1, D_MODEL), const),
            pl.BlockSpec((D_MODEL, D_IN), const, pipeline_mode=pl.Buffered(1)),
            pl.BlockSpec((1, LANES), const),
            pl.BlockSpec((1, LANES), const),
        ],
        out_specs=[
            pl.BlockSpec((tm, 2 * D_CONV), lambda i: (i, 0)),
            pl.BlockSpec((tm, D_ATTN), lambda i: (i, 0)),
            pl.BlockSpec((tm, KV_DUP), lambda i: (i, 0)),
            pl.BlockSpec((tm, KV_DUP), lambda i: (i, 0)),
        ],
        out_shape=[
            jax.ShapeDtypeStruct((n, 2 * D_CONV), _BF16),
            jax.ShapeDtypeStruct((n, D_ATTN), _BF16),
            jax.ShapeDtypeStruct((n, KV_DUP), _BF16),
            jax.ShapeDtypeStruct((n, KV_DUP), _BF16),
        ],
        compiler_params=pltpu.CompilerParams(
            dimension_semantics=("parallel",), vmem_limit_bytes=VMEM_LIMIT),
        name="inproj",
    )(x2, g, w_in, qg2, kg2)


def _mixer_kernel(tiles_per_seq,
                  bu_ref, uh_ref, q_ref, kd_ref, vd_ref, kdh_ref, vdh_ref, x_ref,
                  cw_ref, tab_ref, sink_ref, onc_ref, ona_ref, wout_ref, nffn_ref,
                  wr_ref, rb_ref,
                  h1_ref, xp_ref, lg_ref, ya_ref):
    tm = x_ref.shape[0]
    blk = ATTN_BLOCK
    is_first = (pl.program_id(0) % tiles_per_seq) == 0

    lo = lax.broadcasted_iota(jnp.int32, (blk, LANES), 1) < HEAD_DIM
    qi = lax.broadcasted_iota(jnp.int32, (4 * blk, blk), 0) % blk
    from_prev = lax.broadcasted_iota(jnp.int32, (4 * blk, blk), 1) > qi
    no_prev = jnp.logical_and(is_first, from_prev)
    zero = jnp.zeros((blk, LANES), _BF16)
    zero_p = jnp.zeros((4 * blk, blk), _BF16)
    for j in range(tm // blk):
        rows = slice(j * blk, (j + 1) * blk)
        for g in range(N_KV_HEADS):
            cols = slice(g * LANES, (g + 1) * LANES)
            qa = q_ref[rows, 2 * g * LANES:(2 * g + 1) * LANES]
            qb = q_ref[rows, (2 * g + 1) * LANES:(2 * g + 2) * LANES]
            lhs = jnp.concatenate(
                [jnp.where(lo, qa, zero), jnp.where(lo, zero, qa),
                 jnp.where(lo, qb, zero), jnp.where(lo, zero, qb)], axis=0)
            if j == 0:
                keys = jnp.concatenate([kdh_ref[:, cols], kd_ref[0:blk, cols]], axis=0)
                vals = jnp.concatenate([vdh_ref[:, cols], vd_ref[0:blk, cols]], axis=0)
            else:
                keys = kd_ref[(j - 1) * blk:(j + 1) * blk, cols]
                vals = vd_ref[(j - 1) * blk:(j + 1) * blk, cols]
            s = lax.dot_general(lhs, keys, (((1,), (1,)), ((), ())),
                                preferred_element_type=_F32)
            s = jnp.where(from_prev, s[:, 0:blk], s[:, blk:2 * blk]) + tab_ref[g]
            if j == 0:
                s = jnp.where(no_prev, NEG_INF, s)
            sink = sink_ref[g]
            m = jnp.maximum(jnp.max(s, axis=-1, keepdims=True), sink)
            p = jnp.exp(s - m)
            denom = jnp.sum(p, axis=-1, keepdims=True) + jnp.exp(sink - m)
            p = p.astype(_BF16)
            p = jnp.concatenate([jnp.where(from_prev, p, zero_p),
                                 jnp.where(from_prev, zero_p, p)], axis=1)
            o = _dot(p, vals) / denom
            ya_ref[rows, 2 * g * LANES:(2 * g + 1) * LANES] = jnp.where(
                lo, o[0:blk], o[blk:2 * blk])
            ya_ref[rows, (2 * g + 1) * LANES:(2 * g + 2) * LANES] = jnp.where(
                lo, o[2 * blk:3 * blk], o[3 * blk:4 * blk])

    b = bu_ref[:, 0:D_CONV].astype(_F32)
    u = bu_ref[:, D_CONV:2 * D_CONV].astype(_F32)
    uh = jnp.where(is_first, 0.0, uh_ref[...].astype(_F32))
    row = lax.broadcasted_iota(jnp.int32, (tm, D_CONV), 0)
    u1 = jnp.where(row == 0, uh[15:16], pltpu.roll(u, 1, 0))
    u2 = jnp.where(row == 0, uh[14:15],
                   jnp.where(row == 1, uh[15:16], pltpu.roll(u, 2, 0)))
    z = cw_ref[2:3, :] * u + cw_ref[1:2, :] * u1 + cw_ref[0:1, :] * u2
    yc = _rms(b * z, onc_ref[...])
    ya = _rms(ya_ref[...], ona_ref[...])
    mixed = jnp.concatenate([yc, ya], axis=-1).astype(_BF16)
    h1 = x_ref[...] + _dot(mixed, wout_ref[...])
    h1_ref[...] = h1

    xn = _rms(h1, nffn_ref[...])
    xp_ref[...] = _pack_rows(xn)
    lg_ref[...] = _dot(xn.astype(_BF16), wr_ref[...]) + rb_ref[...]


def _mixer(bu, q, kd, vd, x2, cw, tab, sink, onc, ona, w_out, nffn, wr, rb, seq):
    n = x2.shape[0]
    tm = TM_MIX
    const2 = lambda i: (0, 0)
    const3 = lambda i: (0, 0, 0)
    prev_blk = lambda i: (jnp.maximum(i * (tm // ATTN_BLOCK) - 1, 0), 0)
    return pl.pallas_call(
        functools.partial(_mixer_kernel, seq // tm),
        grid=(n // tm,),
        in_specs=[
            pl.BlockSpec((tm, 2 * D_CONV), lambda i: (i, 0)),
            pl.BlockSpec((16, D_CONV), lambda i: (jnp.maximum(i * (tm // 16) - 1, 0), 1)),
            pl.BlockSpec((tm, D_ATTN), lambda i: (i, 0)),
            pl.BlockSpec((tm, KV_DUP), lambda i: (i, 0)),
            pl.BlockSpec((tm, KV_DUP), lambda i: (i, 0)),
            pl.BlockSpec((ATTN_BLOCK, KV_DUP), prev_blk),
            pl.BlockSpec((ATTN_BLOCK, KV_DUP), prev_blk),
            pl.BlockSpec((tm, D_MODEL), lambda i: (i, 0)),
            pl.BlockSpec((3, D_CONV), const2),
            pl.BlockSpec((N_KV_HEADS, 4 * ATTN_BLOCK, ATTN_BLOCK), const3,
                         pipeline_mode=pl.Buffered(1)),
            pl.BlockSpec((N_KV_HEADS, 4 * ATTN_BLOCK, 1), const3, pipeline_mode=pl.Buffered(1)),
            pl.BlockSpec((1, D_CONV), const2),
            pl.BlockSpec((1, D_ATTN), const2),
            pl.BlockSpec((D_MODEL, D_MODEL), const2, pipeline_mode=pl.Buffered(1)),
            pl.BlockSpec((1, D_MODEL), const2),
            pl.BlockSpec((D_MODEL, LANES), const2),
            pl.BlockSpec((1, LANES), const2),
        ],
        out_specs=[
            pl.BlockSpec((tm, D_MODEL), lambda i: (i, 0)),
            pl.BlockSpec((tm, HALF), lambda i: (i, 0)),
            pl.BlockSpec((tm, LANES), lambda i: (i, 0)),
        ],
        out_shape=[
            jax.ShapeDtypeStruct((n, D_MODEL), _F32),
            jax.ShapeDtypeStruct((n, HALF), _U32),
            jax.ShapeDtypeStruct((n, LANES), _F32),
        ],
        scratch_shapes=[pltpu.VMEM((tm, D_ATTN), _F32)],
        compiler_params=pltpu.CompilerParams(
            dimension_semantics=("parallel",), vmem_limit_bytes=VMEM_LIMIT),
        name="mixer",
    )(bu, bu, q, kd, vd, kd, vd, x2, cw, tab, sink, onc, ona, w_out, nffn, wr, rb)


def _route_kernel(lg_ref, route_ref, cnt_ref, run_ref):
    t = lg_ref.shape[0]

    @pl.when(pl.program_id(0) == 0)
    def _():
        run_ref[...] = jnp.zeros_like(run_ref)

    lg = lg_ref[...]
    lane = lax.broadcasted_iota(jnp.int32, (t, LANES), 1)
    ninf = -jnp.inf

    lane_f = lane.astype(_F32)

    def first_max(v):
        m = jnp.max(v, axis=-1, keepdims=True)
        idx = jnp.min(jnp.where(v == m, lane_f, float(LANES)), axis=-1, keepdims=True)
        return m, idx.astype(jnp.int32)

    is_g = lane < N_GROUPS
    gmax, gidx = first_max(jnp.where(is_g, lg, ninf))
    gsum = jnp.sum(jnp.where(is_g, jnp.exp(jnp.where(is_g, lg, gmax) - gmax), 0.0),
                   axis=-1, keepdims=True)
    g_w = 1.0 / gsum
    base = N_GROUPS + EXPERTS_PER_GROUP * gidx
    in_grp = jnp.logical_and(lane >= base, lane < base + EXPERTS_PER_GROUP)
    el = jnp.where(in_grp, lg, ninf)
    m1, i1 = first_max(el)
    m2, i2 = first_max(jnp.where(lane == i1, ninf, el))
    e21 = jnp.exp(m2 - m1)
    w1 = g_w / (1.0 + e21)
    w2 = g_w * e21 / (1.0 + e21)
    e1 = i1 - N_GROUPS
    e2 = i2 - N_GROUPS

    oh1 = lane == e1
    oh2 = lane == e2
    earlier = (lax.broadcasted_iota(jnp.int32, (t, t), 0)
               > lax.broadcasted_iota(jnp.int32, (t, t), 1)).astype(_BF16)
    c1 = _dot(earlier, oh1.astype(_BF16))
    c2 = _dot(earlier, oh2.astype(_BF16))
    tot1 = jnp.sum(oh1.astype(_F32), axis=0, keepdims=True)
    tot2 = jnp.sum(oh2.astype(_F32), axis=0, keepdims=True)
    run = run_ref[0:1, :]
    r1 = jnp.sum(jnp.where(oh1, run + c1, 0.0), axis=-1, keepdims=True)
    r2 = jnp.sum(jnp.where(oh2, run + tot1 + c2, 0.0), axis=-1, keepdims=True)
    run_ref[...] = jnp.broadcast_to(run + tot1 + tot2, run_ref.shape)
    cnt_ref[...] = run_ref[...]

    out = jnp.where(lane == 0, e1.astype(_F32), 0.0)
    out = jnp.where(lane == 1, e2.astype(_F32), out)
    out = jnp.where(lane == 2, w1, out)
    out = jnp.where(lane == 3, w2, out)
    out = jnp.where(lane == 4, r1, out)
    out = jnp.where(lane == 5, r2, out)
    route_ref[...] = out


def _route(logits):
    n = logits.shape[0]
    t = TM_ROUTE
    return pl.pallas_call(
        _route_kernel,
        grid=(n // t,),
        in_specs=[pl.BlockSpec((t, LANES), lambda i: (i, 0))],
        out_specs=[pl.BlockSpec((t, LANES), lambda i: (i, 0)),
                   pl.BlockSpec((8, LANES), lambda i: (0, 0))],
        out_shape=[jax.ShapeDtypeStruct((n, LANES), _F32),
                   jax.ShapeDtypeStruct((8, LANES), _F32)],
        scratch_shapes=[pltpu.VMEM((8, LANES), _F32)],
        compiler_params=pltpu.CompilerParams(dimension_semantics=("arbitrary",)),
        name="route",
    )(logits)


def _dispatch_kernel(pad_end_ref, padded_ref, dest_ref, xp_ref, xs_ref,
                     buf_ref, zero_ref, lsem, rsem, zsem):
    td = buf_ref.shape[1]
    nb = xs_ref.shape[0] // MOE_BLK
    nb_used = pad_end_ref[N_EXPERTS - 1] // MOE_BLK
    i = pl.program_id(0)
    n_steps = pl.num_programs(0)
    slot = i % 3
    nxt = (i + 1) % 3
    prv = (i + 2) % 3

    def zero_copy(start):
        start = pl.multiple_of(start, MOE_BLK)
        return pltpu.make_async_copy(zero_ref, xs_ref.at[pl.ds(start, MOE_BLK)], zsem)

    def zero_blocks(fn):
        for e in range(N_EXPERTS):
            @pl.when(padded_ref[e] > 0)
            def _():
                fn(zero_copy(pad_end_ref[e] - MOE_BLK))
        for b in range(nb - N_EXPERTS, nb):
            @pl.when(b >= nb_used)
            def _():
                fn(zero_copy(b * MOE_BLK))

    def load(t, s):
        start = pl.multiple_of(t * td, td)
        return pltpu.make_async_copy(xp_ref.at[pl.ds(start, td)], buf_ref.at[s], lsem.at[s])

    def wait_rows(s):
        for k in range(TOP_K):
            pltpu.make_async_copy(buf_ref.at[s], xs_ref.at[pl.ds(0, td)], rsem.at[s]).wait()

    @pl.when(i == 0)
    def _():
        zero_ref[...] = jnp.zeros_like(zero_ref)
        zero_blocks(lambda c: c.start())
        zero_blocks(lambda c: c.wait())
        load(0, 0).start()

    @pl.when(i + 1 < n_steps)
    def _():
        load(i + 1, nxt).start()

    load(i, slot).wait()

    def issue(r, c):
        for k in range(TOP_K):
            d = dest_ref[0, 0, TOP_K * r + k]
            pltpu.make_async_copy(buf_ref.at[slot, pl.ds(r, 1)], xs_ref.at[pl.ds(d, 1)],
                                  rsem.at[slot]).start()
        return c

    lax.fori_loop(0, td, issue, 0, unroll=8)

    @pl.when(i > 0)
    def _():
        wait_rows(prv)

    @pl.when(i == n_steps - 1)
    def _():
        wait_rows(slot)


def _dispatch(xp, dest3, pad_end, padded, n_rows):
    steps = dest3.shape[0]
    td = dest3.shape[-1] // TOP_K
    return pl.pallas_call(
        _dispatch_kernel,
        grid_spec=pltpu.PrefetchScalarGridSpec(
            num_scalar_prefetch=2,
            grid=(steps,),
            in_specs=[
                pl.BlockSpec((1, 1, dest3.shape[-1]), lambda i, pe, pd: (i, 0, 0),
                             memory_space=pltpu.SMEM),
                pl.BlockSpec(memory_space=pl.ANY),
            ],
            out_specs=pl.BlockSpec(memory_space=pl.ANY),
            scratch_shapes=[pltpu.VMEM((3, td, HALF), _U32),
                            pltpu.VMEM((MOE_BLK, HALF), _U32),
                            pltpu.SemaphoreType.DMA((3,)), pltpu.SemaphoreType.DMA((3,)),
                            pltpu.SemaphoreType.DMA],
        ),
        out_shape=jax.ShapeDtypeStruct((n_rows, HALF), _U32),
        compiler_params=pltpu.CompilerParams(
            dimension_semantics=("arbitrary",)),
        name="dispatch",
    )(pad_end, padded, dest3, xp)


def _experts_kernel(be_ref, nxt_ref, nb_ref, xs_ref, wg_hbm, wu_hbm, wd_hbm, ys_ref,
                    wgs_ref, wus_ref, wds_ref, wgb_ref, wub_ref, wdb_ref, sem):
    i = pl.program_id(0)
    valid = i < nb_ref[0]
    e = be_ref[i]
    changed = jnp.logical_or(i == 0, e != be_ref[jnp.maximum(i - 1, 0)])

    def fetch(ex):
        return (pltpu.make_async_copy(wg_hbm.at[ex], wgs_ref, sem.at[0]),
                pltpu.make_async_copy(wu_hbm.at[ex], wus_ref, sem.at[1]),
                pltpu.make_async_copy(wd_hbm.at[ex], wds_ref, sem.at[2]))

    @pl.when(i == 0)
    def _():
        for c in fetch(e):
            c.start()

    @pl.when(jnp.logical_and(valid, changed))
    def _():
        for c in fetch(e):
            c.wait()
        wgb_ref[...] = wgs_ref[...].astype(_BF16)
        wub_ref[...] = wus_ref[...].astype(_BF16)
        wdb_ref[...] = wds_ref[...].astype(_BF16)
        nxt = nxt_ref[i]

        @pl.when(nxt >= 0)
        def _():
            for c in fetch(nxt):
                c.start()

    @pl.when(valid)
    def _():
        lo, hi = _unpack_rows(xs_ref[...])
        lo = lo.astype(_BF16)
        hi = hi.astype(_BF16)
        g = _dot(lo, wgb_ref[0:HALF, :]) + _dot(hi, wgb_ref[HALF:D_MODEL, :])
        u = _dot(lo, wub_ref[0:HALF, :]) + _dot(hi, wub_ref[HALF:D_MODEL, :])
        hidden = (g * jax.nn.sigmoid(g) * u).astype(_BF16)
        ys_ref[...] = _pack_rows(_dot(hidden, wdb_ref[...]))

    @pl.when(jnp.logical_not(valid))
    def _():
        ys_ref[...] = jnp.zeros_like(ys_ref)


def _experts(xs, w_gate, w_up, w_down, block_e, next_e, nb_used):
    n_rows = xs.shape[0]
    nb = n_rows // MOE_BLK
    row_blk = lambda i, be, nx, nbu: (jnp.minimum(i, nbu[0] - 1), 0)
    any_spec = pl.BlockSpec(memory_space=pl.ANY)
    return pl.pallas_call(
        _experts_kernel,
        grid_spec=pltpu.PrefetchScalarGridSpec(
            num_scalar_prefetch=3,
            grid=(nb,),
            in_specs=[pl.BlockSpec((MOE_BLK, HALF), row_blk), any_spec, any_spec, any_spec],
            out_specs=pl.BlockSpec((MOE_BLK, HALF), lambda i, be, nx, nbu: (i, 0)),
            scratch_shapes=[pltpu.VMEM((D_MODEL, D_EXPERT), _F32),
                            pltpu.VMEM((D_MODEL, D_EXPERT), _F32),
                            pltpu.VMEM((D_EXPERT, D_MODEL), _F32),
                            pltpu.VMEM((D_MODEL, D_EXPERT), _BF16),
                            pltpu.VMEM((D_MODEL, D_EXPERT), _BF16),
                            pltpu.VMEM((D_EXPERT, D_MODEL), _BF16),
                            pltpu.SemaphoreType.DMA((3,))],
        ),
        out_shape=jax.ShapeDtypeStruct((n_rows, HALF), _U32),
        compiler_params=pltpu.CompilerParams(
            dimension_semantics=("arbitrary",), vmem_limit_bytes=VMEM_LIMIT),
        name="experts",
    )(block_e, next_e, nb_used, xs, w_gate, w_up, w_down)


def _combine_kernel(h1_ref, route_ref, dcur_ref, dnext_ref, ys_ref, p_ref, nple_ref,
                    wgate_ref, wple_ref, out_ref, ybuf_ref, h2_ref, xn_ref, pb_ref, sem):
    tk = h1_ref.shape[0]
    cw = D_MODEL // N_CHUNK
    per = TOP_K * tk // N_CHUNK
    i = pl.program_id(0)
    n_steps = pl.num_programs(0)
    slot = i % 2

    def row_copy(d, s, k, r):
        return pltpu.make_async_copy(ys_ref.at[pl.ds(d, 1)],
                                     ybuf_ref.at[s, k, pl.ds(r, 1)], sem.at[s])

    def wait_slot(s):
        for k in range(TOP_K):
            pltpu.make_async_copy(ys_ref.at[pl.ds(0, tk)], ybuf_ref.at[s, k], sem.at[s]).wait()

    @pl.when(i == 0)
    def _():
        def body(r, c):
            for k in range(TOP_K):
                row_copy(dcur_ref[0, 0, TOP_K * r + k], 0, k, r).start()
            return c
        lax.fori_loop(0, tk, body, 0, unroll=8)

    wait_slot(slot)
    y1 = jnp.concatenate(_unpack_rows(ybuf_ref[slot, 0]), axis=-1)
    y2 = jnp.concatenate(_unpack_rows(ybuf_ref[slot, 1]), axis=-1)
    route = route_ref[...]
    h2 = h1_ref[...] + route[:, 2:3] * y1 + route[:, 3:4] * y2
    xn = _rms(h2, nple_ref[...]).astype(_BF16)
    pb = p_ref[...].astype(_BF16)

    def issue_batch(c):
        for j in range(per):
            d = dnext_ref[0, 0, c * per + j]
            row_copy(d, 1 - slot, j % TOP_K, c * (per // TOP_K) + j // TOP_K).start()

    issue_batch(0)
    h2_ref[...] = h2
    xn_ref[...] = xn
    pb_ref[...] = pb
    for c in range(N_CHUNK):
        if c > 0:
            issue_batch(c)
        cols = slice(c * cw, (c + 1) * cw)
        gate = jax.nn.sigmoid(_dot(xn_ref[...], wgate_ref[:, cols]))
        emb = _dot(pb_ref[...], wple_ref[:, cols])
        out_ref[:, cols] = h2_ref[:, cols] + gate * emb

    @pl.when(i == n_steps - 1)
    def _():
        wait_slot(1 - slot)


def _combine(h1, route, dest3, ys, p2, nple, w_gate, w_ple):
    n = h1.shape[0]
    tk = TM_COMB
    steps = n // tk
    const2 = lambda i: (0, 0)
    return pl.pallas_call(
        _combine_kernel,
        grid=(steps,),
        in_specs=[
            pl.BlockSpec((tk, D_MODEL), lambda i: (i, 0)),
            pl.BlockSpec((tk, LANES), lambda i: (i, 0)),
            pl.BlockSpec((1, 1, TOP_K * tk), lambda i: (i, 0, 0), memory_space=pltpu.SMEM),
            pl.BlockSpec((1, 1, TOP_K * tk), lambda i: (jnp.minimum(i + 1, steps - 1), 0, 0),
                         memory_space=pltpu.SMEM),
            pl.BlockSpec(memory_space=pl.ANY),
            pl.BlockSpec((tk, PLE_DIM), lambda i: (i, 0)),
            pl.BlockSpec((1, D_MODEL), const2),
            pl.BlockSpec((D_MODEL, D_MODEL), const2, pipeline_mode=pl.Buffered(1)),
            pl.BlockSpec((PLE_DIM, D_MODEL), const2, pipeline_mode=pl.Buffered(1)),
        ],
        out_specs=pl.BlockSpec((tk, D_MODEL), lambda i: (i, 0)),
        out_shape=jax.ShapeDtypeStruct((n, D_MODEL), _F32),
        scratch_shapes=[pltpu.VMEM((2, TOP_K, tk, HALF), _U32),
                        pltpu.VMEM((tk, D_MODEL), _F32),
                        pltpu.VMEM((tk, D_MODEL), _BF16),
                        pltpu.VMEM((tk, PLE_DIM), _BF16),
                        pltpu.SemaphoreType.DMA((2,))],
        compiler_params=pltpu.CompilerParams(
            dimension_semantics=("arbitrary",), vmem_limit_bytes=VMEM_LIMIT),
        name="combine",
    )(h1, route, dest3, dest3, ys, p2, nple, w_gate, w_ple)


def _attention_tables(sinks):
    slopes = 2.0 ** (-8.0 * jnp.arange(1, N_HEADS + 1, dtype=_F32) / N_HEADS)
    qi = jnp.arange(ATTN_BLOCK)[:, None]
    kj = jnp.arange(ATTN_BLOCK)[None, :]
    dist = jnp.where(kj > qi, ATTN_BLOCK + qi - kj, qi - kj)
    tab = -slopes[:, None, None] * dist.astype(_F32)[None]
    tab = tab.reshape(N_KV_HEADS, 4 * ATTN_BLOCK, ATTN_BLOCK)
    sink = jnp.repeat(sinks.astype(_F32), ATTN_BLOCK).reshape(N_KV_HEADS, 4 * ATTN_BLOCK, 1)
    return tab, sink


def _layer(h, p_i, norm_mix, w_in, conv_w, q_norm, k_norm, sinks, out_norm_conv,
           out_norm_attn, w_out, norm_ffn, w_group, b_group, w_router, b_router,
           w_gate, w_up, w_down, norm_ple, w_ple_gate, w_ple):
    bsz, seq, _ = h.shape
    n = bsz * seq
    x2 = h.reshape(n, D_MODEL)
    row = lambda v: v.reshape(1, -1).astype(_F32)

    bu, q, kd, vd = _inproj(
        x2, row(norm_mix), w_in.astype(_BF16),
        row(jnp.tile(q_norm, 2)), row(jnp.tile(k_norm, 2)))

    tab, sink = _attention_tables(sinks)
    w_r = jnp.zeros((D_MODEL, LANES), _F32)
    w_r = w_r.at[:, :N_GROUPS].set(w_group).at[:, N_GROUPS:N_GROUPS + N_EXPERTS].set(w_router)
    r_bias = jnp.zeros((1, LANES), _F32)
    r_bias = r_bias.at[0, :N_GROUPS].set(b_group).at[0, N_GROUPS:N_GROUPS + N_EXPERTS].set(b_router)
    h1, xp, logits = _mixer(
        bu, q, kd, vd, x2, conv_w.astype(_F32), tab, sink, row(out_norm_conv),
        row(out_norm_attn), w_out.astype(_BF16), row(norm_ffn), w_r.astype(_BF16), r_bias, seq)

    route, counts = _route(logits)

    cnt = counts[0, :N_EXPERTS].astype(jnp.int32)
    padded = (cnt + MOE_BLK - 1) // MOE_BLK * MOE_BLK
    pad_end = jnp.cumsum(padded)
    pad_start = pad_end - padded
    eid = route[:, 0:TOP_K].astype(jnp.int32)
    onehot = eid[..., None] == jnp.arange(N_EXPERTS, dtype=jnp.int32)
    dest = (jnp.sum(jnp.where(onehot, pad_start, 0), axis=-1)
            + route[:, 4:4 + TOP_K].astype(jnp.int32))
    n_rows = n * TOP_K + N_EXPERTS * MOE_BLK
    nb = n_rows // MOE_BLK
    nb_used = (pad_end[-1] // MOE_BLK).astype(jnp.int32)
    blk_start = jnp.arange(nb, dtype=jnp.int32) * MOE_BLK
    block_e = jnp.sum((blk_start[:, None] >= pad_end[None, :]).astype(jnp.int32), axis=1)
    block_e = jnp.minimum(block_e, N_EXPERTS - 1)
    block_e = jnp.where(jnp.arange(nb) < nb_used, block_e, block_e[nb_used - 1])
    ids = jnp.arange(N_EXPERTS, dtype=jnp.int32)
    later = jnp.logical_and(ids[None, :] > ids[:, None], cnt[None, :] > 0)
    next_present = jnp.min(jnp.where(later, ids[None, :], N_EXPERTS), axis=1)
    next_present = jnp.where(next_present < N_EXPERTS, next_present, -1)
    onehot_b = block_e[:, None] == ids[None, :]
    next_e = jnp.sum(jnp.where(onehot_b, next_present[None, :], 0), axis=1).astype(jnp.int32)

    xs = _dispatch(xp, dest.reshape(n // TM_DISP, 1, TOP_K * TM_DISP),
                   pad_end.astype(jnp.int32), padded.astype(jnp.int32), n_rows)
    ys = _experts(xs, w_gate, w_up, w_down, block_e, next_e, nb_used.reshape(1))
    out = _combine(h1, route, dest.reshape(n // TM_COMB, 1, TOP_K * TM_COMB), ys,
                   p_i.reshape(n, PLE_DIM), row(norm_ple), w_ple_gate.astype(_BF16),
                   w_ple.astype(_BF16))
    return out.reshape(bsz, seq, D_MODEL)


def kernel(x, p, norm_mix, w_in, conv_w, q_norm, k_norm, sinks, out_norm_conv, out_norm_attn,
           w_out, norm_ffn, w_group, b_group, w_router, b_router, w_gate, w_up, w_down,
           norm_ple, w_ple_gate, w_ple):
    h = x
    for i in range(p.shape[0]):
        h = _layer(h, p[i], norm_mix[i], w_in[i], conv_w[i], q_norm[i], k_norm[i], sinks[i],
                   out_norm_conv[i], out_norm_attn[i], w_out[i], norm_ffn[i], w_group[i],
                   b_group[i], w_router[i], b_router[i], w_gate[i], w_up[i], w_down[i],
                   norm_ple[i], w_ple_gate[i], w_ple[i])
    return h
```

```python
import functools

import jax
import jax.numpy as jnp
from jax import lax
from jax.experimental import pallas as pl
from jax.experimental.pallas import tpu as pltpu

D_MODEL = 2048
D_CONV = 1024
N_HEADS = 16
N_KV_HEADS = 4
HEAD_DIM = 64
D_ATTN = N_HEADS * HEAD_DIM
D_KV = N_KV_HEADS * HEAD_DIM
D_IN = 3 * D_CONV + D_ATTN + 2 * D_KV
ATTN_BLOCK = 128
N_GROUPS = 4
EXPERTS_PER_GROUP = 8
N_EXPERTS = N_GROUPS * EXPERTS_PER_GROUP
TOP_K = 2
D_EXPERT = 512
PLE_DIM = 256
EPS = 1e-6
NEG_INF = -1e30

LANES = 128
HALF = D_MODEL // 2
KV_DUP = 2 * D_KV

TM_PROJ = 512
TM_MIX = 512
TM_ROUTE = 1024
TM_DISP = 512
TM_COMB = 512
MOE_BLK = 512
N_CHUNK = 8
VMEM_LIMIT = 56 * 1024 * 1024

_F32 = jnp.float32
_BF16 = jnp.bfloat16
_U32 = jnp.uint32
_HI_MASK = 0xFFFF0000


def _rms(x, gain):
    return x * lax.rsqrt(jnp.mean(x * x, axis=-1, keepdims=True) + EPS) * gain


def _dot(a, b):
    return jnp.dot(a, b, preferred_element_type=_F32)


def _pack_rows(x):
    h = x.shape[-1] // 2
    bits = lax.bitcast_convert_type(x.astype(_BF16).astype(_F32), _U32)
    return (bits[:, h:] & _U32(_HI_MASK)) | (bits[:, :h] >> 16)


def _unpack_rows(w):
    lo = lax.bitcast_convert_type(w << 16, _F32)
    hi = lax.bitcast_convert_type(w & _U32(_HI_MASK), _F32)
    return lo, hi


def _inproj_kernel(x_ref, g_ref, w_ref, qg_ref, kg_ref, bu_ref, q_ref, kd_ref, vd_ref):
    tm = x_ref.shape[0]
    xn = _rms(x_ref[...], g_ref[...]).astype(_BF16)
    ch = 512
    for j in range(D_CONV // ch):
        b = _dot(xn, w_ref[:, j * ch:(j + 1) * ch])
        c = _dot(xn, w_ref[:, D_CONV + j * ch:D_CONV + (j + 1) * ch])
        hc = _dot(xn, w_ref[:, 2 * D_CONV + j * ch:2 * D_CONV + (j + 1) * ch])
        bu_ref[:, j * ch:(j + 1) * ch] = b.astype(_BF16)
        bu_ref[:, D_CONV + j * ch:D_CONV + (j + 1) * ch] = (c * hc).astype(_BF16)

    lo = lax.broadcasted_iota(jnp.int32, (tm, LANES), 1) < HEAD_DIM
    q = _dot(xn, w_ref[:, 3 * D_CONV:3 * D_CONV + D_ATTN])
    scale = HEAD_DIM ** -0.5
    for pr in range(D_ATTN // LANES):
        qp = q[:, pr * LANES:(pr + 1) * LANES]
        sq = qp * qp
        s_lo = jnp.sum(jnp.where(lo, sq, 0.0), axis=-1, keepdims=True)
        s_hi = jnp.sum(jnp.where(lo, 0.0, sq), axis=-1, keepdims=True)
        ms = jnp.where(lo, s_lo, s_hi) * (1.0 / HEAD_DIM)
        qn = qp * lax.rsqrt(ms + EPS) * qg_ref[...] * scale
        q_ref[:, pr * LANES:(pr + 1) * LANES] = qn.astype(_BF16)

    kv = _dot(xn, w_ref[:, 3 * D_CONV + D_ATTN:D_IN])
    for pr in range(2 * D_KV // LANES):
        pair = kv[:, pr * LANES:(pr + 1) * LANES]
        swapped = pltpu.roll(pair, HEAD_DIM, 1)
        for half in range(2):
            dup = jnp.where(lo, pair, swapped) if half == 0 else jnp.where(lo, swapped, pair)
            head = 2 * pr + half
            if head < N_KV_HEADS:
                ms = jnp.sum(dup * dup, axis=-1, keepdims=True) * (1.0 / LANES)
                dup = dup * lax.rsqrt(ms + EPS) * kg_ref[...]
                kd_ref[:, head * LANES:(head + 1) * LANES] = dup.astype(_BF16)
            else:
                head -= N_KV_HEADS
                vd_ref[:, head * LANES:(head + 1) * LANES] = dup.astype(_BF16)


def _inproj(x2, g, w_in, qg2, kg2):
    n = x2.shape[0]
    tm = TM_PROJ
    const = lambda i: (0, 0)
    return pl.pallas_call(
        _inproj_kernel,
        grid=(n // tm,),
        in_specs=[
            pl.BlockSpec((tm, D_MODEL), lambda i: (i, 0)),
            pl.BlockSpec((1, D_MODEL), const),
            pl.BlockSpec((D_MODEL, D_IN), const, pipeline_mode=pl.Buffered(1)),
            pl.BlockSpec((1, LANES), const),
            pl.BlockSpec((1, LANES), const),
        ],
        out_specs=[
            pl.BlockSpec((tm, 2 * D_CONV), lambda i: (i, 0)),
            pl.BlockSpec((tm, D_ATTN), lambda i: (i, 0)),
            pl.BlockSpec((tm, KV_DUP), lambda i: (i, 0)),
            pl.BlockSpec((tm, KV_DUP), lambda i: (i, 0)),
        ],
        out_shape=[
            jax.ShapeDtypeStruct((n, 2 * D_CONV), _BF16),
            jax.ShapeDtypeStruct((n, D_ATTN), _BF16),
            jax.ShapeDtypeStruct((n, KV_DUP), _BF16),
            jax.ShapeDtypeStruct((n, KV_DUP), _BF16),
        ],
        compiler_params=pltpu.CompilerParams(
            dimension_semantics=("parallel",), vmem_limit_bytes=VMEM_LIMIT),
        name="inproj",
    )(x2, g, w_in, qg2, kg2)


def _mixer_kernel(tiles_per_seq,
                  bu_ref, uh_ref, q_ref, kd_ref, vd_ref, kdh_ref, vdh_ref, x_ref,
                  cw_ref, tab_ref, sink_ref, onc_ref, ona_ref, wout_ref, nffn_ref,
                  wr_ref, rb_ref,
                  h1_ref, xp_ref, lg_ref, ya_ref):
    tm = x_ref.shape[0]
    blk = ATTN_BLOCK
    is_first = (pl.program_id(0) % tiles_per_seq) == 0

    lo = lax.broadcasted_iota(jnp.int32, (blk, LANES), 1) < HEAD_DIM
    qi = lax.broadcasted_iota(jnp.int32, (4 * blk, blk), 0) % blk
    from_prev = lax.broadcasted_iota(jnp.int32, (4 * blk, blk), 1) > qi
    no_prev = jnp.logical_and(is_first, from_prev)
    zero = jnp.zeros((blk, LANES), _BF16)
    zero_p = jnp.zeros((4 * blk, blk), _BF16)
    for j in range(tm // blk):
        rows = slice(j * blk, (j + 1) * blk)
        for g in range(N_KV_HEADS):
            cols = slice(g * LANES, (g + 1) * LANES)
            qa = q_ref[rows, 2 * g * LANES:(2 * g + 1) * LANES]
            qb = q_ref[rows, (2 * g + 1) * LANES:(2 * g + 2) * LANES]
            lhs = jnp.concatenate(
                [jnp.where(lo, qa, zero), jnp.where(lo, zero, qa),
                 jnp.where(lo, qb, zero), jnp.where(lo, zero, qb)], axis=0)
            if j == 0:
                keys = jnp.concatenate([kdh_ref[:, cols], kd_ref[0:blk, cols]], axis=0)
                vals = jnp.concatenate([vdh_ref[:, cols], vd_ref[0:blk, cols]], axis=0)
            else:
                keys = kd_ref[(j - 1) * blk:(j + 1) * blk, cols]
                vals = vd_ref[(j - 1) * blk:(j + 1) * blk, cols]
            s = lax.dot_general(lhs, keys, (((1,), (1,)), ((), ())),
                                preferred_element_type=_F32)
            s = jnp.where(from_prev, s[:, 0:blk], s[:, blk:2 * blk]) + tab_ref[g]
            if j == 0:
                s = jnp.where(no_prev, NEG_INF, s)
            sink = sink_ref[g]
            m = jnp.maximum(jnp.max(s, axis=-1, keepdims=True), sink)
            p = jnp.exp(s - m)
            denom = jnp.sum(p, axis=-1, keepdims=True) + jnp.exp(sink - m)
            p = p.astype(_BF16)
            p = jnp.concatenate([jnp.where(from_prev, p, zero_p),
                                 jnp.where(from_prev, zero_p, p)], axis=1)
            o = _dot(p, vals) / denom
            ya_ref[rows, 2 * g * LANES:(2 * g + 1) * LANES] = jnp.where(
                lo, o[0:blk], o[blk:2 * blk])
            ya_ref[rows, (2 * g + 1) * LANES:(2 * g + 2) * LANES] = jnp.where(
                lo, o[2 * blk:3 * blk], o[3 * blk:4 * blk])

    b = bu_ref[:, 0:D_CONV].astype(_F32)
    u = bu_ref[:, D_CONV:2 * D_CONV].astype(_F32)
    uh = jnp.where(is_first, 0.0, uh_ref[...].astype(_F32))
    row = lax.broadcasted_iota(jnp.int32, (tm, D_CONV), 0)
    u1 = jnp.where(row == 0, uh[15:16], pltpu.roll(u, 1, 0))
    u2 = jnp.where(row == 0, uh[14:15],
                   jnp.where(row == 1, uh[15:16], pltpu.roll(u, 2, 0)))
    z = cw_ref[2:3, :] * u + cw_ref[1:2, :] * u1 + cw_ref[0:1, :] * u2
    yc = _rms(b * z, onc_ref[...])
    ya = _rms(ya_ref[...], ona_ref[...])
    mixed = jnp.concatenate([yc, ya], axis=-1).astype(_BF16)
    h1 = x_ref[...] + _dot(mixed, wout_ref[...])
    h1_ref[...] = h1

    xn = _rms(h1, nffn_ref[...])
    xp_ref[...] = _pack_rows(xn)
    lg_ref[...] = _dot(xn.astype(_BF16), wr_ref[...]) + rb_ref[...]


def _mixer(bu, q, kd, vd, x2, cw, tab, sink, onc, ona, w_out, nffn, wr, rb, seq):
    n = x2.shape[0]
    tm = TM_MIX
    const2 = lambda i: (0, 0)
    const3 = lambda i: (0, 0, 0)
    prev_blk = lambda i: (jnp.maximum(i * (tm // ATTN_BLOCK) - 1, 0), 0)
    return pl.pallas_call(
        functools.partial(_mixer_kernel, seq // tm),
        grid=(n // tm,),
        in_specs=[
            pl.BlockSpec((tm, 2 * D_CONV), lambda i: (i, 0)),
            pl.BlockSpec((16, D_CONV), lambda i: (jnp.maximum(i * (tm // 16) - 1, 0), 1)),
            pl.BlockSpec((tm, D_ATTN), lambda i: (i, 0)),
            pl.BlockSpec((tm, KV_DUP), lambda i: (i, 0)),
            pl.BlockSpec((tm, KV_DUP), lambda i: (i, 0)),
            pl.BlockSpec((ATTN_BLOCK, KV_DUP), prev_blk),
            pl.BlockSpec((ATTN_BLOCK, KV_DUP), prev_blk),
            pl.BlockSpec((tm, D_MODEL), lambda i: (i, 0)),
            pl.BlockSpec((3, D_CONV), const2),
            pl.BlockSpec((N_KV_HEADS, 4 * ATTN_BLOCK, ATTN_BLOCK), const3,
                         pipeline_mode=pl.Buffered(1)),
            pl.BlockSpec((N_KV_HEADS, 4 * ATTN_BLOCK, 1), const3, pipeline_mode=pl.Buffered(1)),
            pl.BlockSpec((1, D_CONV), const2),
            pl.BlockSpec((1, D_ATTN), const2),
            pl.BlockSpec((D_MODEL, D_MODEL), const2, pipeline_mode=pl.Buffered(1)),
            pl.BlockSpec((1, D_MODEL), const2),
            pl.BlockSpec((D_MODEL, LANES), const2),
            pl.BlockSpec((1, LANES), const2),
        ],
        out_specs=[
            pl.BlockSpec((tm, D_MODEL), lambda i: (i, 0)),
            pl.BlockSpec((tm, HALF), lambda i: (i, 0)),
            pl.BlockSpec((tm, LANES), lambda i: (i, 0)),
        ],
        out_shape=[
            jax.ShapeDtypeStruct((n, D_MODEL), _F32),
            jax.ShapeDtypeStruct((n, HALF), _U32),
            jax.ShapeDtypeStruct((n, LANES), _F32),
        ],
        scratch_shapes=[pltpu.VMEM((tm, D_ATTN), _F32)],
        compiler_params=pltpu.CompilerParams(
            dimension_semantics=("parallel",), vmem_limit_bytes=VMEM_LIMIT),
        name="mixer",
    )(bu, bu, q, kd, vd, kd, vd, x2, cw, tab, sink, onc, ona, w_out, nffn, wr, rb)


def _route_kernel(lg_ref, route_ref, route_t_ref, cnt_ref, run_ref):
    t = lg_ref.shape[0]

    @pl.when(pl.program_id(0) == 0)
    def _():
        run_ref[...] = jnp.zeros_like(run_ref)

    lg = lg_ref[...]
    lane = lax.broadcasted_iota(jnp.int32, (t, LANES), 1)
    ninf = -jnp.inf

    lane_f = lane.astype(_F32)

    def first_max(v):
        m = jnp.max(v, axis=-1, keepdims=True)
        idx = jnp.min(jnp.where(v == m, lane_f, float(LANES)), axis=-1, keepdims=True)
        return m, idx.astype(jnp.int32)

    is_g = lane < N_GROUPS
    gmax, gidx = first_max(jnp.where(is_g, lg, ninf))
    gsum = jnp.sum(jnp.where(is_g, jnp.exp(jnp.where(is_g, lg, gmax) - gmax), 0.0),
                   axis=-1, keepdims=True)
    g_w = 1.0 / gsum
    base = N_GROUPS + EXPERTS_PER_GROUP * gidx
    in_grp = jnp.logical_and(lane >= base, lane < base + EXPERTS_PER_GROUP)
    el = jnp.where(in_grp, lg, ninf)
    m1, i1 = first_max(el)
    m2, i2 = first_max(jnp.where(lane == i1, ninf, el))
    e21 = jnp.exp(m2 - m1)
    w1 = g_w / (1.0 + e21)
    w2 = g_w * e21 / (1.0 + e21)
    e1 = i1 - N_GROUPS
    e2 = i2 - N_GROUPS

    oh1 = lane == e1
    oh2 = lane == e2
    earlier = (lax.broadcasted_iota(jnp.int32, (t, t), 0)
               > lax.broadcasted_iota(jnp.int32, (t, t), 1)).astype(_BF16)
    c1 = _dot(earlier, oh1.astype(_BF16))
    c2 = _dot(earlier, oh2.astype(_BF16))
    tot1 = jnp.sum(oh1.astype(_F32), axis=0, keepdims=True)
    tot2 = jnp.sum(oh2.astype(_F32), axis=0, keepdims=True)
    run = run_ref[0:1, :]
    r1 = jnp.sum(jnp.where(oh1, run + c1, 0.0), axis=-1, keepdims=True)
    r2 = jnp.sum(jnp.where(oh2, run + tot1 + c2, 0.0), axis=-1, keepdims=True)
    run_ref[...] = jnp.broadcast_to(run + tot1 + tot2, run_ref.shape)
    cnt_ref[...] = run_ref[...]

    out = jnp.where(lane == 0, e1.astype(_F32), 0.0)
    out = jnp.where(lane == 1, e2.astype(_F32), out)
    out = jnp.where(lane == 2, w1, out)
    out = jnp.where(lane == 3, w2, out)
    out = jnp.where(lane == 4, r1, out)
    out = jnp.where(lane == 5, r2, out)
    route_ref[...] = out
    route_t_ref[...] = out.T[0:8, :]


def _route(logits):
    n = logits.shape[0]
    t = TM_ROUTE
    return pl.pallas_call(
        _route_kernel,
        grid=(n // t,),
        in_specs=[pl.BlockSpec((t, LANES), lambda i: (i, 0))],
        out_specs=[pl.BlockSpec((t, LANES), lambda i: (i, 0)),
                   pl.BlockSpec((8, t), lambda i: (0, i)),
                   pl.BlockSpec((8, LANES), lambda i: (0, 0))],
        out_shape=[jax.ShapeDtypeStruct((n, LANES), _F32),
                   jax.ShapeDtypeStruct((8, n), _F32),
                   jax.ShapeDtypeStruct((8, LANES), _F32)],
        scratch_shapes=[pltpu.VMEM((8, LANES), _F32)],
        compiler_params=pltpu.CompilerParams(dimension_semantics=("arbitrary",)),
        name="route",
    )(logits)


def _dispatch_kernel(pad_end_ref, padded_ref, dest_ref, xp_ref, xs_ref,
                     buf_ref, zero_ref, lsem, rsem, zsem):
    td = buf_ref.shape[1]
    nb = xs_ref.shape[0] // MOE_BLK
    nb_used = pad_end_ref[N_EXPERTS - 1] // MOE_BLK
    i = pl.program_id(0)
    n_steps = pl.num_programs(0)
    slot = i % 3
    nxt = (i + 1) % 3
    prv = (i + 2) % 3

    def zero_copy(start):
        start = pl.multiple_of(start, MOE_BLK)
        return pltpu.make_async_copy(zero_ref, xs_ref.at[pl.ds(start, MOE_BLK)], zsem)

    def zero_blocks(fn):
        for e in range(N_EXPERTS):
            @pl.when(padded_ref[e] > 0)
            def _():
                fn(zero_copy(pad_end_ref[e] - MOE_BLK))
        for b in range(nb - N_EXPERTS, nb):
            @pl.when(b >= nb_used)
            def _():
                fn(zero_copy(b * MOE_BLK))

    def load(t, s):
        start = pl.multiple_of(t * td, td)
        return pltpu.make_async_copy(xp_ref.at[pl.ds(start, td)], buf_ref.at[s], lsem.at[s])

    def wait_rows(s):
        for k in range(TOP_K):
            pltpu.make_async_copy(buf_ref.at[s], xs_ref.at[pl.ds(0, td)], rsem.at[s]).wait()

    @pl.when(i == 0)
    def _():
        zero_ref[...] = jnp.zeros_like(zero_ref)
        zero_blocks(lambda c: c.start())
        zero_blocks(lambda c: c.wait())
        load(0, 0).start()

    @pl.when(i + 1 < n_steps)
    def _():
        load(i + 1, nxt).start()

    load(i, slot).wait()

    def issue(r, c):
        for k in range(TOP_K):
            d = dest_ref[0, 0, k * td + r]
            pltpu.make_async_copy(buf_ref.at[slot, pl.ds(r, 1)], xs_ref.at[pl.ds(d, 1)],
                                  rsem.at[slot]).start()
        return c

    lax.fori_loop(0, td, issue, 0, unroll=8)

    @pl.when(i > 0)
    def _():
        wait_rows(prv)

    @pl.when(i == n_steps - 1)
    def _():
        wait_rows(slot)


def _dispatch(xp, dest3, pad_end, padded, n_rows):
    steps = dest3.shape[0]
    td = dest3.shape[-1] // TOP_K
    return pl.pallas_call(
        _dispatch_kernel,
        grid_spec=pltpu.PrefetchScalarGridSpec(
            num_scalar_prefetch=2,
            grid=(steps,),
            in_specs=[
                pl.BlockSpec((1, 1, dest3.shape[-1]), lambda i, pe, pd: (i, 0, 0),
                             memory_space=pltpu.SMEM),
                pl.BlockSpec(memory_space=pl.ANY),
            ],
            out_specs=pl.BlockSpec(memory_space=pl.ANY),
            scratch_shapes=[pltpu.VMEM((3, td, HALF), _U32),
                            pltpu.VMEM((MOE_BLK, HALF), _U32),
                            pltpu.SemaphoreType.DMA((3,)), pltpu.SemaphoreType.DMA((3,)),
                            pltpu.SemaphoreType.DMA],
        ),
        out_shape=jax.ShapeDtypeStruct((n_rows, HALF), _U32),
        compiler_params=pltpu.CompilerParams(
            dimension_semantics=("arbitrary",)),
        name="dispatch",
    )(pad_end, padded, dest3, xp)


def _experts_kernel(be_ref, nxt_ref, nb_ref, xs_ref, wg_hbm, wu_hbm, wd_hbm, ys_ref,
                    wgs_ref, wus_ref, wds_ref, wgb_ref, wub_ref, wdb_ref, sem):
    i = pl.program_id(0)
    valid = i < nb_ref[0]
    e = be_ref[i]
    changed = jnp.logical_or(i == 0, e != be_ref[jnp.maximum(i - 1, 0)])

    def fetch(ex):
        return (pltpu.make_async_copy(wg_hbm.at[ex], wgs_ref, sem.at[0]),
                pltpu.make_async_copy(wu_hbm.at[ex], wus_ref, sem.at[1]),
                pltpu.make_async_copy(wd_hbm.at[ex], wds_ref, sem.at[2]))

    @pl.when(i == 0)
    def _():
        for c in fetch(e):
            c.start()

    @pl.when(jnp.logical_and(valid, changed))
    def _():
        for c in fetch(e):
            c.wait()
        wgb_ref[...] = wgs_ref[...].astype(_BF16)
        wub_ref[...] = wus_ref[...].astype(_BF16)
        wdb_ref[...] = wds_ref[...].astype(_BF16)
        nxt = nxt_ref[i]

        @pl.when(nxt >= 0)
        def _():
            for c in fetch(nxt):
                c.start()

    @pl.when(valid)
    def _():
        lo, hi = _unpack_rows(xs_ref[...])
        lo = lo.astype(_BF16)
        hi = hi.astype(_BF16)
        g = _dot(lo, wgb_ref[0:HALF, :]) + _dot(hi, wgb_ref[HALF:D_MODEL, :])
        u = _dot(lo, wub_ref[0:HALF, :]) + _dot(hi, wub_ref[HALF:D_MODEL, :])
        hidden = (g * jax.nn.sigmoid(g) * u).astype(_BF16)
        ys_ref[...] = _pack_rows(_dot(hidden, wdb_ref[...]))

    @pl.when(jnp.logical_not(valid))
    def _():
        ys_ref[...] = jnp.zeros_like(ys_ref)


def _experts(xs, w_gate, w_up, w_down, block_e, next_e, nb_used):
    n_rows = xs.shape[0]
    nb = n_rows // MOE_BLK
    row_blk = lambda i, be, nx, nbu: (jnp.minimum(i, nbu[0] - 1), 0)
    any_spec = pl.BlockSpec(memory_space=pl.ANY)
    return pl.pallas_call(
        _experts_kernel,
        grid_spec=pltpu.PrefetchScalarGridSpec(
            num_scalar_prefetch=3,
            grid=(nb,),
            in_specs=[pl.BlockSpec((MOE_BLK, HALF), row_blk), any_spec, any_spec, any_spec],
            out_specs=pl.BlockSpec((MOE_BLK, HALF), lambda i, be, nx, nbu: (i, 0)),
            scratch_shapes=[pltpu.VMEM((D_MODEL, D_EXPERT), _F32),
                            pltpu.VMEM((D_MODEL, D_EXPERT), _F32),
                            pltpu.VMEM((D_EXPERT, D_MODEL), _F32),
                            pltpu.VMEM((D_MODEL, D_EXPERT), _BF16),
                            pltpu.VMEM((D_MODEL, D_EXPERT), _BF16),
                            pltpu.VMEM((D_EXPERT, D_MODEL), _BF16),
                            pltpu.SemaphoreType.DMA((3,))],
        ),
        out_shape=jax.ShapeDtypeStruct((n_rows, HALF), _U32),
        compiler_params=pltpu.CompilerParams(
            dimension_semantics=("arbitrary",), vmem_limit_bytes=VMEM_LIMIT),
        name="experts",
    )(block_e, next_e, nb_used, xs, w_gate, w_up, w_down)


def _combine_kernel(h1_ref, route_ref, dcur_ref, dnext_ref, ys_ref, p_ref, nple_ref,
                    wgate_ref, wple_ref, out_ref, ybuf_ref, h2_ref, xn_ref, pb_ref, sem):
    tk = h1_ref.shape[0]
    cw = D_MODEL // N_CHUNK
    per = TOP_K * tk // N_CHUNK
    i = pl.program_id(0)
    n_steps = pl.num_programs(0)
    slot = i % 2

    def row_copy(d, s, k, r):
        return pltpu.make_async_copy(ys_ref.at[pl.ds(d, 1)],
                                     ybuf_ref.at[s, k, pl.ds(r, 1)], sem.at[s])

    def wait_slot(s):
        for k in range(TOP_K):
            pltpu.make_async_copy(ys_ref.at[pl.ds(0, tk)], ybuf_ref.at[s, k], sem.at[s]).wait()

    @pl.when(i == 0)
    def _():
        def body(r, c):
            for k in range(TOP_K):
                row_copy(dcur_ref[0, 0, k * tk + r], 0, k, r).start()
            return c
        lax.fori_loop(0, tk, body, 0, unroll=8)

    wait_slot(slot)
    y1 = jnp.concatenate(_unpack_rows(ybuf_ref[slot, 0]), axis=-1)
    y2 = jnp.concatenate(_unpack_rows(ybuf_ref[slot, 1]), axis=-1)
    route = route_ref[...]
    h2 = h1_ref[...] + route[:, 2:3] * y1 + route[:, 3:4] * y2
    xn = _rms(h2, nple_ref[...]).astype(_BF16)
    pb = p_ref[...].astype(_BF16)

    def issue_batch(c):
        for j in range(per):
            k, r = j % TOP_K, c * (per // TOP_K) + j // TOP_K
            row_copy(dnext_ref[0, 0, k * tk + r], 1 - slot, k, r).start()

    issue_batch(0)
    h2_ref[...] = h2
    xn_ref[...] = xn
    pb_ref[...] = pb
    for c in range(N_CHUNK):
        if c > 0:
            issue_batch(c)
        cols = slice(c * cw, (c + 1) * cw)
        gate = jax.nn.sigmoid(_dot(xn_ref[...], wgate_ref[:, cols]))
        emb = _dot(pb_ref[...], wple_ref[:, cols])
        out_ref[:, cols] = h2_ref[:, cols] + gate * emb

    @pl.when(i == n_steps - 1)
    def _():
        wait_slot(1 - slot)


def _combine(h1, route, dest3, ys, p2, nple, w_gate, w_ple):
    n = h1.shape[0]
    tk = TM_COMB
    steps = n // tk
    const2 = lambda i: (0, 0)
    return pl.pallas_call(
        _combine_kernel,
        grid=(steps,),
        in_specs=[
            pl.BlockSpec((tk, D_MODEL), lambda i: (i, 0)),
            pl.BlockSpec((tk, LANES), lambda i: (i, 0)),
            pl.BlockSpec((1, 1, TOP_K * tk), lambda i: (i, 0, 0), memory_space=pltpu.SMEM),
            pl.BlockSpec((1, 1, TOP_K * tk), lambda i: (jnp.minimum(i + 1, steps - 1), 0, 0),
                         memory_space=pltpu.SMEM),
            pl.BlockSpec(memory_space=pl.ANY),
            pl.BlockSpec((tk, PLE_DIM), lambda i: (i, 0)),
            pl.BlockSpec((1, D_MODEL), const2),
            pl.BlockSpec((D_MODEL, D_MODEL), const2, pipeline_mode=pl.Buffered(1)),
            pl.BlockSpec((PLE_DIM, D_MODEL), const2, pipeline_mode=pl.Buffered(1)),
        ],
        out_specs=pl.BlockSpec((tk, D_MODEL), lambda i: (i, 0)),
        out_shape=jax.ShapeDtypeStruct((n, D_MODEL), _F32),
        scratch_shapes=[pltpu.VMEM((2, TOP_K, tk, HALF), _U32),
                        pltpu.VMEM((tk, D_MODEL), _F32),
                        pltpu.VMEM((tk, D_MODEL), _BF16),
                        pltpu.VMEM((tk, PLE_DIM), _BF16),
                        pltpu.SemaphoreType.DMA((2,))],
        compiler_params=pltpu.CompilerParams(
            dimension_semantics=("arbitrary",), vmem_limit_bytes=VMEM_LIMIT),
        name="combine",
    )(h1, route, dest3, dest3, ys, p2, nple, w_gate, w_ple)


def _attention_tables(sinks):
    slopes = 2.0 ** (-8.0 * jnp.arange(1, N_HEADS + 1, dtype=_F32) / N_HEADS)
    qi = jnp.arange(ATTN_BLOCK)[:, None]
    kj = jnp.arange(ATTN_BLOCK)[None, :]
    dist = jnp.where(kj > qi, ATTN_BLOCK + qi - kj, qi - kj)
    tab = -slopes[:, None, None] * dist.astype(_F32)[None]
    tab = tab.reshape(N_KV_HEADS, 4 * ATTN_BLOCK, ATTN_BLOCK)
    sink = jnp.repeat(sinks.astype(_F32), ATTN_BLOCK).reshape(N_KV_HEADS, 4 * ATTN_BLOCK, 1)
    return tab, sink


def _layer(h, p_i, norm_mix, w_in, conv_w, q_norm, k_norm, sinks, out_norm_conv,
           out_norm_attn, w_out, norm_ffn, w_group, b_group, w_router, b_router,
           w_gate, w_up, w_down, norm_ple, w_ple_gate, w_ple):
    bsz, seq, _ = h.shape
    n = bsz * seq
    x2 = h.reshape(n, D_MODEL)
    row = lambda v: v.reshape(1, -1).astype(_F32)

    bu, q, kd, vd = _inproj(
        x2, row(norm_mix), w_in.astype(_BF16),
        row(jnp.tile(q_norm, 2)), row(jnp.tile(k_norm, 2)))

    tab, sink = _attention_tables(sinks)
    w_r = jnp.zeros((D_MODEL, LANES), _F32)
    w_r = w_r.at[:, :N_GROUPS].set(w_group).at[:, N_GROUPS:N_GROUPS + N_EXPERTS].set(w_router)
    r_bias = jnp.zeros((1, LANES), _F32)
    r_bias = r_bias.at[0, :N_GROUPS].set(b_group).at[0, N_GROUPS:N_GROUPS + N_EXPERTS].set(b_router)
    h1, xp, logits = _mixer(
        bu, q, kd, vd, x2, conv_w.astype(_F32), tab, sink, row(out_norm_conv),
        row(out_norm_attn), w_out.astype(_BF16), row(norm_ffn), w_r.astype(_BF16), r_bias, seq)

    route, route_t, counts = _route(logits)

    cnt = counts[0, :N_EXPERTS].astype(jnp.int32)
    padded = (cnt + MOE_BLK - 1) // MOE_BLK * MOE_BLK
    pad_end = jnp.cumsum(padded)
    pad_start = pad_end - padded
    eid = route_t[0:TOP_K].astype(jnp.int32)
    onehot = eid[..., None] == jnp.arange(N_EXPERTS, dtype=jnp.int32)
    dest = (jnp.sum(jnp.where(onehot, pad_start, 0), axis=-1)
            + route_t[4:4 + TOP_K].astype(jnp.int32))

    def step_blocks(tile):
        return dest.reshape(TOP_K, n // tile, 1, tile).transpose(1, 2, 0, 3).reshape(
            n // tile, 1, TOP_K * tile)
    n_rows = n * TOP_K + N_EXPERTS * MOE_BLK
    nb = n_rows // MOE_BLK
    nb_used = (pad_end[-1] // MOE_BLK).astype(jnp.int32)
    blk_start = jnp.arange(nb, dtype=jnp.int32) * MOE_BLK
    block_e = jnp.sum((blk_start[:, None] >= pad_end[None, :]).astype(jnp.int32), axis=1)
    block_e = jnp.minimum(block_e, N_EXPERTS - 1)
    block_e = jnp.where(jnp.arange(nb) < nb_used, block_e, block_e[nb_used - 1])
    ids = jnp.arange(N_EXPERTS, dtype=jnp.int32)
    later = jnp.logical_and(ids[None, :] > ids[:, None], cnt[None, :] > 0)
    next_present = jnp.min(jnp.where(later, ids[None, :], N_EXPERTS), axis=1)
    next_present = jnp.where(next_present < N_EXPERTS, next_present, -1)
    onehot_b = block_e[:, None] == ids[None, :]
    next_e = jnp.sum(jnp.where(onehot_b, next_present[None, :], 0), axis=1).astype(jnp.int32)

    xs = _dispatch(xp, step_blocks(TM_DISP),
                   pad_end.astype(jnp.int32), padded.astype(jnp.int32), n_rows)
    ys = _experts(xs, w_gate, w_up, w_down, block_e, next_e, nb_used.reshape(1))
    out = _combine(h1, route, step_blocks(TM_COMB), ys,
                   p_i.reshape(n, PLE_DIM), row(norm_ple), w_ple_gate.astype(_BF16),
                   w_ple.astype(_BF16))
    return out.reshape(bsz, seq, D_MODEL)


def kernel(x, p, norm_mix, w_in, conv_w, q_norm, k_norm, sinks, out_norm_conv, out_norm_attn,
           w_out, norm_ffn, w_group, b_group, w_router, b_router, w_gate, w_up, w_down,
           norm_ple, w_ple_gate, w_ple):
    h = x
    for i in range(p.shape[0]):
        h = _layer(h, p[i], norm_mix[i], w_in[i], conv_w[i], q_norm[i], k_norm[i], sinks[i],
                   out_norm_conv[i], out_norm_attn[i], w_out[i], norm_ffn[i], w_group[i],
                   b_group[i], w_router[i], b_router[i], w_gate[i], w_up[i], w_down[i],
                   norm_ple[i], w_ple_gate[i], w_ple[i])
    return h
```

```python
import functools

import jax
import jax.numpy as jnp
from jax import lax
from jax.experimental import pallas as pl
from jax.experimental.pallas import tpu as pltpu

D_MODEL = 2048
D_CONV = 1024
N_HEADS = 16
N_KV_HEADS = 4
HEAD_DIM = 64
D_ATTN = N_HEADS * HEAD_DIM
D_KV = N_KV_HEADS * HEAD_DIM
D_IN = 3 * D_CONV + D_ATTN + 2 * D_KV
ATTN_BLOCK = 128
N_GROUPS = 4
EXPERTS_PER_GROUP = 8
N_EXPERTS = N_GROUPS * EXPERTS_PER_GROUP
TOP_K = 2
D_EXPERT = 512
PLE_DIM = 256
EPS = 1e-6
NEG_INF = -1e30

LANES = 128
HALF = D_MODEL // 2
KV_DUP = 2 * D_KV

TM_PROJ = 512
TM_MIX = 512
TM_ROUTE = 1024
TM_DISP = 512
TM_COMB = 512
MOE_BLK = 512
N_CHUNK = 8
VMEM_LIMIT = 56 * 1024 * 1024

_F32 = jnp.float32
_BF16 = jnp.bfloat16
_U32 = jnp.uint32
_HI_MASK = 0xFFFF0000


def _rms(x, gain):
    return x * lax.rsqrt(jnp.mean(x * x, axis=-1, keepdims=True) + EPS) * gain


def _dot(a, b):
    return jnp.dot(a, b, preferred_element_type=_F32)


def _pack_rows(x):
    h = x.shape[-1] // 2
    bits = lax.bitcast_convert_type(x.astype(_BF16).astype(_F32), _U32)
    return (bits[:, h:] & _U32(_HI_MASK)) | (bits[:, :h] >> 16)


def _unpack_rows(w):
    lo = lax.bitcast_convert_type(w << 16, _F32)
    hi = lax.bitcast_convert_type(w & _U32(_HI_MASK), _F32)
    return lo, hi


def _inproj_kernel(tiles_per_seq, x_ref, g_ref, w_ref, qg_ref, kg_ref, cw_ref, onc_ref,
                   yc_ref, q_ref, kd_ref, vd_ref, t_ref, uprev_ref):
    tm = x_ref.shape[0]
    is_first = (pl.program_id(0) % tiles_per_seq) == 0
    xn = _rms(x_ref[...], g_ref[...]).astype(_BF16)

    @pl.when(pl.program_id(0) == 0)
    def _():
        uprev_ref[...] = jnp.zeros_like(uprev_ref)

    lo = lax.broadcasted_iota(jnp.int32, (tm, LANES), 1) < HEAD_DIM
    q = _dot(xn, w_ref[:, 3 * D_CONV:3 * D_CONV + D_ATTN])
    scale = HEAD_DIM ** -0.5
    for pr in range(D_ATTN // LANES):
        qp = q[:, pr * LANES:(pr + 1) * LANES]
        sq = qp * qp
        s_lo = jnp.sum(jnp.where(lo, sq, 0.0), axis=-1, keepdims=True)
        s_hi = jnp.sum(jnp.where(lo, 0.0, sq), axis=-1, keepdims=True)
        ms = jnp.where(lo, s_lo, s_hi) * (1.0 / HEAD_DIM)
        qn = qp * lax.rsqrt(ms + EPS) * qg_ref[...] * scale
        q_ref[:, pr * LANES:(pr + 1) * LANES] = qn.astype(_BF16)

    kv = _dot(xn, w_ref[:, 3 * D_CONV + D_ATTN:D_IN])
    for pr in range(2 * D_KV // LANES):
        pair = kv[:, pr * LANES:(pr + 1) * LANES]
        swapped = pltpu.roll(pair, HEAD_DIM, 1)
        for half in range(2):
            dup = jnp.where(lo, pair, swapped) if half == 0 else jnp.where(lo, swapped, pair)
            head = 2 * pr + half
            if head < N_KV_HEADS:
                ms = jnp.sum(dup * dup, axis=-1, keepdims=True) * (1.0 / LANES)
                dup = dup * lax.rsqrt(ms + EPS) * kg_ref[...]
                kd_ref[:, head * LANES:(head + 1) * LANES] = dup.astype(_BF16)
            else:
                head -= N_KV_HEADS
                vd_ref[:, head * LANES:(head + 1) * LANES] = dup.astype(_BF16)

    ch = 256
    row = lax.broadcasted_iota(jnp.int32, (tm, ch), 0)
    last = uprev_ref.shape[0] - 1
    ssq = jnp.zeros((tm, 1), _F32)
    for j in range(D_CONV // ch):
        cs = slice(j * ch, (j + 1) * ch)
        b = _dot(xn, w_ref[:, j * ch:(j + 1) * ch])
        c = _dot(xn, w_ref[:, D_CONV + j * ch:D_CONV + (j + 1) * ch])
        hc = _dot(xn, w_ref[:, 2 * D_CONV + j * ch:2 * D_CONV + (j + 1) * ch])
        u = c * hc
        uh = jnp.where(is_first, 0.0, uprev_ref[:, cs])
        uprev_ref[:, cs] = u[tm - last - 1:tm]
        u1 = jnp.where(row == 0, uh[last:last + 1], pltpu.roll(u, 1, 0))
        u2 = jnp.where(row == 0, uh[last - 1:last],
                       jnp.where(row == 1, uh[last:last + 1], pltpu.roll(u, 2, 0)))
        t = b * (cw_ref[2:3, cs] * u + cw_ref[1:2, cs] * u1 + cw_ref[0:1, cs] * u2)
        ssq = ssq + jnp.sum(t * t, axis=-1, keepdims=True)
        t_ref[:, cs] = t
    inv = lax.rsqrt(ssq * (1.0 / D_CONV) + EPS)
    yc_ref[...] = (t_ref[...] * inv * onc_ref[...]).astype(_BF16)


def _inproj(x2, g, w_in, qg2, kg2, cw, onc, seq):
    n = x2.shape[0]
    tm = TM_PROJ
    const = lambda i: (0, 0)
    return pl.pallas_call(
        functools.partial(_inproj_kernel, seq // tm),
        grid=(n // tm,),
        in_specs=[
            pl.BlockSpec((tm, D_MODEL), lambda i: (i, 0)),
            pl.BlockSpec((1, D_MODEL), const),
            pl.BlockSpec((D_MODEL, D_IN), const, pipeline_mode=pl.Buffered(1)),
            pl.BlockSpec((1, LANES), const),
            pl.BlockSpec((1, LANES), const),
            pl.BlockSpec((3, D_CONV), const),
            pl.BlockSpec((1, D_CONV), const),
        ],
        out_specs=[
            pl.BlockSpec((tm, D_CONV), lambda i: (i, 0)),
            pl.BlockSpec((tm, D_ATTN), lambda i: (i, 0)),
            pl.BlockSpec((tm, KV_DUP), lambda i: (i, 0)),
            pl.BlockSpec((tm, KV_DUP), lambda i: (i, 0)),
        ],
        out_shape=[
            jax.ShapeDtypeStruct((n, D_CONV), _BF16),
            jax.ShapeDtypeStruct((n, D_ATTN), _BF16),
            jax.ShapeDtypeStruct((n, KV_DUP), _BF16),
            jax.ShapeDtypeStruct((n, KV_DUP), _BF16),
        ],
        scratch_shapes=[pltpu.VMEM((tm, D_CONV), _F32),
                        pltpu.VMEM((8, D_CONV), _F32)],
        compiler_params=pltpu.CompilerParams(
            dimension_semantics=("arbitrary",), vmem_limit_bytes=VMEM_LIMIT),
        name="inproj",
    )(x2, g, w_in, qg2, kg2, cw, onc)


def _mixer_kernel(tiles_per_seq,
                  yc_ref, q_ref, kd_ref, vd_ref, kdh_ref, vdh_ref, x_ref,
                  tab_ref, sink_ref, ona_ref, wout_ref, nffn_ref,
                  wr_ref, rb_ref,
                  h1_ref, xp_ref, lg_ref, ya_ref):
    tm = x_ref.shape[0]
    blk = ATTN_BLOCK
    is_first = (pl.program_id(0) % tiles_per_seq) == 0

    lo = lax.broadcasted_iota(jnp.int32, (blk, LANES), 1) < HEAD_DIM
    qi = lax.broadcasted_iota(jnp.int32, (4 * blk, blk), 0) % blk
    from_prev = lax.broadcasted_iota(jnp.int32, (4 * blk, blk), 1) > qi
    no_prev = jnp.logical_and(is_first, from_prev)
    zero = jnp.zeros((blk, LANES), _BF16)
    zero_p = jnp.zeros((4 * blk, blk), _BF16)
    for j in range(tm // blk):
        rows = slice(j * blk, (j + 1) * blk)
        for g in range(N_KV_HEADS):
            cols = slice(g * LANES, (g + 1) * LANES)
            qa = q_ref[rows, 2 * g * LANES:(2 * g + 1) * LANES]
            qb = q_ref[rows, (2 * g + 1) * LANES:(2 * g + 2) * LANES]
            lhs = jnp.concatenate(
                [jnp.where(lo, qa, zero), jnp.where(lo, zero, qa),
                 jnp.where(lo, qb, zero), jnp.where(lo, zero, qb)], axis=0)
            if j == 0:
                keys = jnp.concatenate([kdh_ref[:, cols], kd_ref[0:blk, cols]], axis=0)
                vals = jnp.concatenate([vdh_ref[:, cols], vd_ref[0:blk, cols]], axis=0)
            else:
                keys = kd_ref[(j - 1) * blk:(j + 1) * blk, cols]
                vals = vd_ref[(j - 1) * blk:(j + 1) * blk, cols]
            s = lax.dot_general(lhs, keys, (((1,), (1,)), ((), ())),
                                preferred_element_type=_F32)
            s = jnp.where(from_prev, s[:, 0:blk], s[:, blk:2 * blk]) + tab_ref[g]
            if j == 0:
                s = jnp.where(no_prev, NEG_INF, s)
            sink = sink_ref[g]
            m = jnp.maximum(jnp.max(s, axis=-1, keepdims=True), sink)
            p = jnp.exp(s - m)
            denom = jnp.sum(p, axis=-1, keepdims=True) + jnp.exp(sink - m)
            p = p.astype(_BF16)
            p = jnp.concatenate([jnp.where(from_prev, p, zero_p),
                                 jnp.where(from_prev, zero_p, p)], axis=1)
            o = _dot(p, vals) / denom
            ya_ref[rows, 2 * g * LANES:(2 * g + 1) * LANES] = jnp.where(
                lo, o[0:blk], o[blk:2 * blk])
            ya_ref[rows, (2 * g + 1) * LANES:(2 * g + 2) * LANES] = jnp.where(
                lo, o[2 * blk:3 * blk], o[3 * blk:4 * blk])

    ya = _rms(ya_ref[...], ona_ref[...]).astype(_BF16)
    mixed = jnp.concatenate([yc_ref[...], ya], axis=-1)
    h1 = x_ref[...] + _dot(mixed, wout_ref[...])
    h1_ref[...] = h1

    xn = _rms(h1, nffn_ref[...])
    xp_ref[...] = _pack_rows(xn)
    lg_ref[...] = _dot(xn.astype(_BF16), wr_ref[...]) + rb_ref[...]


def _mixer(yc, q, kd, vd, x2, tab, sink, ona, w_out, nffn, wr, rb, seq):
    n = x2.shape[0]
    tm = TM_MIX
    const2 = lambda i: (0, 0)
    const3 = lambda i: (0, 0, 0)
    prev_blk = lambda i: (jnp.maximum(i * (tm // ATTN_BLOCK) - 1, 0), 0)
    return pl.pallas_call(
        functools.partial(_mixer_kernel, seq // tm),
        grid=(n // tm,),
        in_specs=[
            pl.BlockSpec((tm, D_CONV), lambda i: (i, 0)),
            pl.BlockSpec((tm, D_ATTN), lambda i: (i, 0)),
            pl.BlockSpec((tm, KV_DUP), lambda i: (i, 0)),
            pl.BlockSpec((tm, KV_DUP), lambda i: (i, 0)),
            pl.BlockSpec((ATTN_BLOCK, KV_DUP), prev_blk),
            pl.BlockSpec((ATTN_BLOCK, KV_DUP), prev_blk),
            pl.BlockSpec((tm, D_MODEL), lambda i: (i, 0)),
            pl.BlockSpec((N_KV_HEADS, 4 * ATTN_BLOCK, ATTN_BLOCK), const3,
                         pipeline_mode=pl.Buffered(1)),
            pl.BlockSpec((N_KV_HEADS, 4 * ATTN_BLOCK, 1), const3, pipeline_mode=pl.Buffered(1)),
            pl.BlockSpec((1, D_ATTN), const2),
            pl.BlockSpec((D_MODEL, D_MODEL), const2, pipeline_mode=pl.Buffered(1)),
            pl.BlockSpec((1, D_MODEL), const2),
            pl.BlockSpec((D_MODEL, LANES), const2),
            pl.BlockSpec((1, LANES), const2),
        ],
        out_specs=[
            pl.BlockSpec((tm, D_MODEL), lambda i: (i, 0)),
            pl.BlockSpec((tm, HALF), lambda i: (i, 0)),
            pl.BlockSpec((tm, LANES), lambda i: (i, 0)),
        ],
        out_shape=[
            jax.ShapeDtypeStruct((n, D_MODEL), _F32),
            jax.ShapeDtypeStruct((n, HALF), _U32),
            jax.ShapeDtypeStruct((n, LANES), _F32),
        ],
        scratch_shapes=[pltpu.VMEM((tm, D_ATTN), _F32)],
        compiler_params=pltpu.CompilerParams(
            dimension_semantics=("parallel",), vmem_limit_bytes=VMEM_LIMIT),
        name="mixer",
    )(yc, q, kd, vd, kd, vd, x2, tab, sink, ona, w_out, nffn, wr, rb)


def _route_kernel(lg_ref, route_ref, route_t_ref, cnt_ref, run_ref):
    t = lg_ref.shape[0]

    @pl.when(pl.program_id(0) == 0)
    def _():
        run_ref[...] = jnp.zeros_like(run_ref)

    lg = lg_ref[...]
    lane = lax.broadcasted_iota(jnp.int32, (t, LANES), 1)
    ninf = -jnp.inf

    lane_f = lane.astype(_F32)

    def first_max(v):
        m = jnp.max(v, axis=-1, keepdims=True)
        idx = jnp.min(jnp.where(v == m, lane_f, float(LANES)), axis=-1, keepdims=True)
        return m, idx.astype(jnp.int32)

    is_g = lane < N_GROUPS
    gmax, gidx = first_max(jnp.where(is_g, lg, ninf))
    gsum = jnp.sum(jnp.where(is_g, jnp.exp(jnp.where(is_g, lg, gmax) - gmax), 0.0),
                   axis=-1, keepdims=True)
    g_w = 1.0 / gsum
    base = N_GROUPS + EXPERTS_PER_GROUP * gidx
    in_grp = jnp.logical_and(lane >= base, lane < base + EXPERTS_PER_GROUP)
    el = jnp.where(in_grp, lg, ninf)
    m1, i1 = first_max(el)
    m2, i2 = first_max(jnp.where(lane == i1, ninf, el))
    e21 = jnp.exp(m2 - m1)
    w1 = g_w / (1.0 + e21)
    w2 = g_w * e21 / (1.0 + e21)
    e1 = i1 - N_GROUPS
    e2 = i2 - N_GROUPS

    oh1 = lane == e1
    oh2 = lane == e2
    earlier = (lax.broadcasted_iota(jnp.int32, (t, t), 0)
               > lax.broadcasted_iota(jnp.int32, (t, t), 1)).astype(_BF16)
    c1 = _dot(earlier, oh1.astype(_BF16))
    c2 = _dot(earlier, oh2.astype(_BF16))
    tot1 = jnp.sum(oh1.astype(_F32), axis=0, keepdims=True)
    tot2 = jnp.sum(oh2.astype(_F32), axis=0, keepdims=True)
    run = run_ref[0:1, :]
    r1 = jnp.sum(jnp.where(oh1, run + c1, 0.0), axis=-1, keepdims=True)
    r2 = jnp.sum(jnp.where(oh2, run + tot1 + c2, 0.0), axis=-1, keepdims=True)
    run_ref[...] = jnp.broadcast_to(run + tot1 + tot2, run_ref.shape)
    cnt_ref[...] = run_ref[...]

    out = jnp.where(lane == 0, e1.astype(_F32), 0.0)
    out = jnp.where(lane == 1, e2.astype(_F32), out)
    out = jnp.where(lane == 2, w1, out)
    out = jnp.where(lane == 3, w2, out)
    out = jnp.where(lane == 4, r1, out)
    out = jnp.where(lane == 5, r2, out)
    route_ref[...] = out
    route_t_ref[...] = out.T[0:8, :]


def _route(logits):
    n = logits.shape[0]
    t = TM_ROUTE
    return pl.pallas_call(
        _route_kernel,
        grid=(n // t,),
        in_specs=[pl.BlockSpec((t, LANES), lambda i: (i, 0))],
        out_specs=[pl.BlockSpec((t, LANES), lambda i: (i, 0)),
                   pl.BlockSpec((8, t), lambda i: (0, i)),
                   pl.BlockSpec((8, LANES), lambda i: (0, 0))],
        out_shape=[jax.ShapeDtypeStruct((n, LANES), _F32),
                   jax.ShapeDtypeStruct((8, n), _F32),
                   jax.ShapeDtypeStruct((8, LANES), _F32)],
        scratch_shapes=[pltpu.VMEM((8, LANES), _F32)],
        compiler_params=pltpu.CompilerParams(dimension_semantics=("arbitrary",)),
        name="route",
    )(logits)


def _dispatch_kernel(pad_end_ref, padded_ref, dest_ref, xp_ref, xs_ref,
                     buf_ref, zero_ref, lsem, rsem, zsem):
    td = buf_ref.shape[1]
    nb = xs_ref.shape[0] // MOE_BLK
    nb_used = pad_end_ref[N_EXPERTS - 1] // MOE_BLK
    i = pl.program_id(0)
    n_steps = pl.num_programs(0)
    slot = i % 3
    nxt = (i + 1) % 3
    prv = (i + 2) % 3

    def zero_copy(start):
        start = pl.multiple_of(start, MOE_BLK)
        return pltpu.make_async_copy(zero_ref, xs_ref.at[pl.ds(start, MOE_BLK)], zsem)

    def zero_blocks(fn):
        for e in range(N_EXPERTS):
            @pl.when(padded_ref[e] > 0)
            def _():
                fn(zero_copy(pad_end_ref[e] - MOE_BLK))
        for b in range(nb - N_EXPERTS, nb):
            @pl.when(b >= nb_used)
            def _():
                fn(zero_copy(b * MOE_BLK))

    def load(t, s):
        start = pl.multiple_of(t * td, td)
        return pltpu.make_async_copy(xp_ref.at[pl.ds(start, td)], buf_ref.at[s], lsem.at[s])

    def wait_rows(s):
        for k in range(TOP_K):
            pltpu.make_async_copy(buf_ref.at[s], xs_ref.at[pl.ds(0, td)], rsem.at[s]).wait()

    @pl.when(i == 0)
    def _():
        zero_ref[...] = jnp.zeros_like(zero_ref)
        zero_blocks(lambda c: c.start())
        zero_blocks(lambda c: c.wait())
        load(0, 0).start()

    @pl.when(i + 1 < n_steps)
    def _():
        load(i + 1, nxt).start()

    load(i, slot).wait()

    def issue(r, c):
        for k in range(TOP_K):
            d = dest_ref[0, 0, k * td + r]
            pltpu.make_async_copy(buf_ref.at[slot, pl.ds(r, 1)], xs_ref.at[pl.ds(d, 1)],
                                  rsem.at[slot]).start()
        return c

    lax.fori_loop(0, td, issue, 0, unroll=8)

    @pl.when(i > 0)
    def _():
        wait_rows(prv)

    @pl.when(i == n_steps - 1)
    def _():
        wait_rows(slot)


def _dispatch(xp, dest3, pad_end, padded, n_rows):
    steps = dest3.shape[0]
    td = dest3.shape[-1] // TOP_K
    return pl.pallas_call(
        _dispatch_kernel,
        grid_spec=pltpu.PrefetchScalarGridSpec(
            num_scalar_prefetch=2,
            grid=(steps,),
            in_specs=[
                pl.BlockSpec((1, 1, dest3.shape[-1]), lambda i, pe, pd: (i, 0, 0),
                             memory_space=pltpu.SMEM),
                pl.BlockSpec(memory_space=pl.ANY),
            ],
            out_specs=pl.BlockSpec(memory_space=pl.ANY),
            scratch_shapes=[pltpu.VMEM((3, td, HALF), _U32),
                            pltpu.VMEM((MOE_BLK, HALF), _U32),
                            pltpu.SemaphoreType.DMA((3,)), pltpu.SemaphoreType.DMA((3,)),
                            pltpu.SemaphoreType.DMA],
        ),
        out_shape=jax.ShapeDtypeStruct((n_rows, HALF), _U32),
        compiler_params=pltpu.CompilerParams(
            dimension_semantics=("arbitrary",)),
        name="dispatch",
    )(pad_end, padded, dest3, xp)


def _experts_kernel(be_ref, nxt_ref, nb_ref, xs_ref, wg_hbm, wu_hbm, wd_hbm, ys_ref,
                    wgs_ref, wus_ref, wds_ref, wgb_ref, wub_ref, wdb_ref, sem):
    i = pl.program_id(0)
    valid = i < nb_ref[0]
    e = be_ref[i]
    changed = jnp.logical_or(i == 0, e != be_ref[jnp.maximum(i - 1, 0)])

    def fetch(ex):
        return (pltpu.make_async_copy(wg_hbm.at[ex], wgs_ref, sem.at[0]),
                pltpu.make_async_copy(wu_hbm.at[ex], wus_ref, sem.at[1]),
                pltpu.make_async_copy(wd_hbm.at[ex], wds_ref, sem.at[2]))

    @pl.when(i == 0)
    def _():
        for c in fetch(e):
            c.start()

    @pl.when(jnp.logical_and(valid, changed))
    def _():
        for c in fetch(e):
            c.wait()
        wgb_ref[...] = wgs_ref[...].astype(_BF16)
        wub_ref[...] = wus_ref[...].astype(_BF16)
        wdb_ref[...] = wds_ref[...].astype(_BF16)
        nxt = nxt_ref[i]

        @pl.when(nxt >= 0)
        def _():
            for c in fetch(nxt):
                c.start()

    @pl.when(valid)
    def _():
        lo, hi = _unpack_rows(xs_ref[...])
        lo = lo.astype(_BF16)
        hi = hi.astype(_BF16)
        g = _dot(lo, wgb_ref[0:HALF, :]) + _dot(hi, wgb_ref[HALF:D_MODEL, :])
        u = _dot(lo, wub_ref[0:HALF, :]) + _dot(hi, wub_ref[HALF:D_MODEL, :])
        hidden = (g * jax.nn.sigmoid(g) * u).astype(_BF16)
        ys_ref[...] = _pack_rows(_dot(hidden, wdb_ref[...]))

    @pl.when(jnp.logical_not(valid))
    def _():
        ys_ref[...] = jnp.zeros_like(ys_ref)


def _experts(xs, w_gate, w_up, w_down, block_e, next_e, nb_used):
    n_rows = xs.shape[0]
    nb = n_rows // MOE_BLK
    row_blk = lambda i, be, nx, nbu: (jnp.minimum(i, nbu[0] - 1), 0)
    any_spec = pl.BlockSpec(memory_space=pl.ANY)
    return pl.pallas_call(
        _experts_kernel,
        grid_spec=pltpu.PrefetchScalarGridSpec(
            num_scalar_prefetch=3,
            grid=(nb,),
            in_specs=[pl.BlockSpec((MOE_BLK, HALF), row_blk), any_spec, any_spec, any_spec],
            out_specs=pl.BlockSpec((MOE_BLK, HALF), lambda i, be, nx, nbu: (i, 0)),
            scratch_shapes=[pltpu.VMEM((D_MODEL, D_EXPERT), _F32),
                            pltpu.VMEM((D_MODEL, D_EXPERT), _F32),
                            pltpu.VMEM((D_EXPERT, D_MODEL), _F32),
                            pltpu.VMEM((D_MODEL, D_EXPERT), _BF16),
                            pltpu.VMEM((D_MODEL, D_EXPERT), _BF16),
                            pltpu.VMEM((D_EXPERT, D_MODEL), _BF16),
                            pltpu.SemaphoreType.DMA((3,))],
        ),
        out_shape=jax.ShapeDtypeStruct((n_rows, HALF), _U32),
        compiler_params=pltpu.CompilerParams(
            dimension_semantics=("arbitrary",), vmem_limit_bytes=VMEM_LIMIT),
        name="experts",
    )(block_e, next_e, nb_used, xs, w_gate, w_up, w_down)


def _combine_kernel(h1_ref, route_ref, dcur_ref, dnext_ref, ys_ref, p_ref, nple_ref,
                    wgate_ref, wple_ref, out_ref, ybuf_ref, h2_ref, xn_ref, pb_ref, sem):
    tk = h1_ref.shape[0]
    cw = D_MODEL // N_CHUNK
    per = TOP_K * tk // N_CHUNK
    i = pl.program_id(0)
    n_steps = pl.num_programs(0)
    slot = i % 2

    def row_copy(d, s, k, r):
        return pltpu.make_async_copy(ys_ref.at[pl.ds(d, 1)],
                                     ybuf_ref.at[s, k, pl.ds(r, 1)], sem.at[s])

    def wait_slot(s):
        for k in range(TOP_K):
            pltpu.make_async_copy(ys_ref.at[pl.ds(0, tk)], ybuf_ref.at[s, k], sem.at[s]).wait()

    @pl.when(i == 0)
    def _():
        def body(r, c):
            for k in range(TOP_K):
                row_copy(dcur_ref[0, 0, k * tk + r], 0, k, r).start()
            return c
        lax.fori_loop(0, tk, body, 0, unroll=8)

    wait_slot(slot)
    y1 = jnp.concatenate(_unpack_rows(ybuf_ref[slot, 0]), axis=-1)
    y2 = jnp.concatenate(_unpack_rows(ybuf_ref[slot, 1]), axis=-1)
    route = route_ref[...]
    h2 = h1_ref[...] + route[:, 2:3] * y1 + route[:, 3:4] * y2
    xn = _rms(h2, nple_ref[...]).astype(_BF16)
    pb = p_ref[...].astype(_BF16)

    def issue_batch(c):
        for j in range(per):
            k, r = j % TOP_K, c * (per // TOP_K) + j // TOP_K
            row_copy(dnext_ref[0, 0, k * tk + r], 1 - slot, k, r).start()

    issue_batch(0)
    h2_ref[...] = h2
    xn_ref[...] = xn
    pb_ref[...] = pb
    for c in range(N_CHUNK):
        if c > 0:
            issue_batch(c)
        cols = slice(c * cw, (c + 1) * cw)
        gate = jax.nn.sigmoid(_dot(xn_ref[...], wgate_ref[:, cols]))
        emb = _dot(pb_ref[...], wple_ref[:, cols])
        out_ref[:, cols] = h2_ref[:, cols] + gate * emb

    @pl.when(i == n_steps - 1)
    def _():
        wait_slot(1 - slot)


def _combine(h1, route, dest3, ys, p2, nple, w_gate, w_ple):
    n = h1.shape[0]
    tk = TM_COMB
    steps = n // tk
    const2 = lambda i: (0, 0)
    return pl.pallas_call(
        _combine_kernel,
        grid=(steps,),
        in_specs=[
            pl.BlockSpec((tk, D_MODEL), lambda i: (i, 0)),
            pl.BlockSpec((tk, LANES), lambda i: (i, 0)),
            pl.BlockSpec((1, 1, TOP_K * tk), lambda i: (i, 0, 0), memory_space=pltpu.SMEM),
            pl.BlockSpec((1, 1, TOP_K * tk), lambda i: (jnp.minimum(i + 1, steps - 1), 0, 0),
                         memory_space=pltpu.SMEM),
            pl.BlockSpec(memory_space=pl.ANY),
            pl.BlockSpec((tk, PLE_DIM), lambda i: (i, 0)),
            pl.BlockSpec((1, D_MODEL), const2),
            pl.BlockSpec((D_MODEL, D_MODEL), const2, pipeline_mode=pl.Buffered(1)),
            pl.BlockSpec((PLE_DIM, D_MODEL), const2, pipeline_mode=pl.Buffered(1)),
        ],
        out_specs=pl.BlockSpec((tk, D_MODEL), lambda i: (i, 0)),
        out_shape=jax.ShapeDtypeStruct((n, D_MODEL), _F32),
        scratch_shapes=[pltpu.VMEM((2, TOP_K, tk, HALF), _U32),
                        pltpu.VMEM((tk, D_MODEL), _F32),
                        pltpu.VMEM((tk, D_MODEL), _BF16),
                        pltpu.VMEM((tk, PLE_DIM), _BF16),
                        pltpu.SemaphoreType.DMA((2,))],
        compiler_params=pltpu.CompilerParams(
            dimension_semantics=("arbitrary",), vmem_limit_bytes=VMEM_LIMIT),
        name="combine",
    )(h1, route, dest3, dest3, ys, p2, nple, w_gate, w_ple)


def _attention_tables(sinks):
    slopes = 2.0 ** (-8.0 * jnp.arange(1, N_HEADS + 1, dtype=_F32) / N_HEADS)
    qi = jnp.arange(ATTN_BLOCK)[:, None]
    kj = jnp.arange(ATTN_BLOCK)[None, :]
    dist = jnp.where(kj > qi, ATTN_BLOCK + qi - kj, qi - kj)
    tab = -slopes[:, None, None] * dist.astype(_F32)[None]
    tab = tab.reshape(N_KV_HEADS, 4 * ATTN_BLOCK, ATTN_BLOCK)
    sink = jnp.repeat(sinks.astype(_F32), ATTN_BLOCK).reshape(N_KV_HEADS, 4 * ATTN_BLOCK, 1)
    return tab, sink


def _layer(h, p_i, norm_mix, w_in, conv_w, q_norm, k_norm, sinks, out_norm_conv,
           out_norm_attn, w_out, norm_ffn, w_group, b_group, w_router, b_router,
           w_gate, w_up, w_down, norm_ple, w_ple_gate, w_ple):
    bsz, seq, _ = h.shape
    n = bsz * seq
    x2 = h.reshape(n, D_MODEL)
    row = lambda v: v.reshape(1, -1).astype(_F32)

    yc, q, kd, vd = _inproj(
        x2, row(norm_mix), w_in.astype(_BF16), row(jnp.tile(q_norm, 2)),
        row(jnp.tile(k_norm, 2)), conv_w.astype(_F32), row(out_norm_conv), seq)

    tab, sink = _attention_tables(sinks)
    w_r = jnp.zeros((D_MODEL, LANES), _F32)
    w_r = w_r.at[:, :N_GROUPS].set(w_group).at[:, N_GROUPS:N_GROUPS + N_EXPERTS].set(w_router)
    r_bias = jnp.zeros((1, LANES), _F32)
    r_bias = r_bias.at[0, :N_GROUPS].set(b_group).at[0, N_GROUPS:N_GROUPS + N_EXPERTS].set(b_router)
    h1, xp, logits = _mixer(
        yc, q, kd, vd, x2, tab, sink, row(out_norm_attn), w_out.astype(_BF16),
        row(norm_ffn), w_r.astype(_BF16), r_bias, seq)

    route, route_t, counts = _route(logits)

    cnt = counts[0, :N_EXPERTS].astype(jnp.int32)
    padded = (cnt + MOE_BLK - 1) // MOE_BLK * MOE_BLK
    pad_end = jnp.cumsum(padded)
    pad_start = pad_end - padded
    eid = route_t[0:TOP_K].astype(jnp.int32)
    onehot = eid[..., None] == jnp.arange(N_EXPERTS, dtype=jnp.int32)
    dest = (jnp.sum(jnp.where(onehot, pad_start, 0), axis=-1)
            + route_t[4:4 + TOP_K].astype(jnp.int32))

    def step_blocks(tile):
        return dest.reshape(TOP_K, n // tile, 1, tile).transpose(1, 2, 0, 3).reshape(
            n // tile, 1, TOP_K * tile)
    n_rows = n * TOP_K + N_EXPERTS * MOE_BLK
    nb = n_rows // MOE_BLK
    nb_used = (pad_end[-1] // MOE_BLK).astype(jnp.int32)
    blk_start = jnp.arange(nb, dtype=jnp.int32) * MOE_BLK
    block_e = jnp.sum((blk_start[:, None] >= pad_end[None, :]).astype(jnp.int32), axis=1)
    block_e = jnp.minimum(block_e, N_EXPERTS - 1)
    block_e = jnp.where(jnp.arange(nb) < nb_used, block_e, block_e[nb_used - 1])
    ids = jnp.arange(N_EXPERTS, dtype=jnp.int32)
    later = jnp.logical_and(ids[None, :] > ids[:, None], cnt[None, :] > 0)
    next_present = jnp.min(jnp.where(later, ids[None, :], N_EXPERTS), axis=1)
    next_present = jnp.where(next_present < N_EXPERTS, next_present, -1)
    onehot_b = block_e[:, None] == ids[None, :]
    next_e = jnp.sum(jnp.where(onehot_b, next_present[None, :], 0), axis=1).astype(jnp.int32)

    xs = _dispatch(xp, step_blocks(TM_DISP),
                   pad_end.astype(jnp.int32), padded.astype(jnp.int32), n_rows)
    ys = _experts(xs, w_gate, w_up, w_down, block_e, next_e, nb_used.reshape(1))
    out = _combine(h1, route, step_blocks(TM_COMB), ys,
                   p_i.reshape(n, PLE_DIM), row(norm_ple), w_ple_gate.astype(_BF16),
                   w_ple.astype(_BF16))
    return out.reshape(bsz, seq, D_MODEL)


def kernel(x, p, norm_mix, w_in, conv_w, q_norm, k_norm, sinks, out_norm_conv, out_norm_attn,
           w_out, norm_ffn, w_group, b_group, w_router, b_router, w_gate, w_up, w_down,
           norm_ple, w_ple_gate, w_ple):
    h = x
    for i in range(p.shape[0]):
        h = _layer(h, p[i], norm_mix[i], w_in[i], conv_w[i], q_norm[i], k_norm[i], sinks[i],
                   out_norm_conv[i], out_norm_attn[i], w_out[i], norm_ffn[i], w_group[i],
                   b_group[i], w_router[i], b_router[i], w_gate[i], w_up[i], w_down[i],
                   norm_ple[i], w_ple_gate[i], w_ple[i])
    return h
```

```python
import functools

import jax
import jax.numpy as jnp
from jax import lax
from jax.experimental import pallas as pl
from jax.experimental.pallas import tpu as pltpu

D_MODEL = 2048
D_CONV = 1024
N_HEADS = 16
N_KV_HEADS = 4
HEAD_DIM = 64
D_ATTN = N_HEADS * HEAD_DIM
D_KV = N_KV_HEADS * HEAD_DIM
D_IN = 3 * D_CONV + D_ATTN + 2 * D_KV
ATTN_BLOCK = 128
N_GROUPS = 4
EXPERTS_PER_GROUP = 8
N_EXPERTS = N_GROUPS * EXPERTS_PER_GROUP
TOP_K = 2
D_EXPERT = 512
PLE_DIM = 256
EPS = 1e-6
NEG_INF = -1e30

LANES = 128
HALF = D_MODEL // 2
KV_DUP = 2 * D_KV

TM_PROJ = 512
TM_MIX = 512
TM_ROUTE = 1024
TM_DISP = 512
TM_COMB = 512
MOE_BLK = 512
N_CHUNK = 8
VMEM_LIMIT = 56 * 1024 * 1024

_F32 = jnp.float32
_BF16 = jnp.bfloat16
_U32 = jnp.uint32
_HI_MASK = 0xFFFF0000


def _rms(x, gain):
    return x * lax.rsqrt(jnp.mean(x * x, axis=-1, keepdims=True) + EPS) * gain


def _dot(a, b):
    return jnp.dot(a, b, preferred_element_type=_F32)


def _pack_rows(x):
    h = x.shape[-1] // 2
    bits = lax.bitcast_convert_type(x.astype(_BF16).astype(_F32), _U32)
    return (bits[:, h:] & _U32(_HI_MASK)) | (bits[:, :h] >> 16)


def _unpack_rows(w):
    lo = lax.bitcast_convert_type(w << 16, _F32)
    hi = lax.bitcast_convert_type(w & _U32(_HI_MASK), _F32)
    return lo, hi


def _inproj_kernel(tiles_per_seq, x_ref, g_ref, w_ref, qg_ref, kg_ref, cw_ref, onc_ref,
                   yc_ref, q_ref, kd_ref, vd_ref, t_ref, uprev_ref):
    tm = x_ref.shape[0]
    is_first = (pl.program_id(0) % tiles_per_seq) == 0
    xn = _rms(x_ref[...], g_ref[...]).astype(_BF16)

    @pl.when(pl.program_id(0) == 0)
    def _():
        uprev_ref[...] = jnp.zeros_like(uprev_ref)

    lo = lax.broadcasted_iota(jnp.int32, (tm, LANES), 1) < HEAD_DIM
    q = _dot(xn, w_ref[:, 3 * D_CONV:3 * D_CONV + D_ATTN])
    scale = HEAD_DIM ** -0.5
    for pr in range(D_ATTN // LANES):
        qp = q[:, pr * LANES:(pr + 1) * LANES]
        sq = qp * qp
        s_lo = jnp.sum(jnp.where(lo, sq, 0.0), axis=-1, keepdims=True)
        s_hi = jnp.sum(jnp.where(lo, 0.0, sq), axis=-1, keepdims=True)
        ms = jnp.where(lo, s_lo, s_hi) * (1.0 / HEAD_DIM)
        qn = (qp * lax.rsqrt(ms + EPS) * qg_ref[...] * scale).astype(_BF16)
        q_lo = jnp.where(lo, qn, jnp.zeros_like(qn))
        q_hi = jnp.where(lo, jnp.zeros_like(qn), qn)
        for j in range(tm // ATTN_BLOCK):
            rows = slice(j * ATTN_BLOCK, (j + 1) * ATTN_BLOCK)
            base = (j * N_HEADS + 2 * pr) * ATTN_BLOCK
            q_ref[base:base + ATTN_BLOCK, :] = q_lo[rows]
            q_ref[base + ATTN_BLOCK:base + 2 * ATTN_BLOCK, :] = q_hi[rows]

    kv = _dot(xn, w_ref[:, 3 * D_CONV + D_ATTN:D_IN])
    for pr in range(2 * D_KV // LANES):
        pair = kv[:, pr * LANES:(pr + 1) * LANES]
        swapped = pltpu.roll(pair, HEAD_DIM, 1)
        for half in range(2):
            dup = jnp.where(lo, pair, swapped) if half == 0 else jnp.where(lo, swapped, pair)
            head = 2 * pr + half
            if head < N_KV_HEADS:
                ms = jnp.sum(dup * dup, axis=-1, keepdims=True) * (1.0 / LANES)
                dup = dup * lax.rsqrt(ms + EPS) * kg_ref[...]
                kd_ref[:, head * LANES:(head + 1) * LANES] = dup.astype(_BF16)
            else:
                head -= N_KV_HEADS
                vd_ref[:, head * LANES:(head + 1) * LANES] = dup.astype(_BF16)

    ch = 256
    row = lax.broadcasted_iota(jnp.int32, (tm, ch), 0)
    last = uprev_ref.shape[0] - 1
    ssq = jnp.zeros((tm, 1), _F32)
    for j in range(D_CONV // ch):
        cs = slice(j * ch, (j + 1) * ch)
        b = _dot(xn, w_ref[:, j * ch:(j + 1) * ch])
        c = _dot(xn, w_ref[:, D_CONV + j * ch:D_CONV + (j + 1) * ch])
        hc = _dot(xn, w_ref[:, 2 * D_CONV + j * ch:2 * D_CONV + (j + 1) * ch])
        u = c * hc
        uh = jnp.where(is_first, 0.0, uprev_ref[:, cs])
        uprev_ref[:, cs] = u[tm - last - 1:tm]
        u1 = jnp.where(row == 0, uh[last:last + 1], pltpu.roll(u, 1, 0))
        u2 = jnp.where(row == 0, uh[last - 1:last],
                       jnp.where(row == 1, uh[last:last + 1], pltpu.roll(u, 2, 0)))
        t = b * (cw_ref[2:3, cs] * u + cw_ref[1:2, cs] * u1 + cw_ref[0:1, cs] * u2)
        ssq = ssq + jnp.sum(t * t, axis=-1, keepdims=True)
        t_ref[:, cs] = t
    inv = lax.rsqrt(ssq * (1.0 / D_CONV) + EPS)
    yc_ref[...] = (t_ref[...] * inv * onc_ref[...]).astype(_BF16)


def _inproj(x2, g, w_in, qg2, kg2, cw, onc, seq):
    n = x2.shape[0]
    tm = TM_PROJ
    const = lambda i: (0, 0)
    return pl.pallas_call(
        functools.partial(_inproj_kernel, seq // tm),
        grid=(n // tm,),
        in_specs=[
            pl.BlockSpec((tm, D_MODEL), lambda i: (i, 0)),
            pl.BlockSpec((1, D_MODEL), const),
            pl.BlockSpec((D_MODEL, D_IN), const, pipeline_mode=pl.Buffered(1)),
            pl.BlockSpec((1, LANES), const),
            pl.BlockSpec((1, LANES), const),
            pl.BlockSpec((3, D_CONV), const),
            pl.BlockSpec((1, D_CONV), const),
        ],
        out_specs=[
            pl.BlockSpec((tm, D_CONV), lambda i: (i, 0)),
            pl.BlockSpec((tm * N_HEADS, LANES), lambda i: (i, 0)),
            pl.BlockSpec((tm, KV_DUP), lambda i: (i, 0)),
            pl.BlockSpec((tm, KV_DUP), lambda i: (i, 0)),
        ],
        out_shape=[
            jax.ShapeDtypeStruct((n, D_CONV), _BF16),
            jax.ShapeDtypeStruct((n * N_HEADS, LANES), _BF16),
            jax.ShapeDtypeStruct((n, KV_DUP), _BF16),
            jax.ShapeDtypeStruct((n, KV_DUP), _BF16),
        ],
        scratch_shapes=[pltpu.VMEM((tm, D_CONV), _F32),
                        pltpu.VMEM((8, D_CONV), _F32)],
        compiler_params=pltpu.CompilerParams(
            dimension_semantics=("arbitrary",), vmem_limit_bytes=VMEM_LIMIT),
        name="inproj",
    )(x2, g, w_in, qg2, kg2, cw, onc)


def _mixer_kernel(tiles_per_seq,
                  yc_ref, q_ref, kd_ref, vd_ref, kdh_ref, vdh_ref, x_ref,
                  tab_ref, sink_ref, ona_ref, wout_ref, nffn_ref,
                  wr_ref, rb_ref,
                  h1_ref, xp_ref, lg_ref, ya_ref):
    tm = x_ref.shape[0]
    blk = ATTN_BLOCK
    is_first = (pl.program_id(0) % tiles_per_seq) == 0

    lo = lax.broadcasted_iota(jnp.int32, (blk, LANES), 1) < HEAD_DIM
    qi = lax.broadcasted_iota(jnp.int32, (4 * blk, blk), 0) % blk
    from_prev = lax.broadcasted_iota(jnp.int32, (4 * blk, blk), 1) > qi
    no_prev = jnp.logical_and(is_first, from_prev)
    zero_p = jnp.zeros((4 * blk, blk), _BF16)
    for j in range(tm // blk):
        rows = slice(j * blk, (j + 1) * blk)
        for g in range(N_KV_HEADS):
            cols = slice(g * LANES, (g + 1) * LANES)
            grp = (j * N_HEADS + 4 * g) * blk
            lhs = q_ref[grp:grp + 4 * blk, :]
            if j == 0:
                keys = jnp.concatenate([kdh_ref[:, cols], kd_ref[0:blk, cols]], axis=0)
                vals = jnp.concatenate([vdh_ref[:, cols], vd_ref[0:blk, cols]], axis=0)
            else:
                keys = kd_ref[(j - 1) * blk:(j + 1) * blk, cols]
                vals = vd_ref[(j - 1) * blk:(j + 1) * blk, cols]
            s = lax.dot_general(lhs, keys, (((1,), (1,)), ((), ())),
                                preferred_element_type=_F32)
            s = jnp.where(from_prev, s[:, 0:blk], s[:, blk:2 * blk]) + tab_ref[g]
            if j == 0:
                s = jnp.where(no_prev, NEG_INF, s)
            sink = sink_ref[g]
            m = jnp.maximum(jnp.max(s, axis=-1, keepdims=True), sink)
            p = jnp.exp(s - m)
            denom = jnp.sum(p, axis=-1, keepdims=True) + jnp.exp(sink - m)
            p = p.astype(_BF16)
            p = jnp.concatenate([jnp.where(from_prev, p, zero_p),
                                 jnp.where(from_prev, zero_p, p)], axis=1)
            o = _dot(p, vals) / denom
            ya_ref[rows, 2 * g * LANES:(2 * g + 1) * LANES] = jnp.where(
                lo, o[0:blk], o[blk:2 * blk])
            ya_ref[rows, (2 * g + 1) * LANES:(2 * g + 2) * LANES] = jnp.where(
                lo, o[2 * blk:3 * blk], o[3 * blk:4 * blk])

    ya = _rms(ya_ref[...], ona_ref[...]).astype(_BF16)
    mixed = jnp.concatenate([yc_ref[...], ya], axis=-1)
    h1 = x_ref[...] + _dot(mixed, wout_ref[...])
    h1_ref[...] = h1

    xn = _rms(h1, nffn_ref[...])
    xp_ref[...] = _pack_rows(xn)
    lg_ref[...] = _dot(xn.astype(_BF16), wr_ref[...]) + rb_ref[...]


def _mixer(yc, q, kd, vd, x2, tab, sink, ona, w_out, nffn, wr, rb, seq):
    n = x2.shape[0]
    tm = TM_MIX
    const2 = lambda i: (0, 0)
    const3 = lambda i: (0, 0, 0)
    prev_blk = lambda i: (jnp.maximum(i * (tm // ATTN_BLOCK) - 1, 0), 0)
    return pl.pallas_call(
        functools.partial(_mixer_kernel, seq // tm),
        grid=(n // tm,),
        in_specs=[
            pl.BlockSpec((tm, D_CONV), lambda i: (i, 0)),
            pl.BlockSpec((tm * N_HEADS, LANES), lambda i: (i, 0)),
            pl.BlockSpec((tm, KV_DUP), lambda i: (i, 0)),
            pl.BlockSpec((tm, KV_DUP), lambda i: (i, 0)),
            pl.BlockSpec((ATTN_BLOCK, KV_DUP), prev_blk),
            pl.BlockSpec((ATTN_BLOCK, KV_DUP), prev_blk),
            pl.BlockSpec((tm, D_MODEL), lambda i: (i, 0)),
            pl.BlockSpec((N_KV_HEADS, 4 * ATTN_BLOCK, ATTN_BLOCK), const3,
                         pipeline_mode=pl.Buffered(1)),
            pl.BlockSpec((N_KV_HEADS, 4 * ATTN_BLOCK, 1), const3, pipeline_mode=pl.Buffered(1)),
            pl.BlockSpec((1, D_ATTN), const2),
            pl.BlockSpec((D_MODEL, D_MODEL), const2, pipeline_mode=pl.Buffered(1)),
            pl.BlockSpec((1, D_MODEL), const2),
            pl.BlockSpec((D_MODEL, LANES), const2),
            pl.BlockSpec((1, LANES), const2),
        ],
        out_specs=[
            pl.BlockSpec((tm, D_MODEL), lambda i: (i, 0)),
            pl.BlockSpec((tm, HALF), lambda i: (i, 0)),
            pl.BlockSpec((tm, LANES), lambda i: (i, 0)),
        ],
        out_shape=[
            jax.ShapeDtypeStruct((n, D_MODEL), _F32),
            jax.ShapeDtypeStruct((n, HALF), _U32),
            jax.ShapeDtypeStruct((n, LANES), _F32),
        ],
        scratch_shapes=[pltpu.VMEM((tm, D_ATTN), _F32)],
        compiler_params=pltpu.CompilerParams(
            dimension_semantics=("parallel",), vmem_limit_bytes=VMEM_LIMIT),
        name="mixer",
    )(yc, q, kd, vd, kd, vd, x2, tab, sink, ona, w_out, nffn, wr, rb)


def _route_kernel(lg_ref, route_ref, route_t_ref, cnt_ref, run_ref, earlier_ref):
    t = lg_ref.shape[0]

    @pl.when(pl.program_id(0) == 0)
    def _():
        run_ref[...] = jnp.zeros_like(run_ref)
        earlier_ref[...] = (lax.broadcasted_iota(jnp.int32, (t, t), 0)
                            > lax.broadcasted_iota(jnp.int32, (t, t), 1)).astype(_BF16)

    lg = lg_ref[...]
    lane = lax.broadcasted_iota(jnp.int32, (t, LANES), 1)
    ninf = -jnp.inf

    lane_f = lane.astype(_F32)

    def first_max(v):
        m = jnp.max(v, axis=-1, keepdims=True)
        idx = jnp.min(jnp.where(v == m, lane_f, float(LANES)), axis=-1, keepdims=True)
        return m, idx.astype(jnp.int32)

    is_g = lane < N_GROUPS
    gmax, gidx = first_max(jnp.where(is_g, lg, ninf))
    gsum = jnp.sum(jnp.where(is_g, jnp.exp(jnp.where(is_g, lg, gmax) - gmax), 0.0),
                   axis=-1, keepdims=True)
    g_w = 1.0 / gsum
    base = N_GROUPS + EXPERTS_PER_GROUP * gidx
    in_grp = jnp.logical_and(lane >= base, lane < base + EXPERTS_PER_GROUP)
    el = jnp.where(in_grp, lg, ninf)
    m1, i1 = first_max(el)
    m2, i2 = first_max(jnp.where(lane == i1, ninf, el))
    e21 = jnp.exp(m2 - m1)
    w1 = g_w / (1.0 + e21)
    w2 = g_w * e21 / (1.0 + e21)
    e1 = i1 - N_GROUPS
    e2 = i2 - N_GROUPS

    oh1 = lane == e1
    oh2 = lane == e2
    c12 = _dot(earlier_ref[...],
               jnp.concatenate([oh1.astype(_BF16), oh2.astype(_BF16)], axis=1))
    c1 = c12[:, 0:LANES]
    c2 = c12[:, LANES:2 * LANES]
    tot1 = jnp.sum(oh1.astype(_F32), axis=0, keepdims=True)
    tot2 = jnp.sum(oh2.astype(_F32), axis=0, keepdims=True)
    run = run_ref[0:1, :]
    r1 = jnp.sum(jnp.where(oh1, run + c1, 0.0), axis=-1, keepdims=True)
    r2 = jnp.sum(jnp.where(oh2, run + tot1 + c2, 0.0), axis=-1, keepdims=True)
    run_ref[...] = jnp.broadcast_to(run + tot1 + tot2, run_ref.shape)
    cnt_ref[...] = run_ref[...]

    out = jnp.where(lane == 0, e1.astype(_F32), 0.0)
    out = jnp.where(lane == 1, e2.astype(_F32), out)
    out = jnp.where(lane == 2, w1, out)
    out = jnp.where(lane == 3, w2, out)
    out = jnp.where(lane == 4, r1, out)
    out = jnp.where(lane == 5, r2, out)
    route_ref[...] = out
    route_t_ref[...] = out.T[0:8, :]


def _route(logits):
    n = logits.shape[0]
    t = TM_ROUTE
    return pl.pallas_call(
        _route_kernel,
        grid=(n // t,),
        in_specs=[pl.BlockSpec((t, LANES), lambda i: (i, 0))],
        out_specs=[pl.BlockSpec((t, LANES), lambda i: (i, 0)),
                   pl.BlockSpec((8, t), lambda i: (0, i)),
                   pl.BlockSpec((8, LANES), lambda i: (0, 0))],
        out_shape=[jax.ShapeDtypeStruct((n, LANES), _F32),
                   jax.ShapeDtypeStruct((8, n), _F32),
                   jax.ShapeDtypeStruct((8, LANES), _F32)],
        scratch_shapes=[pltpu.VMEM((8, LANES), _F32), pltpu.VMEM((t, t), _BF16)],
        compiler_params=pltpu.CompilerParams(dimension_semantics=("arbitrary",)),
        name="route",
    )(logits)


def _dispatch_kernel(pad_end_ref, padded_ref, dest_ref, xp_ref, xs_ref,
                     buf_ref, zero_ref, lsem, rsem, zsem):
    td = buf_ref.shape[1]
    nb = xs_ref.shape[0] // MOE_BLK
    nb_used = pad_end_ref[N_EXPERTS - 1] // MOE_BLK
    i = pl.program_id(0)
    n_steps = pl.num_programs(0)
    slot = i % 3
    nxt = (i + 1) % 3
    prv = (i + 2) % 3

    def zero_copy(start):
        start = pl.multiple_of(start, MOE_BLK)
        return pltpu.make_async_copy(zero_ref, xs_ref.at[pl.ds(start, MOE_BLK)], zsem)

    def zero_blocks(fn):
        for e in range(N_EXPERTS):
            @pl.when(padded_ref[e] > 0)
            def _():
                fn(zero_copy(pad_end_ref[e] - MOE_BLK))
        for b in range(nb - N_EXPERTS, nb):
            @pl.when(b >= nb_used)
            def _():
                fn(zero_copy(b * MOE_BLK))

    def load(t, s):
        start = pl.multiple_of(t * td, td)
        return pltpu.make_async_copy(xp_ref.at[pl.ds(start, td)], buf_ref.at[s], lsem.at[s])

    def wait_rows(s):
        for k in range(TOP_K):
            pltpu.make_async_copy(buf_ref.at[s], xs_ref.at[pl.ds(0, td)], rsem.at[s]).wait()

    @pl.when(i == 0)
    def _():
        zero_ref[...] = jnp.zeros_like(zero_ref)
        zero_blocks(lambda c: c.start())
        zero_blocks(lambda c: c.wait())
        load(0, 0).start()

    @pl.when(i + 1 < n_steps)
    def _():
        load(i + 1, nxt).start()

    load(i, slot).wait()

    def issue(r, c):
        for k in range(TOP_K):
            d = dest_ref[0, 0, k * td + r]
            pltpu.make_async_copy(buf_ref.at[slot, pl.ds(r, 1)], xs_ref.at[pl.ds(d, 1)],
                                  rsem.at[slot]).start()
        return c

    lax.fori_loop(0, td, issue, 0, unroll=8)

    @pl.when(i > 0)
    def _():
        wait_rows(prv)

    @pl.when(i == n_steps - 1)
    def _():
        wait_rows(slot)


def _dispatch(xp, dest3, pad_end, padded, n_rows):
    steps = dest3.shape[0]
    td = dest3.shape[-1] // TOP_K
    return pl.pallas_call(
        _dispatch_kernel,
        grid_spec=pltpu.PrefetchScalarGridSpec(
            num_scalar_prefetch=2,
            grid=(steps,),
            in_specs=[
                pl.BlockSpec((1, 1, dest3.shape[-1]), lambda i, pe, pd: (i, 0, 0),
                             memory_space=pltpu.SMEM),
                pl.BlockSpec(memory_space=pl.ANY),
            ],
            out_specs=pl.BlockSpec(memory_space=pl.ANY),
            scratch_shapes=[pltpu.VMEM((3, td, HALF), _U32),
                            pltpu.VMEM((MOE_BLK, HALF), _U32),
                            pltpu.SemaphoreType.DMA((3,)), pltpu.SemaphoreType.DMA((3,)),
                            pltpu.SemaphoreType.DMA],
        ),
        out_shape=jax.ShapeDtypeStruct((n_rows, HALF), _U32),
        compiler_params=pltpu.CompilerParams(
            dimension_semantics=("arbitrary",)),
        name="dispatch",
    )(pad_end, padded, dest3, xp)


def _experts_kernel(be_ref, nxt_ref, nb_ref, xs_ref, wg_hbm, wu_hbm, wd_hbm, ys_ref,
                    wgs_ref, wus_ref, wds_ref, wgb_ref, wub_ref, wdb_ref, sem):
    i = pl.program_id(0)
    valid = i < nb_ref[0]
    e = be_ref[i]
    changed = jnp.logical_or(i == 0, e != be_ref[jnp.maximum(i - 1, 0)])

    def fetch(ex):
        return (pltpu.make_async_copy(wg_hbm.at[ex], wgs_ref, sem.at[0]),
                pltpu.make_async_copy(wu_hbm.at[ex], wus_ref, sem.at[1]),
                pltpu.make_async_copy(wd_hbm.at[ex], wds_ref, sem.at[2]))

    @pl.when(i == 0)
    def _():
        for c in fetch(e):
            c.start()

    @pl.when(jnp.logical_and(valid, changed))
    def _():
        for c in fetch(e):
            c.wait()
        wgb_ref[...] = wgs_ref[...].astype(_BF16)
        wub_ref[...] = wus_ref[...].astype(_BF16)
        wdb_ref[...] = wds_ref[...].astype(_BF16)
        nxt = nxt_ref[i]

        @pl.when(nxt >= 0)
        def _():
            for c in fetch(nxt):
                c.start()

    @pl.when(valid)
    def _():
        lo, hi = _unpack_rows(xs_ref[...])
        lo = lo.astype(_BF16)
        hi = hi.astype(_BF16)
        g = _dot(lo, wgb_ref[0:HALF, :]) + _dot(hi, wgb_ref[HALF:D_MODEL, :])
        u = _dot(lo, wub_ref[0:HALF, :]) + _dot(hi, wub_ref[HALF:D_MODEL, :])
        hidden = (g * jax.nn.sigmoid(g) * u).astype(_BF16)
        ys_ref[...] = _pack_rows(_dot(hidden, wdb_ref[...]))

    @pl.when(jnp.logical_not(valid))
    def _():
        ys_ref[...] = jnp.zeros_like(ys_ref)


def _experts(xs, w_gate, w_up, w_down, block_e, next_e, nb_used):
    n_rows = xs.shape[0]
    nb = n_rows // MOE_BLK
    row_blk = lambda i, be, nx, nbu: (jnp.minimum(i, nbu[0] - 1), 0)
    any_spec = pl.BlockSpec(memory_space=pl.ANY)
    return pl.pallas_call(
        _experts_kernel,
        grid_spec=pltpu.PrefetchScalarGridSpec(
            num_scalar_prefetch=3,
            grid=(nb,),
            in_specs=[pl.BlockSpec((MOE_BLK, HALF), row_blk), any_spec, any_spec, any_spec],
            out_specs=pl.BlockSpec((MOE_BLK, HALF), lambda i, be, nx, nbu: (i, 0)),
            scratch_shapes=[pltpu.VMEM((D_MODEL, D_EXPERT), _F32),
                            pltpu.VMEM((D_MODEL, D_EXPERT), _F32),
                            pltpu.VMEM((D_EXPERT, D_MODEL), _F32),
                            pltpu.VMEM((D_MODEL, D_EXPERT), _BF16),
                            pltpu.VMEM((D_MODEL, D_EXPERT), _BF16),
                            pltpu.VMEM((D_EXPERT, D_MODEL), _BF16),
                            pltpu.SemaphoreType.DMA((3,))],
        ),
        out_shape=jax.ShapeDtypeStruct((n_rows, HALF), _U32),
        compiler_params=pltpu.CompilerParams(
            dimension_semantics=("arbitrary",), vmem_limit_bytes=VMEM_LIMIT),
        name="experts",
    )(block_e, next_e, nb_used, xs, w_gate, w_up, w_down)


def _combine_kernel(h1_ref, route_ref, dcur_ref, dnext_ref, ys_ref, p_ref, nple_ref,
                    wgate_ref, wple_ref, out_ref, ybuf_ref, h2_ref, xn_ref, pb_ref, sem):
    tk = h1_ref.shape[0]
    cw = D_MODEL // N_CHUNK
    per = TOP_K * tk // N_CHUNK
    i = pl.program_id(0)
    n_steps = pl.num_programs(0)
    slot = i % 2

    def row_copy(d, s, k, r):
        return pltpu.make_async_copy(ys_ref.at[pl.ds(d, 1)],
                                     ybuf_ref.at[s, k, pl.ds(r, 1)], sem.at[s])

    def wait_slot(s):
        for k in range(TOP_K):
            pltpu.make_async_copy(ys_ref.at[pl.ds(0, tk)], ybuf_ref.at[s, k], sem.at[s]).wait()

    @pl.when(i == 0)
    def _():
        def body(r, c):
            for k in range(TOP_K):
                row_copy(dcur_ref[0, 0, k * tk + r], 0, k, r).start()
            return c
        lax.fori_loop(0, tk, body, 0, unroll=8)

    wait_slot(slot)
    y1 = jnp.concatenate(_unpack_rows(ybuf_ref[slot, 0]), axis=-1)
    y2 = jnp.concatenate(_unpack_rows(ybuf_ref[slot, 1]), axis=-1)
    route = route_ref[...]
    h2 = h1_ref[...] + route[:, 2:3] * y1 + route[:, 3:4] * y2
    xn = _rms(h2, nple_ref[...]).astype(_BF16)
    pb = p_ref[...].astype(_BF16)

    def issue_batch(c):
        for j in range(per):
            k, r = j % TOP_K, c * (per // TOP_K) + j // TOP_K
            row_copy(dnext_ref[0, 0, k * tk + r], 1 - slot, k, r).start()

    issue_batch(0)
    h2_ref[...] = h2
    xn_ref[...] = xn
    pb_ref[...] = pb
    for c in range(N_CHUNK):
        if c > 0:
            issue_batch(c)
        cols = slice(c * cw, (c + 1) * cw)
        gate = jax.nn.sigmoid(_dot(xn_ref[...], wgate_ref[:, cols]))
        emb = _dot(pb_ref[...], wple_ref[:, cols])
        out_ref[:, cols] = h2_ref[:, cols] + gate * emb

    @pl.when(i == n_steps - 1)
    def _():
        wait_slot(1 - slot)


def _combine(h1, route, dest3, ys, p2, nple, w_gate, w_ple):
    n = h1.shape[0]
    tk = TM_COMB
    steps = n // tk
    const2 = lambda i: (0, 0)
    return pl.pallas_call(
        _combine_kernel,
        grid=(steps,),
        in_specs=[
            pl.BlockSpec((tk, D_MODEL), lambda i: (i, 0)),
            pl.BlockSpec((tk, LANES), lambda i: (i, 0)),
            pl.BlockSpec((1, 1, TOP_K * tk), lambda i: (i, 0, 0), memory_space=pltpu.SMEM),
            pl.BlockSpec((1, 1, TOP_K * tk), lambda i: (jnp.minimum(i + 1, steps - 1), 0, 0),
                         memory_space=pltpu.SMEM),
            pl.BlockSpec(memory_space=pl.ANY),
            pl.BlockSpec((tk, PLE_DIM), lambda i: (i, 0)),
            pl.BlockSpec((1, D_MODEL), const2),
            pl.BlockSpec((D_MODEL, D_MODEL), const2, pipeline_mode=pl.Buffered(1)),
            pl.BlockSpec((PLE_DIM, D_MODEL), const2, pipeline_mode=pl.Buffered(1)),
        ],
        out_specs=pl.BlockSpec((tk, D_MODEL), lambda i: (i, 0)),
        out_shape=jax.ShapeDtypeStruct((n, D_MODEL), _F32),
        scratch_shapes=[pltpu.VMEM((2, TOP_K, tk, HALF), _U32),
                        pltpu.VMEM((tk, D_MODEL), _F32),
                        pltpu.VMEM((tk, D_MODEL), _BF16),
                        pltpu.VMEM((tk, PLE_DIM), _BF16),
                        pltpu.SemaphoreType.DMA((2,))],
        compiler_params=pltpu.CompilerParams(
            dimension_semantics=("arbitrary",), vmem_limit_bytes=VMEM_LIMIT),
        name="combine",
    )(h1, route, dest3, dest3, ys, p2, nple, w_gate, w_ple)


def _attention_tables(sinks):
    slopes = 2.0 ** (-8.0 * jnp.arange(1, N_HEADS + 1, dtype=_F32) / N_HEADS)
    qi = jnp.arange(ATTN_BLOCK)[:, None]
    kj = jnp.arange(ATTN_BLOCK)[None, :]
    dist = jnp.where(kj > qi, ATTN_BLOCK + qi - kj, qi - kj)
    tab = -slopes[:, None, None] * dist.astype(_F32)[None]
    tab = tab.reshape(N_KV_HEADS, 4 * ATTN_BLOCK, ATTN_BLOCK)
    sink = jnp.repeat(sinks.astype(_F32), ATTN_BLOCK).reshape(N_KV_HEADS, 4 * ATTN_BLOCK, 1)
    return tab, sink


def _layer(h, p_i, norm_mix, w_in, conv_w, q_norm, k_norm, sinks, out_norm_conv,
           out_norm_attn, w_out, norm_ffn, w_group, b_group, w_router, b_router,
           w_gate, w_up, w_down, norm_ple, w_ple_gate, w_ple):
    bsz, seq, _ = h.shape
    n = bsz * seq
    x2 = h.reshape(n, D_MODEL)
    row = lambda v: v.reshape(1, -1).astype(_F32)

    yc, q, kd, vd = _inproj(
        x2, row(norm_mix), w_in.astype(_BF16), row(jnp.tile(q_norm, 2)),
        row(jnp.tile(k_norm, 2)), conv_w.astype(_F32), row(out_norm_conv), seq)

    tab, sink = _attention_tables(sinks)
    w_r = jnp.zeros((D_MODEL, LANES), _F32)
    w_r = w_r.at[:, :N_GROUPS].set(w_group).at[:, N_GROUPS:N_GROUPS + N_EXPERTS].set(w_router)
    r_bias = jnp.zeros((1, LANES), _F32)
    r_bias = r_bias.at[0, :N_GROUPS].set(b_group).at[0, N_GROUPS:N_GROUPS + N_EXPERTS].set(b_router)
    h1, xp, logits = _mixer(
        yc, q, kd, vd, x2, tab, sink, row(out_norm_attn), w_out.astype(_BF16),
        row(norm_ffn), w_r.astype(_BF16), r_bias, seq)

    route, route_t, counts = _route(logits)

    cnt = counts[0, :N_EXPERTS].astype(jnp.int32)
    padded = (cnt + MOE_BLK - 1) // MOE_BLK * MOE_BLK
    pad_end = jnp.cumsum(padded)
    pad_start = pad_end - padded
    eid = route_t[0:TOP_K].astype(jnp.int32)
    onehot = eid[..., None] == jnp.arange(N_EXPERTS, dtype=jnp.int32)
    dest = (jnp.sum(jnp.where(onehot, pad_start, 0), axis=-1)
            + route_t[4:4 + TOP_K].astype(jnp.int32))

    def step_blocks(tile):
        return dest.reshape(TOP_K, n // tile, 1, tile).transpose(1, 2, 0, 3).reshape(
            n // tile, 1, TOP_K * tile)
    n_rows = n * TOP_K + N_EXPERTS * MOE_BLK
    nb = n_rows // MOE_BLK
    nb_used = (pad_end[-1] // MOE_BLK).astype(jnp.int32)
    blk_start = jnp.arange(nb, dtype=jnp.int32) * MOE_BLK
    block_e = jnp.sum((blk_start[:, None] >= pad_end[None, :]).astype(jnp.int32), axis=1)
    block_e = jnp.minimum(block_e, N_EXPERTS - 1)
    block_e = jnp.where(jnp.arange(nb) < nb_used, block_e, block_e[nb_used - 1])
    ids = jnp.arange(N_EXPERTS, dtype=jnp.int32)
    later = jnp.logical_and(ids[None, :] > ids[:, None], cnt[None, :] > 0)
    next_present = jnp.min(jnp.where(later, ids[None, :], N_EXPERTS), axis=1)
    next_present = jnp.where(next_present < N_EXPERTS, next_present, -1)
    onehot_b = block_e[:, None] == ids[None, :]
    next_e = jnp.sum(jnp.where(onehot_b, next_present[None, :], 0), axis=1).astype(jnp.int32)

    xs = _dispatch(xp, step_blocks(TM_DISP),
                   pad_end.astype(jnp.int32), padded.astype(jnp.int32), n_rows)
    ys = _experts(xs, w_gate, w_up, w_down, block_e, next_e, nb_used.reshape(1))
    out = _combine(h1, route, step_blocks(TM_COMB), ys,
                   p_i.reshape(n, PLE_DIM), row(norm_ple), w_ple_gate.astype(_BF16),
                   w_ple.astype(_BF16))
    return out.reshape(bsz, seq, D_MODEL)


def kernel(x, p, norm_mix, w_in, conv_w, q_norm, k_norm, sinks, out_norm_conv, out_norm_attn,
           w_out, norm_ffn, w_group, b_group, w_router, b_router, w_gate, w_up, w_down,
           norm_ple, w_ple_gate, w_ple):
    h = x
    for i in range(p.shape[0]):
        h = _layer(h, p[i], norm_mix[i], w_in[i], conv_w[i], q_norm[i], k_norm[i], sinks[i],
                   out_norm_conv[i], out_norm_attn[i], w_out[i], norm_ffn[i], w_group[i],
                   b_group[i], w_router[i], b_router[i], w_gate[i], w_up[i], w_down[i],
                   norm_ple[i], w_ple_gate[i], w_ple[i])
    return h
```

```python
import functools

import jax
import jax.numpy as jnp
from jax import lax
from jax.experimental import pallas as pl
from jax.experimental.pallas import tpu as pltpu

D_MODEL = 2048
D_CONV = 1024
N_HEADS = 16
N_KV_HEADS = 4
HEAD_DIM = 64
D_ATTN = N_HEADS * HEAD_DIM
D_KV = N_KV_HEADS * HEAD_DIM
D_IN = 3 * D_CONV + D_ATTN + 2 * D_KV
ATTN_BLOCK = 128
N_GROUPS = 4
EXPERTS_PER_GROUP = 8
N_EXPERTS = N_GROUPS * EXPERTS_PER_GROUP
TOP_K = 2
D_EXPERT = 512
PLE_DIM = 256
EPS = 1e-6
NEG_INF = -1e30

LANES = 128
HALF = D_MODEL // 2
KV_DUP = 2 * D_KV

TM_PROJ = 512
TM_MIX = 512
TM_ROUTE = 1024
TM_DISP = 512
TM_COMB = 512
MOE_BLK = 512
N_CHUNK = 4
VMEM_LIMIT = 56 * 1024 * 1024

_F32 = jnp.float32
_BF16 = jnp.bfloat16
_U32 = jnp.uint32
_HI_MASK = 0xFFFF0000


def _rms(x, gain):
    return x * lax.rsqrt(jnp.mean(x * x, axis=-1, keepdims=True) + EPS) * gain


def _rms_split(x, gain):
    inv = lax.rsqrt(jnp.mean(x * x, axis=-1, keepdims=True) + EPS)
    return (x * gain).astype(_BF16), inv


def _dot(a, b):
    return jnp.dot(a, b, preferred_element_type=_F32)


def _pack_rows(x):
    h = x.shape[-1] // 2
    bits = lax.bitcast_convert_type(x.astype(_BF16).astype(_F32), _U32)
    return (bits[:, h:] & _U32(_HI_MASK)) | (bits[:, :h] >> 16)


def _unpack_rows(w):
    lo = lax.bitcast_convert_type(w << 16, _F32)
    hi = lax.bitcast_convert_type(w & _U32(_HI_MASK), _F32)
    return lo, hi


def _inproj_kernel(tiles_per_seq, x_ref, g_ref, w_ref, qg_ref, kg_ref, cw_ref, onc_ref,
                   yc_ref, q_ref, kd_ref, vd_ref, t_ref, uprev_ref):
    tm = x_ref.shape[0]
    is_first = (pl.program_id(0) % tiles_per_seq) == 0
    xn = _rms(x_ref[...], g_ref[...]).astype(_BF16)

    @pl.when(pl.program_id(0) == 0)
    def _():
        uprev_ref[...] = jnp.zeros_like(uprev_ref)

    lo = lax.broadcasted_iota(jnp.int32, (tm, LANES), 1) < HEAD_DIM
    q = _dot(xn, w_ref[:, 3 * D_CONV:3 * D_CONV + D_ATTN])
    scale = HEAD_DIM ** -0.5
    for pr in range(D_ATTN // LANES):
        qp = q[:, pr * LANES:(pr + 1) * LANES]
        sq = qp * qp
        s_lo = jnp.sum(jnp.where(lo, sq, 0.0), axis=-1, keepdims=True)
        s_hi = jnp.sum(jnp.where(lo, 0.0, sq), axis=-1, keepdims=True)
        ms = jnp.where(lo, s_lo, s_hi) * (1.0 / HEAD_DIM)
        qn = (qp * lax.rsqrt(ms + EPS) * qg_ref[...] * scale).astype(_BF16)
        q_lo = jnp.where(lo, qn, jnp.zeros_like(qn))
        q_hi = jnp.where(lo, jnp.zeros_like(qn), qn)
        for j in range(tm // ATTN_BLOCK):
            rows = slice(j * ATTN_BLOCK, (j + 1) * ATTN_BLOCK)
            base = (j * N_HEADS + 2 * pr) * ATTN_BLOCK
            q_ref[base:base + ATTN_BLOCK, :] = q_lo[rows]
            q_ref[base + ATTN_BLOCK:base + 2 * ATTN_BLOCK, :] = q_hi[rows]

    kv = _dot(xn, w_ref[:, 3 * D_CONV + D_ATTN:D_IN])
    for pr in range(2 * D_KV // LANES):
        pair = kv[:, pr * LANES:(pr + 1) * LANES]
        swapped = pltpu.roll(pair, HEAD_DIM, 1)
        for half in range(2):
            dup = jnp.where(lo, pair, swapped) if half == 0 else jnp.where(lo, swapped, pair)
            head = 2 * pr + half
            if head < N_KV_HEADS:
                ms = jnp.sum(dup * dup, axis=-1, keepdims=True) * (1.0 / LANES)
                dup = dup * lax.rsqrt(ms + EPS) * kg_ref[...]
                kd_ref[:, head * LANES:(head + 1) * LANES] = dup.astype(_BF16)
            else:
                head -= N_KV_HEADS
                vd_ref[:, head * LANES:(head + 1) * LANES] = dup.astype(_BF16)

    ch = 256
    row = lax.broadcasted_iota(jnp.int32, (tm, ch), 0)
    last = uprev_ref.shape[0] - 1
    ssq = jnp.zeros((tm, 1), _F32)
    for j in range(D_CONV // ch):
        cs = slice(j * ch, (j + 1) * ch)
        b = _dot(xn, w_ref[:, j * ch:(j + 1) * ch])
        c = _dot(xn, w_ref[:, D_CONV + j * ch:D_CONV + (j + 1) * ch])
        hc = _dot(xn, w_ref[:, 2 * D_CONV + j * ch:2 * D_CONV + (j + 1) * ch])
        u = c * hc
        uh = jnp.where(is_first, 0.0, uprev_ref[:, cs])
        uprev_ref[:, cs] = u[tm - last - 1:tm]
        u1 = jnp.where(row == 0, uh[last:last + 1], pltpu.roll(u, 1, 0))
        u2 = jnp.where(row == 0, uh[last - 1:last],
                       jnp.where(row == 1, uh[last:last + 1], pltpu.roll(u, 2, 0)))
        t = b * (cw_ref[2:3, cs] * u + cw_ref[1:2, cs] * u1 + cw_ref[0:1, cs] * u2)
        ssq = ssq + jnp.sum(t * t, axis=-1, keepdims=True)
        t_ref[:, cs] = t
    inv = lax.rsqrt(ssq * (1.0 / D_CONV) + EPS)
    yc_ref[...] = (t_ref[...] * inv * onc_ref[...]).astype(_BF16)


def _inproj(x2, g, w_in, qg2, kg2, cw, onc, seq):
    n = x2.shape[0]
    tm = TM_PROJ
    const = lambda i: (0, 0)
    return pl.pallas_call(
        functools.partial(_inproj_kernel, seq // tm),
        grid=(n // tm,),
        in_specs=[
            pl.BlockSpec((tm, D_MODEL), lambda i: (i, 0)),
            pl.BlockSpec((1, D_MODEL), const),
            pl.BlockSpec((D_MODEL, D_IN), const, pipeline_mode=pl.Buffered(1)),
            pl.BlockSpec((1, LANES), const),
            pl.BlockSpec((1, LANES), const),
            pl.BlockSpec((3, D_CONV), const),
            pl.BlockSpec((1, D_CONV), const),
        ],
        out_specs=[
            pl.BlockSpec((tm, D_CONV), lambda i: (i, 0)),
            pl.BlockSpec((tm * N_HEADS, LANES), lambda i: (i, 0)),
            pl.BlockSpec((tm, KV_DUP), lambda i: (i, 0)),
            pl.BlockSpec((tm, KV_DUP), lambda i: (i, 0)),
        ],
        out_shape=[
            jax.ShapeDtypeStruct((n, D_CONV), _BF16),
            jax.ShapeDtypeStruct((n * N_HEADS, LANES), _BF16),
            jax.ShapeDtypeStruct((n, KV_DUP), _BF16),
            jax.ShapeDtypeStruct((n, KV_DUP), _BF16),
        ],
        scratch_shapes=[pltpu.VMEM((tm, D_CONV), _F32),
                        pltpu.VMEM((8, D_CONV), _F32)],
        compiler_params=pltpu.CompilerParams(
            dimension_semantics=("arbitrary",), vmem_limit_bytes=VMEM_LIMIT),
        name="inproj",
    )(x2, g, w_in, qg2, kg2, cw, onc)


def _mixer_kernel(tiles_per_seq,
                  yc_ref, q_ref, kd_ref, vd_ref, kdh_ref, vdh_ref, x_ref,
                  tab_ref, sink_ref, ona_ref, wout_ref, nffn_ref,
                  wr_ref, rb_ref,
                  h1_ref, xp_ref, lg_ref, ya_ref):
    tm = x_ref.shape[0]
    blk = ATTN_BLOCK
    is_first = (pl.program_id(0) % tiles_per_seq) == 0

    lo = lax.broadcasted_iota(jnp.int32, (blk, LANES), 1) < HEAD_DIM
    qi = lax.broadcasted_iota(jnp.int32, (4 * blk, blk), 0) % blk
    from_prev = lax.broadcasted_iota(jnp.int32, (4 * blk, blk), 1) > qi
    no_prev = jnp.logical_and(is_first, from_prev)
    zero_p = jnp.zeros((4 * blk, blk), _BF16)
    for j in range(tm // blk):
        rows = slice(j * blk, (j + 1) * blk)
        for g in range(N_KV_HEADS):
            cols = slice(g * LANES, (g + 1) * LANES)
            grp = (j * N_HEADS + 4 * g) * blk
            lhs = q_ref[grp:grp + 4 * blk, :]
            if j == 0:
                keys = jnp.concatenate([kdh_ref[:, cols], kd_ref[0:blk, cols]], axis=0)
                vals = jnp.concatenate([vdh_ref[:, cols], vd_ref[0:blk, cols]], axis=0)
            else:
                keys = kd_ref[(j - 1) * blk:(j + 1) * blk, cols]
                vals = vd_ref[(j - 1) * blk:(j + 1) * blk, cols]
            s = lax.dot_general(lhs, keys, (((1,), (1,)), ((), ())),
                                preferred_element_type=_F32)
            s = jnp.where(from_prev, s[:, 0:blk], s[:, blk:2 * blk]) + tab_ref[g]
            if j == 0:
                s = jnp.where(no_prev, NEG_INF, s)
            sink = sink_ref[g]
            m = jnp.maximum(jnp.max(s, axis=-1, keepdims=True), sink)
            p = jnp.exp(s - m)
            denom = jnp.sum(p, axis=-1, keepdims=True) + jnp.exp(sink - m)
            p = p.astype(_BF16)
            p = jnp.concatenate([jnp.where(from_prev, p, zero_p),
                                 jnp.where(from_prev, zero_p, p)], axis=1)
            o = _dot(p, vals) / denom
            ya_ref[rows, 2 * g * LANES:(2 * g + 1) * LANES] = jnp.where(
                lo, o[0:blk], o[blk:2 * blk])
            ya_ref[rows, (2 * g + 1) * LANES:(2 * g + 2) * LANES] = jnp.where(
                lo, o[2 * blk:3 * blk], o[3 * blk:4 * blk])

    ya, inv_a = _rms_split(ya_ref[...], ona_ref[...])
    h1 = (x_ref[...] + _dot(yc_ref[...], wout_ref[0:D_CONV, :])
          + _dot(ya, wout_ref[D_CONV:D_CONV + D_ATTN, :]) * inv_a)
    h1_ref[...] = h1

    xn = _rms(h1, nffn_ref[...])
    xp_ref[...] = _pack_rows(xn)
    lg_ref[...] = _dot(xn.astype(_BF16), wr_ref[...]) + rb_ref[...]


def _mixer(yc, q, kd, vd, x2, tab, sink, ona, w_out, nffn, wr, rb, seq):
    n = x2.shape[0]
    tm = TM_MIX
    const2 = lambda i: (0, 0)
    const3 = lambda i: (0, 0, 0)
    prev_blk = lambda i: (jnp.maximum(i * (tm // ATTN_BLOCK) - 1, 0), 0)
    return pl.pallas_call(
        functools.partial(_mixer_kernel, seq // tm),
        grid=(n // tm,),
        in_specs=[
            pl.BlockSpec((tm, D_CONV), lambda i: (i, 0)),
            pl.BlockSpec((tm * N_HEADS, LANES), lambda i: (i, 0)),
            pl.BlockSpec((tm, KV_DUP), lambda i: (i, 0)),
            pl.BlockSpec((tm, KV_DUP), lambda i: (i, 0)),
            pl.BlockSpec((ATTN_BLOCK, KV_DUP), prev_blk),
            pl.BlockSpec((ATTN_BLOCK, KV_DUP), prev_blk),
            pl.BlockSpec((tm, D_MODEL), lambda i: (i, 0)),
            pl.BlockSpec((N_KV_HEADS, 4 * ATTN_BLOCK, ATTN_BLOCK), const3,
                         pipeline_mode=pl.Buffered(1)),
            pl.BlockSpec((N_KV_HEADS, 4 * ATTN_BLOCK, 1), const3, pipeline_mode=pl.Buffered(1)),
            pl.BlockSpec((1, D_ATTN), const2),
            pl.BlockSpec((D_MODEL, D_MODEL), const2, pipeline_mode=pl.Buffered(1)),
            pl.BlockSpec((1, D_MODEL), const2),
            pl.BlockSpec((D_MODEL, LANES), const2),
            pl.BlockSpec((1, LANES), const2),
        ],
        out_specs=[
            pl.BlockSpec((tm, D_MODEL), lambda i: (i, 0)),
            pl.BlockSpec((tm, HALF), lambda i: (i, 0)),
            pl.BlockSpec((tm, LANES), lambda i: (i, 0)),
        ],
        out_shape=[
            jax.ShapeDtypeStruct((n, D_MODEL), _F32),
            jax.ShapeDtypeStruct((n, HALF), _U32),
            jax.ShapeDtypeStruct((n, LANES), _F32),
        ],
        scratch_shapes=[pltpu.VMEM((tm, D_ATTN), _F32)],
        compiler_params=pltpu.CompilerParams(
            dimension_semantics=("parallel",), vmem_limit_bytes=VMEM_LIMIT),
        name="mixer",
    )(yc, q, kd, vd, kd, vd, x2, tab, sink, ona, w_out, nffn, wr, rb)


def _route_kernel(lg_ref, route_ref, route_t_ref, cnt_ref, run_ref, earlier_ref):
    t = lg_ref.shape[0]

    @pl.when(pl.program_id(0) == 0)
    def _():
        run_ref[...] = jnp.zeros_like(run_ref)
        earlier_ref[...] = (lax.broadcasted_iota(jnp.int32, (t, t), 0)
                            > lax.broadcasted_iota(jnp.int32, (t, t), 1)).astype(_BF16)

    lg = lg_ref[...]
    lane = lax.broadcasted_iota(jnp.int32, (t, LANES), 1)
    ninf = -jnp.inf

    lane_f = lane.astype(_F32)

    def first_max(v):
        m = jnp.max(v, axis=-1, keepdims=True)
        idx = jnp.min(jnp.where(v == m, lane_f, float(LANES)), axis=-1, keepdims=True)
        return m, idx.astype(jnp.int32)

    is_g = lane < N_GROUPS
    gmax, gidx = first_max(jnp.where(is_g, lg, ninf))
    gsum = jnp.sum(jnp.where(is_g, jnp.exp(jnp.where(is_g, lg, gmax) - gmax), 0.0),
                   axis=-1, keepdims=True)
    g_w = 1.0 / gsum
    base = N_GROUPS + EXPERTS_PER_GROUP * gidx
    in_grp = jnp.logical_and(lane >= base, lane < base + EXPERTS_PER_GROUP)
    el = jnp.where(in_grp, lg, ninf)
    m1, i1 = first_max(el)
    m2, i2 = first_max(jnp.where(lane == i1, ninf, el))
    e21 = jnp.exp(m2 - m1)
    w1 = g_w / (1.0 + e21)
    w2 = g_w * e21 / (1.0 + e21)
    e1 = i1 - N_GROUPS
    e2 = i2 - N_GROUPS

    oh1 = lane == e1
    oh2 = lane == e2
    c12 = _dot(earlier_ref[...],
               jnp.concatenate([oh1.astype(_BF16), oh2.astype(_BF16)], axis=1))
    c1 = c12[:, 0:LANES]
    c2 = c12[:, LANES:2 * LANES]
    tot1 = jnp.sum(oh1.astype(_F32), axis=0, keepdims=True)
    tot2 = jnp.sum(oh2.astype(_F32), axis=0, keepdims=True)
    run = run_ref[0:1, :]
    r1 = jnp.sum(jnp.where(oh1, run + c1, 0.0), axis=-1, keepdims=True)
    r2 = jnp.sum(jnp.where(oh2, run + tot1 + c2, 0.0), axis=-1, keepdims=True)
    run_ref[...] = jnp.broadcast_to(run + tot1 + tot2, run_ref.shape)
    cnt_ref[...] = run_ref[...]

    out = jnp.where(lane == 0, e1.astype(_F32), 0.0)
    out = jnp.where(lane == 1, e2.astype(_F32), out)
    out = jnp.where(lane == 2, w1, out)
    out = jnp.where(lane == 3, w2, out)
    out = jnp.where(lane == 4, r1, out)
    out = jnp.where(lane == 5, r2, out)
    route_ref[...] = out
    route_t_ref[...] = out.T[0:8, :]


def _route(logits):
    n = logits.shape[0]
    t = TM_ROUTE
    return pl.pallas_call(
        _route_kernel,
        grid=(n // t,),
        in_specs=[pl.BlockSpec((t, LANES), lambda i: (i, 0))],
        out_specs=[pl.BlockSpec((t, LANES), lambda i: (i, 0)),
                   pl.BlockSpec((8, t), lambda i: (0, i)),
                   pl.BlockSpec((8, LANES), lambda i: (0, 0))],
        out_shape=[jax.ShapeDtypeStruct((n, LANES), _F32),
                   jax.ShapeDtypeStruct((8, n), _F32),
                   jax.ShapeDtypeStruct((8, LANES), _F32)],
        scratch_shapes=[pltpu.VMEM((8, LANES), _F32), pltpu.VMEM((t, t), _BF16)],
        compiler_params=pltpu.CompilerParams(dimension_semantics=("arbitrary",)),
        name="route",
    )(logits)


def _dispatch_kernel(pad_end_ref, padded_ref, dest_ref, xp_ref, xs_ref,
                     buf_ref, zero_ref, lsem, rsem, zsem):
    td = buf_ref.shape[1]
    nb = xs_ref.shape[0] // MOE_BLK
    nb_used = pad_end_ref[N_EXPERTS - 1] // MOE_BLK
    i = pl.program_id(0)
    n_steps = pl.num_programs(0)
    slot = i % 3
    nxt = (i + 1) % 3
    prv = (i + 2) % 3

    def zero_copy(start):
        start = pl.multiple_of(start, MOE_BLK)
        return pltpu.make_async_copy(zero_ref, xs_ref.at[pl.ds(start, MOE_BLK)], zsem)

    def zero_blocks(fn):
        for e in range(N_EXPERTS):
            @pl.when(padded_ref[e] > 0)
            def _():
                fn(zero_copy(pad_end_ref[e] - MOE_BLK))
        for b in range(nb - N_EXPERTS, nb):
            @pl.when(b >= nb_used)
            def _():
                fn(zero_copy(b * MOE_BLK))

    def load(t, s):
        start = pl.multiple_of(t * td, td)
        return pltpu.make_async_copy(xp_ref.at[pl.ds(start, td)], buf_ref.at[s], lsem.at[s])

    def wait_rows(s):
        for k in range(TOP_K):
            pltpu.make_async_copy(buf_ref.at[s], xs_ref.at[pl.ds(0, td)], rsem.at[s]).wait()

    @pl.when(i == 0)
    def _():
        zero_ref[...] = jnp.zeros_like(zero_ref)
        zero_blocks(lambda c: c.start())
        zero_blocks(lambda c: c.wait())
        load(0, 0).start()

    @pl.when(i + 1 < n_steps)
    def _():
        load(i + 1, nxt).start()

    load(i, slot).wait()

    def issue(r, c):
        for k in range(TOP_K):
            d = dest_ref[0, 0, k * td + r]
            pltpu.make_async_copy(buf_ref.at[slot, pl.ds(r, 1)], xs_ref.at[pl.ds(d, 1)],
                                  rsem.at[slot]).start()
        return c

    lax.fori_loop(0, td, issue, 0, unroll=8)

    @pl.when(i > 0)
    def _():
        wait_rows(prv)

    @pl.when(i == n_steps - 1)
    def _():
        wait_rows(slot)


def _dispatch(xp, dest3, pad_end, padded, n_rows):
    steps = dest3.shape[0]
    td = dest3.shape[-1] // TOP_K
    return pl.pallas_call(
        _dispatch_kernel,
        grid_spec=pltpu.PrefetchScalarGridSpec(
            num_scalar_prefetch=2,
            grid=(steps,),
            in_specs=[
                pl.BlockSpec((1, 1, dest3.shape[-1]), lambda i, pe, pd: (i, 0, 0),
                             memory_space=pltpu.SMEM),
                pl.BlockSpec(memory_space=pl.ANY),
            ],
            out_specs=pl.BlockSpec(memory_space=pl.ANY),
            scratch_shapes=[pltpu.VMEM((3, td, HALF), _U32),
                            pltpu.VMEM((MOE_BLK, HALF), _U32),
                            pltpu.SemaphoreType.DMA((3,)), pltpu.SemaphoreType.DMA((3,)),
                            pltpu.SemaphoreType.DMA],
        ),
        out_shape=jax.ShapeDtypeStruct((n_rows, HALF), _U32),
        compiler_params=pltpu.CompilerParams(
            dimension_semantics=("arbitrary",)),
        name="dispatch",
    )(pad_end, padded, dest3, xp)


def _experts_kernel(be_ref, nxt_ref, nb_ref, xs_ref, wg_hbm, wu_hbm, wd_hbm, ys_ref,
                    wgs_ref, wus_ref, wds_ref, wgb_ref, wub_ref, wdb_ref, sem):
    i = pl.program_id(0)
    valid = i < nb_ref[0]
    e = be_ref[i]
    changed = jnp.logical_or(i == 0, e != be_ref[jnp.maximum(i - 1, 0)])

    def fetch(ex):
        return (pltpu.make_async_copy(wg_hbm.at[ex], wgs_ref, sem.at[0]),
                pltpu.make_async_copy(wu_hbm.at[ex], wus_ref, sem.at[1]),
                pltpu.make_async_copy(wd_hbm.at[ex], wds_ref, sem.at[2]))

    @pl.when(i == 0)
    def _():
        for c in fetch(e):
            c.start()

    @pl.when(jnp.logical_and(valid, changed))
    def _():
        for c in fetch(e):
            c.wait()
        wgb_ref[...] = wgs_ref[...].astype(_BF16)
        wub_ref[...] = wus_ref[...].astype(_BF16)
        wdb_ref[...] = wds_ref[...].astype(_BF16)
        nxt = nxt_ref[i]

        @pl.when(nxt >= 0)
        def _():
            for c in fetch(nxt):
                c.start()

    @pl.when(valid)
    def _():
        lo, hi = _unpack_rows(xs_ref[...])
        lo = lo.astype(_BF16)
        hi = hi.astype(_BF16)
        g = _dot(lo, wgb_ref[0:HALF, :]) + _dot(hi, wgb_ref[HALF:D_MODEL, :])
        u = _dot(lo, wub_ref[0:HALF, :]) + _dot(hi, wub_ref[HALF:D_MODEL, :])
        hidden = (g * jax.nn.sigmoid(g) * u).astype(_BF16)
        ys_ref[...] = _pack_rows(_dot(hidden, wdb_ref[...]))

    @pl.when(jnp.logical_not(valid))
    def _():
        ys_ref[...] = jnp.zeros_like(ys_ref)


def _experts(xs, w_gate, w_up, w_down, block_e, next_e, nb_used):
    n_rows = xs.shape[0]
    nb = n_rows // MOE_BLK
    row_blk = lambda i, be, nx, nbu: (jnp.minimum(i, nbu[0] - 1), 0)
    any_spec = pl.BlockSpec(memory_space=pl.ANY)
    return pl.pallas_call(
        _experts_kernel,
        grid_spec=pltpu.PrefetchScalarGridSpec(
            num_scalar_prefetch=3,
            grid=(nb,),
            in_specs=[pl.BlockSpec((MOE_BLK, HALF), row_blk), any_spec, any_spec, any_spec],
            out_specs=pl.BlockSpec((MOE_BLK, HALF), lambda i, be, nx, nbu: (i, 0)),
            scratch_shapes=[pltpu.VMEM((D_MODEL, D_EXPERT), _F32),
                            pltpu.VMEM((D_MODEL, D_EXPERT), _F32),
                            pltpu.VMEM((D_EXPERT, D_MODEL), _F32),
                            pltpu.VMEM((D_MODEL, D_EXPERT), _BF16),
                            pltpu.VMEM((D_MODEL, D_EXPERT), _BF16),
                            pltpu.VMEM((D_EXPERT, D_MODEL), _BF16),
                            pltpu.SemaphoreType.DMA((3,))],
        ),
        out_shape=jax.ShapeDtypeStruct((n_rows, HALF), _U32),
        compiler_params=pltpu.CompilerParams(
            dimension_semantics=("arbitrary",), vmem_limit_bytes=VMEM_LIMIT),
        name="experts",
    )(block_e, next_e, nb_used, xs, w_gate, w_up, w_down)


def _combine_kernel(h1_ref, route_ref, dcur_ref, dnext_ref, ys_ref, p_ref, nple_ref,
                    wgate_ref, wple_ref, out_ref, ybuf_ref, h2_ref, xn_ref, pb_ref, sem):
    tk = h1_ref.shape[0]
    cw = D_MODEL // N_CHUNK
    per = TOP_K * tk // N_CHUNK
    i = pl.program_id(0)
    n_steps = pl.num_programs(0)
    slot = i % 2

    def row_copy(d, s, k, r):
        return pltpu.make_async_copy(ys_ref.at[pl.ds(d, 1)],
                                     ybuf_ref.at[s, k, pl.ds(r, 1)], sem.at[s])

    def wait_slot(s):
        for k in range(TOP_K):
            pltpu.make_async_copy(ys_ref.at[pl.ds(0, tk)], ybuf_ref.at[s, k], sem.at[s]).wait()

    @pl.when(i == 0)
    def _():
        def body(r, c):
            for k in range(TOP_K):
                row_copy(dcur_ref[0, 0, k * tk + r], 0, k, r).start()
            return c
        lax.fori_loop(0, tk, body, 0, unroll=8)

    wait_slot(slot)
    y1 = jnp.concatenate(_unpack_rows(ybuf_ref[slot, 0]), axis=-1)
    y2 = jnp.concatenate(_unpack_rows(ybuf_ref[slot, 1]), axis=-1)
    route = route_ref[...]
    h2 = h1_ref[...] + route[:, 2:3] * y1 + route[:, 3:4] * y2
    xn, inv = _rms_split(h2, nple_ref[...])
    pb = p_ref[...].astype(_BF16)

    def issue_batch(c):
        for j in range(per):
            k, r = j % TOP_K, c * (per // TOP_K) + j // TOP_K
            row_copy(dnext_ref[0, 0, k * tk + r], 1 - slot, k, r).start()

    issue_batch(0)
    h2_ref[...] = h2
    xn_ref[...] = xn
    pb_ref[...] = pb
    for c in range(N_CHUNK):
        if c > 0:
            issue_batch(c)
        cols = slice(c * cw, (c + 1) * cw)
        gate = jax.nn.sigmoid(_dot(xn_ref[...], wgate_ref[:, cols]) * inv)
        emb = _dot(pb_ref[...], wple_ref[:, cols])
        out_ref[:, cols] = h2_ref[:, cols] + gate * emb

    @pl.when(i == n_steps - 1)
    def _():
        wait_slot(1 - slot)


def _combine(h1, route, dest3, ys, p2, nple, w_gate, w_ple):
    n = h1.shape[0]
    tk = TM_COMB
    steps = n // tk
    const2 = lambda i: (0, 0)
    return pl.pallas_call(
        _combine_kernel,
        grid=(steps,),
        in_specs=[
            pl.BlockSpec((tk, D_MODEL), lambda i: (i, 0)),
            pl.BlockSpec((tk, LANES), lambda i: (i, 0)),
            pl.BlockSpec((1, 1, TOP_K * tk), lambda i: (i, 0, 0), memory_space=pltpu.SMEM),
            pl.BlockSpec((1, 1, TOP_K * tk), lambda i: (jnp.minimum(i + 1, steps - 1), 0, 0),
                         memory_space=pltpu.SMEM),
            pl.BlockSpec(memory_space=pl.ANY),
            pl.BlockSpec((tk, PLE_DIM), lambda i: (i, 0)),
            pl.BlockSpec((1, D_MODEL), const2),
            pl.BlockSpec((D_MODEL, D_MODEL), const2, pipeline_mode=pl.Buffered(1)),
            pl.BlockSpec((PLE_DIM, D_MODEL), const2, pipeline_mode=pl.Buffered(1)),
        ],
        out_specs=pl.BlockSpec((tk, D_MODEL), lambda i: (i, 0)),
        out_shape=jax.ShapeDtypeStruct((n, D_MODEL), _F32),
        scratch_shapes=[pltpu.VMEM((2, TOP_K, tk, HALF), _U32),
                        pltpu.VMEM((tk, D_MODEL), _F32),
                        pltpu.VMEM((tk, D_MODEL), _BF16),
                        pltpu.VMEM((tk, PLE_DIM), _BF16),
                        pltpu.SemaphoreType.DMA((2,))],
        compiler_params=pltpu.CompilerParams(
            dimension_semantics=("arbitrary",), vmem_limit_bytes=VMEM_LIMIT),
        name="combine",
    )(h1, route, dest3, dest3, ys, p2, nple, w_gate, w_ple)


def _attention_tables(sinks):
    slopes = 2.0 ** (-8.0 * jnp.arange(1, N_HEADS + 1, dtype=_F32) / N_HEADS)
    qi = jnp.arange(ATTN_BLOCK)[:, None]
    kj = jnp.arange(ATTN_BLOCK)[None, :]
    dist = jnp.where(kj > qi, ATTN_BLOCK + qi - kj, qi - kj)
    tab = -slopes[:, None, None] * dist.astype(_F32)[None]
    tab = tab.reshape(N_KV_HEADS, 4 * ATTN_BLOCK, ATTN_BLOCK)
    sink = jnp.repeat(sinks.astype(_F32), ATTN_BLOCK).reshape(N_KV_HEADS, 4 * ATTN_BLOCK, 1)
    return tab, sink


def _layer(h, p_i, norm_mix, w_in, conv_w, q_norm, k_norm, sinks, out_norm_conv,
           out_norm_attn, w_out, norm_ffn, w_group, b_group, w_router, b_router,
           w_gate, w_up, w_down, norm_ple, w_ple_gate, w_ple):
    bsz, seq, _ = h.shape
    n = bsz * seq
    x2 = h.reshape(n, D_MODEL)
    row = lambda v: v.reshape(1, -1).astype(_F32)

    yc, q, kd, vd = _inproj(
        x2, row(norm_mix), w_in.astype(_BF16), row(jnp.tile(q_norm, 2)),
        row(jnp.tile(k_norm, 2)), conv_w.astype(_F32), row(out_norm_conv), seq)

    tab, sink = _attention_tables(sinks)
    w_r = jnp.zeros((D_MODEL, LANES), _F32)
    w_r = w_r.at[:, :N_GROUPS].set(w_group).at[:, N_GROUPS:N_GROUPS + N_EXPERTS].set(w_router)
    r_bias = jnp.zeros((1, LANES), _F32)
    r_bias = r_bias.at[0, :N_GROUPS].set(b_group).at[0, N_GROUPS:N_GROUPS + N_EXPERTS].set(b_router)
    h1, xp, logits = _mixer(
        yc, q, kd, vd, x2, tab, sink, row(out_norm_attn), w_out.astype(_BF16),
        row(norm_ffn), w_r.astype(_BF16), r_bias, seq)

    route, route_t, counts = _route(logits)

    cnt = counts[0, :N_EXPERTS].astype(jnp.int32)
    padded = (cnt + MOE_BLK - 1) // MOE_BLK * MOE_BLK
    pad_end = jnp.cumsum(padded)
    pad_start = pad_end - padded
    eid = route_t[0:TOP_K].astype(jnp.int32)
    onehot = eid[..., None] == jnp.arange(N_EXPERTS, dtype=jnp.int32)
    dest = (jnp.sum(jnp.where(onehot, pad_start, 0), axis=-1)
            + route_t[4:4 + TOP_K].astype(jnp.int32))

    def step_blocks(tile):
        return dest.reshape(TOP_K, n // tile, 1, tile).transpose(1, 2, 0, 3).reshape(
            n // tile, 1, TOP_K * tile)
    n_rows = n * TOP_K + N_EXPERTS * MOE_BLK
    nb = n_rows // MOE_BLK
    nb_used = (pad_end[-1] // MOE_BLK).astype(jnp.int32)
    blk_start = jnp.arange(nb, dtype=jnp.int32) * MOE_BLK
    block_e = jnp.sum((blk_start[:, None] >= pad_end[None, :]).astype(jnp.int32), axis=1)
    block_e = jnp.minimum(block_e, N_EXPERTS - 1)
    block_e = jnp.where(jnp.arange(nb) < nb_used, block_e, block_e[nb_used - 1])
    ids = jnp.arange(N_EXPERTS, dtype=jnp.int32)
    later = jnp.logical_and(ids[None, :] > ids[:, None], cnt[None, :] > 0)
    next_present = jnp.min(jnp.where(later, ids[None, :], N_EXPERTS), axis=1)
    next_present = jnp.where(next_present < N_EXPERTS, next_present, -1)
    onehot_b = block_e[:, None] == ids[None, :]
    next_e = jnp.sum(jnp.where(onehot_b, next_present[None, :], 0), axis=1).astype(jnp.int32)

    xs = _dispatch(xp, step_blocks(TM_DISP),
                   pad_end.astype(jnp.int32), padded.astype(jnp.int32), n_rows)
    ys = _experts(xs, w_gate, w_up, w_down, block_e, next_e, nb_used.reshape(1))
    out = _combine(h1, route, step_blocks(TM_COMB), ys,
                   p_i.reshape(n, PLE_DIM), row(norm_ple), w_ple_gate.astype(_BF16),
                   w_ple.astype(_BF16))
    return out.reshape(bsz, seq, D_MODEL)


def kernel(x, p, norm_mix, w_in, conv_w, q_norm, k_norm, sinks, out_norm_conv, out_norm_attn,
           w_out, norm_ffn, w_group, b_group, w_router, b_router, w_gate, w_up, w_down,
           norm_ple, w_ple_gate, w_ple):
    h = x
    for i in range(p.shape[0]):
        h = _layer(h, p[i], norm_mix[i], w_in[i], conv_w[i], q_norm[i], k_norm[i], sinks[i],
                   out_norm_conv[i], out_norm_attn[i], w_out[i], norm_ffn[i], w_group[i],
                   b_group[i], w_router[i], b_router[i], w_gate[i], w_up[i], w_down[i],
                   norm_ple[i], w_ple_gate[i], w_ple[i])
    return h
```

```python
import functools

import jax
import jax.numpy as jnp
from jax import lax
from jax.experimental import pallas as pl
from jax.experimental.pallas import tpu as pltpu

D_MODEL = 2048
D_CONV = 1024
N_HEADS = 16
N_KV_HEADS = 4
HEAD_DIM = 64
D_ATTN = N_HEADS * HEAD_DIM
D_KV = N_KV_HEADS * HEAD_DIM
D_IN = 3 * D_CONV + D_ATTN + 2 * D_KV
ATTN_BLOCK = 128
N_GROUPS = 4
EXPERTS_PER_GROUP = 8
N_EXPERTS = N_GROUPS * EXPERTS_PER_GROUP
TOP_K = 2
D_EXPERT = 512
PLE_DIM = 256
EPS = 1e-6
NEG_INF = -1e30

LANES = 128
HALF = D_MODEL // 2
KV_DUP = 2 * D_KV

TM_PROJ = 512
TM_MIX = 512
TM_ROUTE = 1024
TM_DISP = 1024
TM_COMB = 512
MOE_BLK = 512
N_CHUNK = 4
VMEM_LIMIT = 56 * 1024 * 1024

_F32 = jnp.float32
_BF16 = jnp.bfloat16
_U32 = jnp.uint32
_HI_MASK = 0xFFFF0000


def _rms(x, gain):
    return x * lax.rsqrt(jnp.mean(x * x, axis=-1, keepdims=True) + EPS) * gain


def _rms_split(x, gain):
    inv = lax.rsqrt(jnp.mean(x * x, axis=-1, keepdims=True) + EPS)
    return (x * gain).astype(_BF16), inv


def _dot(a, b):
    return jnp.dot(a, b, preferred_element_type=_F32)


def _pack_rows(x):
    h = x.shape[-1] // 2
    bits = lax.bitcast_convert_type(x.astype(_BF16).astype(_F32), _U32)
    return (bits[:, h:] & _U32(_HI_MASK)) | (bits[:, :h] >> 16)


def _unpack_rows(w):
    lo = lax.bitcast_convert_type(w << 16, _F32)
    hi = lax.bitcast_convert_type(w & _U32(_HI_MASK), _F32)
    return lo, hi


def _inproj_kernel(tiles_per_seq, x_ref, g_ref, w_ref, qg_ref, kg_ref, cw_ref, onc_ref,
                   yc_ref, q_ref, kd_ref, vd_ref, t_ref, uprev_ref):
    tm = x_ref.shape[0]
    is_first = (pl.program_id(0) % tiles_per_seq) == 0
    xn = _rms(x_ref[...], g_ref[...]).astype(_BF16)

    @pl.when(pl.program_id(0) == 0)
    def _():
        uprev_ref[...] = jnp.zeros_like(uprev_ref)

    lo = lax.broadcasted_iota(jnp.int32, (tm, LANES), 1) < HEAD_DIM
    q = _dot(xn, w_ref[:, 3 * D_CONV:3 * D_CONV + D_ATTN])
    scale = HEAD_DIM ** -0.5
    for pr in range(D_ATTN // LANES):
        qp = q[:, pr * LANES:(pr + 1) * LANES]
        sq = qp * qp
        s_lo = jnp.sum(jnp.where(lo, sq, 0.0), axis=-1, keepdims=True)
        s_hi = jnp.sum(jnp.where(lo, 0.0, sq), axis=-1, keepdims=True)
        ms = jnp.where(lo, s_lo, s_hi) * (1.0 / HEAD_DIM)
        qn = (qp * lax.rsqrt(ms + EPS) * qg_ref[...] * scale).astype(_BF16)
        q_lo = jnp.where(lo, qn, jnp.zeros_like(qn))
        q_hi = jnp.where(lo, jnp.zeros_like(qn), qn)
        for j in range(tm // ATTN_BLOCK):
            rows = slice(j * ATTN_BLOCK, (j + 1) * ATTN_BLOCK)
            base = (j * N_HEADS + 2 * pr) * ATTN_BLOCK
            q_ref[base:base + ATTN_BLOCK, :] = q_lo[rows]
            q_ref[base + ATTN_BLOCK:base + 2 * ATTN_BLOCK, :] = q_hi[rows]

    kv = _dot(xn, w_ref[:, 3 * D_CONV + D_ATTN:D_IN])
    for pr in range(2 * D_KV // LANES):
        pair = kv[:, pr * LANES:(pr + 1) * LANES]
        swapped = pltpu.roll(pair, HEAD_DIM, 1)
        for half in range(2):
            dup = jnp.where(lo, pair, swapped) if half == 0 else jnp.where(lo, swapped, pair)
            head = 2 * pr + half
            if head < N_KV_HEADS:
                ms = jnp.sum(dup * dup, axis=-1, keepdims=True) * (1.0 / LANES)
                dup = dup * lax.rsqrt(ms + EPS) * kg_ref[...]
                kd_ref[:, head * LANES:(head + 1) * LANES] = dup.astype(_BF16)
            else:
                head -= N_KV_HEADS
                vd_ref[:, head * LANES:(head + 1) * LANES] = dup.astype(_BF16)

    ch = 256
    row = lax.broadcasted_iota(jnp.int32, (tm, ch), 0)
    last = uprev_ref.shape[0] - 1
    ssq = jnp.zeros((tm, 1), _F32)
    for j in range(D_CONV // ch):
        cs = slice(j * ch, (j + 1) * ch)
        b = _dot(xn, w_ref[:, j * ch:(j + 1) * ch])
        c = _dot(xn, w_ref[:, D_CONV + j * ch:D_CONV + (j + 1) * ch])
        hc = _dot(xn, w_ref[:, 2 * D_CONV + j * ch:2 * D_CONV + (j + 1) * ch])
        u = c * hc
        uh = jnp.where(is_first, 0.0, uprev_ref[:, cs])
        uprev_ref[:, cs] = u[tm - last - 1:tm]
        u1 = jnp.where(row == 0, uh[last:last + 1], pltpu.roll(u, 1, 0))
        u2 = jnp.where(row == 0, uh[last - 1:last],
                       jnp.where(row == 1, uh[last:last + 1], pltpu.roll(u, 2, 0)))
        t = b * (cw_ref[2:3, cs] * u + cw_ref[1:2, cs] * u1 + cw_ref[0:1, cs] * u2)
        ssq = ssq + jnp.sum(t * t, axis=-1, keepdims=True)
        t_ref[:, cs] = t
    inv = lax.rsqrt(ssq * (1.0 / D_CONV) + EPS)
    yc_ref[...] = (t_ref[...] * inv * onc_ref[...]).astype(_BF16)


def _inproj(x2, g, w_in, qg2, kg2, cw, onc, seq):
    n = x2.shape[0]
    tm = TM_PROJ
    const = lambda i: (0, 0)
    return pl.pallas_call(
        functools.partial(_inproj_kernel, seq // tm),
        grid=(n // tm,),
        in_specs=[
            pl.BlockSpec((tm, D_MODEL), lambda i: (i, 0)),
            pl.BlockSpec((1, D_MODEL), const),
            pl.BlockSpec((D_MODEL, D_IN), const, pipeline_mode=pl.Buffered(1)),
            pl.BlockSpec((1, LANES), const),
            pl.BlockSpec((1, LANES), const),
            pl.BlockSpec((3, D_CONV), const),
            pl.BlockSpec((1, D_CONV), const),
        ],
        out_specs=[
            pl.BlockSpec((tm, D_CONV), lambda i: (i, 0)),
            pl.BlockSpec((tm * N_HEADS, LANES), lambda i: (i, 0)),
            pl.BlockSpec((tm, KV_DUP), lambda i: (i, 0)),
            pl.BlockSpec((tm, KV_DUP), lambda i: (i, 0)),
        ],
        out_shape=[
            jax.ShapeDtypeStruct((n, D_CONV), _BF16),
            jax.ShapeDtypeStruct((n * N_HEADS, LANES), _BF16),
            jax.ShapeDtypeStruct((n, KV_DUP), _BF16),
            jax.ShapeDtypeStruct((n, KV_DUP), _BF16),
        ],
        scratch_shapes=[pltpu.VMEM((tm, D_CONV), _F32),
                        pltpu.VMEM((8, D_CONV), _F32)],
        compiler_params=pltpu.CompilerParams(
            dimension_semantics=("arbitrary",), vmem_limit_bytes=VMEM_LIMIT),
        name="inproj",
    )(x2, g, w_in, qg2, kg2, cw, onc)


def _mixer_kernel(tiles_per_seq,
                  yc_ref, q_ref, kd_ref, vd_ref, kdh_ref, vdh_ref, x_ref,
                  tab_ref, sink_ref, ona_ref, wout_ref, nffn_ref,
                  wr_ref, rb_ref,
                  h1_ref, xp_ref, lg_ref, ya_ref):
    tm = x_ref.shape[0]
    blk = ATTN_BLOCK
    is_first = (pl.program_id(0) % tiles_per_seq) == 0

    lo = lax.broadcasted_iota(jnp.int32, (blk, LANES), 1) < HEAD_DIM
    qi = lax.broadcasted_iota(jnp.int32, (4 * blk, blk), 0) % blk
    from_prev = lax.broadcasted_iota(jnp.int32, (4 * blk, blk), 1) > qi
    no_prev = jnp.logical_and(is_first, from_prev)
    zero_p = jnp.zeros((4 * blk, blk), _BF16)
    for j in range(tm // blk):
        rows = slice(j * blk, (j + 1) * blk)
        for g in range(N_KV_HEADS):
            cols = slice(g * LANES, (g + 1) * LANES)
            grp = (j * N_HEADS + 4 * g) * blk
            lhs = q_ref[grp:grp + 4 * blk, :]
            if j == 0:
                keys = jnp.concatenate([kdh_ref[:, cols], kd_ref[0:blk, cols]], axis=0)
                vals = jnp.concatenate([vdh_ref[:, cols], vd_ref[0:blk, cols]], axis=0)
            else:
                keys = kd_ref[(j - 1) * blk:(j + 1) * blk, cols]
                vals = vd_ref[(j - 1) * blk:(j + 1) * blk, cols]
            s = lax.dot_general(lhs, keys, (((1,), (1,)), ((), ())),
                                preferred_element_type=_F32)
            s = jnp.where(from_prev, s[:, 0:blk], s[:, blk:2 * blk]) + tab_ref[g]
            if j == 0:
                s = jnp.where(no_prev, NEG_INF, s)
            sink = sink_ref[g]
            m = jnp.maximum(jnp.max(s, axis=-1, keepdims=True), sink)
            p = jnp.exp(s - m)
            denom = jnp.sum(p, axis=-1, keepdims=True) + jnp.exp(sink - m)
            p = p.astype(_BF16)
            p = jnp.concatenate([jnp.where(from_prev, p, zero_p),
                                 jnp.where(from_prev, zero_p, p)], axis=1)
            o = _dot(p, vals) / denom
            ya_ref[rows, 2 * g * LANES:(2 * g + 1) * LANES] = jnp.where(
                lo, o[0:blk], o[blk:2 * blk])
            ya_ref[rows, (2 * g + 1) * LANES:(2 * g + 2) * LANES] = jnp.where(
                lo, o[2 * blk:3 * blk], o[3 * blk:4 * blk])

    ya, inv_a = _rms_split(ya_ref[...], ona_ref[...])
    h1 = (x_ref[...] + _dot(yc_ref[...], wout_ref[0:D_CONV, :])
          + _dot(ya, wout_ref[D_CONV:D_CONV + D_ATTN, :]) * inv_a)
    h1_ref[...] = h1

    xn = _rms(h1, nffn_ref[...])
    xp_ref[...] = _pack_rows(xn)
    lg_ref[...] = _dot(xn.astype(_BF16), wr_ref[...]) + rb_ref[...]


def _mixer(yc, q, kd, vd, x2, tab, sink, ona, w_out, nffn, wr, rb, seq):
    n = x2.shape[0]
    tm = TM_MIX
    const2 = lambda i: (0, 0)
    const3 = lambda i: (0, 0, 0)
    prev_blk = lambda i: (jnp.maximum(i * (tm // ATTN_BLOCK) - 1, 0), 0)
    return pl.pallas_call(
        functools.partial(_mixer_kernel, seq // tm),
        grid=(n // tm,),
        in_specs=[
            pl.BlockSpec((tm, D_CONV), lambda i: (i, 0)),
            pl.BlockSpec((tm * N_HEADS, LANES), lambda i: (i, 0)),
            pl.BlockSpec((tm, KV_DUP), lambda i: (i, 0)),
            pl.BlockSpec((tm, KV_DUP), lambda i: (i, 0)),
            pl.BlockSpec((ATTN_BLOCK, KV_DUP), prev_blk),
            pl.BlockSpec((ATTN_BLOCK, KV_DUP), prev_blk),
            pl.BlockSpec((tm, D_MODEL), lambda i: (i, 0)),
            pl.BlockSpec((N_KV_HEADS, 4 * ATTN_BLOCK, ATTN_BLOCK), const3,
                         pipeline_mode=pl.Buffered(1)),
            pl.BlockSpec((N_KV_HEADS, 4 * ATTN_BLOCK, 1), const3, pipeline_mode=pl.Buffered(1)),
            pl.BlockSpec((1, D_ATTN), const2),
            pl.BlockSpec((D_MODEL, D_MODEL), const2, pipeline_mode=pl.Buffered(1)),
            pl.BlockSpec((1, D_MODEL), const2),
            pl.BlockSpec((D_MODEL, LANES), const2),
            pl.BlockSpec((1, LANES), const2),
        ],
        out_specs=[
            pl.BlockSpec((tm, D_MODEL), lambda i: (i, 0)),
            pl.BlockSpec((tm, HALF), lambda i: (i, 0)),
            pl.BlockSpec((tm, LANES), lambda i: (i, 0)),
        ],
        out_shape=[
            jax.ShapeDtypeStruct((n, D_MODEL), _F32),
            jax.ShapeDtypeStruct((n, HALF), _U32),
            jax.ShapeDtypeStruct((n, LANES), _F32),
        ],
        scratch_shapes=[pltpu.VMEM((tm, D_ATTN), _F32)],
        compiler_params=pltpu.CompilerParams(
            dimension_semantics=("parallel",), vmem_limit_bytes=VMEM_LIMIT),
        name="mixer",
    )(yc, q, kd, vd, kd, vd, x2, tab, sink, ona, w_out, nffn, wr, rb)


def _route_kernel(lg_ref, route_ref, route_t_ref, cnt_ref, run_ref, earlier_ref):
    t = lg_ref.shape[0]

    @pl.when(pl.program_id(0) == 0)
    def _():
        run_ref[...] = jnp.zeros_like(run_ref)
        earlier_ref[...] = (lax.broadcasted_iota(jnp.int32, (t, t), 0)
                            > lax.broadcasted_iota(jnp.int32, (t, t), 1)).astype(_BF16)

    lg = lg_ref[...]
    lane = lax.broadcasted_iota(jnp.int32, (t, LANES), 1)
    ninf = -jnp.inf

    lane_f = lane.astype(_F32)

    def first_max(v):
        m = jnp.max(v, axis=-1, keepdims=True)
        idx = jnp.min(jnp.where(v == m, lane_f, float(LANES)), axis=-1, keepdims=True)
        return m, idx.astype(jnp.int32)

    is_g = lane < N_GROUPS
    gmax, gidx = first_max(jnp.where(is_g, lg, ninf))
    gsum = jnp.sum(jnp.where(is_g, jnp.exp(jnp.where(is_g, lg, gmax) - gmax), 0.0),
                   axis=-1, keepdims=True)
    g_w = 1.0 / gsum
    base = N_GROUPS + EXPERTS_PER_GROUP * gidx
    in_grp = jnp.logical_and(lane >= base, lane < base + EXPERTS_PER_GROUP)
    el = jnp.where(in_grp, lg, ninf)
    m1, i1 = first_max(el)
    m2, i2 = first_max(jnp.where(lane == i1, ninf, el))
    e21 = jnp.exp(m2 - m1)
    w1 = g_w / (1.0 + e21)
    w2 = g_w * e21 / (1.0 + e21)
    e1 = i1 - N_GROUPS
    e2 = i2 - N_GROUPS

    oh1 = lane == e1
    oh2 = lane == e2
    c12 = _dot(earlier_ref[...],
               jnp.concatenate([oh1.astype(_BF16), oh2.astype(_BF16)], axis=1))
    c1 = c12[:, 0:LANES]
    c2 = c12[:, LANES:2 * LANES]
    tot1 = jnp.sum(oh1.astype(_F32), axis=0, keepdims=True)
    tot2 = jnp.sum(oh2.astype(_F32), axis=0, keepdims=True)
    run = run_ref[0:1, :]
    r1 = jnp.sum(jnp.where(oh1, run + c1, 0.0), axis=-1, keepdims=True)
    r2 = jnp.sum(jnp.where(oh2, run + tot1 + c2, 0.0), axis=-1, keepdims=True)
    run_ref[...] = jnp.broadcast_to(run + tot1 + tot2, run_ref.shape)
    cnt_ref[...] = run_ref[...]

    out = jnp.where(lane == 0, e1.astype(_F32), 0.0)
    out = jnp.where(lane == 1, e2.astype(_F32), out)
    out = jnp.where(lane == 2, w1, out)
    out = jnp.where(lane == 3, w2, out)
    out = jnp.where(lane == 4, r1, out)
    out = jnp.where(lane == 5, r2, out)
    route_ref[...] = out
    route_t_ref[...] = out.T[0:8, :]


def _route(logits):
    n = logits.shape[0]
    t = TM_ROUTE
    return pl.pallas_call(
        _route_kernel,
        grid=(n // t,),
        in_specs=[pl.BlockSpec((t, LANES), lambda i: (i, 0))],
        out_specs=[pl.BlockSpec((t, LANES), lambda i: (i, 0)),
                   pl.BlockSpec((8, t), lambda i: (0, i)),
                   pl.BlockSpec((8, LANES), lambda i: (0, 0))],
        out_shape=[jax.ShapeDtypeStruct((n, LANES), _F32),
                   jax.ShapeDtypeStruct((8, n), _F32),
                   jax.ShapeDtypeStruct((8, LANES), _F32)],
        scratch_shapes=[pltpu.VMEM((8, LANES), _F32), pltpu.VMEM((t, t), _BF16)],
        compiler_params=pltpu.CompilerParams(dimension_semantics=("arbitrary",)),
        name="route",
    )(logits)


def _dispatch_kernel(pad_end_ref, padded_ref, dest_ref, xp_ref, xs_ref,
                     buf_ref, zero_ref, lsem, rsem, zsem):
    td = buf_ref.shape[1]
    nb = xs_ref.shape[0] // MOE_BLK
    nb_used = pad_end_ref[N_EXPERTS - 1] // MOE_BLK
    i = pl.program_id(0)
    n_steps = pl.num_programs(0)
    slot = i % 3
    nxt = (i + 1) % 3
    prv = (i + 2) % 3

    def zero_copy(start):
        start = pl.multiple_of(start, MOE_BLK)
        return pltpu.make_async_copy(zero_ref, xs_ref.at[pl.ds(start, MOE_BLK)], zsem)

    def zero_blocks(fn):
        for e in range(N_EXPERTS):
            @pl.when(padded_ref[e] > 0)
            def _():
                fn(zero_copy(pad_end_ref[e] - MOE_BLK))
        for b in range(nb - N_EXPERTS, nb):
            @pl.when(b >= nb_used)
            def _():
                fn(zero_copy(b * MOE_BLK))

    def load(t, s):
        start = pl.multiple_of(t * td, td)
        return pltpu.make_async_copy(xp_ref.at[pl.ds(start, td)], buf_ref.at[s], lsem.at[s])

    def wait_rows(s):
        for k in range(TOP_K):
            pltpu.make_async_copy(buf_ref.at[s], xs_ref.at[pl.ds(0, td)], rsem.at[s]).wait()

    @pl.when(i == 0)
    def _():
        zero_ref[...] = jnp.zeros_like(zero_ref)
        zero_blocks(lambda c: c.start())
        zero_blocks(lambda c: c.wait())
        load(0, 0).start()

    @pl.when(i + 1 < n_steps)
    def _():
        load(i + 1, nxt).start()

    load(i, slot).wait()

    def issue(r, c):
        for k in range(TOP_K):
            d = dest_ref[0, 0, k * td + r]
            pltpu.make_async_copy(buf_ref.at[slot, pl.ds(r, 1)], xs_ref.at[pl.ds(d, 1)],
                                  rsem.at[slot]).start()
        return c

    lax.fori_loop(0, td, issue, 0, unroll=16)

    @pl.when(i > 0)
    def _():
        wait_rows(prv)

    @pl.when(i == n_steps - 1)
    def _():
        wait_rows(slot)


def _dispatch(xp, dest3, pad_end, padded, n_rows):
    steps = dest3.shape[0]
    td = dest3.shape[-1] // TOP_K
    return pl.pallas_call(
        _dispatch_kernel,
        grid_spec=pltpu.PrefetchScalarGridSpec(
            num_scalar_prefetch=2,
            grid=(steps,),
            in_specs=[
                pl.BlockSpec((1, 1, dest3.shape[-1]), lambda i, pe, pd: (i, 0, 0),
                             memory_space=pltpu.SMEM),
                pl.BlockSpec(memory_space=pl.ANY),
            ],
            out_specs=pl.BlockSpec(memory_space=pl.ANY),
            scratch_shapes=[pltpu.VMEM((3, td, HALF), _U32),
                            pltpu.VMEM((MOE_BLK, HALF), _U32),
                            pltpu.SemaphoreType.DMA((3,)), pltpu.SemaphoreType.DMA((3,)),
                            pltpu.SemaphoreType.DMA],
        ),
        out_shape=jax.ShapeDtypeStruct((n_rows, HALF), _U32),
        compiler_params=pltpu.CompilerParams(
            dimension_semantics=("arbitrary",)),
        name="dispatch",
    )(pad_end, padded, dest3, xp)


def _experts_kernel(be_ref, nxt_ref, nb_ref, xs_ref, wg_hbm, wu_hbm, wd_hbm, ys_ref,
                    wgs_ref, wus_ref, wds_ref, wgb_ref, wub_ref, wdb_ref, sem):
    i = pl.program_id(0)
    valid = i < nb_ref[0]
    e = be_ref[i]
    changed = jnp.logical_or(i == 0, e != be_ref[jnp.maximum(i - 1, 0)])

    def fetch(ex):
        return (pltpu.make_async_copy(wg_hbm.at[ex], wgs_ref, sem.at[0]),
                pltpu.make_async_copy(wu_hbm.at[ex], wus_ref, sem.at[1]),
                pltpu.make_async_copy(wd_hbm.at[ex], wds_ref, sem.at[2]))

    @pl.when(i == 0)
    def _():
        for c in fetch(e):
            c.start()

    @pl.when(jnp.logical_and(valid, changed))
    def _():
        for c in fetch(e):
            c.wait()
        wgb_ref[...] = wgs_ref[...].astype(_BF16)
        wub_ref[...] = wus_ref[...].astype(_BF16)
        wdb_ref[...] = wds_ref[...].astype(_BF16)
        nxt = nxt_ref[i]

        @pl.when(nxt >= 0)
        def _():
            for c in fetch(nxt):
                c.start()

    @pl.when(valid)
    def _():
        lo, hi = _unpack_rows(xs_ref[...])
        lo = lo.astype(_BF16)
        hi = hi.astype(_BF16)
        g = _dot(lo, wgb_ref[0:HALF, :]) + _dot(hi, wgb_ref[HALF:D_MODEL, :])
        u = _dot(lo, wub_ref[0:HALF, :]) + _dot(hi, wub_ref[HALF:D_MODEL, :])
        hidden = (g * jax.nn.sigmoid(g) * u).astype(_BF16)
        ys_ref[...] = _pack_rows(_dot(hidden, wdb_ref[...]))

    @pl.when(jnp.logical_not(valid))
    def _():
        ys_ref[...] = jnp.zeros_like(ys_ref)


def _experts(xs, w_gate, w_up, w_down, block_e, next_e, nb_used):
    n_rows = xs.shape[0]
    nb = n_rows // MOE_BLK
    row_blk = lambda i, be, nx, nbu: (jnp.minimum(i, nbu[0] - 1), 0)
    any_spec = pl.BlockSpec(memory_space=pl.ANY)
    return pl.pallas_call(
        _experts_kernel,
        grid_spec=pltpu.PrefetchScalarGridSpec(
            num_scalar_prefetch=3,
            grid=(nb,),
            in_specs=[pl.BlockSpec((MOE_BLK, HALF), row_blk), any_spec, any_spec, any_spec],
            out_specs=pl.BlockSpec((MOE_BLK, HALF), lambda i, be, nx, nbu: (i, 0)),
            scratch_shapes=[pltpu.VMEM((D_MODEL, D_EXPERT), _F32),
                            pltpu.VMEM((D_MODEL, D_EXPERT), _F32),
                            pltpu.VMEM((D_EXPERT, D_MODEL), _F32),
                            pltpu.VMEM((D_MODEL, D_EXPERT), _BF16),
                            pltpu.VMEM((D_MODEL, D_EXPERT), _BF16),
                            pltpu.VMEM((D_EXPERT, D_MODEL), _BF16),
                            pltpu.SemaphoreType.DMA((3,))],
        ),
        out_shape=jax.ShapeDtypeStruct((n_rows, HALF), _U32),
        compiler_params=pltpu.CompilerParams(
            dimension_semantics=("arbitrary",), vmem_limit_bytes=VMEM_LIMIT),
        name="experts",
    )(block_e, next_e, nb_used, xs, w_gate, w_up, w_down)


def _combine_kernel(h1_ref, route_ref, dcur_ref, dnext_ref, ys_ref, p_ref, nple_ref,
                    wgate_ref, wple_ref, out_ref, ybuf_ref, h2_ref, xn_ref, pb_ref, sem):
    tk = h1_ref.shape[0]
    cw = D_MODEL // N_CHUNK
    per = TOP_K * tk // N_CHUNK
    i = pl.program_id(0)
    n_steps = pl.num_programs(0)
    slot = i % 2

    def row_copy(d, s, k, r):
        return pltpu.make_async_copy(ys_ref.at[pl.ds(d, 1)],
                                     ybuf_ref.at[s, k, pl.ds(r, 1)], sem.at[s])

    def wait_slot(s):
        for k in range(TOP_K):
            pltpu.make_async_copy(ys_ref.at[pl.ds(0, tk)], ybuf_ref.at[s, k], sem.at[s]).wait()

    @pl.when(i == 0)
    def _():
        def body(r, c):
            for k in range(TOP_K):
                row_copy(dcur_ref[0, 0, k * tk + r], 0, k, r).start()
            return c
        lax.fori_loop(0, tk, body, 0, unroll=8)

    wait_slot(slot)
    y1 = jnp.concatenate(_unpack_rows(ybuf_ref[slot, 0]), axis=-1)
    y2 = jnp.concatenate(_unpack_rows(ybuf_ref[slot, 1]), axis=-1)
    route = route_ref[...]
    h2 = h1_ref[...] + route[:, 2:3] * y1 + route[:, 3:4] * y2
    xn, inv = _rms_split(h2, nple_ref[...])
    pb = p_ref[...].astype(_BF16)

    def issue_batch(c):
        for j in range(per):
            k, r = j % TOP_K, c * (per // TOP_K) + j // TOP_K
            row_copy(dnext_ref[0, 0, k * tk + r], 1 - slot, k, r).start()

    issue_batch(0)
    h2_ref[...] = h2
    xn_ref[...] = xn
    pb_ref[...] = pb
    for c in range(N_CHUNK):
        if c > 0:
            issue_batch(c)
        cols = slice(c * cw, (c + 1) * cw)
        gate = jax.nn.sigmoid(_dot(xn_ref[...], wgate_ref[:, cols]) * inv)
        emb = _dot(pb_ref[...], wple_ref[:, cols])
        out_ref[:, cols] = h2_ref[:, cols] + gate * emb

    @pl.when(i == n_steps - 1)
    def _():
        wait_slot(1 - slot)


def _combine(h1, route, dest3, ys, p2, nple, w_gate, w_ple):
    n = h1.shape[0]
    tk = TM_COMB
    steps = n // tk
    const2 = lambda i: (0, 0)
    return pl.pallas_call(
        _combine_kernel,
        grid=(steps,),
        in_specs=[
            pl.BlockSpec((tk, D_MODEL), lambda i: (i, 0)),
            pl.BlockSpec((tk, LANES), lambda i: (i, 0)),
            pl.BlockSpec((1, 1, TOP_K * tk), lambda i: (i, 0, 0), memory_space=pltpu.SMEM),
            pl.BlockSpec((1, 1, TOP_K * tk), lambda i: (jnp.minimum(i + 1, steps - 1), 0, 0),
                         memory_space=pltpu.SMEM),
            pl.BlockSpec(memory_space=pl.ANY),
            pl.BlockSpec((tk, PLE_DIM), lambda i: (i, 0)),
            pl.BlockSpec((1, D_MODEL), const2),
            pl.BlockSpec((D_MODEL, D_MODEL), const2, pipeline_mode=pl.Buffered(1)),
            pl.BlockSpec((PLE_DIM, D_MODEL), const2, pipeline_mode=pl.Buffered(1)),
        ],
        out_specs=pl.BlockSpec((tk, D_MODEL), lambda i: (i, 0)),
        out_shape=jax.ShapeDtypeStruct((n, D_MODEL), _F32),
        scratch_shapes=[pltpu.VMEM((2, TOP_K, tk, HALF), _U32),
                        pltpu.VMEM((tk, D_MODEL), _F32),
                        pltpu.VMEM((tk, D_MODEL), _BF16),
                        pltpu.VMEM((tk, PLE_DIM), _BF16),
                        pltpu.SemaphoreType.DMA((2,))],
        compiler_params=pltpu.CompilerParams(
            dimension_semantics=("arbitrary",), vmem_limit_bytes=VMEM_LIMIT),
        name="combine",
    )(h1, route, dest3, dest3, ys, p2, nple, w_gate, w_ple)


def _attention_tables(sinks):
    slopes = 2.0 ** (-8.0 * jnp.arange(1, N_HEADS + 1, dtype=_F32) / N_HEADS)
    qi = jnp.arange(ATTN_BLOCK)[:, None]
    kj = jnp.arange(ATTN_BLOCK)[None, :]
    dist = jnp.where(kj > qi, ATTN_BLOCK + qi - kj, qi - kj)
    tab = -slopes[:, None, None] * dist.astype(_F32)[None]
    tab = tab.reshape(N_KV_HEADS, 4 * ATTN_BLOCK, ATTN_BLOCK)
    sink = jnp.repeat(sinks.astype(_F32), ATTN_BLOCK).reshape(N_KV_HEADS, 4 * ATTN_BLOCK, 1)
    return tab, sink


def _layer(h, p_i, norm_mix, w_in, conv_w, q_norm, k_norm, sinks, out_norm_conv,
           out_norm_attn, w_out, norm_ffn, w_group, b_group, w_router, b_router,
           w_gate, w_up, w_down, norm_ple, w_ple_gate, w_ple):
    bsz, seq, _ = h.shape
    n = bsz * seq
    x2 = h.reshape(n, D_MODEL)
    row = lambda v: v.reshape(1, -1).astype(_F32)

    yc, q, kd, vd = _inproj(
        x2, row(norm_mix), w_in.astype(_BF16), row(jnp.tile(q_norm, 2)),
        row(jnp.tile(k_norm, 2)), conv_w.astype(_F32), row(out_norm_conv), seq)

    tab, sink = _attention_tables(sinks)
    w_r = jnp.zeros((D_MODEL, LANES), _F32)
    w_r = w_r.at[:, :N_GROUPS].set(w_group).at[:, N_GROUPS:N_GROUPS + N_EXPERTS].set(w_router)
    r_bias = jnp.zeros((1, LANES), _F32)
    r_bias = r_bias.at[0, :N_GROUPS].set(b_group).at[0, N_GROUPS:N_GROUPS + N_EXPERTS].set(b_router)
    h1, xp, logits = _mixer(
        yc, q, kd, vd, x2, tab, sink, row(out_norm_attn), w_out.astype(_BF16),
        row(norm_ffn), w_r.astype(_BF16), r_bias, seq)

    route, route_t, counts = _route(logits)

    cnt = counts[0, :N_EXPERTS].astype(jnp.int32)
    padded = (cnt + MOE_BLK - 1) // MOE_BLK * MOE_BLK
    pad_end = jnp.cumsum(padded)
    pad_start = pad_end - padded
    eid = route_t[0:TOP_K].astype(jnp.int32)
    onehot = eid[..., None] == jnp.arange(N_EXPERTS, dtype=jnp.int32)
    dest = (jnp.sum(jnp.where(onehot, pad_start, 0), axis=-1)
            + route_t[4:4 + TOP_K].astype(jnp.int32))

    def step_blocks(tile):
        return dest.reshape(TOP_K, n // tile, 1, tile).transpose(1, 2, 0, 3).reshape(
            n // tile, 1, TOP_K * tile)
    n_rows = n * TOP_K + N_EXPERTS * MOE_BLK
    nb = n_rows // MOE_BLK
    nb_used = (pad_end[-1] // MOE_BLK).astype(jnp.int32)
    blk_start = jnp.arange(nb, dtype=jnp.int32) * MOE_BLK
    block_e = jnp.sum((blk_start[:, None] >= pad_end[None, :]).astype(jnp.int32), axis=1)
    block_e = jnp.minimum(block_e, N_EXPERTS - 1)
    block_e = jnp.where(jnp.arange(nb) < nb_used, block_e, block_e[nb_used - 1])
    ids = jnp.arange(N_EXPERTS, dtype=jnp.int32)
    later = jnp.logical_and(ids[None, :] > ids[:, None], cnt[None, :] > 0)
    next_present = jnp.min(jnp.where(later, ids[None, :], N_EXPERTS), axis=1)
    next_present = jnp.where(next_present < N_EXPERTS, next_present, -1)
    onehot_b = block_e[:, None] == ids[None, :]
    next_e = jnp.sum(jnp.where(onehot_b, next_present[None, :], 0), axis=1).astype(jnp.int32)

    xs = _dispatch(xp, step_blocks(TM_DISP),
                   pad_end.astype(jnp.int32), padded.astype(jnp.int32), n_rows)
    ys = _experts(xs, w_gate, w_up, w_down, block_e, next_e, nb_used.reshape(1))
    out = _combine(h1, route, step_blocks(TM_COMB), ys,
                   p_i.reshape(n, PLE_DIM), row(norm_ple), w_ple_gate.astype(_BF16),
                   w_ple.astype(_BF16))
    return out.reshape(bsz, seq, D_MODEL)


def kernel(x, p, norm_mix, w_in, conv_w, q_norm, k_norm, sinks, out_norm_conv, out_norm_attn,
           w_out, norm_ffn, w_group, b_group, w_router, b_router, w_gate, w_up, w_down,
           norm_ple, w_ple_gate, w_ple):
    h = x
    for i in range(p.shape[0]):
        h = _layer(h, p[i], norm_mix[i], w_in[i], conv_w[i], q_norm[i], k_norm[i], sinks[i],
                   out_norm_conv[i], out_norm_attn[i], w_out[i], norm_ffn[i], w_group[i],
                   b_group[i], w_router[i], b_router[i], w_gate[i], w_up[i], w_down[i],
                   norm_ple[i], w_ple_gate[i], w_ple[i])
    return h
```

```python
import functools

import jax
import jax.numpy as jnp
from jax import lax
from jax.experimental import pallas as pl
from jax.experimental.pallas import tpu as pltpu

D_MODEL = 2048
D_CONV = 1024
N_HEADS = 16
N_KV_HEADS = 4
HEAD_DIM = 64
D_ATTN = N_HEADS * HEAD_DIM
D_KV = N_KV_HEADS * HEAD_DIM
D_IN = 3 * D_CONV + D_ATTN + 2 * D_KV
ATTN_BLOCK = 128
N_GROUPS = 4
EXPERTS_PER_GROUP = 8
N_EXPERTS = N_GROUPS * EXPERTS_PER_GROUP
TOP_K = 2
D_EXPERT = 512
PLE_DIM = 256
EPS = 1e-6
NEG_INF = -1e30

LANES = 128
HALF = D_MODEL // 2
KV_DUP = 2 * D_KV

TM_PROJ = 512
TM_MIX = 512
TM_ROUTE = 1024
TM_DISP = 1024
TM_COMB = 512
MOE_BLK = 512
N_CHUNK = 4
VMEM_LIMIT = 56 * 1024 * 1024

_F32 = jnp.float32
_BF16 = jnp.bfloat16
_U32 = jnp.uint32
_HI_MASK = 0xFFFF0000


def _rms(x, gain):
    return x * lax.rsqrt(jnp.mean(x * x, axis=-1, keepdims=True) + EPS) * gain


def _rms_split(x, gain):
    inv = lax.rsqrt(jnp.mean(x * x, axis=-1, keepdims=True) + EPS)
    return (x * gain).astype(_BF16), inv


def _dot(a, b):
    return jnp.dot(a, b, preferred_element_type=_F32)


def _pack_rows(x):
    h = x.shape[-1] // 2
    bits = lax.bitcast_convert_type(x.astype(_BF16).astype(_F32), _U32)
    return (bits[:, h:] & _U32(_HI_MASK)) | (bits[:, :h] >> 16)


def _unpack_rows(w):
    lo = lax.bitcast_convert_type(w << 16, _F32)
    hi = lax.bitcast_convert_type(w & _U32(_HI_MASK), _F32)
    return lo, hi


def _inproj_kernel(tiles_per_seq, x_ref, g_ref, w_ref, qg_ref, kg_ref, cw_ref, onc_ref,
                   yc_ref, q_ref, kd_ref, vd_ref, t_ref, uprev_ref):
    tm = x_ref.shape[0]
    is_first = (pl.program_id(0) % tiles_per_seq) == 0
    xn = _rms(x_ref[...], g_ref[...]).astype(_BF16)

    @pl.when(pl.program_id(0) == 0)
    def _():
        uprev_ref[...] = jnp.zeros_like(uprev_ref)

    lo = lax.broadcasted_iota(jnp.int32, (tm, LANES), 1) < HEAD_DIM
    q = _dot(xn, w_ref[:, 3 * D_CONV:3 * D_CONV + D_ATTN])
    scale = HEAD_DIM ** -0.5
    for pr in range(D_ATTN // LANES):
        qp = q[:, pr * LANES:(pr + 1) * LANES]
        sq = qp * qp
        s_lo = jnp.sum(jnp.where(lo, sq, 0.0), axis=-1, keepdims=True)
        s_hi = jnp.sum(jnp.where(lo, 0.0, sq), axis=-1, keepdims=True)
        ms = jnp.where(lo, s_lo, s_hi) * (1.0 / HEAD_DIM)
        qn = (qp * lax.rsqrt(ms + EPS) * qg_ref[...] * scale).astype(_BF16)
        q_lo = jnp.where(lo, qn, jnp.zeros_like(qn))
        q_hi = jnp.where(lo, jnp.zeros_like(qn), qn)
        for j in range(tm // ATTN_BLOCK):
            rows = slice(j * ATTN_BLOCK, (j + 1) * ATTN_BLOCK)
            base = (j * N_HEADS + 2 * pr) * ATTN_BLOCK
            q_ref[base:base + ATTN_BLOCK, :] = q_lo[rows]
            q_ref[base + ATTN_BLOCK:base + 2 * ATTN_BLOCK, :] = q_hi[rows]

    kv = _dot(xn, w_ref[:, 3 * D_CONV + D_ATTN:D_IN])
    for pr in range(2 * D_KV // LANES):
        pair = kv[:, pr * LANES:(pr + 1) * LANES]
        swapped = pltpu.roll(pair, HEAD_DIM, 1)
        for half in range(2):
            dup = jnp.where(lo, pair, swapped) if half == 0 else jnp.where(lo, swapped, pair)
            head = 2 * pr + half
            if head < N_KV_HEADS:
                ms = jnp.sum(dup * dup, axis=-1, keepdims=True) * (1.0 / LANES)
                dup = dup * lax.rsqrt(ms + EPS) * kg_ref[...]
                kd_ref[:, head * LANES:(head + 1) * LANES] = dup.astype(_BF16)
            else:
                head -= N_KV_HEADS
                vd_ref[:, head * LANES:(head + 1) * LANES] = dup.astype(_BF16)

    ch = 256
    row = lax.broadcasted_iota(jnp.int32, (tm, ch), 0)
    last = uprev_ref.shape[0] - 1
    ssq = jnp.zeros((tm, 1), _F32)
    for j in range(D_CONV // ch):
        cs = slice(j * ch, (j + 1) * ch)
        b = _dot(xn, w_ref[:, j * ch:(j + 1) * ch])
        c = _dot(xn, w_ref[:, D_CONV + j * ch:D_CONV + (j + 1) * ch])
        hc = _dot(xn, w_ref[:, 2 * D_CONV + j * ch:2 * D_CONV + (j + 1) * ch])
        u = c * hc
        uh = jnp.where(is_first, 0.0, uprev_ref[:, cs])
        uprev_ref[:, cs] = u[tm - last - 1:tm]
        u1 = jnp.where(row == 0, uh[last:last + 1], pltpu.roll(u, 1, 0))
        u2 = jnp.where(row == 0, uh[last - 1:last],
                       jnp.where(row == 1, uh[last:last + 1], pltpu.roll(u, 2, 0)))
        t = b * (cw_ref[2:3, cs] * u + cw_ref[1:2, cs] * u1 + cw_ref[0:1, cs] * u2)
        ssq = ssq + jnp.sum(t * t, axis=-1, keepdims=True)
        t_ref[:, cs] = t
    inv = lax.rsqrt(ssq * (1.0 / D_CONV) + EPS)
    yc_ref[...] = (t_ref[...] * inv * onc_ref[...]).astype(_BF16)


def _inproj(x2, g, w_in, qg2, kg2, cw, onc, seq):
    n = x2.shape[0]
    tm = TM_PROJ
    const = lambda i: (0, 0)
    return pl.pallas_call(
        functools.partial(_inproj_kernel, seq // tm),
        grid=(n // tm,),
        in_specs=[
            pl.BlockSpec((tm, D_MODEL), lambda i: (i, 0)),
            pl.BlockSpec((1, D_MODEL), const),
            pl.BlockSpec((D_MODEL, D_IN), const, pipeline_mode=pl.Buffered(1)),
            pl.BlockSpec((1, LANES), const),
            pl.BlockSpec((1, LANES), const),
            pl.BlockSpec((3, D_CONV), const),
            pl.BlockSpec((1, D_CONV), const),
        ],
        out_specs=[
            pl.BlockSpec((tm, D_CONV), lambda i: (i, 0)),
            pl.BlockSpec((tm * N_HEADS, LANES), lambda i: (i, 0)),
            pl.BlockSpec((tm, KV_DUP), lambda i: (i, 0)),
            pl.BlockSpec((tm, KV_DUP), lambda i: (i, 0)),
        ],
        out_shape=[
            jax.ShapeDtypeStruct((n, D_CONV), _BF16),
            jax.ShapeDtypeStruct((n * N_HEADS, LANES), _BF16),
            jax.ShapeDtypeStruct((n, KV_DUP), _BF16),
            jax.ShapeDtypeStruct((n, KV_DUP), _BF16),
        ],
        scratch_shapes=[pltpu.VMEM((tm, D_CONV), _F32),
                        pltpu.VMEM((8, D_CONV), _F32)],
        compiler_params=pltpu.CompilerParams(
            dimension_semantics=("arbitrary",), vmem_limit_bytes=VMEM_LIMIT),
        name="inproj",
    )(x2, g, w_in, qg2, kg2, cw, onc)


def _mixer_kernel(tiles_per_seq,
                  yc_ref, q_ref, kd_ref, vd_ref, kdh_ref, vdh_ref, x_ref,
                  tab_ref, sink_ref, ona_ref, wout_ref, nffn_ref,
                  wr_ref, rb_ref,
                  h1_ref, xp_ref, lg_ref, ya_ref):
    tm = x_ref.shape[0]
    blk = ATTN_BLOCK
    is_first = (pl.program_id(0) % tiles_per_seq) == 0

    lo = lax.broadcasted_iota(jnp.int32, (blk, LANES), 1) < HEAD_DIM
    qi = lax.broadcasted_iota(jnp.int32, (4 * blk, blk), 0) % blk
    from_prev = lax.broadcasted_iota(jnp.int32, (4 * blk, blk), 1) > qi
    no_prev = jnp.logical_and(is_first, from_prev)
    zero_p = jnp.zeros((4 * blk, blk), _BF16)
    for j in range(tm // blk):
        rows = slice(j * blk, (j + 1) * blk)
        for g in range(N_KV_HEADS):
            cols = slice(g * LANES, (g + 1) * LANES)
            grp = (j * N_HEADS + 4 * g) * blk
            lhs = q_ref[grp:grp + 4 * blk, :]
            if j == 0:
                keys = jnp.concatenate([kdh_ref[:, cols], kd_ref[0:blk, cols]], axis=0)
                vals = jnp.concatenate([vdh_ref[:, cols], vd_ref[0:blk, cols]], axis=0)
            else:
                keys = kd_ref[(j - 1) * blk:(j + 1) * blk, cols]
                vals = vd_ref[(j - 1) * blk:(j + 1) * blk, cols]
            s = lax.dot_general(lhs, keys, (((1,), (1,)), ((), ())),
                                preferred_element_type=_F32)
            s = jnp.where(from_prev, s[:, 0:blk], s[:, blk:2 * blk]) + tab_ref[g]
            if j == 0:
                s = jnp.where(no_prev, NEG_INF, s)
            sink = sink_ref[g]
            m = jnp.maximum(jnp.max(s, axis=-1, keepdims=True), sink)
            p = jnp.exp(s - m)
            denom = jnp.sum(p, axis=-1, keepdims=True) + jnp.exp(sink - m)
            p = p.astype(_BF16)
            p = jnp.concatenate([jnp.where(from_prev, p, zero_p),
                                 jnp.where(from_prev, zero_p, p)], axis=1)
            o = _dot(p, vals) / denom
            ya_ref[rows, 2 * g * LANES:(2 * g + 1) * LANES] = jnp.where(
                lo, o[0:blk], o[blk:2 * blk])
            ya_ref[rows, (2 * g + 1) * LANES:(2 * g + 2) * LANES] = jnp.where(
                lo, o[2 * blk:3 * blk], o[3 * blk:4 * blk])

    ya, inv_a = _rms_split(ya_ref[...], ona_ref[...])
    h1 = (x_ref[...] + _dot(yc_ref[...], wout_ref[0:D_CONV, :])
          + _dot(ya, wout_ref[D_CONV:D_CONV + D_ATTN, :]) * inv_a)
    h1_ref[...] = h1

    xn = _rms(h1, nffn_ref[...])
    xp_ref[...] = _pack_rows(xn)
    lg_ref[...] = _dot(xn.astype(_BF16), wr_ref[...]) + rb_ref[...]


def _mixer(yc, q, kd, vd, x2, tab, sink, ona, w_out, nffn, wr, rb, seq):
    n = x2.shape[0]
    tm = TM_MIX
    const2 = lambda i: (0, 0)
    const3 = lambda i: (0, 0, 0)
    prev_blk = lambda i: (jnp.maximum(i * (tm // ATTN_BLOCK) - 1, 0), 0)
    return pl.pallas_call(
        functools.partial(_mixer_kernel, seq // tm),
        grid=(n // tm,),
        in_specs=[
            pl.BlockSpec((tm, D_CONV), lambda i: (i, 0)),
            pl.BlockSpec((tm * N_HEADS, LANES), lambda i: (i, 0)),
            pl.BlockSpec((tm, KV_DUP), lambda i: (i, 0)),
            pl.BlockSpec((tm, KV_DUP), lambda i: (i, 0)),
            pl.BlockSpec((ATTN_BLOCK, KV_DUP), prev_blk),
            pl.BlockSpec((ATTN_BLOCK, KV_DUP), prev_blk),
            pl.BlockSpec((tm, D_MODEL), lambda i: (i, 0)),
            pl.BlockSpec((N_KV_HEADS, 4 * ATTN_BLOCK, ATTN_BLOCK), const3,
                         pipeline_mode=pl.Buffered(1)),
            pl.BlockSpec((N_KV_HEADS, 4 * ATTN_BLOCK, 1), const3, pipeline_mode=pl.Buffered(1)),
            pl.BlockSpec((1, D_ATTN), const2),
            pl.BlockSpec((D_MODEL, D_MODEL), const2, pipeline_mode=pl.Buffered(1)),
            pl.BlockSpec((1, D_MODEL), const2),
            pl.BlockSpec((D_MODEL, LANES), const2),
            pl.BlockSpec((1, LANES), const2),
        ],
        out_specs=[
            pl.BlockSpec((tm, D_MODEL), lambda i: (i, 0)),
            pl.BlockSpec((tm, HALF), lambda i: (i, 0)),
            pl.BlockSpec((tm, LANES), lambda i: (i, 0)),
        ],
        out_shape=[
            jax.ShapeDtypeStruct((n, D_MODEL), _F32),
            jax.ShapeDtypeStruct((n, HALF), _U32),
            jax.ShapeDtypeStruct((n, LANES), _F32),
        ],
        scratch_shapes=[pltpu.VMEM((tm, D_ATTN), _F32)],
        compiler_params=pltpu.CompilerParams(
            dimension_semantics=("parallel",), vmem_limit_bytes=VMEM_LIMIT),
        name="mixer",
    )(yc, q, kd, vd, kd, vd, x2, tab, sink, ona, w_out, nffn, wr, rb)


def _route_kernel(lg_ref, route_ref, route_t_ref, cnt_ref, run_ref, earlier_ref):
    t = lg_ref.shape[0]

    @pl.when(pl.program_id(0) == 0)
    def _():
        run_ref[...] = jnp.zeros_like(run_ref)
        earlier_ref[...] = (lax.broadcasted_iota(jnp.int32, (t, t), 0)
                            > lax.broadcasted_iota(jnp.int32, (t, t), 1)).astype(_BF16)

    lg = lg_ref[...]
    lane = lax.broadcasted_iota(jnp.int32, (t, LANES), 1)
    ninf = -jnp.inf

    lane_f = lane.astype(_F32)

    def first_max(v):
        m = jnp.max(v, axis=-1, keepdims=True)
        idx = jnp.min(jnp.where(v == m, lane_f, float(LANES)), axis=-1, keepdims=True)
        return m, idx.astype(jnp.int32)

    is_g = lane < N_GROUPS
    gmax, gidx = first_max(jnp.where(is_g, lg, ninf))
    gsum = jnp.sum(jnp.where(is_g, jnp.exp(jnp.where(is_g, lg, gmax) - gmax), 0.0),
                   axis=-1, keepdims=True)
    g_w = 1.0 / gsum
    base = N_GROUPS + EXPERTS_PER_GROUP * gidx
    in_grp = jnp.logical_and(lane >= base, lane < base + EXPERTS_PER_GROUP)
    el = jnp.where(in_grp, lg, ninf)
    m1, i1 = first_max(el)
    m2, i2 = first_max(jnp.where(lane == i1, ninf, el))
    e21 = jnp.exp(m2 - m1)
    w1 = g_w / (1.0 + e21)
    w2 = g_w * e21 / (1.0 + e21)
    e1 = i1 - N_GROUPS
    e2 = i2 - N_GROUPS

    oh1 = lane == e1
    oh2 = lane == e2
    c12 = _dot(earlier_ref[...],
               jnp.concatenate([oh1.astype(_BF16), oh2.astype(_BF16)], axis=1))
    c1 = c12[:, 0:LANES]
    c2 = c12[:, LANES:2 * LANES]
    tot1 = jnp.sum(oh1.astype(_F32), axis=0, keepdims=True)
    tot2 = jnp.sum(oh2.astype(_F32), axis=0, keepdims=True)
    run = run_ref[0:1, :]
    r1 = jnp.sum(jnp.where(oh1, run + c1, 0.0), axis=-1, keepdims=True)
    r2 = jnp.sum(jnp.where(oh2, run + tot1 + c2, 0.0), axis=-1, keepdims=True)
    run_ref[...] = jnp.broadcast_to(run + tot1 + tot2, run_ref.shape)
    cnt_ref[...] = run_ref[...]

    out = jnp.where(lane == 0, e1.astype(_F32), 0.0)
    out = jnp.where(lane == 1, e2.astype(_F32), out)
    out = jnp.where(lane == 2, w1, out)
    out = jnp.where(lane == 3, w2, out)
    out = jnp.where(lane == 4, r1, out)
    out = jnp.where(lane == 5, r2, out)
    route_ref[...] = out
    route_t_ref[...] = out.T[0:8, :]


def _route(logits):
    n = logits.shape[0]
    t = TM_ROUTE
    return pl.pallas_call(
        _route_kernel,
        grid=(n // t,),
        in_specs=[pl.BlockSpec((t, LANES), lambda i: (i, 0))],
        out_specs=[pl.BlockSpec((t, LANES), lambda i: (i, 0)),
                   pl.BlockSpec((8, t), lambda i: (0, i)),
                   pl.BlockSpec((8, LANES), lambda i: (0, 0))],
        out_shape=[jax.ShapeDtypeStruct((n, LANES), _F32),
                   jax.ShapeDtypeStruct((8, n), _F32),
                   jax.ShapeDtypeStruct((8, LANES), _F32)],
        scratch_shapes=[pltpu.VMEM((8, LANES), _F32), pltpu.VMEM((t, t), _BF16)],
        compiler_params=pltpu.CompilerParams(dimension_semantics=("arbitrary",)),
        name="route",
    )(logits)


def _dispatch_kernel(pad_end_ref, padded_ref, dest_ref, xp_ref, xs_ref,
                     buf_ref, zero_ref, lsem, rsem, zsem):
    td = buf_ref.shape[1]
    nb = xs_ref.shape[0] // MOE_BLK
    nb_used = pad_end_ref[N_EXPERTS - 1] // MOE_BLK
    i = pl.program_id(0)
    n_steps = pl.num_programs(0)
    slot = i % 3
    nxt = (i + 1) % 3
    prv = (i + 2) % 3

    def zero_copy(start):
        start = pl.multiple_of(start, MOE_BLK)
        return pltpu.make_async_copy(zero_ref, xs_ref.at[pl.ds(start, MOE_BLK)], zsem)

    def zero_blocks(fn):
        for e in range(N_EXPERTS):
            @pl.when(padded_ref[e] > 0)
            def _():
                fn(zero_copy(pad_end_ref[e] - MOE_BLK))
        for b in range(nb - N_EXPERTS, nb):
            @pl.when(b >= nb_used)
            def _():
                fn(zero_copy(b * MOE_BLK))

    def load(t, s):
        start = pl.multiple_of(t * td, td)
        return pltpu.make_async_copy(xp_ref.at[pl.ds(start, td)], buf_ref.at[s], lsem.at[s])

    def wait_rows(s):
        for k in range(TOP_K):
            pltpu.make_async_copy(buf_ref.at[s], xs_ref.at[pl.ds(0, td)], rsem.at[s]).wait()

    @pl.when(i == 0)
    def _():
        zero_ref[...] = jnp.zeros_like(zero_ref)
        zero_blocks(lambda c: c.start())
        zero_blocks(lambda c: c.wait())
        load(0, 0).start()

    @pl.when(i + 1 < n_steps)
    def _():
        load(i + 1, nxt).start()

    load(i, slot).wait()

    def issue(r, c):
        for k in range(TOP_K):
            d = dest_ref[0, 0, k * td + r]
            pltpu.make_async_copy(buf_ref.at[slot, pl.ds(r, 1)], xs_ref.at[pl.ds(d, 1)],
                                  rsem.at[slot]).start(priority=k)
        return c

    lax.fori_loop(0, td, issue, 0, unroll=16)

    @pl.when(i > 0)
    def _():
        wait_rows(prv)

    @pl.when(i == n_steps - 1)
    def _():
        wait_rows(slot)


def _dispatch(xp, dest3, pad_end, padded, n_rows):
    steps = dest3.shape[0]
    td = dest3.shape[-1] // TOP_K
    return pl.pallas_call(
        _dispatch_kernel,
        grid_spec=pltpu.PrefetchScalarGridSpec(
            num_scalar_prefetch=2,
            grid=(steps,),
            in_specs=[
                pl.BlockSpec((1, 1, dest3.shape[-1]), lambda i, pe, pd: (i, 0, 0),
                             memory_space=pltpu.SMEM),
                pl.BlockSpec(memory_space=pl.ANY),
            ],
            out_specs=pl.BlockSpec(memory_space=pl.ANY),
            scratch_shapes=[pltpu.VMEM((3, td, HALF), _U32),
                            pltpu.VMEM((MOE_BLK, HALF), _U32),
                            pltpu.SemaphoreType.DMA((3,)), pltpu.SemaphoreType.DMA((3,)),
                            pltpu.SemaphoreType.DMA],
        ),
        out_shape=jax.ShapeDtypeStruct((n_rows, HALF), _U32),
        compiler_params=pltpu.CompilerParams(
            dimension_semantics=("arbitrary",)),
        name="dispatch",
    )(pad_end, padded, dest3, xp)


def _experts_kernel(be_ref, nxt_ref, nb_ref, xs_ref, wg_hbm, wu_hbm, wd_hbm, ys_ref,
                    wgs_ref, wus_ref, wds_ref, wgb_ref, wub_ref, wdb_ref, sem):
    i = pl.program_id(0)
    valid = i < nb_ref[0]
    e = be_ref[i]
    changed = jnp.logical_or(i == 0, e != be_ref[jnp.maximum(i - 1, 0)])

    def fetch(ex):
        return (pltpu.make_async_copy(wg_hbm.at[ex], wgs_ref, sem.at[0]),
                pltpu.make_async_copy(wu_hbm.at[ex], wus_ref, sem.at[1]),
                pltpu.make_async_copy(wd_hbm.at[ex], wds_ref, sem.at[2]))

    @pl.when(i == 0)
    def _():
        for c in fetch(e):
            c.start()

    @pl.when(jnp.logical_and(valid, changed))
    def _():
        for c in fetch(e):
            c.wait()
        wgb_ref[...] = wgs_ref[...].astype(_BF16)
        wub_ref[...] = wus_ref[...].astype(_BF16)
        wdb_ref[...] = wds_ref[...].astype(_BF16)
        nxt = nxt_ref[i]

        @pl.when(nxt >= 0)
        def _():
            for c in fetch(nxt):
                c.start()

    @pl.when(valid)
    def _():
        lo, hi = _unpack_rows(xs_ref[...])
        lo = lo.astype(_BF16)
        hi = hi.astype(_BF16)
        g = _dot(lo, wgb_ref[0:HALF, :]) + _dot(hi, wgb_ref[HALF:D_MODEL, :])
        u = _dot(lo, wub_ref[0:HALF, :]) + _dot(hi, wub_ref[HALF:D_MODEL, :])
        hidden = (g * jax.nn.sigmoid(g) * u).astype(_BF16)
        ys_ref[...] = _pack_rows(_dot(hidden, wdb_ref[...]))

    @pl.when(jnp.logical_not(valid))
    def _():
        ys_ref[...] = jnp.zeros_like(ys_ref)


def _experts(xs, w_gate, w_up, w_down, block_e, next_e, nb_used):
    n_rows = xs.shape[0]
    nb = n_rows // MOE_BLK
    row_blk = lambda i, be, nx, nbu: (jnp.minimum(i, nbu[0] - 1), 0)
    any_spec = pl.BlockSpec(memory_space=pl.ANY)
    return pl.pallas_call(
        _experts_kernel,
        grid_spec=pltpu.PrefetchScalarGridSpec(
            num_scalar_prefetch=3,
            grid=(nb,),
            in_specs=[pl.BlockSpec((MOE_BLK, HALF), row_blk), any_spec, any_spec, any_spec],
            out_specs=pl.BlockSpec((MOE_BLK, HALF), lambda i, be, nx, nbu: (i, 0)),
            scratch_shapes=[pltpu.VMEM((D_MODEL, D_EXPERT), _F32),
                            pltpu.VMEM((D_MODEL, D_EXPERT), _F32),
                            pltpu.VMEM((D_EXPERT, D_MODEL), _F32),
                            pltpu.VMEM((D_MODEL, D_EXPERT), _BF16),
                            pltpu.VMEM((D_MODEL, D_EXPERT), _BF16),
                            pltpu.VMEM((D_EXPERT, D_MODEL), _BF16),
                            pltpu.SemaphoreType.DMA((3,))],
        ),
        out_shape=jax.ShapeDtypeStruct((n_rows, HALF), _U32),
        compiler_params=pltpu.CompilerParams(
            dimension_semantics=("arbitrary",), vmem_limit_bytes=VMEM_LIMIT),
        name="experts",
    )(block_e, next_e, nb_used, xs, w_gate, w_up, w_down)


def _combine_kernel(h1_ref, route_ref, dcur_ref, dnext_ref, ys_ref, p_ref, nple_ref,
                    wgate_ref, wple_ref, out_ref, ybuf_ref, h2_ref, xn_ref, pb_ref, sem):
    tk = h1_ref.shape[0]
    cw = D_MODEL // N_CHUNK
    per = TOP_K * tk // N_CHUNK
    i = pl.program_id(0)
    n_steps = pl.num_programs(0)
    slot = i % 2

    def row_copy(d, s, k, r):
        return pltpu.make_async_copy(ys_ref.at[pl.ds(d, 1)],
                                     ybuf_ref.at[s, k, pl.ds(r, 1)], sem.at[s])

    def wait_slot(s):
        for k in range(TOP_K):
            pltpu.make_async_copy(ys_ref.at[pl.ds(0, tk)], ybuf_ref.at[s, k], sem.at[s]).wait()

    @pl.when(i == 0)
    def _():
        def body(r, c):
            for k in range(TOP_K):
                row_copy(dcur_ref[0, 0, k * tk + r], 0, k, r).start()
            return c
        lax.fori_loop(0, tk, body, 0, unroll=8)

    wait_slot(slot)
    y1 = jnp.concatenate(_unpack_rows(ybuf_ref[slot, 0]), axis=-1)
    y2 = jnp.concatenate(_unpack_rows(ybuf_ref[slot, 1]), axis=-1)
    route = route_ref[...]
    h2 = h1_ref[...] + route[:, 2:3] * y1 + route[:, 3:4] * y2
    xn, inv = _rms_split(h2, nple_ref[...])
    pb = p_ref[...].astype(_BF16)

    def issue_batch(c):
        for j in range(per):
            k, r = j % TOP_K, c * (per // TOP_K) + j // TOP_K
            row_copy(dnext_ref[0, 0, k * tk + r], 1 - slot, k, r).start(priority=k)

    issue_batch(0)
    h2_ref[...] = h2
    xn_ref[...] = xn
    pb_ref[...] = pb
    for c in range(N_CHUNK):
        if c > 0:
            issue_batch(c)
        cols = slice(c * cw, (c + 1) * cw)
        gate = jax.nn.sigmoid(_dot(xn_ref[...], wgate_ref[:, cols]) * inv)
        emb = _dot(pb_ref[...], wple_ref[:, cols])
        out_ref[:, cols] = h2_ref[:, cols] + gate * emb

    @pl.when(i == n_steps - 1)
    def _():
        wait_slot(1 - slot)


def _combine(h1, route, dest3, ys, p2, nple, w_gate, w_ple):
    n = h1.shape[0]
    tk = TM_COMB
    steps = n // tk
    const2 = lambda i: (0, 0)
    return pl.pallas_call(
        _combine_kernel,
        grid=(steps,),
        in_specs=[
            pl.BlockSpec((tk, D_MODEL), lambda i: (i, 0)),
            pl.BlockSpec((tk, LANES), lambda i: (i, 0)),
            pl.BlockSpec((1, 1, TOP_K * tk), lambda i: (i, 0, 0), memory_space=pltpu.SMEM),
            pl.BlockSpec((1, 1, TOP_K * tk), lambda i: (jnp.minimum(i + 1, steps - 1), 0, 0),
                         memory_space=pltpu.SMEM),
            pl.BlockSpec(memory_space=pl.ANY),
            pl.BlockSpec((tk, PLE_DIM), lambda i: (i, 0)),
            pl.BlockSpec((1, D_MODEL), const2),
            pl.BlockSpec((D_MODEL, D_MODEL), const2, pipeline_mode=pl.Buffered(1)),
            pl.BlockSpec((PLE_DIM, D_MODEL), const2, pipeline_mode=pl.Buffered(1)),
        ],
        out_specs=pl.BlockSpec((tk, D_MODEL), lambda i: (i, 0)),
        out_shape=jax.ShapeDtypeStruct((n, D_MODEL), _F32),
        scratch_shapes=[pltpu.VMEM((2, TOP_K, tk, HALF), _U32),
                        pltpu.VMEM((tk, D_MODEL), _F32),
                        pltpu.VMEM((tk, D_MODEL), _BF16),
                        pltpu.VMEM((tk, PLE_DIM), _BF16),
                        pltpu.SemaphoreType.DMA((2,))],
        compiler_params=pltpu.CompilerParams(
            dimension_semantics=("arbitrary",), vmem_limit_bytes=VMEM_LIMIT),
        name="combine",
    )(h1, route, dest3, dest3, ys, p2, nple, w_gate, w_ple)


def _attention_tables(sinks):
    slopes = 2.0 ** (-8.0 * jnp.arange(1, N_HEADS + 1, dtype=_F32) / N_HEADS)
    qi = jnp.arange(ATTN_BLOCK)[:, None]
    kj = jnp.arange(ATTN_BLOCK)[None, :]
    dist = jnp.where(kj > qi, ATTN_BLOCK + qi - kj, qi - kj)
    tab = -slopes[:, None, None] * dist.astype(_F32)[None]
    tab = tab.reshape(N_KV_HEADS, 4 * ATTN_BLOCK, ATTN_BLOCK)
    sink = jnp.repeat(sinks.astype(_F32), ATTN_BLOCK).reshape(N_KV_HEADS, 4 * ATTN_BLOCK, 1)
    return tab, sink


def _layer(h, p_i, norm_mix, w_in, conv_w, q_norm, k_norm, sinks, out_norm_conv,
           out_norm_attn, w_out, norm_ffn, w_group, b_group, w_router, b_router,
           w_gate, w_up, w_down, norm_ple, w_ple_gate, w_ple):
    bsz, seq, _ = h.shape
    n = bsz * seq
    x2 = h.reshape(n, D_MODEL)
    row = lambda v: v.reshape(1, -1).astype(_F32)

    yc, q, kd, vd = _inproj(
        x2, row(norm_mix), w_in.astype(_BF16), row(jnp.tile(q_norm, 2)),
        row(jnp.tile(k_norm, 2)), conv_w.astype(_F32), row(out_norm_conv), seq)

    tab, sink = _attention_tables(sinks)
    w_r = jnp.zeros((D_MODEL, LANES), _F32)
    w_r = w_r.at[:, :N_GROUPS].set(w_group).at[:, N_GROUPS:N_GROUPS + N_EXPERTS].set(w_router)
    r_bias = jnp.zeros((1, LANES), _F32)
    r_bias = r_bias.at[0, :N_GROUPS].set(b_group).at[0, N_GROUPS:N_GROUPS + N_EXPERTS].set(b_router)
    h1, xp, logits = _mixer(
        yc, q, kd, vd, x2, tab, sink, row(out_norm_attn), w_out.astype(_BF16),
        row(norm_ffn), w_r.astype(_BF16), r_bias, seq)

    route, route_t, counts = _route(logits)

    cnt = counts[0, :N_EXPERTS].astype(jnp.int32)
    padded = (cnt + MOE_BLK - 1) // MOE_BLK * MOE_BLK
    pad_end = jnp.cumsum(padded)
    pad_start = pad_end - padded
    eid = route_t[0:TOP_K].astype(jnp.int32)
    onehot = eid[..., None] == jnp.arange(N_EXPERTS, dtype=jnp.int32)
    dest = (jnp.sum(jnp.where(onehot, pad_start, 0), axis=-1)
            + route_t[4:4 + TOP_K].astype(jnp.int32))

    def step_blocks(tile):
        return dest.reshape(TOP_K, n // tile, 1, tile).transpose(1, 2, 0, 3).reshape(
            n // tile, 1, TOP_K * tile)
    n_rows = n * TOP_K + N_EXPERTS * MOE_BLK
    nb = n_rows // MOE_BLK
    nb_used = (pad_end[-1] // MOE_BLK).astype(jnp.int32)
    blk_start = jnp.arange(nb, dtype=jnp.int32) * MOE_BLK
    block_e = jnp.sum((blk_start[:, None] >= pad_end[None, :]).astype(jnp.int32), axis=1)
    block_e = jnp.minimum(block_e, N_EXPERTS - 1)
    block_e = jnp.where(jnp.arange(nb) < nb_used, block_e, block_e[nb_used - 1])
    ids = jnp.arange(N_EXPERTS, dtype=jnp.int32)
    later = jnp.logical_and(ids[None, :] > ids[:, None], cnt[None, :] > 0)
    next_present = jnp.min(jnp.where(later, ids[None, :], N_EXPERTS), axis=1)
    next_present = jnp.where(next_present < N_EXPERTS, next_present, -1)
    onehot_b = block_e[:, None] == ids[None, :]
    next_e = jnp.sum(jnp.where(onehot_b, next_present[None, :], 0), axis=1).astype(jnp.int32)

    xs = _dispatch(xp, step_blocks(TM_DISP),
                   pad_end.astype(jnp.int32), padded.astype(jnp.int32), n_rows)
    ys = _experts(xs, w_gate, w_up, w_down, block_e, next_e, nb_used.reshape(1))
    out = _combine(h1, route, step_blocks(TM_COMB), ys,
                   p_i.reshape(n, PLE_DIM), row(norm_ple), w_ple_gate.astype(_BF16),
                   w_ple.astype(_BF16))
    return out.reshape(bsz, seq, D_MODEL)


def kernel(x, p, norm_mix, w_in, conv_w, q_norm, k_norm, sinks, out_norm_conv, out_norm_attn,
           w_out, norm_ffn, w_group, b_group, w_router, b_router, w_gate, w_up, w_down,
           norm_ple, w_ple_gate, w_ple):
    h = x
    for i in range(p.shape[0]):
        h = _layer(h, p[i], norm_mix[i], w_in[i], conv_w[i], q_norm[i], k_norm[i], sinks[i],
                   out_norm_conv[i], out_norm_attn[i], w_out[i], norm_ffn[i], w_group[i],
                   b_group[i], w_router[i], b_router[i], w_gate[i], w_up[i], w_down[i],
                   norm_ple[i], w_ple_gate[i], w_ple[i])
    return h
```

```python
import functools

import jax
import jax.numpy as jnp
from jax import lax
from jax.experimental import pallas as pl
from jax.experimental.pallas import tpu as pltpu

D_MODEL = 2048
D_CONV = 1024
N_HEADS = 16
N_KV_HEADS = 4
HEAD_DIM = 64
D_ATTN = N_HEADS * HEAD_DIM
D_KV = N_KV_HEADS * HEAD_DIM
D_IN = 3 * D_CONV + D_ATTN + 2 * D_KV
ATTN_BLOCK = 128
N_GROUPS = 4
EXPERTS_PER_GROUP = 8
N_EXPERTS = N_GROUPS * EXPERTS_PER_GROUP
TOP_K = 2
D_EXPERT = 512
PLE_DIM = 256
EPS = 1e-6
NEG_INF = -1e30

LANES = 128
HALF = D_MODEL // 2
KV_DUP = 2 * D_KV

TM_PROJ = 512
TM_MIX = 512
TM_ROUTE = 1024
TM_DISP = 512
TM_COMB = 512
MOE_BLK = 512
N_CHUNK = 4
VMEM_LIMIT = 56 * 1024 * 1024

_F32 = jnp.float32
_BF16 = jnp.bfloat16
_U32 = jnp.uint32
_HI_MASK = 0xFFFF0000


def _rms(x, gain):
    return x * lax.rsqrt(jnp.mean(x * x, axis=-1, keepdims=True) + EPS) * gain


def _rms_split(x, gain):
    inv = lax.rsqrt(jnp.mean(x * x, axis=-1, keepdims=True) + EPS)
    return (x * gain).astype(_BF16), inv


def _dot(a, b):
    return jnp.dot(a, b, preferred_element_type=_F32)


def _pack_rows(x):
    h = x.shape[-1] // 2
    bits = lax.bitcast_convert_type(x.astype(_BF16).astype(_F32), _U32)
    return (bits[:, h:] & _U32(_HI_MASK)) | (bits[:, :h] >> 16)


def _unpack_rows(w):
    lo = lax.bitcast_convert_type(w << 16, _F32)
    hi = lax.bitcast_convert_type(w & _U32(_HI_MASK), _F32)
    return lo, hi


def _inproj_kernel(tiles_per_seq, x_ref, g_ref, w_ref, qg_ref, kg_ref, cw_ref, onc_ref,
                   yc_ref, q_ref, kd_ref, vd_ref, t_ref, uprev_ref):
    tm = x_ref.shape[0]
    is_first = (pl.program_id(0) % tiles_per_seq) == 0
    xn = _rms(x_ref[...], g_ref[...]).astype(_BF16)

    @pl.when(pl.program_id(0) == 0)
    def _():
        uprev_ref[...] = jnp.zeros_like(uprev_ref)

    lo = lax.broadcasted_iota(jnp.int32, (tm, LANES), 1) < HEAD_DIM
    q = _dot(xn, w_ref[:, 3 * D_CONV:3 * D_CONV + D_ATTN])
    scale = HEAD_DIM ** -0.5
    for pr in range(D_ATTN // LANES):
        qp = q[:, pr * LANES:(pr + 1) * LANES]
        sq = qp * qp
        s_lo = jnp.sum(jnp.where(lo, sq, 0.0), axis=-1, keepdims=True)
        s_hi = jnp.sum(jnp.where(lo, 0.0, sq), axis=-1, keepdims=True)
        ms = jnp.where(lo, s_lo, s_hi) * (1.0 / HEAD_DIM)
        qn = (qp * lax.rsqrt(ms + EPS) * qg_ref[...] * scale).astype(_BF16)
        q_lo = jnp.where(lo, qn, jnp.zeros_like(qn))
        q_hi = jnp.where(lo, jnp.zeros_like(qn), qn)
        for j in range(tm // ATTN_BLOCK):
            rows = slice(j * ATTN_BLOCK, (j + 1) * ATTN_BLOCK)
            base = (j * N_HEADS + 2 * pr) * ATTN_BLOCK
            q_ref[base:base + ATTN_BLOCK, :] = q_lo[rows]
            q_ref[base + ATTN_BLOCK:base + 2 * ATTN_BLOCK, :] = q_hi[rows]

    kv = _dot(xn, w_ref[:, 3 * D_CONV + D_ATTN:D_IN])
    for pr in range(2 * D_KV // LANES):
        pair = kv[:, pr * LANES:(pr + 1) * LANES]
        swapped = pltpu.roll(pair, HEAD_DIM, 1)
        for half in range(2):
            dup = jnp.where(lo, pair, swapped) if half == 0 else jnp.where(lo, swapped, pair)
            head = 2 * pr + half
            if head < N_KV_HEADS:
                ms = jnp.sum(dup * dup, axis=-1, keepdims=True) * (1.0 / LANES)
                dup = dup * lax.rsqrt(ms + EPS) * kg_ref[...]
                kd_ref[:, head * LANES:(head + 1) * LANES] = dup.astype(_BF16)
            else:
                head -= N_KV_HEADS
                vd_ref[:, head * LANES:(head + 1) * LANES] = dup.astype(_BF16)

    ch = 256
    row = lax.broadcasted_iota(jnp.int32, (tm, ch), 0)
    last = uprev_ref.shape[0] - 1
    ssq = jnp.zeros((tm, 1), _F32)
    for j in range(D_CONV // ch):
        cs = slice(j * ch, (j + 1) * ch)
        b = _dot(xn, w_ref[:, j * ch:(j + 1) * ch])
        c = _dot(xn, w_ref[:, D_CONV + j * ch:D_CONV + (j + 1) * ch])
        hc = _dot(xn, w_ref[:, 2 * D_CONV + j * ch:2 * D_CONV + (j + 1) * ch])
        u = c * hc
        uh = jnp.where(is_first, 0.0, uprev_ref[:, cs])
        uprev_ref[:, cs] = u[tm - last - 1:tm]
        u1 = jnp.where(row == 0, uh[last:last + 1], pltpu.roll(u, 1, 0))
        u2 = jnp.where(row == 0, uh[last - 1:last],
                       jnp.where(row == 1, uh[last:last + 1], pltpu.roll(u, 2, 0)))
        t = b * (cw_ref[2:3, cs] * u + cw_ref[1:2, cs] * u1 + cw_ref[0:1, cs] * u2)
        ssq = ssq + jnp.sum(t * t, axis=-1, keepdims=True)
        t_ref[:, cs] = t
    inv = lax.rsqrt(ssq * (1.0 / D_CONV) + EPS)
    yc_ref[...] = (t_ref[...] * inv * onc_ref[...]).astype(_BF16)


def _inproj(x2, g, w_in, qg2, kg2, cw, onc, seq):
    n = x2.shape[0]
    tm = TM_PROJ
    const = lambda i: (0, 0)
    return pl.pallas_call(
        functools.partial(_inproj_kernel, seq // tm),
        grid=(n // tm,),
        in_specs=[
            pl.BlockSpec((tm, D_MODEL), lambda i: (i, 0)),
            pl.BlockSpec((1, D_MODEL), const),
            pl.BlockSpec((D_MODEL, D_IN), const, pipeline_mode=pl.Buffered(1)),
            pl.BlockSpec((1, LANES), const),
            pl.BlockSpec((1, LANES), const),
            pl.BlockSpec((3, D_CONV), const),
            pl.BlockSpec((1, D_CONV), const),
        ],
        out_specs=[
            pl.BlockSpec((tm, D_CONV), lambda i: (i, 0)),
            pl.BlockSpec((tm * N_HEADS, LANES), lambda i: (i, 0)),
            pl.BlockSpec((tm, KV_DUP), lambda i: (i, 0)),
            pl.BlockSpec((tm, KV_DUP), lambda i: (i, 0)),
        ],
        out_shape=[
            jax.ShapeDtypeStruct((n, D_CONV), _BF16),
            jax.ShapeDtypeStruct((n * N_HEADS, LANES), _BF16),
            jax.ShapeDtypeStruct((n, KV_DUP), _BF16),
            jax.ShapeDtypeStruct((n, KV_DUP), _BF16),
        ],
        scratch_shapes=[pltpu.VMEM((tm, D_CONV), _F32),
                        pltpu.VMEM((8, D_CONV), _F32)],
        compiler_params=pltpu.CompilerParams(
            dimension_semantics=("arbitrary",), vmem_limit_bytes=VMEM_LIMIT),
        name="inproj",
    )(x2, g, w_in, qg2, kg2, cw, onc)


def _mixer_kernel(tiles_per_seq,
                  yc_ref, q_ref, kd_ref, vd_ref, kdh_ref, vdh_ref, x_ref,
                  tab_ref, sink_ref, ona_ref, wout_ref, nffn_ref,
                  wr_ref, rb_ref,
                  h1_ref, xp_ref, lg_ref, ya_ref):
    tm = x_ref.shape[0]
    blk = ATTN_BLOCK
    is_first = (pl.program_id(0) % tiles_per_seq) == 0

    lo = lax.broadcasted_iota(jnp.int32, (blk, LANES), 1) < HEAD_DIM
    hpd = 2
    qi = lax.broadcasted_iota(jnp.int32, (hpd * blk, blk), 0) % blk
    from_prev = lax.broadcasted_iota(jnp.int32, (hpd * blk, blk), 1) > qi
    no_prev = jnp.logical_and(is_first, from_prev)
    zero_p = jnp.zeros((hpd * blk, blk), _BF16)
    for j in range(tm // blk):
        rows = slice(j * blk, (j + 1) * blk)
        for g in range(N_KV_HEADS):
            cols = slice(g * LANES, (g + 1) * LANES)
            if j == 0:
                keys = jnp.concatenate([kdh_ref[:, cols], kd_ref[0:blk, cols]], axis=0)
                vals = jnp.concatenate([vdh_ref[:, cols], vd_ref[0:blk, cols]], axis=0)
            else:
                keys = kd_ref[(j - 1) * blk:(j + 1) * blk, cols]
                vals = vd_ref[(j - 1) * blk:(j + 1) * blk, cols]
            for hp in range(4 // hpd):
                grp = (j * N_HEADS + 4 * g + hpd * hp) * blk
                lhs = q_ref[grp:grp + hpd * blk, :]
                trow = slice(hp * hpd * blk, (hp + 1) * hpd * blk)
                s = lax.dot_general(lhs, keys, (((1,), (1,)), ((), ())),
                                    preferred_element_type=_F32)
                s = jnp.where(from_prev, s[:, 0:blk], s[:, blk:2 * blk]) + tab_ref[g, trow, :]
                if j == 0:
                    s = jnp.where(no_prev, NEG_INF, s)
                sink = sink_ref[g, trow, :]
                m = jnp.maximum(jnp.max(s, axis=-1, keepdims=True), sink)
                p = jnp.exp(s - m)
                denom = jnp.sum(p, axis=-1, keepdims=True) + jnp.exp(sink - m)
                p = p.astype(_BF16)
                p = jnp.concatenate([jnp.where(from_prev, p, zero_p),
                                     jnp.where(from_prev, zero_p, p)], axis=1)
                o = _dot(p, vals) / denom
                ya_ref[rows, (2 * g + hp) * LANES:(2 * g + hp + 1) * LANES] = jnp.where(
                    lo, o[0:blk], o[blk:2 * blk])

    ya, inv_a = _rms_split(ya_ref[...], ona_ref[...])
    h1 = (x_ref[...] + _dot(yc_ref[...], wout_ref[0:D_CONV, :])
          + _dot(ya, wout_ref[D_CONV:D_CONV + D_ATTN, :]) * inv_a)
    h1_ref[...] = h1

    xn = _rms(h1, nffn_ref[...])
    xp_ref[...] = _pack_rows(xn)
    lg_ref[...] = _dot(xn.astype(_BF16), wr_ref[...]) + rb_ref[...]


def _mixer(yc, q, kd, vd, x2, tab, sink, ona, w_out, nffn, wr, rb, seq):
    n = x2.shape[0]
    tm = TM_MIX
    const2 = lambda i: (0, 0)
    const3 = lambda i: (0, 0, 0)
    prev_blk = lambda i: (jnp.maximum(i * (tm // ATTN_BLOCK) - 1, 0), 0)
    return pl.pallas_call(
        functools.partial(_mixer_kernel, seq // tm),
        grid=(n // tm,),
        in_specs=[
            pl.BlockSpec((tm, D_CONV), lambda i: (i, 0)),
            pl.BlockSpec((tm * N_HEADS, LANES), lambda i: (i, 0)),
            pl.BlockSpec((tm, KV_DUP), lambda i: (i, 0)),
            pl.BlockSpec((tm, KV_DUP), lambda i: (i, 0)),
            pl.BlockSpec((ATTN_BLOCK, KV_DUP), prev_blk),
            pl.BlockSpec((ATTN_BLOCK, KV_DUP), prev_blk),
            pl.BlockSpec((tm, D_MODEL), lambda i: (i, 0)),
            pl.BlockSpec((N_KV_HEADS, 4 * ATTN_BLOCK, ATTN_BLOCK), const3,
                         pipeline_mode=pl.Buffered(1)),
            pl.BlockSpec((N_KV_HEADS, 4 * ATTN_BLOCK, 1), const3, pipeline_mode=pl.Buffered(1)),
            pl.BlockSpec((1, D_ATTN), const2),
            pl.BlockSpec((D_MODEL, D_MODEL), const2, pipeline_mode=pl.Buffered(1)),
            pl.BlockSpec((1, D_MODEL), const2),
            pl.BlockSpec((D_MODEL, LANES), const2),
            pl.BlockSpec((1, LANES), const2),
        ],
        out_specs=[
            pl.BlockSpec((tm, D_MODEL), lambda i: (i, 0)),
            pl.BlockSpec((tm, HALF), lambda i: (i, 0)),
            pl.BlockSpec((tm, LANES), lambda i: (i, 0)),
        ],
        out_shape=[
            jax.ShapeDtypeStruct((n, D_MODEL), _F32),
            jax.ShapeDtypeStruct((n, HALF), _U32),
            jax.ShapeDtypeStruct((n, LANES), _F32),
        ],
        scratch_shapes=[pltpu.VMEM((tm, D_ATTN), _F32)],
        compiler_params=pltpu.CompilerParams(
            dimension_semantics=("parallel",), vmem_limit_bytes=VMEM_LIMIT),
        name="mixer",
    )(yc, q, kd, vd, kd, vd, x2, tab, sink, ona, w_out, nffn, wr, rb)


def _route_kernel(lg_ref, route_ref, route_t_ref, cnt_ref, run_ref, earlier_ref):
    t = lg_ref.shape[0]

    @pl.when(pl.program_id(0) == 0)
    def _():
        run_ref[...] = jnp.zeros_like(run_ref)
        earlier_ref[...] = (lax.broadcasted_iota(jnp.int32, (t, t), 0)
                            > lax.broadcasted_iota(jnp.int32, (t, t), 1)).astype(_BF16)

    lg = lg_ref[...]
    lane = lax.broadcasted_iota(jnp.int32, (t, LANES), 1)
    ninf = -jnp.inf

    lane_f = lane.astype(_F32)

    def first_max(v):
        m = jnp.max(v, axis=-1, keepdims=True)
        idx = jnp.min(jnp.where(v == m, lane_f, float(LANES)), axis=-1, keepdims=True)
        return m, idx.astype(jnp.int32)

    is_g = lane < N_GROUPS
    gmax, gidx = first_max(jnp.where(is_g, lg, ninf))
    gsum = jnp.sum(jnp.where(is_g, jnp.exp(jnp.where(is_g, lg, gmax) - gmax), 0.0),
                   axis=-1, keepdims=True)
    g_w = 1.0 / gsum
    base = N_GROUPS + EXPERTS_PER_GROUP * gidx
    in_grp = jnp.logical_and(lane >= base, lane < base + EXPERTS_PER_GROUP)
    el = jnp.where(in_grp, lg, ninf)
    m1, i1 = first_max(el)
    m2, i2 = first_max(jnp.where(lane == i1, ninf, el))
    e21 = jnp.exp(m2 - m1)
    w1 = g_w / (1.0 + e21)
    w2 = g_w * e21 / (1.0 + e21)
    e1 = i1 - N_GROUPS
    e2 = i2 - N_GROUPS

    oh1 = lane == e1
    oh2 = lane == e2
    c12 = _dot(earlier_ref[...],
               jnp.concatenate([oh1.astype(_BF16), oh2.astype(_BF16)], axis=1))
    c1 = c12[:, 0:LANES]
    c2 = c12[:, LANES:2 * LANES]
    tot1 = jnp.sum(oh1.astype(_F32), axis=0, keepdims=True)
    tot2 = jnp.sum(oh2.astype(_F32), axis=0, keepdims=True)
    run = run_ref[0:1, :]
    r1 = jnp.sum(jnp.where(oh1, run + c1, 0.0), axis=-1, keepdims=True)
    r2 = jnp.sum(jnp.where(oh2, run + tot1 + c2, 0.0), axis=-1, keepdims=True)
    run_ref[...] = jnp.broadcast_to(run + tot1 + tot2, run_ref.shape)
    cnt_ref[...] = run_ref[...]

    out = jnp.where(lane == 0, e1.astype(_F32), 0.0)
    out = jnp.where(lane == 1, e2.astype(_F32), out)
    out = jnp.where(lane == 2, w1, out)
    out = jnp.where(lane == 3, w2, out)
    out = jnp.where(lane == 4, r1, out)
    out = jnp.where(lane == 5, r2, out)
    route_ref[...] = out
    route_t_ref[...] = out.T[0:8, :]


def _route(logits):
    n = logits.shape[0]
    t = TM_ROUTE
    return pl.pallas_call(
        _route_kernel,
        grid=(n // t,),
        in_specs=[pl.BlockSpec((t, LANES), lambda i: (i, 0))],
        out_specs=[pl.BlockSpec((t, LANES), lambda i: (i, 0)),
                   pl.BlockSpec((8, t), lambda i: (0, i)),
                   pl.BlockSpec((8, LANES), lambda i: (0, 0))],
        out_shape=[jax.ShapeDtypeStruct((n, LANES), _F32),
                   jax.ShapeDtypeStruct((8, n), _F32),
                   jax.ShapeDtypeStruct((8, LANES), _F32)],
        scratch_shapes=[pltpu.VMEM((8, LANES), _F32), pltpu.VMEM((t, t), _BF16)],
        compiler_params=pltpu.CompilerParams(dimension_semantics=("arbitrary",)),
        name="route",
    )(logits)


def _dispatch_kernel(pad_end_ref, padded_ref, dest_ref, xp_ref, xs_ref,
                     buf_ref, zero_ref, lsem, rsem, zsem):
    td = buf_ref.shape[1]
    nb = xs_ref.shape[0] // MOE_BLK
    nb_used = pad_end_ref[N_EXPERTS - 1] // MOE_BLK
    i = pl.program_id(0)
    n_steps = pl.num_programs(0)
    slot = i % 3
    nxt = (i + 1) % 3
    prv = (i + 2) % 3

    def zero_copy(start):
        start = pl.multiple_of(start, MOE_BLK)
        return pltpu.make_async_copy(zero_ref, xs_ref.at[pl.ds(start, MOE_BLK)], zsem)

    def zero_blocks(fn):
        for e in range(N_EXPERTS):
            @pl.when(padded_ref[e] > 0)
            def _():
                fn(zero_copy(pad_end_ref[e] - MOE_BLK))
        for b in range(nb - N_EXPERTS, nb):
            @pl.when(b >= nb_used)
            def _():
                fn(zero_copy(b * MOE_BLK))

    def load(t, s):
        start = pl.multiple_of(t * td, td)
        return pltpu.make_async_copy(xp_ref.at[pl.ds(start, td)], buf_ref.at[s], lsem.at[s])

    def wait_rows(s):
        for k in range(TOP_K):
            pltpu.make_async_copy(buf_ref.at[s], xs_ref.at[pl.ds(0, td)], rsem.at[s]).wait()

    @pl.when(i == 0)
    def _():
        zero_ref[...] = jnp.zeros_like(zero_ref)
        zero_blocks(lambda c: c.start())
        zero_blocks(lambda c: c.wait())
        load(0, 0).start()

    @pl.when(i + 1 < n_steps)
    def _():
        load(i + 1, nxt).start()

    load(i, slot).wait()

    def issue(r, c):
        for k in range(TOP_K):
            d = dest_ref[0, 0, k * td + r]
            pltpu.make_async_copy(buf_ref.at[slot, pl.ds(r, 1)], xs_ref.at[pl.ds(d, 1)],
                                  rsem.at[slot]).start()
        return c

    lax.fori_loop(0, td, issue, 0, unroll=8)

    @pl.when(i > 0)
    def _():
        wait_rows(prv)

    @pl.when(i == n_steps - 1)
    def _():
        wait_rows(slot)


def _dispatch(xp, dest3, pad_end, padded, n_rows):
    steps = dest3.shape[0]
    td = dest3.shape[-1] // TOP_K
    return pl.pallas_call(
        _dispatch_kernel,
        grid_spec=pltpu.PrefetchScalarGridSpec(
            num_scalar_prefetch=2,
            grid=(steps,),
            in_specs=[
                pl.BlockSpec((1, 1, dest3.shape[-1]), lambda i, pe, pd: (i, 0, 0),
                             memory_space=pltpu.SMEM),
                pl.BlockSpec(memory_space=pl.ANY),
            ],
            out_specs=pl.BlockSpec(memory_space=pl.ANY),
            scratch_shapes=[pltpu.VMEM((3, td, HALF), _U32),
                            pltpu.VMEM((MOE_BLK, HALF), _U32),
                            pltpu.SemaphoreType.DMA((3,)), pltpu.SemaphoreType.DMA((3,)),
                            pltpu.SemaphoreType.DMA],
        ),
        out_shape=jax.ShapeDtypeStruct((n_rows, HALF), _U32),
        compiler_params=pltpu.CompilerParams(
            dimension_semantics=("arbitrary",)),
        name="dispatch",
    )(pad_end, padded, dest3, xp)


def _experts_kernel(be_ref, nxt_ref, nb_ref, xs_ref, wg_hbm, wu_hbm, wd_hbm, ys_ref,
                    wgs_ref, wus_ref, wds_ref, wgb_ref, wub_ref, wdb_ref, sem):
    i = pl.program_id(0)
    valid = i < nb_ref[0]
    e = be_ref[i]
    changed = jnp.logical_or(i == 0, e != be_ref[jnp.maximum(i - 1, 0)])

    def fetch(ex):
        return (pltpu.make_async_copy(wg_hbm.at[ex], wgs_ref, sem.at[0]),
                pltpu.make_async_copy(wu_hbm.at[ex], wus_ref, sem.at[1]),
                pltpu.make_async_copy(wd_hbm.at[ex], wds_ref, sem.at[2]))

    @pl.when(i == 0)
    def _():
        for c in fetch(e):
            c.start()

    @pl.when(jnp.logical_and(valid, changed))
    def _():
        for c in fetch(e):
            c.wait()
        wgb_ref[...] = wgs_ref[...].astype(_BF16)
        wub_ref[...] = wus_ref[...].astype(_BF16)
        wdb_ref[...] = wds_ref[...].astype(_BF16)
        nxt = nxt_ref[i]

        @pl.when(nxt >= 0)
        def _():
            for c in fetch(nxt):
                c.start()

    @pl.when(valid)
    def _():
        lo, hi = _unpack_rows(xs_ref[...])
        lo = lo.astype(_BF16)
        hi = hi.astype(_BF16)
        g = _dot(lo, wgb_ref[0:HALF, :]) + _dot(hi, wgb_ref[HALF:D_MODEL, :])
        u = _dot(lo, wub_ref[0:HALF, :]) + _dot(hi, wub_ref[HALF:D_MODEL, :])
        hidden = (g * jax.nn.sigmoid(g) * u).astype(_BF16)
        ys_ref[...] = _pack_rows(_dot(hidden, wdb_ref[...]))

    @pl.when(jnp.logical_not(valid))
    def _():
        ys_ref[...] = jnp.zeros_like(ys_ref)


def _experts(xs, w_gate, w_up, w_down, block_e, next_e, nb_used):
    n_rows = xs.shape[0]
    nb = n_rows // MOE_BLK
    row_blk = lambda i, be, nx, nbu: (jnp.minimum(i, nbu[0] - 1), 0)
    any_spec = pl.BlockSpec(memory_space=pl.ANY)
    return pl.pallas_call(
        _experts_kernel,
        grid_spec=pltpu.PrefetchScalarGridSpec(
            num_scalar_prefetch=3,
            grid=(nb,),
            in_specs=[pl.BlockSpec((MOE_BLK, HALF), row_blk), any_spec, any_spec, any_spec],
            out_specs=pl.BlockSpec((MOE_BLK, HALF), lambda i, be, nx, nbu: (i, 0)),
            scratch_shapes=[pltpu.VMEM((D_MODEL, D_EXPERT), _F32),
                            pltpu.VMEM((D_MODEL, D_EXPERT), _F32),
                            pltpu.VMEM((D_EXPERT, D_MODEL), _F32),
                            pltpu.VMEM((D_MODEL, D_EXPERT), _BF16),
                            pltpu.VMEM((D_MODEL, D_EXPERT), _BF16),
                            pltpu.VMEM((D_EXPERT, D_MODEL), _BF16),
                            pltpu.SemaphoreType.DMA((3,))],
        ),
        out_shape=jax.ShapeDtypeStruct((n_rows, HALF), _U32),
        compiler_params=pltpu.CompilerParams(
            dimension_semantics=("arbitrary",), vmem_limit_bytes=VMEM_LIMIT),
        name="experts",
    )(block_e, next_e, nb_used, xs, w_gate, w_up, w_down)


def _combine_kernel(h1_ref, route_ref, dcur_ref, dnext_ref, ys_ref, p_ref, nple_ref,
                    wgate_ref, wple_ref, out_ref, ybuf_ref, h2_ref, xn_ref, pb_ref, sem):
    tk = h1_ref.shape[0]
    cw = D_MODEL // N_CHUNK
    per = TOP_K * tk // N_CHUNK
    i = pl.program_id(0)
    n_steps = pl.num_programs(0)
    slot = i % 2

    def row_copy(d, s, k, r):
        return pltpu.make_async_copy(ys_ref.at[pl.ds(d, 1)],
                                     ybuf_ref.at[s, k, pl.ds(r, 1)], sem.at[s])

    def wait_slot(s):
        for k in range(TOP_K):
            pltpu.make_async_copy(ys_ref.at[pl.ds(0, tk)], ybuf_ref.at[s, k], sem.at[s]).wait()

    @pl.when(i == 0)
    def _():
        def body(r, c):
            for k in range(TOP_K):
                row_copy(dcur_ref[0, 0, k * tk + r], 0, k, r).start()
            return c
        lax.fori_loop(0, tk, body, 0, unroll=8)

    wait_slot(slot)
    y1 = jnp.concatenate(_unpack_rows(ybuf_ref[slot, 0]), axis=-1)
    y2 = jnp.concatenate(_unpack_rows(ybuf_ref[slot, 1]), axis=-1)
    route = route_ref[...]
    h2 = h1_ref[...] + route[:, 2:3] * y1 + route[:, 3:4] * y2
    xn, inv = _rms_split(h2, nple_ref[...])
    pb = p_ref[...].astype(_BF16)

    def issue_batch(c):
        for j in range(per):
            k, r = j % TOP_K, c * (per // TOP_K) + j // TOP_K
            row_copy(dnext_ref[0, 0, k * tk + r], 1 - slot, k, r).start()

    issue_batch(0)
    h2_ref[...] = h2
    xn_ref[...] = xn
    pb_ref[...] = pb
    for c in range(N_CHUNK):
        if c > 0:
            issue_batch(c)
        cols = slice(c * cw, (c + 1) * cw)
        gate = jax.nn.sigmoid(_dot(xn_ref[...], wgate_ref[:, cols]) * inv)
        emb = _dot(pb_ref[...], wple_ref[:, cols])
        out_ref[:, cols] = h2_ref[:, cols] + gate * emb

    @pl.when(i == n_steps - 1)
    def _():
        wait_slot(1 - slot)


def _combine(h1, route, dest3, ys, p2, nple, w_gate, w_ple):
    n = h1.shape[0]
    tk = TM_COMB
    steps = n // tk
    const2 = lambda i: (0, 0)
    return pl.pallas_call(
        _combine_kernel,
        grid=(steps,),
        in_specs=[
            pl.BlockSpec((tk, D_MODEL), lambda i: (i, 0)),
            pl.BlockSpec((tk, LANES), lambda i: (i, 0)),
            pl.BlockSpec((1, 1, TOP_K * tk), lambda i: (i, 0, 0), memory_space=pltpu.SMEM),
            pl.BlockSpec((1, 1, TOP_K * tk), lambda i: (jnp.minimum(i + 1, steps - 1), 0, 0),
                         memory_space=pltpu.SMEM),
            pl.BlockSpec(memory_space=pl.ANY),
            pl.BlockSpec((tk, PLE_DIM), lambda i: (i, 0)),
            pl.BlockSpec((1, D_MODEL), const2),
            pl.BlockSpec((D_MODEL, D_MODEL), const2, pipeline_mode=pl.Buffered(1)),
            pl.BlockSpec((PLE_DIM, D_MODEL), const2, pipeline_mode=pl.Buffered(1)),
        ],
        out_specs=pl.BlockSpec((tk, D_MODEL), lambda i: (i, 0)),
        out_shape=jax.ShapeDtypeStruct((n, D_MODEL), _F32),
        scratch_shapes=[pltpu.VMEM((2, TOP_K, tk, HALF), _U32),
                        pltpu.VMEM((tk, D_MODEL), _F32),
                        pltpu.VMEM((tk, D_MODEL), _BF16),
                        pltpu.VMEM((tk, PLE_DIM), _BF16),
                        pltpu.SemaphoreType.DMA((2,))],
        compiler_params=pltpu.CompilerParams(
            dimension_semantics=("arbitrary",), vmem_limit_bytes=VMEM_LIMIT),
        name="combine",
    )(h1, route, dest3, dest3, ys, p2, nple, w_gate, w_ple)


def _attention_tables(sinks):
    slopes = 2.0 ** (-8.0 * jnp.arange(1, N_HEADS + 1, dtype=_F32) / N_HEADS)
    qi = jnp.arange(ATTN_BLOCK)[:, None]
    kj = jnp.arange(ATTN_BLOCK)[None, :]
    dist = jnp.where(kj > qi, ATTN_BLOCK + qi - kj, qi - kj)
    tab = -slopes[:, None, None] * dist.astype(_F32)[None]
    tab = tab.reshape(N_KV_HEADS, 4 * ATTN_BLOCK, ATTN_BLOCK)
    sink = jnp.repeat(sinks.astype(_F32), ATTN_BLOCK).reshape(N_KV_HEADS, 4 * ATTN_BLOCK, 1)
    return tab, sink


def _layer(h, p_i, norm_mix, w_in, conv_w, q_norm, k_norm, sinks, out_norm_conv,
           out_norm_attn, w_out, norm_ffn, w_group, b_group, w_router, b_router,
           w_gate, w_up, w_down, norm_ple, w_ple_gate, w_ple):
    bsz, seq, _ = h.shape
    n = bsz * seq
    x2 = h.reshape(n, D_MODEL)
    row = lambda v: v.reshape(1, -1).astype(_F32)

    yc, q, kd, vd = _inproj(
        x2, row(norm_mix), w_in.astype(_BF16), row(jnp.tile(q_norm, 2)),
        row(jnp.tile(k_norm, 2)), conv_w.astype(_F32), row(out_norm_conv), seq)

    tab, sink = _attention_tables(sinks)
    w_r = jnp.zeros((D_MODEL, LANES), _F32)
    w_r = w_r.at[:, :N_GROUPS].set(w_group).at[:, N_GROUPS:N_GROUPS + N_EXPERTS].set(w_router)
    r_bias = jnp.zeros((1, LANES), _F32)
    r_bias = r_bias.at[0, :N_GROUPS].set(b_group).at[0, N_GROUPS:N_GROUPS + N_EXPERTS].set(b_router)
    h1, xp, logits = _mixer(
        yc, q, kd, vd, x2, tab, sink, row(out_norm_attn), w_out.astype(_BF16),
        row(norm_ffn), w_r.astype(_BF16), r_bias, seq)

    route, route_t, counts = _route(logits)

    cnt = counts[0, :N_EXPERTS].astype(jnp.int32)
    padded = (cnt + MOE_BLK - 1) // MOE_BLK * MOE_BLK
    pad_end = jnp.cumsum(padded)
    pad_start = pad_end - padded
    eid = route_t[0:TOP_K].astype(jnp.int32)
    onehot = eid[..., None] == jnp.arange(N_EXPERTS, dtype=jnp.int32)
    dest = (jnp.sum(jnp.where(onehot, pad_start, 0), axis=-1)
            + route_t[4:4 + TOP_K].astype(jnp.int32))

    def step_blocks(tile):
        return dest.reshape(TOP_K, n // tile, 1, tile).transpose(1, 2, 0, 3).reshape(
            n // tile, 1, TOP_K * tile)
    n_rows = n * TOP_K + N_EXPERTS * MOE_BLK
    nb = n_rows // MOE_BLK
    nb_used = (pad_end[-1] // MOE_BLK).astype(jnp.int32)
    blk_start = jnp.arange(nb, dtype=jnp.int32) * MOE_BLK
    block_e = jnp.sum((blk_start[:, None] >= pad_end[None, :]).astype(jnp.int32), axis=1)
    block_e = jnp.minimum(block_e, N_EXPERTS - 1)
    block_e = jnp.where(jnp.arange(nb) < nb_used, block_e, block_e[nb_used - 1])
    ids = jnp.arange(N_EXPERTS, dtype=jnp.int32)
    later = jnp.logical_and(ids[None, :] > ids[:, None], cnt[None, :] > 0)
    next_present = jnp.min(jnp.where(later, ids[None, :], N_EXPERTS), axis=1)
    next_present = jnp.where(next_present < N_EXPERTS, next_present, -1)
    onehot_b = block_e[:, None] == ids[None, :]
    next_e = jnp.sum(jnp.where(onehot_b, next_present[None, :], 0), axis=1).astype(jnp.int32)

    xs = _dispatch(xp, step_blocks(TM_DISP),
                   pad_end.astype(jnp.int32), padded.astype(jnp.int32), n_rows)
    ys = _experts(xs, w_gate, w_up, w_down, block_e, next_e, nb_used.reshape(1))
    out = _combine(h1, route, step_blocks(TM_COMB), ys,
                   p_i.reshape(n, PLE_DIM), row(norm_ple), w_ple_gate.astype(_BF16),
                   w_ple.astype(_BF16))
    return out.reshape(bsz, seq, D_MODEL)


def kernel(x, p, norm_mix, w_in, conv_w, q_norm, k_norm, sinks, out_norm_conv, out_norm_attn,
           w_out, norm_ffn, w_group, b_group, w_router, b_router, w_gate, w_up, w_down,
           norm_ple, w_ple_gate, w_ple):
    h = x
    for i in range(p.shape[0]):
        h = _layer(h, p[i], norm_mix[i], w_in[i], conv_w[i], q_norm[i], k_norm[i], sinks[i],
                   out_norm_conv[i], out_norm_attn[i], w_out[i], norm_ffn[i], w_group[i],
                   b_group[i], w_router[i], b_router[i], w_gate[i], w_up[i], w_down[i],
                   norm_ple[i], w_ple_gate[i], w_ple[i])
    return h
```

```python
import functools

import jax
import jax.numpy as jnp
from jax import lax
from jax.experimental import pallas as pl
from jax.experimental.pallas import tpu as pltpu

D_MODEL = 2048
D_CONV = 1024
N_HEADS = 16
N_KV_HEADS = 4
HEAD_DIM = 64
D_ATTN = N_HEADS * HEAD_DIM
D_KV = N_KV_HEADS * HEAD_DIM
D_IN = 3 * D_CONV + D_ATTN + 2 * D_KV
ATTN_BLOCK = 128
N_GROUPS = 4
EXPERTS_PER_GROUP = 8
N_EXPERTS = N_GROUPS * EXPERTS_PER_GROUP
TOP_K = 2
D_EXPERT = 512
PLE_DIM = 256
EPS = 1e-6
NEG_INF = -1e30

LANES = 128
HALF = D_MODEL // 2
KV_DUP = 2 * D_KV

TM_PROJ = 512
TM_MIX = 512
TM_ROUTE = 1024
TM_DISP = 1024
TM_COMB = 512
MOE_BLK = 512
N_CHUNK = 4
VMEM_LIMIT = 56 * 1024 * 1024

_F32 = jnp.float32
_BF16 = jnp.bfloat16
_U32 = jnp.uint32
_HI_MASK = 0xFFFF0000


def _rms(x, gain):
    return x * lax.rsqrt(jnp.mean(x * x, axis=-1, keepdims=True) + EPS) * gain


def _rms_split(x, gain):
    inv = lax.rsqrt(jnp.mean(x * x, axis=-1, keepdims=True) + EPS)
    return (x * gain).astype(_BF16), inv


def _dot(a, b):
    return jnp.dot(a, b, preferred_element_type=_F32)


def _pack_rows(x):
    h = x.shape[-1] // 2
    bits = lax.bitcast_convert_type(x.astype(_BF16).astype(_F32), _U32)
    return (bits[:, h:] & _U32(_HI_MASK)) | (bits[:, :h] >> 16)


def _unpack_rows(w):
    lo = lax.bitcast_convert_type(w << 16, _F32)
    hi = lax.bitcast_convert_type(w & _U32(_HI_MASK), _F32)
    return lo, hi


def _inproj_kernel(tiles_per_seq, x_ref, g_ref, w_ref, qg_ref, kg_ref, cw_ref, onc_ref,
                   yc_ref, q_ref, kd_ref, vd_ref, t_ref, uprev_ref):
    tm = x_ref.shape[0]
    is_first = (pl.program_id(0) % tiles_per_seq) == 0
    xn = _rms(x_ref[...], g_ref[...]).astype(_BF16)

    @pl.when(pl.program_id(0) == 0)
    def _():
        uprev_ref[...] = jnp.zeros_like(uprev_ref)

    lo = lax.broadcasted_iota(jnp.int32, (tm, LANES), 1) < HEAD_DIM
    q = _dot(xn, w_ref[:, 3 * D_CONV:3 * D_CONV + D_ATTN])
    scale = HEAD_DIM ** -0.5
    for pr in range(D_ATTN // LANES):
        qp = q[:, pr * LANES:(pr + 1) * LANES]
        sq = qp * qp
        s_lo = jnp.sum(jnp.where(lo, sq, 0.0), axis=-1, keepdims=True)
        s_hi = jnp.sum(jnp.where(lo, 0.0, sq), axis=-1, keepdims=True)
        ms = jnp.where(lo, s_lo, s_hi) * (1.0 / HEAD_DIM)
        qn = (qp * lax.rsqrt(ms + EPS) * qg_ref[...] * scale).astype(_BF16)
        q_lo = jnp.where(lo, qn, jnp.zeros_like(qn))
        q_hi = jnp.where(lo, jnp.zeros_like(qn), qn)
        for j in range(tm // ATTN_BLOCK):
            rows = slice(j * ATTN_BLOCK, (j + 1) * ATTN_BLOCK)
            base = (j * N_HEADS + 2 * pr) * ATTN_BLOCK
            q_ref[base:base + ATTN_BLOCK, :] = q_lo[rows]
            q_ref[base + ATTN_BLOCK:base + 2 * ATTN_BLOCK, :] = q_hi[rows]

    kv = _dot(xn, w_ref[:, 3 * D_CONV + D_ATTN:D_IN])
    for pr in range(2 * D_KV // LANES):
        pair = kv[:, pr * LANES:(pr + 1) * LANES]
        swapped = pltpu.roll(pair, HEAD_DIM, 1)
        for half in range(2):
            dup = jnp.where(lo, pair, swapped) if half == 0 else jnp.where(lo, swapped, pair)
            head = 2 * pr + half
            if head < N_KV_HEADS:
                ms = jnp.sum(dup * dup, axis=-1, keepdims=True) * (1.0 / LANES)
                dup = dup * lax.rsqrt(ms + EPS) * kg_ref[...]
                kd_ref[:, head * LANES:(head + 1) * LANES] = dup.astype(_BF16)
            else:
                head -= N_KV_HEADS
                vd_ref[:, head * LANES:(head + 1) * LANES] = dup.astype(_BF16)

    ch = 256
    row = lax.broadcasted_iota(jnp.int32, (tm, ch), 0)
    last = uprev_ref.shape[0] - 1
    ssq = jnp.zeros((tm, 1), _F32)
    for j in range(D_CONV // ch):
        cs = slice(j * ch, (j + 1) * ch)
        b = _dot(xn, w_ref[:, j * ch:(j + 1) * ch])
        c = _dot(xn, w_ref[:, D_CONV + j * ch:D_CONV + (j + 1) * ch])
        hc = _dot(xn, w_ref[:, 2 * D_CONV + j * ch:2 * D_CONV + (j + 1) * ch])
        u = c * hc
        uh = jnp.where(is_first, 0.0, uprev_ref[:, cs])
        uprev_ref[:, cs] = u[tm - last - 1:tm]
        u1 = jnp.where(row == 0, uh[last:last + 1], pltpu.roll(u, 1, 0))
        u2 = jnp.where(row == 0, uh[last - 1:last],
                       jnp.where(row == 1, uh[last:last + 1], pltpu.roll(u, 2, 0)))
        t = b * (cw_ref[2:3, cs] * u + cw_ref[1:2, cs] * u1 + cw_ref[0:1, cs] * u2)
        ssq = ssq + jnp.sum(t * t, axis=-1, keepdims=True)
        t_ref[:, cs] = t
    inv = lax.rsqrt(ssq * (1.0 / D_CONV) + EPS)
    yc_ref[...] = (t_ref[...] * inv * onc_ref[...]).astype(_BF16)


def _inproj(x2, g, w_in, qg2, kg2, cw, onc, seq):
    n = x2.shape[0]
    tm = TM_PROJ
    const = lambda i: (0, 0)
    return pl.pallas_call(
        functools.partial(_inproj_kernel, seq // tm),
        grid=(n // tm,),
        in_specs=[
            pl.BlockSpec((tm, D_MODEL), lambda i: (i, 0)),
            pl.BlockSpec((1, D_MODEL), const),
            pl.BlockSpec((D_MODEL, D_IN), const, pipeline_mode=pl.Buffered(1)),
            pl.BlockSpec((1, LANES), const),
            pl.BlockSpec((1, LANES), const),
            pl.BlockSpec((3, D_CONV), const),
            pl.BlockSpec((1, D_CONV), const),
        ],
        out_specs=[
            pl.BlockSpec((tm, D_CONV), lambda i: (i, 0)),
            pl.BlockSpec((tm * N_HEADS, LANES), lambda i: (i, 0)),
            pl.BlockSpec((tm, KV_DUP), lambda i: (i, 0)),
            pl.BlockSpec((tm, KV_DUP), lambda i: (i, 0)),
        ],
        out_shape=[
            jax.ShapeDtypeStruct((n, D_CONV), _BF16),
            jax.ShapeDtypeStruct((n * N_HEADS, LANES), _BF16),
            jax.ShapeDtypeStruct((n, KV_DUP), _BF16),
            jax.ShapeDtypeStruct((n, KV_DUP), _BF16),
        ],
        scratch_shapes=[pltpu.VMEM((tm, D_CONV), _F32),
                        pltpu.VMEM((8, D_CONV), _F32)],
        compiler_params=pltpu.CompilerParams(
            dimension_semantics=("arbitrary",), vmem_limit_bytes=VMEM_LIMIT),
        name="inproj",
    )(x2, g, w_in, qg2, kg2, cw, onc)


def _mixer_kernel(tiles_per_seq,
                  yc_ref, q_ref, kd_ref, vd_ref, kdh_ref, vdh_ref, x_ref,
                  tab_ref, sink_ref, ona_ref, wout_ref, nffn_ref,
                  wr_ref, rb_ref,
                  h1_ref, xp_ref, lg_ref, ya_ref):
    tm = x_ref.shape[0]
    blk = ATTN_BLOCK
    is_first = (pl.program_id(0) % tiles_per_seq) == 0

    lo = lax.broadcasted_iota(jnp.int32, (blk, LANES), 1) < HEAD_DIM
    hpd = 2
    qi = lax.broadcasted_iota(jnp.int32, (hpd * blk, blk), 0) % blk
    from_prev = lax.broadcasted_iota(jnp.int32, (hpd * blk, blk), 1) > qi
    no_prev = jnp.logical_and(is_first, from_prev)
    zero_p = jnp.zeros((hpd * blk, blk), _BF16)
    for j in range(tm // blk):
        rows = slice(j * blk, (j + 1) * blk)
        for g in range(N_KV_HEADS):
            cols = slice(g * LANES, (g + 1) * LANES)
            if j == 0:
                keys = jnp.concatenate([kdh_ref[:, cols], kd_ref[0:blk, cols]], axis=0)
                vals = jnp.concatenate([vdh_ref[:, cols], vd_ref[0:blk, cols]], axis=0)
            else:
                keys = kd_ref[(j - 1) * blk:(j + 1) * blk, cols]
                vals = vd_ref[(j - 1) * blk:(j + 1) * blk, cols]
            for hp in range(4 // hpd):
                grp = (j * N_HEADS + 4 * g + hpd * hp) * blk
                lhs = q_ref[grp:grp + hpd * blk, :]
                trow = slice(hp * hpd * blk, (hp + 1) * hpd * blk)
                s = lax.dot_general(lhs, keys, (((1,), (1,)), ((), ())),
                                    preferred_element_type=_F32)
                s = jnp.where(from_prev, s[:, 0:blk], s[:, blk:2 * blk]) + tab_ref[g, trow, :]
                if j == 0:
                    s = jnp.where(no_prev, NEG_INF, s)
                sink = sink_ref[g, trow, :]
                m = jnp.maximum(jnp.max(s, axis=-1, keepdims=True), sink)
                p = jnp.exp(s - m)
                denom = jnp.sum(p, axis=-1, keepdims=True) + jnp.exp(sink - m)
                p = p.astype(_BF16)
                p = jnp.concatenate([jnp.where(from_prev, p, zero_p),
                                     jnp.where(from_prev, zero_p, p)], axis=1)
                o = _dot(p, vals) / denom
                ya_ref[rows, (2 * g + hp) * LANES:(2 * g + hp + 1) * LANES] = jnp.where(
                    lo, o[0:blk], o[blk:2 * blk])

    ya, inv_a = _rms_split(ya_ref[...], ona_ref[...])
    h1 = (x_ref[...] + _dot(yc_ref[...], wout_ref[0:D_CONV, :])
          + _dot(ya, wout_ref[D_CONV:D_CONV + D_ATTN, :]) * inv_a)
    h1_ref[...] = h1

    xn = _rms(h1, nffn_ref[...])
    xp_ref[...] = _pack_rows(xn)
    lg_ref[...] = _dot(xn.astype(_BF16), wr_ref[...]) + rb_ref[...]


def _mixer(yc, q, kd, vd, x2, tab, sink, ona, w_out, nffn, wr, rb, seq):
    n = x2.shape[0]
    tm = TM_MIX
    const2 = lambda i: (0, 0)
    const3 = lambda i: (0, 0, 0)
    prev_blk = lambda i: (jnp.maximum(i * (tm // ATTN_BLOCK) - 1, 0), 0)
    return pl.pallas_call(
        functools.partial(_mixer_kernel, seq // tm),
        grid=(n // tm,),
        in_specs=[
            pl.BlockSpec((tm, D_CONV), lambda i: (i, 0)),
            pl.BlockSpec((tm * N_HEADS, LANES), lambda i: (i, 0)),
            pl.BlockSpec((tm, KV_DUP), lambda i: (i, 0)),
            pl.BlockSpec((tm, KV_DUP), lambda i: (i, 0)),
            pl.BlockSpec((ATTN_BLOCK, KV_DUP), prev_blk),
            pl.BlockSpec((ATTN_BLOCK, KV_DUP), prev_blk),
            pl.BlockSpec((tm, D_MODEL), lambda i: (i, 0)),
            pl.BlockSpec((N_KV_HEADS, 4 * ATTN_BLOCK, ATTN_BLOCK), const3,
                         pipeline_mode=pl.Buffered(1)),
            pl.BlockSpec((N_KV_HEADS, 4 * ATTN_BLOCK, 1), const3, pipeline_mode=pl.Buffered(1)),
            pl.BlockSpec((1, D_ATTN), const2),
            pl.BlockSpec((D_MODEL, D_MODEL), const2, pipeline_mode=pl.Buffered(1)),
            pl.BlockSpec((1, D_MODEL), const2),
            pl.BlockSpec((D_MODEL, LANES), const2),
            pl.BlockSpec((1, LANES), const2),
        ],
        out_specs=[
            pl.BlockSpec((tm, D_MODEL), lambda i: (i, 0)),
            pl.BlockSpec((tm, HALF), lambda i: (i, 0)),
            pl.BlockSpec((tm, LANES), lambda i: (i, 0)),
        ],
        out_shape=[
            jax.ShapeDtypeStruct((n, D_MODEL), _F32),
            jax.ShapeDtypeStruct((n, HALF), _U32),
            jax.ShapeDtypeStruct((n, LANES), _F32),
        ],
        scratch_shapes=[pltpu.VMEM((tm, D_ATTN), _F32)],
        compiler_params=pltpu.CompilerParams(
            dimension_semantics=("parallel",), vmem_limit_bytes=VMEM_LIMIT),
        name="mixer",
    )(yc, q, kd, vd, kd, vd, x2, tab, sink, ona, w_out, nffn, wr, rb)


def _route_kernel(lg_ref, route_ref, route_t_ref, cnt_ref, run_ref, earlier_ref):
    t = lg_ref.shape[0]

    @pl.when(pl.program_id(0) == 0)
    def _():
        run_ref[...] = jnp.zeros_like(run_ref)
        earlier_ref[...] = (lax.broadcasted_iota(jnp.int32, (t, t), 0)
                            > lax.broadcasted_iota(jnp.int32, (t, t), 1)).astype(_BF16)

    lg = lg_ref[...]
    lane = lax.broadcasted_iota(jnp.int32, (t, LANES), 1)
    ninf = -jnp.inf

    lane_f = lane.astype(_F32)

    def first_max(v):
        m = jnp.max(v, axis=-1, keepdims=True)
        idx = jnp.min(jnp.where(v == m, lane_f, float(LANES)), axis=-1, keepdims=True)
        return m, idx.astype(jnp.int32)

    is_g = lane < N_GROUPS
    gmax, gidx = first_max(jnp.where(is_g, lg, ninf))
    gsum = jnp.sum(jnp.where(is_g, jnp.exp(jnp.where(is_g, lg, gmax) - gmax), 0.0),
                   axis=-1, keepdims=True)
    g_w = 1.0 / gsum
    base = N_GROUPS + EXPERTS_PER_GROUP * gidx
    in_grp = jnp.logical_and(lane >= base, lane < base + EXPERTS_PER_GROUP)
    el = jnp.where(in_grp, lg, ninf)
    m1, i1 = first_max(el)
    m2, i2 = first_max(jnp.where(lane == i1, ninf, el))
    e21 = jnp.exp(m2 - m1)
    w1 = g_w / (1.0 + e21)
    w2 = g_w * e21 / (1.0 + e21)
    e1 = i1 - N_GROUPS
    e2 = i2 - N_GROUPS

    oh1 = lane == e1
    oh2 = lane == e2
    c12 = _dot(earlier_ref[...],
               jnp.concatenate([oh1.astype(_BF16), oh2.astype(_BF16)], axis=1))
    c1 = c12[:, 0:LANES]
    c2 = c12[:, LANES:2 * LANES]
    tot1 = jnp.sum(oh1.astype(_F32), axis=0, keepdims=True)
    tot2 = jnp.sum(oh2.astype(_F32), axis=0, keepdims=True)
    run = run_ref[0:1, :]
    r1 = jnp.sum(jnp.where(oh1, run + c1, 0.0), axis=-1, keepdims=True)
    r2 = jnp.sum(jnp.where(oh2, run + tot1 + c2, 0.0), axis=-1, keepdims=True)
    run_ref[...] = jnp.broadcast_to(run + tot1 + tot2, run_ref.shape)
    cnt_ref[...] = run_ref[...]

    out = jnp.where(lane == 0, e1.astype(_F32), 0.0)
    out = jnp.where(lane == 1, e2.astype(_F32), out)
    out = jnp.where(lane == 2, w1, out)
    out = jnp.where(lane == 3, w2, out)
    out = jnp.where(lane == 4, r1, out)
    out = jnp.where(lane == 5, r2, out)
    route_ref[...] = out
    route_t_ref[...] = out.T[0:8, :]


def _route(logits):
    n = logits.shape[0]
    t = TM_ROUTE
    return pl.pallas_call(
        _route_kernel,
        grid=(n // t,),
        in_specs=[pl.BlockSpec((t, LANES), lambda i: (i, 0))],
        out_specs=[pl.BlockSpec((t, LANES), lambda i: (i, 0)),
                   pl.BlockSpec((8, t), lambda i: (0, i)),
                   pl.BlockSpec((8, LANES), lambda i: (0, 0))],
        out_shape=[jax.ShapeDtypeStruct((n, LANES), _F32),
                   jax.ShapeDtypeStruct((8, n), _F32),
                   jax.ShapeDtypeStruct((8, LANES), _F32)],
        scratch_shapes=[pltpu.VMEM((8, LANES), _F32), pltpu.VMEM((t, t), _BF16)],
        compiler_params=pltpu.CompilerParams(dimension_semantics=("arbitrary",)),
        name="route",
    )(logits)


def _dispatch_kernel(pad_end_ref, padded_ref, dest_ref, xp_ref, xs_ref,
                     buf_ref, zero_ref, lsem, rsem, zsem):
    td = buf_ref.shape[1]
    nb = xs_ref.shape[0] // MOE_BLK
    nb_used = pad_end_ref[N_EXPERTS - 1] // MOE_BLK
    i = pl.program_id(0)
    n_steps = pl.num_programs(0)
    slot = i % 3
    nxt = (i + 1) % 3
    prv = (i + 2) % 3

    def zero_copy(start):
        start = pl.multiple_of(start, MOE_BLK)
        return pltpu.make_async_copy(zero_ref, xs_ref.at[pl.ds(start, MOE_BLK)], zsem)

    def zero_blocks(fn):
        for e in range(N_EXPERTS):
            @pl.when(padded_ref[e] > 0)
            def _():
                fn(zero_copy(pad_end_ref[e] - MOE_BLK))
        for b in range(nb - N_EXPERTS, nb):
            @pl.when(b >= nb_used)
            def _():
                fn(zero_copy(b * MOE_BLK))

    def load(t, s):
        start = pl.multiple_of(t * td, td)
        return pltpu.make_async_copy(xp_ref.at[pl.ds(start, td)], buf_ref.at[s], lsem.at[s])

    def wait_rows(s):
        for k in range(TOP_K):
            pltpu.make_async_copy(buf_ref.at[s], xs_ref.at[pl.ds(0, td)], rsem.at[s]).wait()

    @pl.when(i == 0)
    def _():
        zero_ref[...] = jnp.zeros_like(zero_ref)
        zero_blocks(lambda c: c.start())
        zero_blocks(lambda c: c.wait())
        load(0, 0).start()

    @pl.when(i + 1 < n_steps)
    def _():
        load(i + 1, nxt).start()

    load(i, slot).wait()

    def issue(r, c):
        for k in range(TOP_K):
            d = dest_ref[0, 0, k * td + r]
            pltpu.make_async_copy(buf_ref.at[slot, pl.ds(r, 1)], xs_ref.at[pl.ds(d, 1)],
                                  rsem.at[slot]).start()
        return c

    lax.fori_loop(0, td, issue, 0, unroll=16)

    @pl.when(i > 0)
    def _():
        wait_rows(prv)

    @pl.when(i == n_steps - 1)
    def _():
        wait_rows(slot)


def _dispatch(xp, dest3, pad_end, padded, n_rows):
    steps = dest3.shape[0]
    td = dest3.shape[-1] // TOP_K
    return pl.pallas_call(
        _dispatch_kernel,
        grid_spec=pltpu.PrefetchScalarGridSpec(
            num_scalar_prefetch=2,
            grid=(steps,),
            in_specs=[
                pl.BlockSpec((1, 1, dest3.shape[-1]), lambda i, pe, pd: (i, 0, 0),
                             memory_space=pltpu.SMEM),
                pl.BlockSpec(memory_space=pl.ANY),
            ],
            out_specs=pl.BlockSpec(memory_space=pl.ANY),
            scratch_shapes=[pltpu.VMEM((3, td, HALF), _U32),
                            pltpu.VMEM((MOE_BLK, HALF), _U32),
                            pltpu.SemaphoreType.DMA((3,)), pltpu.SemaphoreType.DMA((3,)),
                            pltpu.SemaphoreType.DMA],
        ),
        out_shape=jax.ShapeDtypeStruct((n_rows, HALF), _U32),
        compiler_params=pltpu.CompilerParams(
            dimension_semantics=("arbitrary",)),
        name="dispatch",
    )(pad_end, padded, dest3, xp)


def _experts_kernel(be_ref, nxt_ref, nb_ref, xs_ref, wg_hbm, wu_hbm, wd_hbm, ys_ref,
                    wgs_ref, wus_ref, wds_ref, wgb_ref, wub_ref, wdb_ref, sem):
    i = pl.program_id(0)
    valid = i < nb_ref[0]
    e = be_ref[i]
    changed = jnp.logical_or(i == 0, e != be_ref[jnp.maximum(i - 1, 0)])

    def fetch(ex):
        return (pltpu.make_async_copy(wg_hbm.at[ex], wgs_ref, sem.at[0]),
                pltpu.make_async_copy(wu_hbm.at[ex], wus_ref, sem.at[1]),
                pltpu.make_async_copy(wd_hbm.at[ex], wds_ref, sem.at[2]))

    @pl.when(i == 0)
    def _():
        for c in fetch(e):
            c.start()

    @pl.when(jnp.logical_and(valid, changed))
    def _():
        for c in fetch(e):
            c.wait()
        wgb_ref[...] = wgs_ref[...].astype(_BF16)
        wub_ref[...] = wus_ref[...].astype(_BF16)
        wdb_ref[...] = wds_ref[...].astype(_BF16)
        nxt = nxt_ref[i]

        @pl.when(nxt >= 0)
        def _():
            for c in fetch(nxt):
                c.start()

    @pl.when(valid)
    def _():
        lo, hi = _unpack_rows(xs_ref[...])
        lo = lo.astype(_BF16)
        hi = hi.astype(_BF16)
        g = _dot(lo, wgb_ref[0:HALF, :]) + _dot(hi, wgb_ref[HALF:D_MODEL, :])
        u = _dot(lo, wub_ref[0:HALF, :]) + _dot(hi, wub_ref[HALF:D_MODEL, :])
        hidden = (g * jax.nn.sigmoid(g) * u).astype(_BF16)
        ys_ref[...] = _pack_rows(_dot(hidden, wdb_ref[...]))

    @pl.when(jnp.logical_not(valid))
    def _():
        ys_ref[...] = jnp.zeros_like(ys_ref)


def _experts(xs, w_gate, w_up, w_down, block_e, next_e, nb_used):
    n_rows = xs.shape[0]
    nb = n_rows // MOE_BLK
    row_blk = lambda i, be, nx, nbu: (jnp.minimum(i, nbu[0] - 1), 0)
    any_spec = pl.BlockSpec(memory_space=pl.ANY)
    return pl.pallas_call(
        _experts_kernel,
        grid_spec=pltpu.PrefetchScalarGridSpec(
            num_scalar_prefetch=3,
            grid=(nb,),
            in_specs=[pl.BlockSpec((MOE_BLK, HALF), row_blk), any_spec, any_spec, any_spec],
            out_specs=pl.BlockSpec((MOE_BLK, HALF), lambda i, be, nx, nbu: (i, 0)),
            scratch_shapes=[pltpu.VMEM((D_MODEL, D_EXPERT), _F32),
                            pltpu.VMEM((D_MODEL, D_EXPERT), _F32),
                            pltpu.VMEM((D_EXPERT, D_MODEL), _F32),
                            pltpu.VMEM((D_MODEL, D_EXPERT), _BF16),
                            pltpu.VMEM((D_MODEL, D_EXPERT), _BF16),
                            pltpu.VMEM((D_EXPERT, D_MODEL), _BF16),
                            pltpu.SemaphoreType.DMA((3,))],
        ),
        out_shape=jax.ShapeDtypeStruct((n_rows, HALF), _U32),
        compiler_params=pltpu.CompilerParams(
            dimension_semantics=("arbitrary",), vmem_limit_bytes=VMEM_LIMIT),
        name="experts",
    )(block_e, next_e, nb_used, xs, w_gate, w_up, w_down)


def _combine_kernel(h1_ref, route_ref, dcur_ref, dnext_ref, ys_ref, p_ref, nple_ref,
                    wgate_ref, wple_ref, out_ref, ybuf_ref, h2_ref, xn_ref, pb_ref, sem):
    tk = h1_ref.shape[0]
    cw = D_MODEL // N_CHUNK
    per = TOP_K * tk // N_CHUNK
    i = pl.program_id(0)
    n_steps = pl.num_programs(0)
    slot = i % 2

    def row_copy(d, s, k, r):
        return pltpu.make_async_copy(ys_ref.at[pl.ds(d, 1)],
                                     ybuf_ref.at[s, k, pl.ds(r, 1)], sem.at[s])

    def wait_slot(s):
        for k in range(TOP_K):
            pltpu.make_async_copy(ys_ref.at[pl.ds(0, tk)], ybuf_ref.at[s, k], sem.at[s]).wait()

    @pl.when(i == 0)
    def _():
        def body(r, c):
            for k in range(TOP_K):
                row_copy(dcur_ref[0, 0, k * tk + r], 0, k, r).start()
            return c
        lax.fori_loop(0, tk, body, 0, unroll=8)

    wait_slot(slot)
    y1 = jnp.concatenate(_unpack_rows(ybuf_ref[slot, 0]), axis=-1)
    y2 = jnp.concatenate(_unpack_rows(ybuf_ref[slot, 1]), axis=-1)
    route = route_ref[...]
    h2 = h1_ref[...] + route[:, 2:3] * y1 + route[:, 3:4] * y2
    xn, inv = _rms_split(h2, nple_ref[...])
    pb = p_ref[...].astype(_BF16)

    def issue_batch(c):
        for j in range(per):
            k, r = j % TOP_K, c * (per // TOP_K) + j // TOP_K
            row_copy(dnext_ref[0, 0, k * tk + r], 1 - slot, k, r).start()

    issue_batch(0)
    h2_ref[...] = h2
    xn_ref[...] = xn
    pb_ref[...] = pb
    for c in range(N_CHUNK):
        if c > 0:
            issue_batch(c)
        cols = slice(c * cw, (c + 1) * cw)
        gate = jax.nn.sigmoid(_dot(xn_ref[...], wgate_ref[:, cols]) * inv)
        emb = _dot(pb_ref[...], wple_ref[:, cols])
        out_ref[:, cols] = h2_ref[:, cols] + gate * emb

    @pl.when(i == n_steps - 1)
    def _():
        wait_slot(1 - slot)


def _combine(h1, route, dest3, ys, p2, nple, w_gate, w_ple):
    n = h1.shape[0]
    tk = TM_COMB
    steps = n // tk
    const2 = lambda i: (0, 0)
    return pl.pallas_call(
        _combine_kernel,
        grid=(steps,),
        in_specs=[
            pl.BlockSpec((tk, D_MODEL), lambda i: (i, 0)),
            pl.BlockSpec((tk, LANES), lambda i: (i, 0)),
            pl.BlockSpec((1, 1, TOP_K * tk), lambda i: (i, 0, 0), memory_space=pltpu.SMEM),
            pl.BlockSpec((1, 1, TOP_K * tk), lambda i: (jnp.minimum(i + 1, steps - 1), 0, 0),
                         memory_space=pltpu.SMEM),
            pl.BlockSpec(memory_space=pl.ANY),
            pl.BlockSpec((tk, PLE_DIM), lambda i: (i, 0)),
            pl.BlockSpec((1, D_MODEL), const2),
            pl.BlockSpec((D_MODEL, D_MODEL), const2, pipeline_mode=pl.Buffered(1)),
            pl.BlockSpec((PLE_DIM, D_MODEL), const2, pipeline_mode=pl.Buffered(1)),
        ],
        out_specs=pl.BlockSpec((tk, D_MODEL), lambda i: (i, 0)),
        out_shape=jax.ShapeDtypeStruct((n, D_MODEL), _F32),
        scratch_shapes=[pltpu.VMEM((2, TOP_K, tk, HALF), _U32),
                        pltpu.VMEM((tk, D_MODEL), _F32),
                        pltpu.VMEM((tk, D_MODEL), _BF16),
                        pltpu.VMEM((tk, PLE_DIM), _BF16),
                        pltpu.SemaphoreType.DMA((2,))],
        compiler_params=pltpu.CompilerParams(
            dimension_semantics=("arbitrary",), vmem_limit_bytes=VMEM_LIMIT),
        name="combine",
    )(h1, route, dest3, dest3, ys, p2, nple, w_gate, w_ple)


def _attention_tables(sinks):
    slopes = 2.0 ** (-8.0 * jnp.arange(1, N_HEADS + 1, dtype=_F32) / N_HEADS)
    qi = jnp.arange(ATTN_BLOCK)[:, None]
    kj = jnp.arange(ATTN_BLOCK)[None, :]
    dist = jnp.where(kj > qi, ATTN_BLOCK + qi - kj, qi - kj)
    tab = -slopes[:, None, None] * dist.astype(_F32)[None]
    tab = tab.reshape(N_KV_HEADS, 4 * ATTN_BLOCK, ATTN_BLOCK)
    sink = jnp.repeat(sinks.astype(_F32), ATTN_BLOCK).reshape(N_KV_HEADS, 4 * ATTN_BLOCK, 1)
    return tab, sink


def _layer(h, p_i, norm_mix, w_in, conv_w, q_norm, k_norm, sinks, out_norm_conv,
           out_norm_attn, w_out, norm_ffn, w_group, b_group, w_router, b_router,
           w_gate, w_up, w_down, norm_ple, w_ple_gate, w_ple):
    bsz, seq, _ = h.shape
    n = bsz * seq
    x2 = h.reshape(n, D_MODEL)
    row = lambda v: v.reshape(1, -1).astype(_F32)

    yc, q, kd, vd = _inproj(
        x2, row(norm_mix), w_in.astype(_BF16), row(jnp.tile(q_norm, 2)),
        row(jnp.tile(k_norm, 2)), conv_w.astype(_F32), row(out_norm_conv), seq)

    tab, sink = _attention_tables(sinks)
    w_r = jnp.zeros((D_MODEL, LANES), _F32)
    w_r = w_r.at[:, :N_GROUPS].set(w_group).at[:, N_GROUPS:N_GROUPS + N_EXPERTS].set(w_router)
    r_bias = jnp.zeros((1, LANES), _F32)
    r_bias = r_bias.at[0, :N_GROUPS].set(b_group).at[0, N_GROUPS:N_GROUPS + N_EXPERTS].set(b_router)
    h1, xp, logits = _mixer(
        yc, q, kd, vd, x2, tab, sink, row(out_norm_attn), w_out.astype(_BF16),
        row(norm_ffn), w_r.astype(_BF16), r_bias, seq)

    route, route_t, counts = _route(logits)

    cnt = counts[0, :N_EXPERTS].astype(jnp.int32)
    padded = (cnt + MOE_BLK - 1) // MOE_BLK * MOE_BLK
    pad_end = jnp.cumsum(padded)
    pad_start = pad_end - padded
    eid = route_t[0:TOP_K].astype(jnp.int32)
    onehot = eid[..., None] == jnp.arange(N_EXPERTS, dtype=jnp.int32)
    dest = (jnp.sum(jnp.where(onehot, pad_start, 0), axis=-1)
            + route_t[4:4 + TOP_K].astype(jnp.int32))

    def step_blocks(tile):
        return dest.reshape(TOP_K, n // tile, 1, tile).transpose(1, 2, 0, 3).reshape(
            n // tile, 1, TOP_K * tile)
    n_rows = n * TOP_K + N_EXPERTS * MOE_BLK
    nb = n_rows // MOE_BLK
    nb_used = (pad_end[-1] // MOE_BLK).astype(jnp.int32)
    blk_start = jnp.arange(nb, dtype=jnp.int32) * MOE_BLK
    block_e = jnp.sum((blk_start[:, None] >= pad_end[None, :]).astype(jnp.int32), axis=1)
    block_e = jnp.minimum(block_e, N_EXPERTS - 1)
    block_e = jnp.where(jnp.arange(nb) < nb_used, block_e, block_e[nb_used - 1])
    ids = jnp.arange(N_EXPERTS, dtype=jnp.int32)
    later = jnp.logical_and(ids[None, :] > ids[:, None], cnt[None, :] > 0)
    next_present = jnp.min(jnp.where(later, ids[None, :], N_EXPERTS), axis=1)
    next_present = jnp.where(next_present < N_EXPERTS, next_present, -1)
    onehot_b = block_e[:, None] == ids[None, :]
    next_e = jnp.sum(jnp.where(onehot_b, next_present[None, :], 0), axis=1).astype(jnp.int32)

    xs = _dispatch(xp, step_blocks(TM_DISP),
                   pad_end.astype(jnp.int32), padded.astype(jnp.int32), n_rows)
    ys = _experts(xs, w_gate, w_up, w_down, block_e, next_e, nb_used.reshape(1))
    out = _combine(h1, route, step_blocks(TM_COMB), ys,
                   p_i.reshape(n, PLE_DIM), row(norm_ple), w_ple_gate.astype(_BF16),
                   w_ple.astype(_BF16))
    return out.reshape(bsz, seq, D_MODEL)


def kernel(x, p, norm_mix, w_in, conv_w, q_norm, k_norm, sinks, out_norm_conv, out_norm_attn,
           w_out, norm_ffn, w_group, b_group, w_router, b_router, w_gate, w_up, w_down,
           norm_ple, w_ple_gate, w_ple):
    h = x
    for i in range(p.shape[0]):
        h = _layer(h, p[i], norm_mix[i], w_in[i], conv_w[i], q_norm[i], k_norm[i], sinks[i],
                   out_norm_conv[i], out_norm_attn[i], w_out[i], norm_ffn[i], w_group[i],
                   b_group[i], w_router[i], b_router[i], w_gate[i], w_up[i], w_down[i],
                   norm_ple[i], w_ple_gate[i], w_ple[i])
    return h
```

```python
import functools

import jax
import jax.numpy as jnp
from jax import lax
from jax.experimental import pallas as pl
from jax.experimental.pallas import tpu as pltpu

D_MODEL = 2048
D_CONV = 1024
N_HEADS = 16
N_KV_HEADS = 4
HEAD_DIM = 64
D_ATTN = N_HEADS * HEAD_DIM
D_KV = N_KV_HEADS * HEAD_DIM
D_IN = 3 * D_CONV + D_ATTN + 2 * D_KV
ATTN_BLOCK = 128
N_GROUPS = 4
EXPERTS_PER_GROUP = 8
N_EXPERTS = N_GROUPS * EXPERTS_PER_GROUP
TOP_K = 2
D_EXPERT = 512
PLE_DIM = 256
EPS = 1e-6
NEG_INF = -1e30

LANES = 128
HALF = D_MODEL // 2
ROW_W = HALF + LANES
KV_DUP = 2 * D_KV

TM_PROJ = 512
TM_MIX = 512
TM_ROUTE = 1024
TM_DISP = 1024
TM_COMB = 512
MOE_BLK = 512
VMEM_LIMIT = 56 * 1024 * 1024

_F32 = jnp.float32
_BF16 = jnp.bfloat16
_U32 = jnp.uint32
_HI_MASK = 0xFFFF0000


def _rms(x, gain):
    return x * lax.rsqrt(jnp.mean(x * x, axis=-1, keepdims=True) + EPS) * gain


def _rms_split(x, gain):
    inv = lax.rsqrt(jnp.mean(x * x, axis=-1, keepdims=True) + EPS)
    return (x * gain).astype(_BF16), inv


def _dot(a, b):
    return jnp.dot(a, b, preferred_element_type=_F32)


def _pack_rows(x):
    h = x.shape[-1] // 2
    bits = lax.bitcast_convert_type(x.astype(_BF16).astype(_F32), _U32)
    return (bits[:, h:] & _U32(_HI_MASK)) | (bits[:, :h] >> 16)


def _unpack_rows(w):
    lo = lax.bitcast_convert_type(w << 16, _F32)
    hi = lax.bitcast_convert_type(w & _U32(_HI_MASK), _F32)
    return lo, hi


def _inproj_kernel(tiles_per_seq, x_ref, g_ref, w_ref, qg_ref, kg_ref, cw_ref, onc_ref,
                   yc_ref, q_ref, kd_ref, vd_ref, t_ref, uprev_ref):
    tm = x_ref.shape[0]
    is_first = (pl.program_id(0) % tiles_per_seq) == 0
    xn = _rms(x_ref[...], g_ref[...]).astype(_BF16)

    @pl.when(pl.program_id(0) == 0)
    def _():
        uprev_ref[...] = jnp.zeros_like(uprev_ref)

    lo = lax.broadcasted_iota(jnp.int32, (tm, LANES), 1) < HEAD_DIM
    q = _dot(xn, w_ref[:, 3 * D_CONV:3 * D_CONV + D_ATTN])
    scale = HEAD_DIM ** -0.5
    for pr in range(D_ATTN // LANES):
        qp = q[:, pr * LANES:(pr + 1) * LANES]
        sq = qp * qp
        s_lo = jnp.sum(jnp.where(lo, sq, 0.0), axis=-1, keepdims=True)
        s_hi = jnp.sum(jnp.where(lo, 0.0, sq), axis=-1, keepdims=True)
        ms = jnp.where(lo, s_lo, s_hi) * (1.0 / HEAD_DIM)
        qn = (qp * lax.rsqrt(ms + EPS) * qg_ref[...] * scale).astype(_BF16)
        q_lo = jnp.where(lo, qn, jnp.zeros_like(qn))
        q_hi = jnp.where(lo, jnp.zeros_like(qn), qn)
        for j in range(tm // ATTN_BLOCK):
            rows = slice(j * ATTN_BLOCK, (j + 1) * ATTN_BLOCK)
            base = (j * N_HEADS + 2 * pr) * ATTN_BLOCK
            q_ref[base:base + ATTN_BLOCK, :] = q_lo[rows]
            q_ref[base + ATTN_BLOCK:base + 2 * ATTN_BLOCK, :] = q_hi[rows]

    kv = _dot(xn, w_ref[:, 3 * D_CONV + D_ATTN:D_IN])
    for pr in range(2 * D_KV // LANES):
        pair = kv[:, pr * LANES:(pr + 1) * LANES]
        swapped = pltpu.roll(pair, HEAD_DIM, 1)
        for half in range(2):
            dup = jnp.where(lo, pair, swapped) if half == 0 else jnp.where(lo, swapped, pair)
            head = 2 * pr + half
            if head < N_KV_HEADS:
                ms = jnp.sum(dup * dup, axis=-1, keepdims=True) * (1.0 / LANES)
                dup = dup * lax.rsqrt(ms + EPS) * kg_ref[...]
                kd_ref[:, head * LANES:(head + 1) * LANES] = dup.astype(_BF16)
            else:
                head -= N_KV_HEADS
                vd_ref[:, head * LANES:(head + 1) * LANES] = dup.astype(_BF16)

    ch = 256
    row = lax.broadcasted_iota(jnp.int32, (tm, ch), 0)
    last = uprev_ref.shape[0] - 1
    ssq = jnp.zeros((tm, 1), _F32)
    for j in range(D_CONV // ch):
        cs = slice(j * ch, (j + 1) * ch)
        b = _dot(xn, w_ref[:, j * ch:(j + 1) * ch])
        c = _dot(xn, w_ref[:, D_CONV + j * ch:D_CONV + (j + 1) * ch])
        hc = _dot(xn, w_ref[:, 2 * D_CONV + j * ch:2 * D_CONV + (j + 1) * ch])
        u = c * hc
        uh = jnp.where(is_first, 0.0, uprev_ref[:, cs])
        uprev_ref[:, cs] = u[tm - last - 1:tm]
        u1 = jnp.where(row == 0, uh[last:last + 1], pltpu.roll(u, 1, 0))
        u2 = jnp.where(row == 0, uh[last - 1:last],
                       jnp.where(row == 1, uh[last:last + 1], pltpu.roll(u, 2, 0)))
        t = b * (cw_ref[2:3, cs] * u + cw_ref[1:2, cs] * u1 + cw_ref[0:1, cs] * u2)
        ssq = ssq + jnp.sum(t * t, axis=-1, keepdims=True)
        t_ref[:, cs] = t
    inv = lax.rsqrt(ssq * (1.0 / D_CONV) + EPS)
    yc_ref[...] = (t_ref[...] * inv * onc_ref[...]).astype(_BF16)


def _inproj(x2, g, w_in, qg2, kg2, cw, onc, seq):
    n = x2.shape[0]
    tm = TM_PROJ
    const = lambda i: (0, 0)
    return pl.pallas_call(
        functools.partial(_inproj_kernel, seq // tm),
        grid=(n // tm,),
        in_specs=[
            pl.BlockSpec((tm, D_MODEL), lambda i: (i, 0)),
            pl.BlockSpec((1, D_MODEL), const),
            pl.BlockSpec((D_MODEL, D_IN), const, pipeline_mode=pl.Buffered(1)),
            pl.BlockSpec((1, LANES), const),
            pl.BlockSpec((1, LANES), const),
            pl.BlockSpec((3, D_CONV), const),
            pl.BlockSpec((1, D_CONV), const),
        ],
        out_specs=[
            pl.BlockSpec((tm, D_CONV), lambda i: (i, 0)),
            pl.BlockSpec((tm * N_HEADS, LANES), lambda i: (i, 0)),
            pl.BlockSpec((tm, KV_DUP), lambda i: (i, 0)),
            pl.BlockSpec((tm, KV_DUP), lambda i: (i, 0)),
        ],
        out_shape=[
            jax.ShapeDtypeStruct((n, D_CONV), _BF16),
            jax.ShapeDtypeStruct((n * N_HEADS, LANES), _BF16),
            jax.ShapeDtypeStruct((n, KV_DUP), _BF16),
            jax.ShapeDtypeStruct((n, KV_DUP), _BF16),
        ],
        scratch_shapes=[pltpu.VMEM((tm, D_CONV), _F32),
                        pltpu.VMEM((8, D_CONV), _F32)],
        compiler_params=pltpu.CompilerParams(
            dimension_semantics=("arbitrary",), vmem_limit_bytes=VMEM_LIMIT),
        name="inproj",
    )(x2, g, w_in, qg2, kg2, cw, onc)


def _mixer_kernel(tiles_per_seq,
                  yc_ref, q_ref, kd_ref, vd_ref, kdh_ref, vdh_ref, x_ref,
                  tab_ref, sink_ref, ona_ref, wout_ref, nffn_ref,
                  wr_ref, rb_ref,
                  h1_ref, xp_ref, lg_ref, ya_ref):
    tm = x_ref.shape[0]
    blk = ATTN_BLOCK
    is_first = (pl.program_id(0) % tiles_per_seq) == 0

    lo = lax.broadcasted_iota(jnp.int32, (blk, LANES), 1) < HEAD_DIM
    hpd = 2
    qi = lax.broadcasted_iota(jnp.int32, (hpd * blk, blk), 0) % blk
    from_prev = lax.broadcasted_iota(jnp.int32, (hpd * blk, blk), 1) > qi
    no_prev = jnp.logical_and(is_first, from_prev)
    zero_p = jnp.zeros((hpd * blk, blk), _BF16)
    for j in range(tm // blk):
        rows = slice(j * blk, (j + 1) * blk)
        for g in range(N_KV_HEADS):
            cols = slice(g * LANES, (g + 1) * LANES)
            if j == 0:
                keys = jnp.concatenate([kdh_ref[:, cols], kd_ref[0:blk, cols]], axis=0)
                vals = jnp.concatenate([vdh_ref[:, cols], vd_ref[0:blk, cols]], axis=0)
            else:
                keys = kd_ref[(j - 1) * blk:(j + 1) * blk, cols]
                vals = vd_ref[(j - 1) * blk:(j + 1) * blk, cols]
            for hp in range(4 // hpd):
                grp = (j * N_HEADS + 4 * g + hpd * hp) * blk
                lhs = q_ref[grp:grp + hpd * blk, :]
                trow = slice(hp * hpd * blk, (hp + 1) * hpd * blk)
                s = lax.dot_general(lhs, keys, (((1,), (1,)), ((), ())),
                                    preferred_element_type=_F32)
                s = jnp.where(from_prev, s[:, 0:blk], s[:, blk:2 * blk]) + tab_ref[g, trow, :]
                if j == 0:
                    s = jnp.where(no_prev, NEG_INF, s)
                sink = sink_ref[g, trow, :]
                m = jnp.maximum(jnp.max(s, axis=-1, keepdims=True), sink)
                p = jnp.exp(s - m)
                denom = jnp.sum(p, axis=-1, keepdims=True) + jnp.exp(sink - m)
                p = p.astype(_BF16)
                p = jnp.concatenate([jnp.where(from_prev, p, zero_p),
                                     jnp.where(from_prev, zero_p, p)], axis=1)
                o = _dot(p, vals) / denom
                ya_ref[rows, (2 * g + hp) * LANES:(2 * g + hp + 1) * LANES] = jnp.where(
                    lo, o[0:blk], o[blk:2 * blk])

    ya, inv_a = _rms_split(ya_ref[...], ona_ref[...])
    h1 = (x_ref[...] + _dot(yc_ref[...], wout_ref[0:D_CONV, :])
          + _dot(ya, wout_ref[D_CONV:D_CONV + D_ATTN, :]) * inv_a)
    h1_ref[...] = h1

    xn = _rms(h1, nffn_ref[...])
    xp_ref[:, 0:HALF] = _pack_rows(xn)
    tok = pl.program_id(0) * tm + lax.broadcasted_iota(jnp.int32, (tm, LANES), 0)
    xp_ref[:, HALF:ROW_W] = tok.astype(_U32)
    lg_ref[...] = _dot(xn.astype(_BF16), wr_ref[...]) + rb_ref[...]


def _mixer(yc, q, kd, vd, x2, tab, sink, ona, w_out, nffn, wr, rb, seq):
    n = x2.shape[0]
    tm = TM_MIX
    const2 = lambda i: (0, 0)
    const3 = lambda i: (0, 0, 0)
    prev_blk = lambda i: (jnp.maximum(i * (tm // ATTN_BLOCK) - 1, 0), 0)
    return pl.pallas_call(
        functools.partial(_mixer_kernel, seq // tm),
        grid=(n // tm,),
        in_specs=[
            pl.BlockSpec((tm, D_CONV), lambda i: (i, 0)),
            pl.BlockSpec((tm * N_HEADS, LANES), lambda i: (i, 0)),
            pl.BlockSpec((tm, KV_DUP), lambda i: (i, 0)),
            pl.BlockSpec((tm, KV_DUP), lambda i: (i, 0)),
            pl.BlockSpec((ATTN_BLOCK, KV_DUP), prev_blk),
            pl.BlockSpec((ATTN_BLOCK, KV_DUP), prev_blk),
            pl.BlockSpec((tm, D_MODEL), lambda i: (i, 0)),
            pl.BlockSpec((N_KV_HEADS, 4 * ATTN_BLOCK, ATTN_BLOCK), const3,
                         pipeline_mode=pl.Buffered(1)),
            pl.BlockSpec((N_KV_HEADS, 4 * ATTN_BLOCK, 1), const3, pipeline_mode=pl.Buffered(1)),
            pl.BlockSpec((1, D_ATTN), const2),
            pl.BlockSpec((D_MODEL, D_MODEL), const2, pipeline_mode=pl.Buffered(1)),
            pl.BlockSpec((1, D_MODEL), const2),
            pl.BlockSpec((D_MODEL, LANES), const2),
            pl.BlockSpec((1, LANES), const2),
        ],
        out_specs=[
            pl.BlockSpec((tm, D_MODEL), lambda i: (i, 0)),
            pl.BlockSpec((tm, ROW_W), lambda i: (i, 0)),
            pl.BlockSpec((tm, LANES), lambda i: (i, 0)),
        ],
        out_shape=[
            jax.ShapeDtypeStruct((n, D_MODEL), _F32),
            jax.ShapeDtypeStruct((n, ROW_W), _U32),
            jax.ShapeDtypeStruct((n, LANES), _F32),
        ],
        scratch_shapes=[pltpu.VMEM((tm, D_ATTN), _F32)],
        compiler_params=pltpu.CompilerParams(
            dimension_semantics=("parallel",), vmem_limit_bytes=VMEM_LIMIT),
        name="mixer",
    )(yc, q, kd, vd, kd, vd, x2, tab, sink, ona, w_out, nffn, wr, rb)


def _route_kernel(lg_ref, route_ref, route_t_ref, cnt_ref, run_ref, earlier_ref):
    t = lg_ref.shape[0]

    @pl.when(pl.program_id(0) == 0)
    def _():
        run_ref[...] = jnp.zeros_like(run_ref)
        earlier_ref[...] = (lax.broadcasted_iota(jnp.int32, (t, t), 0)
                            > lax.broadcasted_iota(jnp.int32, (t, t), 1)).astype(_BF16)

    lg = lg_ref[...]
    lane = lax.broadcasted_iota(jnp.int32, (t, LANES), 1)
    ninf = -jnp.inf

    lane_f = lane.astype(_F32)

    def first_max(v):
        m = jnp.max(v, axis=-1, keepdims=True)
        idx = jnp.min(jnp.where(v == m, lane_f, float(LANES)), axis=-1, keepdims=True)
        return m, idx.astype(jnp.int32)

    is_g = lane < N_GROUPS
    gmax, gidx = first_max(jnp.where(is_g, lg, ninf))
    gsum = jnp.sum(jnp.where(is_g, jnp.exp(jnp.where(is_g, lg, gmax) - gmax), 0.0),
                   axis=-1, keepdims=True)
    g_w = 1.0 / gsum
    base = N_GROUPS + EXPERTS_PER_GROUP * gidx
    in_grp = jnp.logical_and(lane >= base, lane < base + EXPERTS_PER_GROUP)
    el = jnp.where(in_grp, lg, ninf)
    m1, i1 = first_max(el)
    m2, i2 = first_max(jnp.where(lane == i1, ninf, el))
    e21 = jnp.exp(m2 - m1)
    w1 = g_w / (1.0 + e21)
    w2 = g_w * e21 / (1.0 + e21)
    e1 = i1 - N_GROUPS
    e2 = i2 - N_GROUPS

    oh1 = lane == e1
    oh2 = lane == e2
    c12 = _dot(earlier_ref[...],
               jnp.concatenate([oh1.astype(_BF16), oh2.astype(_BF16)], axis=1))
    c1 = c12[:, 0:LANES]
    c2 = c12[:, LANES:2 * LANES]
    tot1 = jnp.sum(oh1.astype(_F32), axis=0, keepdims=True)
    tot2 = jnp.sum(oh2.astype(_F32), axis=0, keepdims=True)
    run = run_ref[0:1, :]
    r1 = jnp.sum(jnp.where(oh1, run + c1, 0.0), axis=-1, keepdims=True)
    r2 = jnp.sum(jnp.where(oh2, run + tot1 + c2, 0.0), axis=-1, keepdims=True)
    run_ref[...] = jnp.broadcast_to(run + tot1 + tot2, run_ref.shape)
    cnt_ref[...] = run_ref[...]

    out = jnp.where(lane == 0, e1.astype(_F32), 0.0)
    out = jnp.where(lane == 1, e2.astype(_F32), out)
    out = jnp.where(lane == 2, w1, out)
    out = jnp.where(lane == 3, w2, out)
    out = jnp.where(lane == 4, r1, out)
    out = jnp.where(lane == 5, r2, out)
    route_ref[...] = out
    route_t_ref[...] = out.T[0:8, :]


def _route(logits):
    n = logits.shape[0]
    t = TM_ROUTE
    return pl.pallas_call(
        _route_kernel,
        grid=(n // t,),
        in_specs=[pl.BlockSpec((t, LANES), lambda i: (i, 0))],
        out_specs=[pl.BlockSpec((t, LANES), lambda i: (i, 0)),
                   pl.BlockSpec((8, t), lambda i: (0, i)),
                   pl.BlockSpec((8, LANES), lambda i: (0, 0))],
        out_shape=[jax.ShapeDtypeStruct((n, LANES), _F32),
                   jax.ShapeDtypeStruct((8, n), _F32),
                   jax.ShapeDtypeStruct((8, LANES), _F32)],
        scratch_shapes=[pltpu.VMEM((8, LANES), _F32), pltpu.VMEM((t, t), _BF16)],
        compiler_params=pltpu.CompilerParams(dimension_semantics=("arbitrary",)),
        name="route",
    )(logits)


def _dispatch_kernel(pad_end_ref, padded_ref, dest_ref, xp_ref, xs_ref,
                     buf_ref, buf2_ref, zero_ref, lsem, rsem, zsem):
    td = buf_ref.shape[1]
    n_tok = xp_ref.shape[0]
    nb = xs_ref.shape[0] // MOE_BLK
    nb_used = pad_end_ref[N_EXPERTS - 1] // MOE_BLK
    i = pl.program_id(0)
    n_steps = pl.num_programs(0)
    slot = i % 3
    nxt = (i + 1) % 3
    prv = (i + 2) % 3

    def zero_copy(start):
        start = pl.multiple_of(start, MOE_BLK)
        return pltpu.make_async_copy(zero_ref, xs_ref.at[pl.ds(start, MOE_BLK)], zsem)

    def zero_blocks(fn):
        for e in range(N_EXPERTS):
            @pl.when(padded_ref[e] > 0)
            def _():
                fn(zero_copy(pad_end_ref[e] - MOE_BLK))
        for b in range(nb - N_EXPERTS, nb):
            @pl.when(b >= nb_used)
            def _():
                fn(zero_copy(b * MOE_BLK))

    def load(t, s):
        start = pl.multiple_of(t * td, td)
        return pltpu.make_async_copy(xp_ref.at[pl.ds(start, td)], buf_ref.at[s], lsem.at[s])

    def wait_rows(s):
        for k in range(TOP_K):
            pltpu.make_async_copy(buf_ref.at[s], xs_ref.at[pl.ds(0, td)], rsem.at[s]).wait()

    @pl.when(i == 0)
    def _():
        zero_ref[...] = jnp.zeros_like(zero_ref)
        zero_blocks(lambda c: c.start())
        zero_blocks(lambda c: c.wait())
        load(0, 0).start()

    @pl.when(i + 1 < n_steps)
    def _():
        load(i + 1, nxt).start()

    load(i, slot).wait()
    buf2_ref[slot, :, 0:HALF] = buf_ref[slot, :, 0:HALF]
    buf2_ref[slot, :, HALF:ROW_W] = buf_ref[slot, :, HALF:ROW_W] + _U32(n_tok)

    def issue(r, c):
        for k, src_ref in enumerate((buf_ref, buf2_ref)):
            d = dest_ref[0, 0, k * td + r]
            pltpu.make_async_copy(src_ref.at[slot, pl.ds(r, 1)], xs_ref.at[pl.ds(d, 1)],
                                  rsem.at[slot]).start()
        return c

    lax.fori_loop(0, td, issue, 0, unroll=16)

    @pl.when(i > 0)
    def _():
        wait_rows(prv)

    @pl.when(i == n_steps - 1)
    def _():
        wait_rows(slot)


def _dispatch(xp, dest3, pad_end, padded, n_rows):
    steps = dest3.shape[0]
    td = dest3.shape[-1] // TOP_K
    return pl.pallas_call(
        _dispatch_kernel,
        grid_spec=pltpu.PrefetchScalarGridSpec(
            num_scalar_prefetch=2,
            grid=(steps,),
            in_specs=[
                pl.BlockSpec((1, 1, dest3.shape[-1]), lambda i, pe, pd: (i, 0, 0),
                             memory_space=pltpu.SMEM),
                pl.BlockSpec(memory_space=pl.ANY),
            ],
            out_specs=pl.BlockSpec(memory_space=pl.ANY),
            scratch_shapes=[pltpu.VMEM((3, td, ROW_W), _U32),
                            pltpu.VMEM((3, td, ROW_W), _U32),
                            pltpu.VMEM((MOE_BLK, ROW_W), _U32),
                            pltpu.SemaphoreType.DMA((3,)), pltpu.SemaphoreType.DMA((3,)),
                            pltpu.SemaphoreType.DMA],
        ),
        out_shape=jax.ShapeDtypeStruct((n_rows, ROW_W), _U32),
        compiler_params=pltpu.CompilerParams(
            dimension_semantics=("arbitrary",), vmem_limit_bytes=VMEM_LIMIT),
        name="dispatch",
    )(pad_end, padded, dest3, xp)


def _experts_kernel(be_ref, nxt_ref, nb_ref, xs_ref, inv_ref, wg_hbm, wu_hbm, wd_hbm, y2_ref,
                    wgs_ref, wus_ref, wds_ref, wgb_ref, wub_ref, wdb_ref,
                    ob0_ref, ob1_ref, wsem, osem):
    i = pl.program_id(0)
    n_used = nb_ref[0]
    valid = i < n_used
    e = be_ref[i]
    changed = jnp.logical_or(i == 0, e != be_ref[jnp.maximum(i - 1, 0)])
    obufs = (ob0_ref, ob1_ref)
    trash = y2_ref.shape[0] - 2 * MOE_BLK

    def fetch(ex):
        return (pltpu.make_async_copy(wg_hbm.at[ex], wgs_ref, wsem.at[0]),
                pltpu.make_async_copy(wu_hbm.at[ex], wus_ref, wsem.at[1]),
                pltpu.make_async_copy(wd_hbm.at[ex], wds_ref, wsem.at[2]))

    def row_copy(par, r):
        return pltpu.make_async_copy(obufs[par].at[pl.ds(r, 1)],
                                     y2_ref.at[pl.ds(inv_ref[0, 0, r], 1)], osem.at[par])

    def wait_block(par):
        pltpu.make_async_copy(obufs[par], y2_ref.at[pl.ds(0, MOE_BLK)], osem.at[par]).wait()

    @pl.when(i == 0)
    def _():
        for c in fetch(e):
            c.start()
        ob1_ref[...] = jnp.zeros_like(ob1_ref)
        fills = [pltpu.make_async_copy(ob1_ref, y2_ref.at[pl.ds(trash + h * MOE_BLK, MOE_BLK)],
                                       osem.at[1]) for h in range(2)]
        for c in fills:
            c.start()
        for c in fills:
            c.wait()

    @pl.when(jnp.logical_and(valid, changed))
    def _():
        for c in fetch(e):
            c.wait()
        wgb_ref[...] = wgs_ref[...].astype(_BF16)
        wub_ref[...] = wus_ref[...].astype(_BF16)
        wdb_ref[...] = wds_ref[...].astype(_BF16)
        nxt = nxt_ref[i]

        @pl.when(nxt >= 0)
        def _():
            for c in fetch(nxt):
                c.start()

    batch = MOE_BLK // 4
    for par in range(2):
        @pl.when(jnp.logical_and(valid, i % 2 == par))
        def _():
            @pl.when(i >= 1)
            def _():
                wait_block(par)

            def issue(b):
                for r in range(b * batch, (b + 1) * batch):
                    row_copy(1 - par, r).start()

            lo, hi = _unpack_rows(xs_ref[:, 0:HALF])
            lo = lo.astype(_BF16)
            hi = hi.astype(_BF16)
            issue(0)
            g = _dot(lo, wgb_ref[0:HALF, :])
            issue(1)
            g = g + _dot(hi, wgb_ref[HALF:D_MODEL, :])
            issue(2)
            u = _dot(lo, wub_ref[0:HALF, :])
            issue(3)
            u = u + _dot(hi, wub_ref[HALF:D_MODEL, :])
            hidden = (g * jax.nn.sigmoid(g) * u).astype(_BF16)
            obufs[par][...] = _pack_rows(_dot(hidden, wdb_ref[...]))

        @pl.when(jnp.logical_and(i == n_used, i % 2 == par))
        def _():
            wait_block(par)

            def body(r, c):
                row_copy(1 - par, r).start()
                return c
            lax.fori_loop(0, MOE_BLK, body, 0, unroll=8)
            wait_block(1 - par)


def _experts(xs, inv3, w_gate, w_up, w_down, block_e, next_e, nb_used, n_slots):
    n_rows = xs.shape[0]
    nb = n_rows // MOE_BLK
    row_blk = lambda i, be, nx, nbu: (jnp.minimum(i, nbu[0] - 1), 0)
    inv_blk = lambda i, be, nx, nbu: (jnp.minimum(i, nbu[0]), 0, 0)
    any_spec = pl.BlockSpec(memory_space=pl.ANY)
    return pl.pallas_call(
        _experts_kernel,
        grid_spec=pltpu.PrefetchScalarGridSpec(
            num_scalar_prefetch=3,
            grid=(nb,),
            in_specs=[pl.BlockSpec((MOE_BLK, ROW_W), row_blk),
                      pl.BlockSpec((1, 1, MOE_BLK), inv_blk, memory_space=pltpu.SMEM),
                      any_spec, any_spec, any_spec],
            out_specs=any_spec,
            scratch_shapes=[pltpu.VMEM((D_MODEL, D_EXPERT), _F32),
                            pltpu.VMEM((D_MODEL, D_EXPERT), _F32),
                            pltpu.VMEM((D_EXPERT, D_MODEL), _F32),
                            pltpu.VMEM((D_MODEL, D_EXPERT), _BF16),
                            pltpu.VMEM((D_MODEL, D_EXPERT), _BF16),
                            pltpu.VMEM((D_EXPERT, D_MODEL), _BF16),
                            pltpu.VMEM((MOE_BLK, HALF), _U32),
                            pltpu.VMEM((MOE_BLK, HALF), _U32),
                            pltpu.SemaphoreType.DMA((3,)),
                            pltpu.SemaphoreType.DMA((2,))],
        ),
        out_shape=jax.ShapeDtypeStruct((n_slots + 2 * MOE_BLK, HALF), _U32),
        compiler_params=pltpu.CompilerParams(
            dimension_semantics=("arbitrary",), vmem_limit_bytes=VMEM_LIMIT),
        name="experts",
    )(block_e, next_e, nb_used, xs, inv3, w_gate, w_up, w_down)


def _combine_kernel(h1_ref, route_ref, ya_ref, yb_ref, p_ref, nple_ref, wgate_ref, wple_ref,
                    out_ref):
    y1 = jnp.concatenate(_unpack_rows(ya_ref[...]), axis=-1)
    y2 = jnp.concatenate(_unpack_rows(yb_ref[...]), axis=-1)
    route = route_ref[...]
    h2 = h1_ref[...] + route[:, 2:3] * y1 + route[:, 3:4] * y2
    xn, inv = _rms_split(h2, nple_ref[...])
    gate = jax.nn.sigmoid(_dot(xn, wgate_ref[...]) * inv)
    emb = _dot(p_ref[...].astype(_BF16), wple_ref[...])
    out_ref[...] = h2 + gate * emb


def _combine(h1, route, y2, p2, nple, w_gate, w_ple):
    n = h1.shape[0]
    tk = TM_COMB
    steps = n // tk
    const2 = lambda i: (0, 0)
    return pl.pallas_call(
        _combine_kernel,
        grid=(steps,),
        in_specs=[
            pl.BlockSpec((tk, D_MODEL), lambda i: (i, 0)),
            pl.BlockSpec((tk, LANES), lambda i: (i, 0)),
            pl.BlockSpec((tk, HALF), lambda i: (i, 0)),
            pl.BlockSpec((tk, HALF), lambda i: (i + steps, 0)),
            pl.BlockSpec((tk, PLE_DIM), lambda i: (i, 0)),
            pl.BlockSpec((1, D_MODEL), const2),
            pl.BlockSpec((D_MODEL, D_MODEL), const2, pipeline_mode=pl.Buffered(1)),
            pl.BlockSpec((PLE_DIM, D_MODEL), const2, pipeline_mode=pl.Buffered(1)),
        ],
        out_specs=pl.BlockSpec((tk, D_MODEL), lambda i: (i, 0)),
        out_shape=jax.ShapeDtypeStruct((n, D_MODEL), _F32),
        compiler_params=pltpu.CompilerParams(
            dimension_semantics=("parallel",), vmem_limit_bytes=VMEM_LIMIT),
        name="combine",
    )(h1, route, y2, y2, p2, nple, w_gate, w_ple)


def _attention_tables(sinks):
    slopes = 2.0 ** (-8.0 * jnp.arange(1, N_HEADS + 1, dtype=_F32) / N_HEADS)
    qi = jnp.arange(ATTN_BLOCK)[:, None]
    kj = jnp.arange(ATTN_BLOCK)[None, :]
    dist = jnp.where(kj > qi, ATTN_BLOCK + qi - kj, qi - kj)
    tab = -slopes[:, None, None] * dist.astype(_F32)[None]
    tab = tab.reshape(N_KV_HEADS, 4 * ATTN_BLOCK, ATTN_BLOCK)
    sink = jnp.repeat(sinks.astype(_F32), ATTN_BLOCK).reshape(N_KV_HEADS, 4 * ATTN_BLOCK, 1)
    return tab, sink


def _layer(h, p_i, norm_mix, w_in, conv_w, q_norm, k_norm, sinks, out_norm_conv,
           out_norm_attn, w_out, norm_ffn, w_group, b_group, w_router, b_router,
           w_gate, w_up, w_down, norm_ple, w_ple_gate, w_ple):
    bsz, seq, _ = h.shape
    n = bsz * seq
    x2 = h.reshape(n, D_MODEL)
    row = lambda v: v.reshape(1, -1).astype(_F32)

    yc, q, kd, vd = _inproj(
        x2, row(norm_mix), w_in.astype(_BF16), row(jnp.tile(q_norm, 2)),
        row(jnp.tile(k_norm, 2)), conv_w.astype(_F32), row(out_norm_conv), seq)

    tab, sink = _attention_tables(sinks)
    w_r = jnp.zeros((D_MODEL, LANES), _F32)
    w_r = w_r.at[:, :N_GROUPS].set(w_group).at[:, N_GROUPS:N_GROUPS + N_EXPERTS].set(w_router)
    r_bias = jnp.zeros((1, LANES), _F32)
    r_bias = r_bias.at[0, :N_GROUPS].set(b_group).at[0, N_GROUPS:N_GROUPS + N_EXPERTS].set(b_router)
    h1, xp, logits = _mixer(
        yc, q, kd, vd, x2, tab, sink, row(out_norm_attn), w_out.astype(_BF16),
        row(norm_ffn), w_r.astype(_BF16), r_bias, seq)

    route, route_t, counts = _route(logits)

    cnt = counts[0, :N_EXPERTS].astype(jnp.int32)
    padded = (cnt + MOE_BLK - 1) // MOE_BLK * MOE_BLK
    pad_end = jnp.cumsum(padded)
    pad_start = pad_end - padded
    eid = route_t[0:TOP_K].astype(jnp.int32)
    onehot = eid[..., None] == jnp.arange(N_EXPERTS, dtype=jnp.int32)
    dest = (jnp.sum(jnp.where(onehot, pad_start, 0), axis=-1)
            + route_t[4:4 + TOP_K].astype(jnp.int32))

    def step_blocks(tile):
        return dest.reshape(TOP_K, n // tile, 1, tile).transpose(1, 2, 0, 3).reshape(
            n // tile, 1, TOP_K * tile)
    n_rows = n * TOP_K + N_EXPERTS * MOE_BLK
    nb = n_rows // MOE_BLK
    nb_used = (pad_end[-1] // MOE_BLK).astype(jnp.int32)
    blk_start = jnp.arange(nb, dtype=jnp.int32) * MOE_BLK
    block_e = jnp.sum((blk_start[:, None] >= pad_end[None, :]).astype(jnp.int32), axis=1)
    block_e = jnp.minimum(block_e, N_EXPERTS - 1)
    block_e = jnp.where(jnp.arange(nb) < nb_used, block_e, block_e[nb_used - 1])
    ids = jnp.arange(N_EXPERTS, dtype=jnp.int32)
    later = jnp.logical_and(ids[None, :] > ids[:, None], cnt[None, :] > 0)
    next_present = jnp.min(jnp.where(later, ids[None, :], N_EXPERTS), axis=1)
    next_present = jnp.where(next_present < N_EXPERTS, next_present, -1)
    onehot_b = block_e[:, None] == ids[None, :]
    next_e = jnp.sum(jnp.where(onehot_b, next_present[None, :], 0), axis=1).astype(jnp.int32)

    xs = _dispatch(xp, step_blocks(TM_DISP),
                   pad_end.astype(jnp.int32), padded.astype(jnp.int32), n_rows)

    rows = jnp.arange(n_rows, dtype=jnp.int32)
    blk_of = rows // MOE_BLK
    onehot_b32 = (block_e[:, None] == ids[None, :]).astype(jnp.int32)
    seg_start = jnp.sum(onehot_b32 * pad_start[None, :], axis=1)
    seg_cnt = jnp.sum(onehot_b32 * cnt[None, :], axis=1)
    local = rows - jnp.repeat(seg_start, MOE_BLK)
    is_row = jnp.logical_and(local < jnp.repeat(seg_cnt, MOE_BLK), blk_of < nb_used)
    scratch_slot = TOP_K * n + (blk_of % 2) * MOE_BLK + rows % MOE_BLK
    slot = jnp.where(is_row, xs[:, HALF].astype(jnp.int32), scratch_slot)
    lead = TOP_K * n + MOE_BLK + jnp.arange(MOE_BLK, dtype=jnp.int32)
    inv3 = jnp.concatenate([lead, slot]).reshape(nb + 1, 1, MOE_BLK)

    y2 = _experts(xs, inv3, w_gate, w_up, w_down, block_e, next_e, nb_used.reshape(1),
                  TOP_K * n)
    out = _combine(h1, route, y2, p_i.reshape(n, PLE_DIM), row(norm_ple),
                   w_ple_gate.astype(_BF16), w_ple.astype(_BF16))
    return out.reshape(bsz, seq, D_MODEL)


def kernel(x, p, norm_mix, w_in, conv_w, q_norm, k_norm, sinks, out_norm_conv, out_norm_attn,
           w_out, norm_ffn, w_group, b_group, w_router, b_router, w_gate, w_up, w_down,
           norm_ple, w_ple_gate, w_ple):
    h = x
    for i in range(p.shape[0]):
        h = _layer(h, p[i], norm_mix[i], w_in[i], conv_w[i], q_norm[i], k_norm[i], sinks[i],
                   out_norm_conv[i], out_norm_attn[i], w_out[i], norm_ffn[i], w_group[i],
                   b_group[i], w_router[i], b_router[i], w_gate[i], w_up[i], w_down[i],
                   norm_ple[i], w_ple_gate[i], w_ple[i])
    return h
```

```python
import functools

import jax
import jax.numpy as jnp
from jax import lax
from jax.experimental import pallas as pl
from jax.experimental.pallas import tpu as pltpu

D_MODEL = 2048
D_CONV = 1024
N_HEADS = 16
N_KV_HEADS = 4
HEAD_DIM = 64
D_ATTN = N_HEADS * HEAD_DIM
D_KV = N_KV_HEADS * HEAD_DIM
D_IN = 3 * D_CONV + D_ATTN + 2 * D_KV
ATTN_BLOCK = 128
N_GROUPS = 4
EXPERTS_PER_GROUP = 8
N_EXPERTS = N_GROUPS * EXPERTS_PER_GROUP
TOP_K = 2
D_EXPERT = 512
PLE_DIM = 256
EPS = 1e-6
NEG_INF = -1e30

LANES = 128
HALF = D_MODEL // 2
ROW_W = HALF + LANES
KV_DUP = 2 * D_KV

TM_PROJ = 512
TM_MIX = 512
TM_ROUTE = 1024
TM_DISP = 1024
TM_COMB = 512
MOE_BLK = 512
VMEM_LIMIT = 56 * 1024 * 1024

_F32 = jnp.float32
_BF16 = jnp.bfloat16
_U32 = jnp.uint32
_HI_MASK = 0xFFFF0000


def _rms(x, gain):
    return x * lax.rsqrt(jnp.mean(x * x, axis=-1, keepdims=True) + EPS) * gain


def _rms_split(x, gain):
    inv = lax.rsqrt(jnp.mean(x * x, axis=-1, keepdims=True) + EPS)
    return (x * gain).astype(_BF16), inv


def _dot(a, b):
    return jnp.dot(a, b, preferred_element_type=_F32)


def _pack_rows(x):
    h = x.shape[-1] // 2
    bits = lax.bitcast_convert_type(x.astype(_BF16).astype(_F32), _U32)
    return (bits[:, h:] & _U32(_HI_MASK)) | (bits[:, :h] >> 16)


def _unpack_rows(w):
    lo = lax.bitcast_convert_type(w << 16, _F32)
    hi = lax.bitcast_convert_type(w & _U32(_HI_MASK), _F32)
    return lo, hi


def _inproj_kernel(tiles_per_seq, x_ref, g_ref, w_ref, qg_ref, kg_ref, cw_ref, onc_ref,
                   yc_ref, q_ref, kd_ref, vd_ref, t_ref, uprev_ref):
    tm = x_ref.shape[0]
    is_first = (pl.program_id(0) % tiles_per_seq) == 0
    xn = _rms(x_ref[...], g_ref[...]).astype(_BF16)

    @pl.when(pl.program_id(0) == 0)
    def _():
        uprev_ref[...] = jnp.zeros_like(uprev_ref)

    lo = lax.broadcasted_iota(jnp.int32, (tm, LANES), 1) < HEAD_DIM
    q = _dot(xn, w_ref[:, 3 * D_CONV:3 * D_CONV + D_ATTN])
    scale = HEAD_DIM ** -0.5
    for pr in range(D_ATTN // LANES):
        qp = q[:, pr * LANES:(pr + 1) * LANES]
        sq = qp * qp
        s_lo = jnp.sum(jnp.where(lo, sq, 0.0), axis=-1, keepdims=True)
        s_hi = jnp.sum(jnp.where(lo, 0.0, sq), axis=-1, keepdims=True)
        ms = jnp.where(lo, s_lo, s_hi) * (1.0 / HEAD_DIM)
        qn = (qp * lax.rsqrt(ms + EPS) * qg_ref[...] * scale).astype(_BF16)
        q_lo = jnp.where(lo, qn, jnp.zeros_like(qn))
        q_hi = jnp.where(lo, jnp.zeros_like(qn), qn)
        for j in range(tm // ATTN_BLOCK):
            rows = slice(j * ATTN_BLOCK, (j + 1) * ATTN_BLOCK)
            base = (j * N_HEADS + 2 * pr) * ATTN_BLOCK
            q_ref[base:base + ATTN_BLOCK, :] = q_lo[rows]
            q_ref[base + ATTN_BLOCK:base + 2 * ATTN_BLOCK, :] = q_hi[rows]

    kv = _dot(xn, w_ref[:, 3 * D_CONV + D_ATTN:D_IN])
    for pr in range(2 * D_KV // LANES):
        pair = kv[:, pr * LANES:(pr + 1) * LANES]
        swapped = pltpu.roll(pair, HEAD_DIM, 1)
        for half in range(2):
            dup = jnp.where(lo, pair, swapped) if half == 0 else jnp.where(lo, swapped, pair)
            head = 2 * pr + half
            if head < N_KV_HEADS:
                ms = jnp.sum(dup * dup, axis=-1, keepdims=True) * (1.0 / LANES)
                dup = dup * lax.rsqrt(ms + EPS) * kg_ref[...]
                kd_ref[:, head * LANES:(head + 1) * LANES] = dup.astype(_BF16)
            else:
                head -= N_KV_HEADS
                vd_ref[:, head * LANES:(head + 1) * LANES] = dup.astype(_BF16)

    ch = 256
    row = lax.broadcasted_iota(jnp.int32, (tm, ch), 0)
    last = uprev_ref.shape[0] - 1
    ssq = jnp.zeros((tm, 1), _F32)
    for j in range(D_CONV // ch):
        cs = slice(j * ch, (j + 1) * ch)
        b = _dot(xn, w_ref[:, j * ch:(j + 1) * ch])
        c = _dot(xn, w_ref[:, D_CONV + j * ch:D_CONV + (j + 1) * ch])
        hc = _dot(xn, w_ref[:, 2 * D_CONV + j * ch:2 * D_CONV + (j + 1) * ch])
        u = c * hc
        uh = jnp.where(is_first, 0.0, uprev_ref[:, cs])
        uprev_ref[:, cs] = u[tm - last - 1:tm]
        u1 = jnp.where(row == 0, uh[last:last + 1], pltpu.roll(u, 1, 0))
        u2 = jnp.where(row == 0, uh[last - 1:last],
                       jnp.where(row == 1, uh[last:last + 1], pltpu.roll(u, 2, 0)))
        t = b * (cw_ref[2:3, cs] * u + cw_ref[1:2, cs] * u1 + cw_ref[0:1, cs] * u2)
        ssq = ssq + jnp.sum(t * t, axis=-1, keepdims=True)
        t_ref[:, cs] = t
    inv = lax.rsqrt(ssq * (1.0 / D_CONV) + EPS)
    yc_ref[...] = (t_ref[...] * inv * onc_ref[...]).astype(_BF16)


def _inproj(x2, g, w_in, qg2, kg2, cw, onc, seq):
    n = x2.shape[0]
    tm = TM_PROJ
    const = lambda i: (0, 0)
    return pl.pallas_call(
        functools.partial(_inproj_kernel, seq // tm),
        grid=(n // tm,),
        in_specs=[
            pl.BlockSpec((tm, D_MODEL), lambda i: (i, 0)),
            pl.BlockSpec((1, D_MODEL), const),
            pl.BlockSpec((D_MODEL, D_IN), const, pipeline_mode=pl.Buffered(1)),
            pl.BlockSpec((1, LANES), const),
            pl.BlockSpec((1, LANES), const),
            pl.BlockSpec((3, D_CONV), const),
            pl.BlockSpec((1, D_CONV), const),
        ],
        out_specs=[
            pl.BlockSpec((tm, D_CONV), lambda i: (i, 0)),
            pl.BlockSpec((tm * N_HEADS, LANES), lambda i: (i, 0)),
            pl.BlockSpec((tm, KV_DUP), lambda i: (i, 0)),
            pl.BlockSpec((tm, KV_DUP), lambda i: (i, 0)),
        ],
        out_shape=[
            jax.ShapeDtypeStruct((n, D_CONV), _BF16),
            jax.ShapeDtypeStruct((n * N_HEADS, LANES), _BF16),
            jax.ShapeDtypeStruct((n, KV_DUP), _BF16),
            jax.ShapeDtypeStruct((n, KV_DUP), _BF16),
        ],
        scratch_shapes=[pltpu.VMEM((tm, D_CONV), _F32),
                        pltpu.VMEM((8, D_CONV), _F32)],
        compiler_params=pltpu.CompilerParams(
            dimension_semantics=("arbitrary",), vmem_limit_bytes=VMEM_LIMIT),
        name="inproj",
    )(x2, g, w_in, qg2, kg2, cw, onc)


def _mixer_kernel(tiles_per_seq,
                  yc_ref, q_ref, kd_ref, vd_ref, kdh_ref, vdh_ref, x_ref,
                  tab_ref, sink_ref, ona_ref, wout_ref, nffn_ref,
                  wr_ref, rb_ref,
                  h1_ref, xp_ref, lg_ref, ya_ref):
    tm = x_ref.shape[0]
    blk = ATTN_BLOCK
    is_first = (pl.program_id(0) % tiles_per_seq) == 0

    lo = lax.broadcasted_iota(jnp.int32, (blk, LANES), 1) < HEAD_DIM
    hpd = 2
    qi = lax.broadcasted_iota(jnp.int32, (hpd * blk, blk), 0) % blk
    from_prev = lax.broadcasted_iota(jnp.int32, (hpd * blk, blk), 1) > qi
    no_prev = jnp.logical_and(is_first, from_prev)
    zero_p = jnp.zeros((hpd * blk, blk), _BF16)
    for j in range(tm // blk):
        rows = slice(j * blk, (j + 1) * blk)
        for g in range(N_KV_HEADS):
            cols = slice(g * LANES, (g + 1) * LANES)
            if j == 0:
                keys = jnp.concatenate([kdh_ref[:, cols], kd_ref[0:blk, cols]], axis=0)
                vals = jnp.concatenate([vdh_ref[:, cols], vd_ref[0:blk, cols]], axis=0)
            else:
                keys = kd_ref[(j - 1) * blk:(j + 1) * blk, cols]
                vals = vd_ref[(j - 1) * blk:(j + 1) * blk, cols]
            for hp in range(4 // hpd):
                grp = (j * N_HEADS + 4 * g + hpd * hp) * blk
                lhs = q_ref[grp:grp + hpd * blk, :]
                trow = slice(hp * hpd * blk, (hp + 1) * hpd * blk)
                s = lax.dot_general(lhs, keys, (((1,), (1,)), ((), ())),
                                    preferred_element_type=_F32)
                s = jnp.where(from_prev, s[:, 0:blk], s[:, blk:2 * blk]) + tab_ref[g, trow, :]
                if j == 0:
                    s = jnp.where(no_prev, NEG_INF, s)
                sink = sink_ref[g, trow, :]
                m = jnp.maximum(jnp.max(s, axis=-1, keepdims=True), sink)
                p = jnp.exp(s - m)
                denom = jnp.sum(p, axis=-1, keepdims=True) + jnp.exp(sink - m)
                p = p.astype(_BF16)
                p = jnp.concatenate([jnp.where(from_prev, p, zero_p),
                                     jnp.where(from_prev, zero_p, p)], axis=1)
                o = _dot(p, vals) / denom
                ya_ref[rows, (2 * g + hp) * LANES:(2 * g + hp + 1) * LANES] = jnp.where(
                    lo, o[0:blk], o[blk:2 * blk])

    ya, inv_a = _rms_split(ya_ref[...], ona_ref[...])
    h1 = (x_ref[...] + _dot(yc_ref[...], wout_ref[0:D_CONV, :])
          + _dot(ya, wout_ref[D_CONV:D_CONV + D_ATTN, :]) * inv_a)
    h1_ref[...] = h1

    xn = _rms(h1, nffn_ref[...])
    xp_ref[:, 0:HALF] = _pack_rows(xn)
    tok = pl.program_id(0) * tm + lax.broadcasted_iota(jnp.int32, (tm, LANES), 0)
    xp_ref[:, HALF:ROW_W] = tok.astype(_U32)
    lg_ref[...] = _dot(xn.astype(_BF16), wr_ref[...]) + rb_ref[...]


def _mixer(yc, q, kd, vd, x2, tab, sink, ona, w_out, nffn, wr, rb, seq):
    n = x2.shape[0]
    tm = TM_MIX
    const2 = lambda i: (0, 0)
    const3 = lambda i: (0, 0, 0)
    prev_blk = lambda i: (jnp.maximum(i * (tm // ATTN_BLOCK) - 1, 0), 0)
    return pl.pallas_call(
        functools.partial(_mixer_kernel, seq // tm),
        grid=(n // tm,),
        in_specs=[
            pl.BlockSpec((tm, D_CONV), lambda i: (i, 0)),
            pl.BlockSpec((tm * N_HEADS, LANES), lambda i: (i, 0)),
            pl.BlockSpec((tm, KV_DUP), lambda i: (i, 0)),
            pl.BlockSpec((tm, KV_DUP), lambda i: (i, 0)),
            pl.BlockSpec((ATTN_BLOCK, KV_DUP), prev_blk),
            pl.BlockSpec((ATTN_BLOCK, KV_DUP), prev_blk),
            pl.BlockSpec((tm, D_MODEL), lambda i: (i, 0)),
            pl.BlockSpec((N_KV_HEADS, 4 * ATTN_BLOCK, ATTN_BLOCK), const3,
                         pipeline_mode=pl.Buffered(1)),
            pl.BlockSpec((N_KV_HEADS, 4 * ATTN_BLOCK, 1), const3, pipeline_mode=pl.Buffered(1)),
            pl.BlockSpec((1, D_ATTN), const2),
            pl.BlockSpec((D_MODEL, D_MODEL), const2, pipeline_mode=pl.Buffered(1)),
            pl.BlockSpec((1, D_MODEL), const2),
            pl.BlockSpec((D_MODEL, LANES), const2),
            pl.BlockSpec((1, LANES), const2),
        ],
        out_specs=[
            pl.BlockSpec((tm, D_MODEL), lambda i: (i, 0)),
            pl.BlockSpec((tm, ROW_W), lambda i: (i, 0)),
            pl.BlockSpec((tm, LANES), lambda i: (i, 0)),
        ],
        out_shape=[
            jax.ShapeDtypeStruct((n, D_MODEL), _F32),
            jax.ShapeDtypeStruct((n, ROW_W), _U32),
            jax.ShapeDtypeStruct((n, LANES), _F32),
        ],
        scratch_shapes=[pltpu.VMEM((tm, D_ATTN), _F32)],
        compiler_params=pltpu.CompilerParams(
            dimension_semantics=("parallel",), vmem_limit_bytes=VMEM_LIMIT),
        name="mixer",
    )(yc, q, kd, vd, kd, vd, x2, tab, sink, ona, w_out, nffn, wr, rb)


def _route_kernel(lg_ref, route_ref, route_t_ref, cnt_ref, run_ref, earlier_ref):
    t = lg_ref.shape[0]

    @pl.when(pl.program_id(0) == 0)
    def _():
        run_ref[...] = jnp.zeros_like(run_ref)
        earlier_ref[...] = (lax.broadcasted_iota(jnp.int32, (t, t), 0)
                            > lax.broadcasted_iota(jnp.int32, (t, t), 1)).astype(_BF16)

    lg = lg_ref[...]
    lane = lax.broadcasted_iota(jnp.int32, (t, LANES), 1)
    ninf = -jnp.inf

    lane_f = lane.astype(_F32)

    def first_max(v):
        m = jnp.max(v, axis=-1, keepdims=True)
        idx = jnp.min(jnp.where(v == m, lane_f, float(LANES)), axis=-1, keepdims=True)
        return m, idx.astype(jnp.int32)

    is_g = lane < N_GROUPS
    gmax, gidx = first_max(jnp.where(is_g, lg, ninf))
    gsum = jnp.sum(jnp.where(is_g, jnp.exp(jnp.where(is_g, lg, gmax) - gmax), 0.0),
                   axis=-1, keepdims=True)
    g_w = 1.0 / gsum
    base = N_GROUPS + EXPERTS_PER_GROUP * gidx
    in_grp = jnp.logical_and(lane >= base, lane < base + EXPERTS_PER_GROUP)
    el = jnp.where(in_grp, lg, ninf)
    m1, i1 = first_max(el)
    m2, i2 = first_max(jnp.where(lane == i1, ninf, el))
    e21 = jnp.exp(m2 - m1)
    w1 = g_w / (1.0 + e21)
    w2 = g_w * e21 / (1.0 + e21)
    e1 = i1 - N_GROUPS
    e2 = i2 - N_GROUPS

    oh1 = lane == e1
    oh2 = lane == e2
    c12 = _dot(earlier_ref[...],
               jnp.concatenate([oh1.astype(_BF16), oh2.astype(_BF16)], axis=1))
    c1 = c12[:, 0:LANES]
    c2 = c12[:, LANES:2 * LANES]
    tot1 = jnp.sum(oh1.astype(_F32), axis=0, keepdims=True)
    tot2 = jnp.sum(oh2.astype(_F32), axis=0, keepdims=True)
    run = run_ref[0:1, :]
    r1 = jnp.sum(jnp.where(oh1, run + c1, 0.0), axis=-1, keepdims=True)
    r2 = jnp.sum(jnp.where(oh2, run + tot1 + c2, 0.0), axis=-1, keepdims=True)
    run_ref[...] = jnp.broadcast_to(run + tot1 + tot2, run_ref.shape)
    cnt_ref[...] = run_ref[...]

    out = jnp.where(lane == 0, e1.astype(_F32), 0.0)
    out = jnp.where(lane == 1, e2.astype(_F32), out)
    out = jnp.where(lane == 2, w1, out)
    out = jnp.where(lane == 3, w2, out)
    out = jnp.where(lane == 4, r1, out)
    out = jnp.where(lane == 5, r2, out)
    route_ref[...] = out
    route_t_ref[...] = out.T[0:8, :]


def _route(logits):
    n = logits.shape[0]
    t = TM_ROUTE
    return pl.pallas_call(
        _route_kernel,
        grid=(n // t,),
        in_specs=[pl.BlockSpec((t, LANES), lambda i: (i, 0))],
        out_specs=[pl.BlockSpec((t, LANES), lambda i: (i, 0)),
                   pl.BlockSpec((8, t), lambda i: (0, i)),
                   pl.BlockSpec((8, LANES), lambda i: (0, 0))],
        out_shape=[jax.ShapeDtypeStruct((n, LANES), _F32),
                   jax.ShapeDtypeStruct((8, n), _F32),
                   jax.ShapeDtypeStruct((8, LANES), _F32)],
        scratch_shapes=[pltpu.VMEM((8, LANES), _F32), pltpu.VMEM((t, t), _BF16)],
        compiler_params=pltpu.CompilerParams(dimension_semantics=("arbitrary",)),
        name="route",
    )(logits)


def _dispatch_kernel(pad_end_ref, padded_ref, dest_ref, xp_ref, xs_ref,
                     buf_ref, buf2_ref, zero_ref, lsem, rsem, zsem):
    td = buf_ref.shape[1]
    n_tok = xp_ref.shape[0]
    nb = xs_ref.shape[0] // MOE_BLK
    nb_used = pad_end_ref[N_EXPERTS - 1] // MOE_BLK
    i = pl.program_id(0)
    n_steps = pl.num_programs(0)
    slot = i % 3
    nxt = (i + 1) % 3
    prv = (i + 2) % 3

    def zero_copy(start):
        start = pl.multiple_of(start, MOE_BLK)
        return pltpu.make_async_copy(zero_ref, xs_ref.at[pl.ds(start, MOE_BLK)], zsem)

    def zero_blocks(fn):
        for e in range(N_EXPERTS):
            @pl.when(padded_ref[e] > 0)
            def _():
                fn(zero_copy(pad_end_ref[e] - MOE_BLK))
        for b in range(nb - N_EXPERTS, nb):
            @pl.when(b >= nb_used)
            def _():
                fn(zero_copy(b * MOE_BLK))

    def load(t, s):
        start = pl.multiple_of(t * td, td)
        return pltpu.make_async_copy(xp_ref.at[pl.ds(start, td)], buf_ref.at[s], lsem.at[s])

    def wait_rows(s):
        for k in range(TOP_K):
            pltpu.make_async_copy(buf_ref.at[s], xs_ref.at[pl.ds(0, td)], rsem.at[s]).wait()

    @pl.when(i == 0)
    def _():
        zero_ref[...] = jnp.zeros_like(zero_ref)
        zero_blocks(lambda c: c.start())
        zero_blocks(lambda c: c.wait())
        load(0, 0).start()

    @pl.when(i + 1 < n_steps)
    def _():
        load(i + 1, nxt).start()

    load(i, slot).wait()

    def issue(k, src_ref):
        def body(r, c):
            d = dest_ref[0, 0, k * td + r]
            pltpu.make_async_copy(src_ref.at[slot, pl.ds(r, 1)], xs_ref.at[pl.ds(d, 1)],
                                  rsem.at[slot]).start()
            return c
        lax.fori_loop(0, td, body, 0, unroll=16)

    issue(0, buf_ref)
    buf2_ref[slot, :, 0:HALF] = buf_ref[slot, :, 0:HALF]
    buf2_ref[slot, :, HALF:ROW_W] = buf_ref[slot, :, HALF:ROW_W] + _U32(n_tok)
    issue(1, buf2_ref)

    @pl.when(i > 0)
    def _():
        wait_rows(prv)

    @pl.when(i == n_steps - 1)
    def _():
        wait_rows(slot)


def _dispatch(xp, dest3, pad_end, padded, n_rows):
    steps = dest3.shape[0]
    td = dest3.shape[-1] // TOP_K
    return pl.pallas_call(
        _dispatch_kernel,
        grid_spec=pltpu.PrefetchScalarGridSpec(
            num_scalar_prefetch=2,
            grid=(steps,),
            in_specs=[
                pl.BlockSpec((1, 1, dest3.shape[-1]), lambda i, pe, pd: (i, 0, 0),
                             memory_space=pltpu.SMEM),
                pl.BlockSpec(memory_space=pl.ANY),
            ],
            out_specs=pl.BlockSpec(memory_space=pl.ANY),
            scratch_shapes=[pltpu.VMEM((3, td, ROW_W), _U32),
                            pltpu.VMEM((3, td, ROW_W), _U32),
                            pltpu.VMEM((MOE_BLK, ROW_W), _U32),
                            pltpu.SemaphoreType.DMA((3,)), pltpu.SemaphoreType.DMA((3,)),
                            pltpu.SemaphoreType.DMA],
        ),
        out_shape=jax.ShapeDtypeStruct((n_rows, ROW_W), _U32),
        compiler_params=pltpu.CompilerParams(
            dimension_semantics=("arbitrary",), vmem_limit_bytes=VMEM_LIMIT),
        name="dispatch",
    )(pad_end, padded, dest3, xp)


def _experts_kernel(be_ref, nxt_ref, nb_ref, xs_ref, inv_ref, wg_hbm, wu_hbm, wd_hbm, y2_ref,
                    wgs_ref, wus_ref, wds_ref, wgb_ref, wub_ref, wdb_ref,
                    ob0_ref, ob1_ref, wsem, osem):
    i = pl.program_id(0)
    n_used = nb_ref[0]
    valid = i < n_used
    e = be_ref[i]
    changed = jnp.logical_or(i == 0, e != be_ref[jnp.maximum(i - 1, 0)])
    obufs = (ob0_ref, ob1_ref)
    trash = y2_ref.shape[0] - 2 * MOE_BLK

    def fetch(ex):
        return (pltpu.make_async_copy(wg_hbm.at[ex], wgs_ref, wsem.at[0]),
                pltpu.make_async_copy(wu_hbm.at[ex], wus_ref, wsem.at[1]),
                pltpu.make_async_copy(wd_hbm.at[ex], wds_ref, wsem.at[2]))

    def row_copy(par, r):
        return pltpu.make_async_copy(obufs[par].at[pl.ds(r, 1)],
                                     y2_ref.at[pl.ds(inv_ref[0, 0, r], 1)], osem.at[par])

    def wait_block(par):
        pltpu.make_async_copy(obufs[par], y2_ref.at[pl.ds(0, MOE_BLK)], osem.at[par]).wait()

    @pl.when(i == 0)
    def _():
        for c in fetch(e):
            c.start()
        ob1_ref[...] = jnp.zeros_like(ob1_ref)
        fills = [pltpu.make_async_copy(ob1_ref, y2_ref.at[pl.ds(trash + h * MOE_BLK, MOE_BLK)],
                                       osem.at[1]) for h in range(2)]
        for c in fills:
            c.start()
        for c in fills:
            c.wait()

    @pl.when(jnp.logical_and(valid, changed))
    def _():
        for c in fetch(e):
            c.wait()
        wgb_ref[...] = wgs_ref[...].astype(_BF16)
        wub_ref[...] = wus_ref[...].astype(_BF16)
        wdb_ref[...] = wds_ref[...].astype(_BF16)
        nxt = nxt_ref[i]

        @pl.when(nxt >= 0)
        def _():
            for c in fetch(nxt):
                c.start()

    batch = MOE_BLK // 4
    for par in range(2):
        @pl.when(jnp.logical_and(valid, i % 2 == par))
        def _():
            @pl.when(i >= 1)
            def _():
                wait_block(par)

            def issue(b):
                for r in range(b * batch, (b + 1) * batch):
                    row_copy(1 - par, r).start()

            lo, hi = _unpack_rows(xs_ref[:, 0:HALF])
            lo = lo.astype(_BF16)
            hi = hi.astype(_BF16)
            issue(0)
            g = _dot(lo, wgb_ref[0:HALF, :])
            issue(1)
            g = g + _dot(hi, wgb_ref[HALF:D_MODEL, :])
            issue(2)
            u = _dot(lo, wub_ref[0:HALF, :])
            issue(3)
            u = u + _dot(hi, wub_ref[HALF:D_MODEL, :])
            hidden = (g * jax.nn.sigmoid(g) * u).astype(_BF16)
            obufs[par][...] = _pack_rows(_dot(hidden, wdb_ref[...]))

        @pl.when(jnp.logical_and(i == n_used, i % 2 == par))
        def _():
            wait_block(par)

            def body(r, c):
                row_copy(1 - par, r).start()
                return c
            lax.fori_loop(0, MOE_BLK, body, 0, unroll=8)
            wait_block(1 - par)


def _experts(xs, inv3, w_gate, w_up, w_down, block_e, next_e, nb_used, n_slots):
    n_rows = xs.shape[0]
    nb = n_rows // MOE_BLK
    row_blk = lambda i, be, nx, nbu: (jnp.minimum(i, nbu[0] - 1), 0)
    inv_blk = lambda i, be, nx, nbu: (jnp.minimum(i, nbu[0]), 0, 0)
    any_spec = pl.BlockSpec(memory_space=pl.ANY)
    return pl.pallas_call(
        _experts_kernel,
        grid_spec=pltpu.PrefetchScalarGridSpec(
            num_scalar_prefetch=3,
            grid=(nb,),
            in_specs=[pl.BlockSpec((MOE_BLK, ROW_W), row_blk),
                      pl.BlockSpec((1, 1, MOE_BLK), inv_blk, memory_space=pltpu.SMEM),
                      any_spec, any_spec, any_spec],
            out_specs=any_spec,
            scratch_shapes=[pltpu.VMEM((D_MODEL, D_EXPERT), _F32),
                            pltpu.VMEM((D_MODEL, D_EXPERT), _F32),
                            pltpu.VMEM((D_EXPERT, D_MODEL), _F32),
                            pltpu.VMEM((D_MODEL, D_EXPERT), _BF16),
                            pltpu.VMEM((D_MODEL, D_EXPERT), _BF16),
                            pltpu.VMEM((D_EXPERT, D_MODEL), _BF16),
                            pltpu.VMEM((MOE_BLK, HALF), _U32),
                            pltpu.VMEM((MOE_BLK, HALF), _U32),
                            pltpu.SemaphoreType.DMA((3,)),
                            pltpu.SemaphoreType.DMA((2,))],
        ),
        out_shape=jax.ShapeDtypeStruct((n_slots + 2 * MOE_BLK, HALF), _U32),
        compiler_params=pltpu.CompilerParams(
            dimension_semantics=("arbitrary",), vmem_limit_bytes=VMEM_LIMIT),
        name="experts",
    )(block_e, next_e, nb_used, xs, inv3, w_gate, w_up, w_down)


def _combine_kernel(h1_ref, route_ref, ya_ref, yb_ref, p_ref, nple_ref, wgate_ref, wple_ref,
                    out_ref):
    y1 = jnp.concatenate(_unpack_rows(ya_ref[...]), axis=-1)
    y2 = jnp.concatenate(_unpack_rows(yb_ref[...]), axis=-1)
    route = route_ref[...]
    h2 = h1_ref[...] + route[:, 2:3] * y1 + route[:, 3:4] * y2
    xn, inv = _rms_split(h2, nple_ref[...])
    gate = jax.nn.sigmoid(_dot(xn, wgate_ref[...]) * inv)
    emb = _dot(p_ref[...].astype(_BF16), wple_ref[...])
    out_ref[...] = h2 + gate * emb


def _combine(h1, route, y2, p2, nple, w_gate, w_ple):
    n = h1.shape[0]
    tk = TM_COMB
    steps = n // tk
    const2 = lambda i: (0, 0)
    return pl.pallas_call(
        _combine_kernel,
        grid=(steps,),
        in_specs=[
            pl.BlockSpec((tk, D_MODEL), lambda i: (i, 0)),
            pl.BlockSpec((tk, LANES), lambda i: (i, 0)),
            pl.BlockSpec((tk, HALF), lambda i: (i, 0)),
            pl.BlockSpec((tk, HALF), lambda i: (i + steps, 0)),
            pl.BlockSpec((tk, PLE_DIM), lambda i: (i, 0)),
            pl.BlockSpec((1, D_MODEL), const2),
            pl.BlockSpec((D_MODEL, D_MODEL), const2, pipeline_mode=pl.Buffered(1)),
            pl.BlockSpec((PLE_DIM, D_MODEL), const2, pipeline_mode=pl.Buffered(1)),
        ],
        out_specs=pl.BlockSpec((tk, D_MODEL), lambda i: (i, 0)),
        out_shape=jax.ShapeDtypeStruct((n, D_MODEL), _F32),
        compiler_params=pltpu.CompilerParams(
            dimension_semantics=("parallel",), vmem_limit_bytes=VMEM_LIMIT),
        name="combine",
    )(h1, route, y2, y2, p2, nple, w_gate, w_ple)


def _attention_tables(sinks):
    slopes = 2.0 ** (-8.0 * jnp.arange(1, N_HEADS + 1, dtype=_F32) / N_HEADS)
    qi = jnp.arange(ATTN_BLOCK)[:, None]
    kj = jnp.arange(ATTN_BLOCK)[None, :]
    dist = jnp.where(kj > qi, ATTN_BLOCK + qi - kj, qi - kj)
    tab = -slopes[:, None, None] * dist.astype(_F32)[None]
    tab = tab.reshape(N_KV_HEADS, 4 * ATTN_BLOCK, ATTN_BLOCK)
    sink = jnp.repeat(sinks.astype(_F32), ATTN_BLOCK).reshape(N_KV_HEADS, 4 * ATTN_BLOCK, 1)
    return tab, sink


def _layer(h, p_i, norm_mix, w_in, conv_w, q_norm, k_norm, sinks, out_norm_conv,
           out_norm_attn, w_out, norm_ffn, w_group, b_group, w_router, b_router,
           w_gate, w_up, w_down, norm_ple, w_ple_gate, w_ple):
    bsz, seq, _ = h.shape
    n = bsz * seq
    x2 = h.reshape(n, D_MODEL)
    row = lambda v: v.reshape(1, -1).astype(_F32)

    yc, q, kd, vd = _inproj(
        x2, row(norm_mix), w_in.astype(_BF16), row(jnp.tile(q_norm, 2)),
        row(jnp.tile(k_norm, 2)), conv_w.astype(_F32), row(out_norm_conv), seq)

    tab, sink = _attention_tables(sinks)
    w_r = jnp.zeros((D_MODEL, LANES), _F32)
    w_r = w_r.at[:, :N_GROUPS].set(w_group).at[:, N_GROUPS:N_GROUPS + N_EXPERTS].set(w_router)
    r_bias = jnp.zeros((1, LANES), _F32)
    r_bias = r_bias.at[0, :N_GROUPS].set(b_group).at[0, N_GROUPS:N_GROUPS + N_EXPERTS].set(b_router)
    h1, xp, logits = _mixer(
        yc, q, kd, vd, x2, tab, sink, row(out_norm_attn), w_out.astype(_BF16),
        row(norm_ffn), w_r.astype(_BF16), r_bias, seq)

    route, route_t, counts = _route(logits)

    cnt = counts[0, :N_EXPERTS].astype(jnp.int32)
    padded = (cnt + MOE_BLK - 1) // MOE_BLK * MOE_BLK
    pad_end = jnp.cumsum(padded)
    pad_start = pad_end - padded
    eid = route_t[0:TOP_K].astype(jnp.int32)
    onehot = eid[..., None] == jnp.arange(N_EXPERTS, dtype=jnp.int32)
    dest = (jnp.sum(jnp.where(onehot, pad_start, 0), axis=-1)
            + route_t[4:4 + TOP_K].astype(jnp.int32))

    def step_blocks(tile):
        return dest.reshape(TOP_K, n // tile, 1, tile).transpose(1, 2, 0, 3).reshape(
            n // tile, 1, TOP_K * tile)
    n_rows = n * TOP_K + N_EXPERTS * MOE_BLK
    nb = n_rows // MOE_BLK
    nb_used = (pad_end[-1] // MOE_BLK).astype(jnp.int32)
    blk_start = jnp.arange(nb, dtype=jnp.int32) * MOE_BLK
    block_e = jnp.sum((blk_start[:, None] >= pad_end[None, :]).astype(jnp.int32), axis=1)
    block_e = jnp.minimum(block_e, N_EXPERTS - 1)
    block_e = jnp.where(jnp.arange(nb) < nb_used, block_e, block_e[nb_used - 1])
    ids = jnp.arange(N_EXPERTS, dtype=jnp.int32)
    later = jnp.logical_and(ids[None, :] > ids[:, None], cnt[None, :] > 0)
    next_present = jnp.min(jnp.where(later, ids[None, :], N_EXPERTS), axis=1)
    next_present = jnp.where(next_present < N_EXPERTS, next_present, -1)
    onehot_b = block_e[:, None] == ids[None, :]
    next_e = jnp.sum(jnp.where(onehot_b, next_present[None, :], 0), axis=1).astype(jnp.int32)

    xs = _dispatch(xp, step_blocks(TM_DISP),
                   pad_end.astype(jnp.int32), padded.astype(jnp.int32), n_rows)

    rows = jnp.arange(n_rows, dtype=jnp.int32)
    blk_of = rows // MOE_BLK
    onehot_b32 = (block_e[:, None] == ids[None, :]).astype(jnp.int32)
    seg_start = jnp.sum(onehot_b32 * pad_start[None, :], axis=1)
    seg_cnt = jnp.sum(onehot_b32 * cnt[None, :], axis=1)
    local = rows - jnp.repeat(seg_start, MOE_BLK)
    is_row = jnp.logical_and(local < jnp.repeat(seg_cnt, MOE_BLK), blk_of < nb_used)
    scratch_slot = TOP_K * n + (blk_of % 2) * MOE_BLK + rows % MOE_BLK
    slot = jnp.where(is_row, xs[:, HALF].astype(jnp.int32), scratch_slot)
    lead = TOP_K * n + MOE_BLK + jnp.arange(MOE_BLK, dtype=jnp.int32)
    inv3 = jnp.concatenate([lead, slot]).reshape(nb + 1, 1, MOE_BLK)

    y2 = _experts(xs, inv3, w_gate, w_up, w_down, block_e, next_e, nb_used.reshape(1),
                  TOP_K * n)
    out = _combine(h1, route, y2, p_i.reshape(n, PLE_DIM), row(norm_ple),
                   w_ple_gate.astype(_BF16), w_ple.astype(_BF16))
    return out.reshape(bsz, seq, D_MODEL)


def kernel(x, p, norm_mix, w_in, conv_w, q_norm, k_norm, sinks, out_norm_conv, out_norm_attn,
           w_out, norm_ffn, w_group, b_group, w_router, b_router, w_gate, w_up, w_down,
           norm_ple, w_ple_gate, w_ple):
    h = x
    for i in range(p.shape[0]):
        h = _layer(h, p[i], norm_mix[i], w_in[i], conv_w[i], q_norm[i], k_norm[i], sinks[i],
                   out_norm_conv[i], out_norm_attn[i], w_out[i], norm_ffn[i], w_group[i],
                   b_group[i], w_router[i], b_router[i], w_gate[i], w_up[i], w_down[i],
                   norm_ple[i], w_ple_gate[i], w_ple[i])
    return h
```

```python
import functools

import jax
import jax.numpy as jnp
from jax import lax
from jax.experimental import pallas as pl
from jax.experimental.pallas import tpu as pltpu

D_MODEL = 2048
D_CONV = 1024
N_HEADS = 16
N_KV_HEADS = 4
HEAD_DIM = 64
D_ATTN = N_HEADS * HEAD_DIM
D_KV = N_KV_HEADS * HEAD_DIM
D_IN = 3 * D_CONV + D_ATTN + 2 * D_KV
ATTN_BLOCK = 128
N_GROUPS = 4
EXPERTS_PER_GROUP = 8
N_EXPERTS = N_GROUPS * EXPERTS_PER_GROUP
TOP_K = 2
D_EXPERT = 512
PLE_DIM = 256
EPS = 1e-6
NEG_INF = -1e30

LANES = 128
HALF = D_MODEL // 2
ROW_W = HALF + LANES
KV_DUP = 2 * D_KV

TM_PROJ = 512
TM_MIX = 512
TM_ROUTE = 1024
TM_DISP = 2048
TM_COMB = 512
MOE_BLK = 512
VMEM_LIMIT = 56 * 1024 * 1024

_F32 = jnp.float32
_BF16 = jnp.bfloat16
_U32 = jnp.uint32
_HI_MASK = 0xFFFF0000


def _rms(x, gain):
    return x * lax.rsqrt(jnp.mean(x * x, axis=-1, keepdims=True) + EPS) * gain


def _rms_split(x, gain):
    inv = lax.rsqrt(jnp.mean(x * x, axis=-1, keepdims=True) + EPS)
    return (x * gain).astype(_BF16), inv


def _dot(a, b):
    return jnp.dot(a, b, preferred_element_type=_F32)


def _pack_rows(x):
    h = x.shape[-1] // 2
    bits = lax.bitcast_convert_type(x.astype(_BF16).astype(_F32), _U32)
    return (bits[:, h:] & _U32(_HI_MASK)) | (bits[:, :h] >> 16)


def _unpack_rows(w):
    lo = lax.bitcast_convert_type(w << 16, _F32)
    hi = lax.bitcast_convert_type(w & _U32(_HI_MASK), _F32)
    return lo, hi


def _inproj_kernel(tiles_per_seq, x_ref, g_ref, w_ref, qg_ref, kg_ref, cw_ref, onc_ref,
                   yc_ref, q_ref, kd_ref, vd_ref, t_ref, uprev_ref):
    tm = x_ref.shape[0]
    is_first = (pl.program_id(0) % tiles_per_seq) == 0
    xn = _rms(x_ref[...], g_ref[...]).astype(_BF16)

    @pl.when(pl.program_id(0) == 0)
    def _():
        uprev_ref[...] = jnp.zeros_like(uprev_ref)

    lo = lax.broadcasted_iota(jnp.int32, (tm, LANES), 1) < HEAD_DIM
    q = _dot(xn, w_ref[:, 3 * D_CONV:3 * D_CONV + D_ATTN])
    scale = HEAD_DIM ** -0.5
    for pr in range(D_ATTN // LANES):
        qp = q[:, pr * LANES:(pr + 1) * LANES]
        sq = qp * qp
        s_lo = jnp.sum(jnp.where(lo, sq, 0.0), axis=-1, keepdims=True)
        s_hi = jnp.sum(jnp.where(lo, 0.0, sq), axis=-1, keepdims=True)
        ms = jnp.where(lo, s_lo, s_hi) * (1.0 / HEAD_DIM)
        qn = (qp * lax.rsqrt(ms + EPS) * qg_ref[...] * scale).astype(_BF16)
        q_lo = jnp.where(lo, qn, jnp.zeros_like(qn))
        q_hi = jnp.where(lo, jnp.zeros_like(qn), qn)
        for j in range(tm // ATTN_BLOCK):
            rows = slice(j * ATTN_BLOCK, (j + 1) * ATTN_BLOCK)
            base = (j * N_HEADS + 2 * pr) * ATTN_BLOCK
            q_ref[base:base + ATTN_BLOCK, :] = q_lo[rows]
            q_ref[base + ATTN_BLOCK:base + 2 * ATTN_BLOCK, :] = q_hi[rows]

    kv = _dot(xn, w_ref[:, 3 * D_CONV + D_ATTN:D_IN])
    for pr in range(2 * D_KV // LANES):
        pair = kv[:, pr * LANES:(pr + 1) * LANES]
        swapped = pltpu.roll(pair, HEAD_DIM, 1)
        for half in range(2):
            dup = jnp.where(lo, pair, swapped) if half == 0 else jnp.where(lo, swapped, pair)
            head = 2 * pr + half
            if head < N_KV_HEADS:
                ms = jnp.sum(dup * dup, axis=-1, keepdims=True) * (1.0 / LANES)
                dup = dup * lax.rsqrt(ms + EPS) * kg_ref[...]
                kd_ref[:, head * LANES:(head + 1) * LANES] = dup.astype(_BF16)
            else:
                head -= N_KV_HEADS
                vd_ref[:, head * LANES:(head + 1) * LANES] = dup.astype(_BF16)

    ch = 256
    row = lax.broadcasted_iota(jnp.int32, (tm, ch), 0)
    last = uprev_ref.shape[0] - 1
    ssq = jnp.zeros((tm, 1), _F32)
    for j in range(D_CONV // ch):
        cs = slice(j * ch, (j + 1) * ch)
        b = _dot(xn, w_ref[:, j * ch:(j + 1) * ch])
        c = _dot(xn, w_ref[:, D_CONV + j * ch:D_CONV + (j + 1) * ch])
        hc = _dot(xn, w_ref[:, 2 * D_CONV + j * ch:2 * D_CONV + (j + 1) * ch])
        u = c * hc
        uh = jnp.where(is_first, 0.0, uprev_ref[:, cs])
        uprev_ref[:, cs] = u[tm - last - 1:tm]
        u1 = jnp.where(row == 0, uh[last:last + 1], pltpu.roll(u, 1, 0))
        u2 = jnp.where(row == 0, uh[last - 1:last],
                       jnp.where(row == 1, uh[last:last + 1], pltpu.roll(u, 2, 0)))
        t = b * (cw_ref[2:3, cs] * u + cw_ref[1:2, cs] * u1 + cw_ref[0:1, cs] * u2)
        ssq = ssq + jnp.sum(t * t, axis=-1, keepdims=True)
        t_ref[:, cs] = t
    inv = lax.rsqrt(ssq * (1.0 / D_CONV) + EPS)
    yc_ref[...] = (t_ref[...] * inv * onc_ref[...]).astype(_BF16)


def _inproj(x2, g, w_in, qg2, kg2, cw, onc, seq):
    n = x2.shape[0]
    tm = TM_PROJ
    const = lambda i: (0, 0)
    return pl.pallas_call(
        functools.partial(_inproj_kernel, seq // tm),
        grid=(n // tm,),
        in_specs=[
            pl.BlockSpec((tm, D_MODEL), lambda i: (i, 0)),
            pl.BlockSpec((1, D_MODEL), const),
            pl.BlockSpec((D_MODEL, D_IN), const, pipeline_mode=pl.Buffered(1)),
            pl.BlockSpec((1, LANES), const),
            pl.BlockSpec((1, LANES), const),
            pl.BlockSpec((3, D_CONV), const),
            pl.BlockSpec((1, D_CONV), const),
        ],
        out_specs=[
            pl.BlockSpec((tm, D_CONV), lambda i: (i, 0)),
            pl.BlockSpec((tm * N_HEADS, LANES), lambda i: (i, 0)),
            pl.BlockSpec((tm, KV_DUP), lambda i: (i, 0)),
            pl.BlockSpec((tm, KV_DUP), lambda i: (i, 0)),
        ],
        out_shape=[
            jax.ShapeDtypeStruct((n, D_CONV), _BF16),
            jax.ShapeDtypeStruct((n * N_HEADS, LANES), _BF16),
            jax.ShapeDtypeStruct((n, KV_DUP), _BF16),
            jax.ShapeDtypeStruct((n, KV_DUP), _BF16),
        ],
        scratch_shapes=[pltpu.VMEM((tm, D_CONV), _F32),
                        pltpu.VMEM((8, D_CONV), _F32)],
        compiler_params=pltpu.CompilerParams(
            dimension_semantics=("arbitrary",), vmem_limit_bytes=VMEM_LIMIT),
        name="inproj",
    )(x2, g, w_in, qg2, kg2, cw, onc)


def _mixer_kernel(tiles_per_seq,
                  yc_ref, q_ref, kd_ref, vd_ref, kdh_ref, vdh_ref, x_ref,
                  tab_ref, sink_ref, ona_ref, wout_ref, nffn_ref,
                  wr_ref, rb_ref,
                  h1_ref, xp_ref, lg_ref, ya_ref):
    tm = x_ref.shape[0]
    blk = ATTN_BLOCK
    is_first = (pl.program_id(0) % tiles_per_seq) == 0

    lo = lax.broadcasted_iota(jnp.int32, (blk, LANES), 1) < HEAD_DIM
    hpd = 2
    qi = lax.broadcasted_iota(jnp.int32, (hpd * blk, blk), 0) % blk
    from_prev = lax.broadcasted_iota(jnp.int32, (hpd * blk, blk), 1) > qi
    no_prev = jnp.logical_and(is_first, from_prev)
    zero_p = jnp.zeros((hpd * blk, blk), _BF16)
    for j in range(tm // blk):
        rows = slice(j * blk, (j + 1) * blk)
        for g in range(N_KV_HEADS):
            cols = slice(g * LANES, (g + 1) * LANES)
            if j == 0:
                keys = jnp.concatenate([kdh_ref[:, cols], kd_ref[0:blk, cols]], axis=0)
                vals = jnp.concatenate([vdh_ref[:, cols], vd_ref[0:blk, cols]], axis=0)
            else:
                keys = kd_ref[(j - 1) * blk:(j + 1) * blk, cols]
                vals = vd_ref[(j - 1) * blk:(j + 1) * blk, cols]
            for hp in range(4 // hpd):
                grp = (j * N_HEADS + 4 * g + hpd * hp) * blk
                lhs = q_ref[grp:grp + hpd * blk, :]
                trow = slice(hp * hpd * blk, (hp + 1) * hpd * blk)
                s = lax.dot_general(lhs, keys, (((1,), (1,)), ((), ())),
                                    preferred_element_type=_F32)
                s = jnp.where(from_prev, s[:, 0:blk], s[:, blk:2 * blk]) + tab_ref[g, trow, :]
                if j == 0:
                    s = jnp.where(no_prev, NEG_INF, s)
                sink = sink_ref[g, trow, :]
                m = jnp.maximum(jnp.max(s, axis=-1, keepdims=True), sink)
                p = jnp.exp(s - m)
                denom = jnp.sum(p, axis=-1, keepdims=True) + jnp.exp(sink - m)
                p = p.astype(_BF16)
                p = jnp.concatenate([jnp.where(from_prev, p, zero_p),
                                     jnp.where(from_prev, zero_p, p)], axis=1)
                o = _dot(p, vals) / denom
                ya_ref[rows, (2 * g + hp) * LANES:(2 * g + hp + 1) * LANES] = jnp.where(
                    lo, o[0:blk], o[blk:2 * blk])

    ya, inv_a = _rms_split(ya_ref[...], ona_ref[...])
    h1 = (x_ref[...] + _dot(yc_ref[...], wout_ref[0:D_CONV, :])
          + _dot(ya, wout_ref[D_CONV:D_CONV + D_ATTN, :]) * inv_a)
    h1_ref[...] = h1

    xn = _rms(h1, nffn_ref[...])
    xp_ref[:, 0:HALF] = _pack_rows(xn)
    tok = pl.program_id(0) * tm + lax.broadcasted_iota(jnp.int32, (tm, LANES), 0)
    xp_ref[:, HALF:ROW_W] = tok.astype(_U32)
    lg_ref[...] = _dot(xn.astype(_BF16), wr_ref[...]) + rb_ref[...]


def _mixer(yc, q, kd, vd, x2, tab, sink, ona, w_out, nffn, wr, rb, seq):
    n = x2.shape[0]
    tm = TM_MIX
    const2 = lambda i: (0, 0)
    const3 = lambda i: (0, 0, 0)
    prev_blk = lambda i: (jnp.maximum(i * (tm // ATTN_BLOCK) - 1, 0), 0)
    return pl.pallas_call(
        functools.partial(_mixer_kernel, seq // tm),
        grid=(n // tm,),
        in_specs=[
            pl.BlockSpec((tm, D_CONV), lambda i: (i, 0)),
            pl.BlockSpec((tm * N_HEADS, LANES), lambda i: (i, 0)),
            pl.BlockSpec((tm, KV_DUP), lambda i: (i, 0)),
            pl.BlockSpec((tm, KV_DUP), lambda i: (i, 0)),
            pl.BlockSpec((ATTN_BLOCK, KV_DUP), prev_blk),
            pl.BlockSpec((ATTN_BLOCK, KV_DUP), prev_blk),
            pl.BlockSpec((tm, D_MODEL), lambda i: (i, 0)),
            pl.BlockSpec((N_KV_HEADS, 4 * ATTN_BLOCK, ATTN_BLOCK), const3,
                         pipeline_mode=pl.Buffered(1)),
            pl.BlockSpec((N_KV_HEADS, 4 * ATTN_BLOCK, 1), const3, pipeline_mode=pl.Buffered(1)),
            pl.BlockSpec((1, D_ATTN), const2),
            pl.BlockSpec((D_MODEL, D_MODEL), const2, pipeline_mode=pl.Buffered(1)),
            pl.BlockSpec((1, D_MODEL), const2),
            pl.BlockSpec((D_MODEL, LANES), const2),
            pl.BlockSpec((1, LANES), const2),
        ],
        out_specs=[
            pl.BlockSpec((tm, D_MODEL), lambda i: (i, 0)),
            pl.BlockSpec((tm, ROW_W), lambda i: (i, 0)),
            pl.BlockSpec((tm, LANES), lambda i: (i, 0)),
        ],
        out_shape=[
            jax.ShapeDtypeStruct((n, D_MODEL), _F32),
            jax.ShapeDtypeStruct((n, ROW_W), _U32),
            jax.ShapeDtypeStruct((n, LANES), _F32),
        ],
        scratch_shapes=[pltpu.VMEM((tm, D_ATTN), _F32)],
        compiler_params=pltpu.CompilerParams(
            dimension_semantics=("parallel",), vmem_limit_bytes=VMEM_LIMIT),
        name="mixer",
    )(yc, q, kd, vd, kd, vd, x2, tab, sink, ona, w_out, nffn, wr, rb)


def _route_kernel(lg_ref, route_ref, route_t_ref, cnt_ref, run_ref, earlier_ref):
    t = lg_ref.shape[0]

    @pl.when(pl.program_id(0) == 0)
    def _():
        run_ref[...] = jnp.zeros_like(run_ref)
        earlier_ref[...] = (lax.broadcasted_iota(jnp.int32, (t, t), 0)
                            > lax.broadcasted_iota(jnp.int32, (t, t), 1)).astype(_BF16)

    lg = lg_ref[...]
    lane = lax.broadcasted_iota(jnp.int32, (t, LANES), 1)
    ninf = -jnp.inf

    lane_f = lane.astype(_F32)

    def first_max(v):
        m = jnp.max(v, axis=-1, keepdims=True)
        idx = jnp.min(jnp.where(v == m, lane_f, float(LANES)), axis=-1, keepdims=True)
        return m, idx.astype(jnp.int32)

    is_g = lane < N_GROUPS
    gmax, gidx = first_max(jnp.where(is_g, lg, ninf))
    gsum = jnp.sum(jnp.where(is_g, jnp.exp(jnp.where(is_g, lg, gmax) - gmax), 0.0),
                   axis=-1, keepdims=True)
    g_w = 1.0 / gsum
    base = N_GROUPS + EXPERTS_PER_GROUP * gidx
    in_grp = jnp.logical_and(lane >= base, lane < base + EXPERTS_PER_GROUP)
    el = jnp.where(in_grp, lg, ninf)
    m1, i1 = first_max(el)
    m2, i2 = first_max(jnp.where(lane == i1, ninf, el))
    e21 = jnp.exp(m2 - m1)
    w1 = g_w / (1.0 + e21)
    w2 = g_w * e21 / (1.0 + e21)
    e1 = i1 - N_GROUPS
    e2 = i2 - N_GROUPS

    oh1 = lane == e1
    oh2 = lane == e2
    c12 = _dot(earlier_ref[...],
               jnp.concatenate([oh1.astype(_BF16), oh2.astype(_BF16)], axis=1))
    c1 = c12[:, 0:LANES]
    c2 = c12[:, LANES:2 * LANES]
    tot1 = jnp.sum(oh1.astype(_F32), axis=0, keepdims=True)
    tot2 = jnp.sum(oh2.astype(_F32), axis=0, keepdims=True)
    run1 = run_ref[0:1, :]
    run2 = run_ref[1:2, :]
    r1 = jnp.sum(jnp.where(oh1, run1 + c1, 0.0), axis=-1, keepdims=True)
    r2 = jnp.sum(jnp.where(oh2, run2 + c2, 0.0), axis=-1, keepdims=True)
    run_ref[0:1, :] = run1 + tot1
    run_ref[1:2, :] = run2 + tot2
    cnt_ref[...] = run_ref[...]

    out = jnp.where(lane == 0, e1.astype(_F32), 0.0)
    out = jnp.where(lane == 1, e2.astype(_F32), out)
    out = jnp.where(lane == 2, w1, out)
    out = jnp.where(lane == 3, w2, out)
    out = jnp.where(lane == 4, r1, out)
    out = jnp.where(lane == 5, r2, out)
    route_ref[...] = out
    route_t_ref[...] = out.T[0:8, :]


def _route(logits):
    n = logits.shape[0]
    t = TM_ROUTE
    return pl.pallas_call(
        _route_kernel,
        grid=(n // t,),
        in_specs=[pl.BlockSpec((t, LANES), lambda i: (i, 0))],
        out_specs=[pl.BlockSpec((t, LANES), lambda i: (i, 0)),
                   pl.BlockSpec((8, t), lambda i: (0, i)),
                   pl.BlockSpec((8, LANES), lambda i: (0, 0))],
        out_shape=[jax.ShapeDtypeStruct((n, LANES), _F32),
                   jax.ShapeDtypeStruct((8, n), _F32),
                   jax.ShapeDtypeStruct((8, LANES), _F32)],
        scratch_shapes=[pltpu.VMEM((8, LANES), _F32), pltpu.VMEM((t, t), _BF16)],
        compiler_params=pltpu.CompilerParams(dimension_semantics=("arbitrary",)),
        name="route",
    )(logits)


def _dispatch_kernel(pad_end_ref, padded_ref, dest_ref, xp_ref, xs_ref,
                     buf_ref, zero_ref, lsem, rsem, zsem):
    td = buf_ref.shape[1]
    nb = xs_ref.shape[0] // MOE_BLK
    nb_used = pad_end_ref[N_EXPERTS - 1] // MOE_BLK
    i = pl.program_id(0)
    n_steps = pl.num_programs(0)
    slot = i % 3
    nxt = (i + 1) % 3
    prv = (i + 2) % 3

    def zero_copy(start):
        start = pl.multiple_of(start, MOE_BLK)
        return pltpu.make_async_copy(zero_ref, xs_ref.at[pl.ds(start, MOE_BLK)], zsem)

    def zero_blocks(fn):
        for e in range(N_EXPERTS):
            @pl.when(padded_ref[e] > 0)
            def _():
                fn(zero_copy(pad_end_ref[e] - MOE_BLK))
        for b in range(nb - N_EXPERTS, nb):
            @pl.when(b >= nb_used)
            def _():
                fn(zero_copy(b * MOE_BLK))

    def load(t, s):
        start = pl.multiple_of(t * td, td)
        return pltpu.make_async_copy(xp_ref.at[pl.ds(start, td)], buf_ref.at[s], lsem.at[s])

    def wait_rows(s):
        for k in range(TOP_K):
            pltpu.make_async_copy(buf_ref.at[s], xs_ref.at[pl.ds(0, td)], rsem.at[s]).wait()

    @pl.when(i == 0)
    def _():
        zero_ref[...] = jnp.zeros_like(zero_ref)
        zero_blocks(lambda c: c.start())
        zero_blocks(lambda c: c.wait())
        load(0, 0).start()

    @pl.when(i + 1 < n_steps)
    def _():
        load(i + 1, nxt).start()

    load(i, slot).wait()

    def issue(r, c):
        for k in range(TOP_K):
            d = dest_ref[0, 0, k * td + r]
            pltpu.make_async_copy(buf_ref.at[slot, pl.ds(r, 1)], xs_ref.at[pl.ds(d, 1)],
                                  rsem.at[slot]).start()
        return c

    lax.fori_loop(0, td, issue, 0, unroll=16)

    @pl.when(i > 0)
    def _():
        wait_rows(prv)

    @pl.when(i == n_steps - 1)
    def _():
        wait_rows(slot)


def _dispatch(xp, dest3, pad_end, padded, n_rows):
    steps = dest3.shape[0]
    td = dest3.shape[-1] // TOP_K
    return pl.pallas_call(
        _dispatch_kernel,
        grid_spec=pltpu.PrefetchScalarGridSpec(
            num_scalar_prefetch=2,
            grid=(steps,),
            in_specs=[
                pl.BlockSpec((1, 1, dest3.shape[-1]), lambda i, pe, pd: (i, 0, 0),
                             memory_space=pltpu.SMEM),
                pl.BlockSpec(memory_space=pl.ANY),
            ],
            out_specs=pl.BlockSpec(memory_space=pl.ANY),
            scratch_shapes=[pltpu.VMEM((3, td, ROW_W), _U32),
                            pltpu.VMEM((MOE_BLK, ROW_W), _U32),
                            pltpu.SemaphoreType.DMA((3,)), pltpu.SemaphoreType.DMA((3,)),
                            pltpu.SemaphoreType.DMA],
        ),
        out_shape=jax.ShapeDtypeStruct((n_rows, ROW_W), _U32),
        compiler_params=pltpu.CompilerParams(
            dimension_semantics=("arbitrary",), vmem_limit_bytes=VMEM_LIMIT),
        name="dispatch",
    )(pad_end, padded, dest3, xp)


def _experts_kernel(be_ref, nxt_ref, nb_ref, xs_ref, inv_ref, wg_hbm, wu_hbm, wd_hbm, y2_ref,
                    wgs_ref, wus_ref, wds_ref, wgb_ref, wub_ref, wdb_ref,
                    ob0_ref, ob1_ref, wsem, osem):
    i = pl.program_id(0)
    n_used = nb_ref[0]
    valid = i < n_used
    e = be_ref[i]
    changed = jnp.logical_or(i == 0, e != be_ref[jnp.maximum(i - 1, 0)])
    obufs = (ob0_ref, ob1_ref)
    trash = y2_ref.shape[0] - 2 * MOE_BLK

    def fetch(ex):
        return (pltpu.make_async_copy(wg_hbm.at[ex], wgs_ref, wsem.at[0]),
                pltpu.make_async_copy(wu_hbm.at[ex], wus_ref, wsem.at[1]),
                pltpu.make_async_copy(wd_hbm.at[ex], wds_ref, wsem.at[2]))

    def row_copy(par, r):
        return pltpu.make_async_copy(obufs[par].at[pl.ds(r, 1)],
                                     y2_ref.at[pl.ds(inv_ref[0, 0, r], 1)], osem.at[par])

    def wait_block(par):
        pltpu.make_async_copy(obufs[par], y2_ref.at[pl.ds(0, MOE_BLK)], osem.at[par]).wait()

    @pl.when(i == 0)
    def _():
        for c in fetch(e):
            c.start()
        ob1_ref[...] = jnp.zeros_like(ob1_ref)
        fills = [pltpu.make_async_copy(ob1_ref, y2_ref.at[pl.ds(trash + h * MOE_BLK, MOE_BLK)],
                                       osem.at[1]) for h in range(2)]
        for c in fills:
            c.start()
        for c in fills:
            c.wait()

    @pl.when(jnp.logical_and(valid, changed))
    def _():
        for c in fetch(e):
            c.wait()
        wgb_ref[...] = wgs_ref[...].astype(_BF16)
        wub_ref[...] = wus_ref[...].astype(_BF16)
        wdb_ref[...] = wds_ref[...].astype(_BF16)
        nxt = nxt_ref[i]

        @pl.when(nxt >= 0)
        def _():
            for c in fetch(nxt):
                c.start()

    batch = MOE_BLK // 4
    for par in range(2):
        @pl.when(jnp.logical_and(valid, i % 2 == par))
        def _():
            @pl.when(i >= 1)
            def _():
                wait_block(par)

            def issue(b):
                for r in range(b * batch, (b + 1) * batch):
                    row_copy(1 - par, r).start()

            lo, hi = _unpack_rows(xs_ref[:, 0:HALF])
            lo = lo.astype(_BF16)
            hi = hi.astype(_BF16)
            issue(0)
            g = _dot(lo, wgb_ref[0:HALF, :])
            issue(1)
            g = g + _dot(hi, wgb_ref[HALF:D_MODEL, :])
            issue(2)
            u = _dot(lo, wub_ref[0:HALF, :])
            issue(3)
            u = u + _dot(hi, wub_ref[HALF:D_MODEL, :])
            hidden = (g * jax.nn.sigmoid(g) * u).astype(_BF16)
            obufs[par][...] = _pack_rows(_dot(hidden, wdb_ref[...]))

        @pl.when(jnp.logical_and(i == n_used, i % 2 == par))
        def _():
            wait_block(par)

            def body(r, c):
                row_copy(1 - par, r).start()
                return c
            lax.fori_loop(0, MOE_BLK, body, 0, unroll=8)
            wait_block(1 - par)


def _experts(xs, inv3, w_gate, w_up, w_down, block_e, next_e, nb_used, n_slots):
    n_rows = xs.shape[0]
    nb = n_rows // MOE_BLK
    row_blk = lambda i, be, nx, nbu: (jnp.minimum(i, nbu[0] - 1), 0)
    inv_blk = lambda i, be, nx, nbu: (jnp.minimum(i, nbu[0]), 0, 0)
    any_spec = pl.BlockSpec(memory_space=pl.ANY)
    return pl.pallas_call(
        _experts_kernel,
        grid_spec=pltpu.PrefetchScalarGridSpec(
            num_scalar_prefetch=3,
            grid=(nb,),
            in_specs=[pl.BlockSpec((MOE_BLK, ROW_W), row_blk),
                      pl.BlockSpec((1, 1, MOE_BLK), inv_blk, memory_space=pltpu.SMEM),
                      any_spec, any_spec, any_spec],
            out_specs=any_spec,
            scratch_shapes=[pltpu.VMEM((D_MODEL, D_EXPERT), _F32),
                            pltpu.VMEM((D_MODEL, D_EXPERT), _F32),
                            pltpu.VMEM((D_EXPERT, D_MODEL), _F32),
                            pltpu.VMEM((D_MODEL, D_EXPERT), _BF16),
                            pltpu.VMEM((D_MODEL, D_EXPERT), _BF16),
                            pltpu.VMEM((D_EXPERT, D_MODEL), _BF16),
                            pltpu.VMEM((MOE_BLK, HALF), _U32),
                            pltpu.VMEM((MOE_BLK, HALF), _U32),
                            pltpu.SemaphoreType.DMA((3,)),
                            pltpu.SemaphoreType.DMA((2,))],
        ),
        out_shape=jax.ShapeDtypeStruct((n_slots + 2 * MOE_BLK, HALF), _U32),
        compiler_params=pltpu.CompilerParams(
            dimension_semantics=("arbitrary",), vmem_limit_bytes=VMEM_LIMIT),
        name="experts",
    )(block_e, next_e, nb_used, xs, inv3, w_gate, w_up, w_down)


def _combine_kernel(h1_ref, route_ref, ya_ref, yb_ref, p_ref, nple_ref, wgate_ref, wple_ref,
                    out_ref):
    y1 = jnp.concatenate(_unpack_rows(ya_ref[...]), axis=-1)
    y2 = jnp.concatenate(_unpack_rows(yb_ref[...]), axis=-1)
    route = route_ref[...]
    h2 = h1_ref[...] + route[:, 2:3] * y1 + route[:, 3:4] * y2
    xn, inv = _rms_split(h2, nple_ref[...])
    gate = jax.nn.sigmoid(_dot(xn, wgate_ref[...]) * inv)
    emb = _dot(p_ref[...].astype(_BF16), wple_ref[...])
    out_ref[...] = h2 + gate * emb


def _combine(h1, route, y2, p2, nple, w_gate, w_ple):
    n = h1.shape[0]
    tk = TM_COMB
    steps = n // tk
    const2 = lambda i: (0, 0)
    return pl.pallas_call(
        _combine_kernel,
        grid=(steps,),
        in_specs=[
            pl.BlockSpec((tk, D_MODEL), lambda i: (i, 0)),
            pl.BlockSpec((tk, LANES), lambda i: (i, 0)),
            pl.BlockSpec((tk, HALF), lambda i: (i, 0)),
            pl.BlockSpec((tk, HALF), lambda i: (i + steps, 0)),
            pl.BlockSpec((tk, PLE_DIM), lambda i: (i, 0)),
            pl.BlockSpec((1, D_MODEL), const2),
            pl.BlockSpec((D_MODEL, D_MODEL), const2, pipeline_mode=pl.Buffered(1)),
            pl.BlockSpec((PLE_DIM, D_MODEL), const2, pipeline_mode=pl.Buffered(1)),
        ],
        out_specs=pl.BlockSpec((tk, D_MODEL), lambda i: (i, 0)),
        out_shape=jax.ShapeDtypeStruct((n, D_MODEL), _F32),
        compiler_params=pltpu.CompilerParams(
            dimension_semantics=("parallel",), vmem_limit_bytes=VMEM_LIMIT),
        name="combine",
    )(h1, route, y2, y2, p2, nple, w_gate, w_ple)


def _attention_tables(sinks):
    slopes = 2.0 ** (-8.0 * jnp.arange(1, N_HEADS + 1, dtype=_F32) / N_HEADS)
    qi = jnp.arange(ATTN_BLOCK)[:, None]
    kj = jnp.arange(ATTN_BLOCK)[None, :]
    dist = jnp.where(kj > qi, ATTN_BLOCK + qi - kj, qi - kj)
    tab = -slopes[:, None, None] * dist.astype(_F32)[None]
    tab = tab.reshape(N_KV_HEADS, 4 * ATTN_BLOCK, ATTN_BLOCK)
    sink = jnp.repeat(sinks.astype(_F32), ATTN_BLOCK).reshape(N_KV_HEADS, 4 * ATTN_BLOCK, 1)
    return tab, sink


def _layer(h, p_i, norm_mix, w_in, conv_w, q_norm, k_norm, sinks, out_norm_conv,
           out_norm_attn, w_out, norm_ffn, w_group, b_group, w_router, b_router,
           w_gate, w_up, w_down, norm_ple, w_ple_gate, w_ple):
    bsz, seq, _ = h.shape
    n = bsz * seq
    x2 = h.reshape(n, D_MODEL)
    row = lambda v: v.reshape(1, -1).astype(_F32)

    yc, q, kd, vd = _inproj(
        x2, row(norm_mix), w_in.astype(_BF16), row(jnp.tile(q_norm, 2)),
        row(jnp.tile(k_norm, 2)), conv_w.astype(_F32), row(out_norm_conv), seq)

    tab, sink = _attention_tables(sinks)
    w_r = jnp.zeros((D_MODEL, LANES), _F32)
    w_r = w_r.at[:, :N_GROUPS].set(w_group).at[:, N_GROUPS:N_GROUPS + N_EXPERTS].set(w_router)
    r_bias = jnp.zeros((1, LANES), _F32)
    r_bias = r_bias.at[0, :N_GROUPS].set(b_group).at[0, N_GROUPS:N_GROUPS + N_EXPERTS].set(b_router)
    h1, xp, logits = _mixer(
        yc, q, kd, vd, x2, tab, sink, row(out_norm_attn), w_out.astype(_BF16),
        row(norm_ffn), w_r.astype(_BF16), r_bias, seq)

    route, route_t, counts = _route(logits)

    cnt_k = counts[0:TOP_K, :N_EXPERTS].astype(jnp.int32)
    cnt = jnp.sum(cnt_k, axis=0)
    padded = (cnt + MOE_BLK - 1) // MOE_BLK * MOE_BLK
    pad_end = jnp.cumsum(padded)
    pad_start = pad_end - padded
    eid = route_t[0:TOP_K].astype(jnp.int32)
    onehot = eid[..., None] == jnp.arange(N_EXPERTS, dtype=jnp.int32)
    seg_base = jnp.stack([pad_start, pad_start + cnt_k[0]])
    dest = (jnp.sum(jnp.where(onehot, seg_base[:, None, :], 0), axis=-1)
            + route_t[4:4 + TOP_K].astype(jnp.int32))

    def step_blocks(tile):
        return dest.reshape(TOP_K, n // tile, 1, tile).transpose(1, 2, 0, 3).reshape(
            n // tile, 1, TOP_K * tile)
    n_rows = n * TOP_K + N_EXPERTS * MOE_BLK
    nb = n_rows // MOE_BLK
    nb_used = (pad_end[-1] // MOE_BLK).astype(jnp.int32)
    blk_start = jnp.arange(nb, dtype=jnp.int32) * MOE_BLK
    block_e = jnp.sum((blk_start[:, None] >= pad_end[None, :]).astype(jnp.int32), axis=1)
    block_e = jnp.minimum(block_e, N_EXPERTS - 1)
    block_e = jnp.where(jnp.arange(nb) < nb_used, block_e, block_e[nb_used - 1])
    ids = jnp.arange(N_EXPERTS, dtype=jnp.int32)
    later = jnp.logical_and(ids[None, :] > ids[:, None], cnt[None, :] > 0)
    next_present = jnp.min(jnp.where(later, ids[None, :], N_EXPERTS), axis=1)
    next_present = jnp.where(next_present < N_EXPERTS, next_present, -1)
    onehot_b = block_e[:, None] == ids[None, :]
    next_e = jnp.sum(jnp.where(onehot_b, next_present[None, :], 0), axis=1).astype(jnp.int32)

    xs = _dispatch(xp, step_blocks(TM_DISP),
                   pad_end.astype(jnp.int32), padded.astype(jnp.int32), n_rows)

    rows = jnp.arange(n_rows, dtype=jnp.int32)
    blk_of = rows // MOE_BLK
    onehot_b32 = (block_e[:, None] == ids[None, :]).astype(jnp.int32)
    seg_start = jnp.sum(onehot_b32 * pad_start[None, :], axis=1)
    seg_cnt = jnp.sum(onehot_b32 * cnt[None, :], axis=1)
    seg_first = jnp.sum(onehot_b32 * cnt_k[0][None, :], axis=1)
    local = rows - jnp.repeat(seg_start, MOE_BLK)
    is_row = jnp.logical_and(local < jnp.repeat(seg_cnt, MOE_BLK), blk_of < nb_used)
    choice = (local >= jnp.repeat(seg_first, MOE_BLK)).astype(jnp.int32)
    scratch_slot = TOP_K * n + (blk_of % 2) * MOE_BLK + rows % MOE_BLK
    slot = jnp.where(is_row, xs[:, HALF].astype(jnp.int32) + choice * n, scratch_slot)
    lead = TOP_K * n + MOE_BLK + jnp.arange(MOE_BLK, dtype=jnp.int32)
    inv3 = jnp.concatenate([lead, slot]).reshape(nb + 1, 1, MOE_BLK)

    y2 = _experts(xs, inv3, w_gate, w_up, w_down, block_e, next_e, nb_used.reshape(1),
                  TOP_K * n)
    out = _combine(h1, route, y2, p_i.reshape(n, PLE_DIM), row(norm_ple),
                   w_ple_gate.astype(_BF16), w_ple.astype(_BF16))
    return out.reshape(bsz, seq, D_MODEL)


def kernel(x, p, norm_mix, w_in, conv_w, q_norm, k_norm, sinks, out_norm_conv, out_norm_attn,
           w_out, norm_ffn, w_group, b_group, w_router, b_router, w_gate, w_up, w_down,
           norm_ple, w_ple_gate, w_ple):
    h = x
    for i in range(p.shape[0]):
        h = _layer(h, p[i], norm_mix[i], w_in[i], conv_w[i], q_norm[i], k_norm[i], sinks[i],
                   out_norm_conv[i], out_norm_attn[i], w_out[i], norm_ffn[i], w_group[i],
                   b_group[i], w_router[i], b_router[i], w_gate[i], w_up[i], w_down[i],
                   norm_ple[i], w_ple_gate[i], w_ple[i])
    return h
```

```python
import functools

import jax
import jax.numpy as jnp
from jax import lax
from jax.experimental import pallas as pl
from jax.experimental.pallas import tpu as pltpu

D_MODEL = 2048
D_CONV = 1024
N_HEADS = 16
N_KV_HEADS = 4
HEAD_DIM = 64
D_ATTN = N_HEADS * HEAD_DIM
D_KV = N_KV_HEADS * HEAD_DIM
D_IN = 3 * D_CONV + D_ATTN + 2 * D_KV
ATTN_BLOCK = 128
N_GROUPS = 4
EXPERTS_PER_GROUP = 8
N_EXPERTS = N_GROUPS * EXPERTS_PER_GROUP
TOP_K = 2
D_EXPERT = 512
PLE_DIM = 256
EPS = 1e-6
NEG_INF = -1e30

LANES = 128
HALF = D_MODEL // 2
ROW_W = HALF + LANES
KV_DUP = 2 * D_KV

TM_PROJ = 512
TM_MIX = 512
TM_ROUTE = 1024
TM_DISP = 2048
TM_COMB = 512
MOE_BLK = 512
VMEM_LIMIT = 56 * 1024 * 1024

_F32 = jnp.float32
_BF16 = jnp.bfloat16
_U32 = jnp.uint32
_HI_MASK = 0xFFFF0000


def _rms(x, gain):
    return x * lax.rsqrt(jnp.mean(x * x, axis=-1, keepdims=True) + EPS) * gain


def _rms_split(x, gain):
    inv = lax.rsqrt(jnp.mean(x * x, axis=-1, keepdims=True) + EPS)
    return (x * gain).astype(_BF16), inv


def _dot(a, b):
    return jnp.dot(a, b, preferred_element_type=_F32)


def _pack_rows(x):
    h = x.shape[-1] // 2
    bits = lax.bitcast_convert_type(x.astype(_BF16).astype(_F32), _U32)
    return (bits[:, h:] & _U32(_HI_MASK)) | (bits[:, :h] >> 16)


def _unpack_rows(w):
    lo = lax.bitcast_convert_type(w << 16, _F32)
    hi = lax.bitcast_convert_type(w & _U32(_HI_MASK), _F32)
    return lo, hi


def _inproj_kernel(tiles_per_seq, x_ref, g_ref, w_ref, qg_ref, kg_ref, cw_ref, onc_ref,
                   yc_ref, q_ref, kd_ref, vd_ref, t_ref, uprev_ref):
    tm = x_ref.shape[0]
    is_first = (pl.program_id(0) % tiles_per_seq) == 0
    xn = _rms(x_ref[...], g_ref[...]).astype(_BF16)

    @pl.when(pl.program_id(0) == 0)
    def _():
        uprev_ref[...] = jnp.zeros_like(uprev_ref)

    lo = lax.broadcasted_iota(jnp.int32, (tm, LANES), 1) < HEAD_DIM
    q = _dot(xn, w_ref[:, 3 * D_CONV:3 * D_CONV + D_ATTN])
    scale = HEAD_DIM ** -0.5
    for pr in range(D_ATTN // LANES):
        qp = q[:, pr * LANES:(pr + 1) * LANES]
        sq = qp * qp
        s_lo = jnp.sum(jnp.where(lo, sq, 0.0), axis=-1, keepdims=True)
        s_hi = jnp.sum(jnp.where(lo, 0.0, sq), axis=-1, keepdims=True)
        ms = jnp.where(lo, s_lo, s_hi) * (1.0 / HEAD_DIM)
        qn = (qp * lax.rsqrt(ms + EPS) * qg_ref[...] * scale).astype(_BF16)
        q_lo = jnp.where(lo, qn, jnp.zeros_like(qn))
        q_hi = jnp.where(lo, jnp.zeros_like(qn), qn)
        for j in range(tm // ATTN_BLOCK):
            rows = slice(j * ATTN_BLOCK, (j + 1) * ATTN_BLOCK)
            base = (j * N_HEADS + 2 * pr) * ATTN_BLOCK
            q_ref[base:base + ATTN_BLOCK, :] = q_lo[rows]
            q_ref[base + ATTN_BLOCK:base + 2 * ATTN_BLOCK, :] = q_hi[rows]

    kv = _dot(xn, w_ref[:, 3 * D_CONV + D_ATTN:D_IN])
    for pr in range(2 * D_KV // LANES):
        pair = kv[:, pr * LANES:(pr + 1) * LANES]
        swapped = pltpu.roll(pair, HEAD_DIM, 1)
        for half in range(2):
            dup = jnp.where(lo, pair, swapped) if half == 0 else jnp.where(lo, swapped, pair)
            head = 2 * pr + half
            if head < N_KV_HEADS:
                ms = jnp.sum(dup * dup, axis=-1, keepdims=True) * (1.0 / LANES)
                dup = dup * lax.rsqrt(ms + EPS) * kg_ref[...]
                kd_ref[:, head * LANES:(head + 1) * LANES] = dup.astype(_BF16)
            else:
                head -= N_KV_HEADS
                vd_ref[:, head * LANES:(head + 1) * LANES] = dup.astype(_BF16)

    ch = 256
    row = lax.broadcasted_iota(jnp.int32, (tm, ch), 0)
    last = uprev_ref.shape[0] - 1
    ssq = jnp.zeros((tm, 1), _F32)
    for j in range(D_CONV // ch):
        cs = slice(j * ch, (j + 1) * ch)
        b = _dot(xn, w_ref[:, j * ch:(j + 1) * ch])
        c = _dot(xn, w_ref[:, D_CONV + j * ch:D_CONV + (j + 1) * ch])
        hc = _dot(xn, w_ref[:, 2 * D_CONV + j * ch:2 * D_CONV + (j + 1) * ch])
        u = c * hc
        uh = jnp.where(is_first, 0.0, uprev_ref[:, cs])
        uprev_ref[:, cs] = u[tm - last - 1:tm]
        u1 = jnp.where(row == 0, uh[last:last + 1], pltpu.roll(u, 1, 0))
        u2 = jnp.where(row == 0, uh[last - 1:last],
                       jnp.where(row == 1, uh[last:last + 1], pltpu.roll(u, 2, 0)))
        t = b * (cw_ref[2:3, cs] * u + cw_ref[1:2, cs] * u1 + cw_ref[0:1, cs] * u2)
        ssq = ssq + jnp.sum(t * t, axis=-1, keepdims=True)
        t_ref[:, cs] = t
    inv = lax.rsqrt(ssq * (1.0 / D_CONV) + EPS)
    yc_ref[...] = (t_ref[...] * inv * onc_ref[...]).astype(_BF16)


def _inproj(x2, g, w_in, qg2, kg2, cw, onc, seq):
    n = x2.shape[0]
    tm = TM_PROJ
    const = lambda i: (0, 0)
    return pl.pallas_call(
        functools.partial(_inproj_kernel, seq // tm),
        grid=(n // tm,),
        in_specs=[
            pl.BlockSpec((tm, D_MODEL), lambda i: (i, 0)),
            pl.BlockSpec((1, D_MODEL), const),
            pl.BlockSpec((D_MODEL, D_IN), const, pipeline_mode=pl.Buffered(1)),
            pl.BlockSpec((1, LANES), const),
            pl.BlockSpec((1, LANES), const),
            pl.BlockSpec((3, D_CONV), const),
            pl.BlockSpec((1, D_CONV), const),
        ],
        out_specs=[
            pl.BlockSpec((tm, D_CONV), lambda i: (i, 0)),
            pl.BlockSpec((tm * N_HEADS, LANES), lambda i: (i, 0)),
            pl.BlockSpec((tm, KV_DUP), lambda i: (i, 0)),
            pl.BlockSpec((tm, KV_DUP), lambda i: (i, 0)),
        ],
        out_shape=[
            jax.ShapeDtypeStruct((n, D_CONV), _BF16),
            jax.ShapeDtypeStruct((n * N_HEADS, LANES), _BF16),
            jax.ShapeDtypeStruct((n, KV_DUP), _BF16),
            jax.ShapeDtypeStruct((n, KV_DUP), _BF16),
        ],
        scratch_shapes=[pltpu.VMEM((tm, D_CONV), _F32),
                        pltpu.VMEM((8, D_CONV), _F32)],
        compiler_params=pltpu.CompilerParams(
            dimension_semantics=("arbitrary",), vmem_limit_bytes=VMEM_LIMIT),
        name="inproj",
    )(x2, g, w_in, qg2, kg2, cw, onc)


def _mixer_kernel(tiles_per_seq,
                  yc_ref, q_ref, kd_ref, vd_ref, kdh_ref, vdh_ref, x_ref,
                  tab_ref, sink_ref, ona_ref, wout_ref, nffn_ref,
                  wr_ref, rb_ref,
                  h1_ref, xp_ref, lg_ref, ya_ref):
    tm = x_ref.shape[0]
    blk = ATTN_BLOCK
    is_first = (pl.program_id(0) % tiles_per_seq) == 0

    lo = lax.broadcasted_iota(jnp.int32, (blk, LANES), 1) < HEAD_DIM
    hpd = 2
    qi = lax.broadcasted_iota(jnp.int32, (hpd * blk, blk), 0) % blk
    from_prev = lax.broadcasted_iota(jnp.int32, (hpd * blk, blk), 1) > qi
    no_prev = jnp.logical_and(is_first, from_prev)
    zero_p = jnp.zeros((hpd * blk, blk), _BF16)
    for j in range(tm // blk):
        rows = slice(j * blk, (j + 1) * blk)
        for g in range(N_KV_HEADS):
            cols = slice(g * LANES, (g + 1) * LANES)
            if j == 0:
                keys = jnp.concatenate([kdh_ref[:, cols], kd_ref[0:blk, cols]], axis=0)
                vals = jnp.concatenate([vdh_ref[:, cols], vd_ref[0:blk, cols]], axis=0)
            else:
                keys = kd_ref[(j - 1) * blk:(j + 1) * blk, cols]
                vals = vd_ref[(j - 1) * blk:(j + 1) * blk, cols]
            for hp in range(4 // hpd):
                grp = (j * N_HEADS + 4 * g + hpd * hp) * blk
                lhs = q_ref[grp:grp + hpd * blk, :]
                trow = slice(hp * hpd * blk, (hp + 1) * hpd * blk)
                s = lax.dot_general(lhs, keys, (((1,), (1,)), ((), ())),
                                    preferred_element_type=_F32)
                s = jnp.where(from_prev, s[:, 0:blk], s[:, blk:2 * blk]) + tab_ref[g, trow, :]
                if j == 0:
                    s = jnp.where(no_prev, NEG_INF, s)
                sink = sink_ref[g, trow, :]
                m = jnp.maximum(jnp.max(s, axis=-1, keepdims=True), sink)
                p = jnp.exp(s - m)
                denom = jnp.sum(p, axis=-1, keepdims=True) + jnp.exp(sink - m)
                p = p.astype(_BF16)
                p = jnp.concatenate([jnp.where(from_prev, p, zero_p),
                                     jnp.where(from_prev, zero_p, p)], axis=1)
                o = _dot(p, vals) / denom
                ya_ref[rows, (2 * g + hp) * LANES:(2 * g + hp + 1) * LANES] = jnp.where(
                    lo, o[0:blk], o[blk:2 * blk])

    ya, inv_a = _rms_split(ya_ref[...], ona_ref[...])
    h1 = (x_ref[...] + _dot(yc_ref[...], wout_ref[0:D_CONV, :])
          + _dot(ya, wout_ref[D_CONV:D_CONV + D_ATTN, :]) * inv_a)
    h1_ref[...] = h1

    xn = _rms(h1, nffn_ref[...])
    xp_ref[:, 0:HALF] = _pack_rows(xn)
    tok = pl.program_id(0) * tm + lax.broadcasted_iota(jnp.int32, (tm, LANES), 0)
    xp_ref[:, HALF:ROW_W] = tok.astype(_U32)
    lg_ref[...] = _dot(xn.astype(_BF16), wr_ref[...]) + rb_ref[...]


def _mixer(yc, q, kd, vd, x2, tab, sink, ona, w_out, nffn, wr, rb, seq):
    n = x2.shape[0]
    tm = TM_MIX
    const2 = lambda i: (0, 0)
    const3 = lambda i: (0, 0, 0)
    prev_blk = lambda i: (jnp.maximum(i * (tm // ATTN_BLOCK) - 1, 0), 0)
    return pl.pallas_call(
        functools.partial(_mixer_kernel, seq // tm),
        grid=(n // tm,),
        in_specs=[
            pl.BlockSpec((tm, D_CONV), lambda i: (i, 0)),
            pl.BlockSpec((tm * N_HEADS, LANES), lambda i: (i, 0)),
            pl.BlockSpec((tm, KV_DUP), lambda i: (i, 0)),
            pl.BlockSpec((tm, KV_DUP), lambda i: (i, 0)),
            pl.BlockSpec((ATTN_BLOCK, KV_DUP), prev_blk),
            pl.BlockSpec((ATTN_BLOCK, KV_DUP), prev_blk),
            pl.BlockSpec((tm, D_MODEL), lambda i: (i, 0)),
            pl.BlockSpec((N_KV_HEADS, 4 * ATTN_BLOCK, ATTN_BLOCK), const3,
                         pipeline_mode=pl.Buffered(1)),
            pl.BlockSpec((N_KV_HEADS, 4 * ATTN_BLOCK, 1), const3, pipeline_mode=pl.Buffered(1)),
            pl.BlockSpec((1, D_ATTN), const2),
            pl.BlockSpec((D_MODEL, D_MODEL), const2, pipeline_mode=pl.Buffered(1)),
            pl.BlockSpec((1, D_MODEL), const2),
            pl.BlockSpec((D_MODEL, LANES), const2),
            pl.BlockSpec((1, LANES), const2),
        ],
        out_specs=[
            pl.BlockSpec((tm, D_MODEL), lambda i: (i, 0)),
            pl.BlockSpec((tm, ROW_W), lambda i: (i, 0)),
            pl.BlockSpec((tm, LANES), lambda i: (i, 0)),
        ],
        out_shape=[
            jax.ShapeDtypeStruct((n, D_MODEL), _F32),
            jax.ShapeDtypeStruct((n, ROW_W), _U32),
            jax.ShapeDtypeStruct((n, LANES), _F32),
        ],
        scratch_shapes=[pltpu.VMEM((tm, D_ATTN), _F32)],
        compiler_params=pltpu.CompilerParams(
            dimension_semantics=("parallel",), vmem_limit_bytes=VMEM_LIMIT),
        name="mixer",
    )(yc, q, kd, vd, kd, vd, x2, tab, sink, ona, w_out, nffn, wr, rb)


def _route_kernel(lg_ref, route_ref, route_t_ref, cnt_ref, run_ref, earlier_ref):
    t = lg_ref.shape[0]

    @pl.when(pl.program_id(0) == 0)
    def _():
        run_ref[...] = jnp.zeros_like(run_ref)
        earlier_ref[...] = (lax.broadcasted_iota(jnp.int32, (t, t), 0)
                            > lax.broadcasted_iota(jnp.int32, (t, t), 1)).astype(_BF16)

    lg = lg_ref[...]
    lane = lax.broadcasted_iota(jnp.int32, (t, LANES), 1)
    ninf = -jnp.inf

    lane_f = lane.astype(_F32)

    def first_max(v):
        m = jnp.max(v, axis=-1, keepdims=True)
        idx = jnp.min(jnp.where(v == m, lane_f, float(LANES)), axis=-1, keepdims=True)
        return m, idx.astype(jnp.int32)

    is_g = lane < N_GROUPS
    gmax, gidx = first_max(jnp.where(is_g, lg, ninf))
    gsum = jnp.sum(jnp.where(is_g, jnp.exp(jnp.where(is_g, lg, gmax) - gmax), 0.0),
                   axis=-1, keepdims=True)
    g_w = 1.0 / gsum
    base = N_GROUPS + EXPERTS_PER_GROUP * gidx
    in_grp = jnp.logical_and(lane >= base, lane < base + EXPERTS_PER_GROUP)
    el = jnp.where(in_grp, lg, ninf)
    m1, i1 = first_max(el)
    m2, i2 = first_max(jnp.where(lane == i1, ninf, el))
    e21 = jnp.exp(m2 - m1)
    w1 = g_w / (1.0 + e21)
    w2 = g_w * e21 / (1.0 + e21)
    e1 = i1 - N_GROUPS
    e2 = i2 - N_GROUPS

    oh1 = lane == e1
    oh2 = lane == e2
    c12 = _dot(earlier_ref[...],
               jnp.concatenate([oh1.astype(_BF16), oh2.astype(_BF16)], axis=1))
    c1 = c12[:, 0:LANES]
    c2 = c12[:, LANES:2 * LANES]
    tot1 = jnp.sum(oh1.astype(_F32), axis=0, keepdims=True)
    tot2 = jnp.sum(oh2.astype(_F32), axis=0, keepdims=True)
    run1 = run_ref[0:1, :]
    run2 = run_ref[1:2, :]
    r1 = jnp.sum(jnp.where(oh1, run1 + c1, 0.0), axis=-1, keepdims=True)
    r2 = jnp.sum(jnp.where(oh2, run2 + c2, 0.0), axis=-1, keepdims=True)
    run_ref[0:1, :] = run1 + tot1
    run_ref[1:2, :] = run2 + tot2
    cnt_ref[...] = run_ref[...]

    out = jnp.where(lane == 0, e1.astype(_F32), 0.0)
    out = jnp.where(lane == 1, e2.astype(_F32), out)
    out = jnp.where(lane == 2, w1, out)
    out = jnp.where(lane == 3, w2, out)
    out = jnp.where(lane == 4, r1, out)
    out = jnp.where(lane == 5, r2, out)
    route_ref[...] = out
    route_t_ref[...] = out.T[0:8, :]


def _route(logits):
    n = logits.shape[0]
    t = TM_ROUTE
    return pl.pallas_call(
        _route_kernel,
        grid=(n // t,),
        in_specs=[pl.BlockSpec((t, LANES), lambda i: (i, 0))],
        out_specs=[pl.BlockSpec((t, LANES), lambda i: (i, 0)),
                   pl.BlockSpec((8, t), lambda i: (0, i)),
                   pl.BlockSpec((8, LANES), lambda i: (0, 0))],
        out_shape=[jax.ShapeDtypeStruct((n, LANES), _F32),
                   jax.ShapeDtypeStruct((8, n), _F32),
                   jax.ShapeDtypeStruct((8, LANES), _F32)],
        scratch_shapes=[pltpu.VMEM((8, LANES), _F32), pltpu.VMEM((t, t), _BF16)],
        compiler_params=pltpu.CompilerParams(dimension_semantics=("arbitrary",)),
        name="route",
    )(logits)


def _dispatch_kernel(pad_end_ref, padded_ref, dest_ref, xp_ref, xs_ref,
                     buf_ref, zero_ref, lsem, rsem, zsem):
    td = buf_ref.shape[1]
    nb = xs_ref.shape[0] // MOE_BLK
    nb_used = pad_end_ref[N_EXPERTS - 1] // MOE_BLK
    i = pl.program_id(0)
    n_steps = pl.num_programs(0)
    slot = i % 3
    nxt = (i + 1) % 3
    prv = (i + 2) % 3

    def zero_copy(start):
        start = pl.multiple_of(start, MOE_BLK)
        return pltpu.make_async_copy(zero_ref, xs_ref.at[pl.ds(start, MOE_BLK)], zsem)

    def zero_blocks(fn):
        for e in range(N_EXPERTS):
            @pl.when(padded_ref[e] > 0)
            def _():
                fn(zero_copy(pad_end_ref[e] - MOE_BLK))
        for b in range(nb - N_EXPERTS, nb):
            @pl.when(b >= nb_used)
            def _():
                fn(zero_copy(b * MOE_BLK))

    def load(t, s):
        start = pl.multiple_of(t * td, td)
        return pltpu.make_async_copy(xp_ref.at[pl.ds(start, td)], buf_ref.at[s], lsem.at[s])

    def wait_rows(s):
        for k in range(TOP_K):
            pltpu.make_async_copy(buf_ref.at[s], xs_ref.at[pl.ds(0, td)], rsem.at[s]).wait()

    @pl.when(i == 0)
    def _():
        zero_ref[...] = jnp.zeros_like(zero_ref)
        zero_blocks(lambda c: c.start())
        zero_blocks(lambda c: c.wait())
        load(0, 0).start()

    @pl.when(i + 1 < n_steps)
    def _():
        load(i + 1, nxt).start()

    load(i, slot).wait()

    def issue(r, c):
        for k in range(TOP_K):
            d = dest_ref[0, 0, k * td + r]
            pltpu.make_async_copy(buf_ref.at[slot, pl.ds(r, 1)], xs_ref.at[pl.ds(d, 1)],
                                  rsem.at[slot]).start()
        return c

    lax.fori_loop(0, td, issue, 0, unroll=16)

    @pl.when(i > 0)
    def _():
        wait_rows(prv)

    @pl.when(i == n_steps - 1)
    def _():
        wait_rows(slot)


def _dispatch(xp, dest3, pad_end, padded, n_rows):
    steps = dest3.shape[0]
    td = dest3.shape[-1] // TOP_K
    return pl.pallas_call(
        _dispatch_kernel,
        grid_spec=pltpu.PrefetchScalarGridSpec(
            num_scalar_prefetch=2,
            grid=(steps,),
            in_specs=[
                pl.BlockSpec((1, 1, dest3.shape[-1]), lambda i, pe, pd: (i, 0, 0),
                             memory_space=pltpu.SMEM),
                pl.BlockSpec(memory_space=pl.ANY),
            ],
            out_specs=pl.BlockSpec(memory_space=pl.ANY),
            scratch_shapes=[pltpu.VMEM((3, td, ROW_W), _U32),
                            pltpu.VMEM((MOE_BLK, ROW_W), _U32),
                            pltpu.SemaphoreType.DMA((3,)), pltpu.SemaphoreType.DMA((3,)),
                            pltpu.SemaphoreType.DMA],
        ),
        out_shape=jax.ShapeDtypeStruct((n_rows, ROW_W), _U32),
        compiler_params=pltpu.CompilerParams(
            dimension_semantics=("arbitrary",), vmem_limit_bytes=VMEM_LIMIT),
        name="dispatch",
    )(pad_end, padded, dest3, xp)


def _experts_kernel(be_ref, nxt_ref, nb_ref, xs_ref, inv_ref, wg_hbm, wu_hbm, wd_hbm, y2_ref,
                    wgs_ref, wus_ref, wds_ref, wgb_ref, wub_ref, wdb_ref,
                    ob0_ref, ob1_ref, wsem, osem):
    i = pl.program_id(0)
    n_used = nb_ref[0]
    valid = i < n_used
    e = be_ref[i]
    changed = jnp.logical_or(i == 0, e != be_ref[jnp.maximum(i - 1, 0)])
    obufs = (ob0_ref, ob1_ref)
    trash = y2_ref.shape[0] - 2 * MOE_BLK

    def fetch(ex):
        return (pltpu.make_async_copy(wg_hbm.at[ex], wgs_ref, wsem.at[0]),
                pltpu.make_async_copy(wu_hbm.at[ex], wus_ref, wsem.at[1]),
                pltpu.make_async_copy(wd_hbm.at[ex], wds_ref, wsem.at[2]))

    def row_copy(par, r):
        return pltpu.make_async_copy(obufs[par].at[pl.ds(r, 1)],
                                     y2_ref.at[pl.ds(inv_ref[0, 0, r], 1)], osem.at[par])

    def wait_block(par):
        pltpu.make_async_copy(obufs[par], y2_ref.at[pl.ds(0, MOE_BLK)], osem.at[par]).wait()

    @pl.when(i == 0)
    def _():
        for c in fetch(e):
            c.start()
        ob0_ref[...] = jnp.zeros_like(ob0_ref)
        ob1_ref[...] = jnp.zeros_like(ob1_ref)
        fills = [pltpu.make_async_copy(ob1_ref, y2_ref.at[pl.ds(trash + h * MOE_BLK, MOE_BLK)],
                                       osem.at[1]) for h in range(2)]
        for c in fills:
            c.start()
        for c in fills:
            c.wait()
        pltpu.make_async_copy(ob0_ref, y2_ref.at[pl.ds(trash, MOE_BLK)], osem.at[0]).start()

    @pl.when(jnp.logical_and(valid, changed))
    def _():
        for c in fetch(e):
            c.wait()
        wgb_ref[...] = wgs_ref[...].astype(_BF16)
        wub_ref[...] = wus_ref[...].astype(_BF16)
        wdb_ref[...] = wds_ref[...].astype(_BF16)
        nxt = nxt_ref[i]

        @pl.when(nxt >= 0)
        def _():
            for c in fetch(nxt):
                c.start()

    batch = MOE_BLK // 4
    for par in range(2):
        @pl.when(jnp.logical_and(valid, i % 2 == par))
        def _():
            def issue(b):
                for r in range(b * batch, (b + 1) * batch):
                    row_copy(1 - par, r).start()

            lo, hi = _unpack_rows(xs_ref[:, 0:HALF])
            lo = lo.astype(_BF16)
            hi = hi.astype(_BF16)
            issue(0)
            g = _dot(lo, wgb_ref[0:HALF, :])
            issue(1)
            g = g + _dot(hi, wgb_ref[HALF:D_MODEL, :])
            issue(2)
            u = _dot(lo, wub_ref[0:HALF, :])
            issue(3)
            u = u + _dot(hi, wub_ref[HALF:D_MODEL, :])
            hidden = (g * jax.nn.sigmoid(g) * u).astype(_BF16)
            words = _pack_rows(_dot(hidden, wdb_ref[...]))
            wait_block(par)
            obufs[par][...] = words

        @pl.when(jnp.logical_and(i == n_used, i % 2 == par))
        def _():
            wait_block(par)

            def body(r, c):
                row_copy(1 - par, r).start()
                return c
            lax.fori_loop(0, MOE_BLK, body, 0, unroll=8)
            wait_block(1 - par)


def _experts(xs, inv3, w_gate, w_up, w_down, block_e, next_e, nb_used, n_slots):
    n_rows = xs.shape[0]
    nb = n_rows // MOE_BLK
    row_blk = lambda i, be, nx, nbu: (jnp.minimum(i, nbu[0] - 1), 0)
    inv_blk = lambda i, be, nx, nbu: (jnp.minimum(i, nbu[0]), 0, 0)
    any_spec = pl.BlockSpec(memory_space=pl.ANY)
    return pl.pallas_call(
        _experts_kernel,
        grid_spec=pltpu.PrefetchScalarGridSpec(
            num_scalar_prefetch=3,
            grid=(nb,),
            in_specs=[pl.BlockSpec((MOE_BLK, ROW_W), row_blk),
                      pl.BlockSpec((1, 1, MOE_BLK), inv_blk, memory_space=pltpu.SMEM),
                      any_spec, any_spec, any_spec],
            out_specs=any_spec,
            scratch_shapes=[pltpu.VMEM((D_MODEL, D_EXPERT), _F32),
                            pltpu.VMEM((D_MODEL, D_EXPERT), _F32),
                            pltpu.VMEM((D_EXPERT, D_MODEL), _F32),
                            pltpu.VMEM((D_MODEL, D_EXPERT), _BF16),
                            pltpu.VMEM((D_MODEL, D_EXPERT), _BF16),
                            pltpu.VMEM((D_EXPERT, D_MODEL), _BF16),
                            pltpu.VMEM((MOE_BLK, HALF), _U32),
                            pltpu.VMEM((MOE_BLK, HALF), _U32),
                            pltpu.SemaphoreType.DMA((3,)),
                            pltpu.SemaphoreType.DMA((2,))],
        ),
        out_shape=jax.ShapeDtypeStruct((n_slots + 2 * MOE_BLK, HALF), _U32),
        compiler_params=pltpu.CompilerParams(
            dimension_semantics=("arbitrary",), vmem_limit_bytes=VMEM_LIMIT),
        name="experts",
    )(block_e, next_e, nb_used, xs, inv3, w_gate, w_up, w_down)


def _combine_kernel(h1_ref, route_ref, ya_ref, yb_ref, p_ref, nple_ref, wgate_ref, wple_ref,
                    out_ref):
    y1 = jnp.concatenate(_unpack_rows(ya_ref[...]), axis=-1)
    y2 = jnp.concatenate(_unpack_rows(yb_ref[...]), axis=-1)
    route = route_ref[...]
    h2 = h1_ref[...] + route[:, 2:3] * y1 + route[:, 3:4] * y2
    xn, inv = _rms_split(h2, nple_ref[...])
    gate = jax.nn.sigmoid(_dot(xn, wgate_ref[...]) * inv)
    emb = _dot(p_ref[...].astype(_BF16), wple_ref[...])
    out_ref[...] = h2 + gate * emb


def _combine(h1, route, y2, p2, nple, w_gate, w_ple):
    n = h1.shape[0]
    tk = TM_COMB
    steps = n // tk
    const2 = lambda i: (0, 0)
    return pl.pallas_call(
        _combine_kernel,
        grid=(steps,),
        in_specs=[
            pl.BlockSpec((tk, D_MODEL), lambda i: (i, 0)),
            pl.BlockSpec((tk, LANES), lambda i: (i, 0)),
            pl.BlockSpec((tk, HALF), lambda i: (i, 0)),
            pl.BlockSpec((tk, HALF), lambda i: (i + steps, 0)),
            pl.BlockSpec((tk, PLE_DIM), lambda i: (i, 0)),
            pl.BlockSpec((1, D_MODEL), const2),
            pl.BlockSpec((D_MODEL, D_MODEL), const2, pipeline_mode=pl.Buffered(1)),
            pl.BlockSpec((PLE_DIM, D_MODEL), const2, pipeline_mode=pl.Buffered(1)),
        ],
        out_specs=pl.BlockSpec((tk, D_MODEL), lambda i: (i, 0)),
        out_shape=jax.ShapeDtypeStruct((n, D_MODEL), _F32),
        compiler_params=pltpu.CompilerParams(
            dimension_semantics=("parallel",), vmem_limit_bytes=VMEM_LIMIT),
        name="combine",
    )(h1, route, y2, y2, p2, nple, w_gate, w_ple)


def _attention_tables(sinks):
    slopes = 2.0 ** (-8.0 * jnp.arange(1, N_HEADS + 1, dtype=_F32) / N_HEADS)
    qi = jnp.arange(ATTN_BLOCK)[:, None]
    kj = jnp.arange(ATTN_BLOCK)[None, :]
    dist = jnp.where(kj > qi, ATTN_BLOCK + qi - kj, qi - kj)
    tab = -slopes[:, None, None] * dist.astype(_F32)[None]
    tab = tab.reshape(N_KV_HEADS, 4 * ATTN_BLOCK, ATTN_BLOCK)
    sink = jnp.repeat(sinks.astype(_F32), ATTN_BLOCK).reshape(N_KV_HEADS, 4 * ATTN_BLOCK, 1)
    return tab, sink


def _layer(h, p_i, norm_mix, w_in, conv_w, q_norm, k_norm, sinks, out_norm_conv,
           out_norm_attn, w_out, norm_ffn, w_group, b_group, w_router, b_router,
           w_gate, w_up, w_down, norm_ple, w_ple_gate, w_ple):
    bsz, seq, _ = h.shape
    n = bsz * seq
    x2 = h.reshape(n, D_MODEL)
    row = lambda v: v.reshape(1, -1).astype(_F32)

    yc, q, kd, vd = _inproj(
        x2, row(norm_mix), w_in.astype(_BF16), row(jnp.tile(q_norm, 2)),
        row(jnp.tile(k_norm, 2)), conv_w.astype(_F32), row(out_norm_conv), seq)

    tab, sink = _attention_tables(sinks)
    w_r = jnp.zeros((D_MODEL, LANES), _F32)
    w_r = w_r.at[:, :N_GROUPS].set(w_group).at[:, N_GROUPS:N_GROUPS + N_EXPERTS].set(w_router)
    r_bias = jnp.zeros((1, LANES), _F32)
    r_bias = r_bias.at[0, :N_GROUPS].set(b_group).at[0, N_GROUPS:N_GROUPS + N_EXPERTS].set(b_router)
    h1, xp, logits = _mixer(
        yc, q, kd, vd, x2, tab, sink, row(out_norm_attn), w_out.astype(_BF16),
        row(norm_ffn), w_r.astype(_BF16), r_bias, seq)

    route, route_t, counts = _route(logits)

    cnt_k = counts[0:TOP_K, :N_EXPERTS].astype(jnp.int32)
    cnt = jnp.sum(cnt_k, axis=0)
    padded = (cnt + MOE_BLK - 1) // MOE_BLK * MOE_BLK
    pad_end = jnp.cumsum(padded)
    pad_start = pad_end - padded
    eid = route_t[0:TOP_K].astype(jnp.int32)
    onehot = eid[..., None] == jnp.arange(N_EXPERTS, dtype=jnp.int32)
    seg_base = jnp.stack([pad_start, pad_start + cnt_k[0]])
    dest = (jnp.sum(jnp.where(onehot, seg_base[:, None, :], 0), axis=-1)
            + route_t[4:4 + TOP_K].astype(jnp.int32))

    def step_blocks(tile):
        return dest.reshape(TOP_K, n // tile, 1, tile).transpose(1, 2, 0, 3).reshape(
            n // tile, 1, TOP_K * tile)
    n_rows = n * TOP_K + N_EXPERTS * MOE_BLK
    nb = n_rows // MOE_BLK
    nb_used = (pad_end[-1] // MOE_BLK).astype(jnp.int32)
    blk_start = jnp.arange(nb, dtype=jnp.int32) * MOE_BLK
    block_e = jnp.sum((blk_start[:, None] >= pad_end[None, :]).astype(jnp.int32), axis=1)
    block_e = jnp.minimum(block_e, N_EXPERTS - 1)
    block_e = jnp.where(jnp.arange(nb) < nb_used, block_e, block_e[nb_used - 1])
    ids = jnp.arange(N_EXPERTS, dtype=jnp.int32)
    later = jnp.logical_and(ids[None, :] > ids[:, None], cnt[None, :] > 0)
    next_present = jnp.min(jnp.where(later, ids[None, :], N_EXPERTS), axis=1)
    next_present = jnp.where(next_present < N_EXPERTS, next_present, -1)
    onehot_b = block_e[:, None] == ids[None, :]
    next_e = jnp.sum(jnp.where(onehot_b, next_present[None, :], 0), axis=1).astype(jnp.int32)

    xs = _dispatch(xp, step_blocks(TM_DISP),
                   pad_end.astype(jnp.int32), padded.astype(jnp.int32), n_rows)

    rows = jnp.arange(n_rows, dtype=jnp.int32)
    blk_of = rows // MOE_BLK
    onehot_b32 = (block_e[:, None] == ids[None, :]).astype(jnp.int32)
    seg_start = jnp.sum(onehot_b32 * pad_start[None, :], axis=1)
    seg_cnt = jnp.sum(onehot_b32 * cnt[None, :], axis=1)
    seg_first = jnp.sum(onehot_b32 * cnt_k[0][None, :], axis=1)
    local = rows - jnp.repeat(seg_start, MOE_BLK)
    is_row = jnp.logical_and(local < jnp.repeat(seg_cnt, MOE_BLK), blk_of < nb_used)
    choice = (local >= jnp.repeat(seg_first, MOE_BLK)).astype(jnp.int32)
    scratch_slot = TOP_K * n + (blk_of % 2) * MOE_BLK + rows % MOE_BLK
    slot = jnp.where(is_row, xs[:, HALF].astype(jnp.int32) + choice * n, scratch_slot)
    lead = TOP_K * n + MOE_BLK + jnp.arange(MOE_BLK, dtype=jnp.int32)
    inv3 = jnp.concatenate([lead, slot]).reshape(nb + 1, 1, MOE_BLK)

    y2 = _experts(xs, inv3, w_gate, w_up, w_down, block_e, next_e, nb_used.reshape(1),
                  TOP_K * n)
    out = _combine(h1, route, y2, p_i.reshape(n, PLE_DIM), row(norm_ple),
                   w_ple_gate.astype(_BF16), w_ple.astype(_BF16))
    return out.reshape(bsz, seq, D_MODEL)


def kernel(x, p, norm_mix, w_in, conv_w, q_norm, k_norm, sinks, out_norm_conv, out_norm_attn,
           w_out, norm_ffn, w_group, b_group, w_router, b_router, w_gate, w_up, w_down,
           norm_ple, w_ple_gate, w_ple):
    h = x
    for i in range(p.shape[0]):
        h = _layer(h, p[i], norm_mix[i], w_in[i], conv_w[i], q_norm[i], k_norm[i], sinks[i],
                   out_norm_conv[i], out_norm_attn[i], w_out[i], norm_ffn[i], w_group[i],
                   b_group[i], w_router[i], b_router[i], w_gate[i], w_up[i], w_down[i],
                   norm_ple[i], w_ple_gate[i], w_ple[i])
    return h
```
